```python
import jax, jax.numpy as jnp
from jax import lax
import numpy as np

D_MODEL = 4096
BATCH = 16
SEQ = 2048
DEPTH = 1

D_MIX = D_MODEL
HG_WIDTH = D_MIX // 2
HG_HEAD_DIM = 128
HG_HEADS = HG_WIDTH // HG_HEAD_DIM
HG_CHUNK = 64
LRU_WIDTH = D_MIX - HG_WIDTH
LRU_BLOCKS = 16
LRU_BLOCK_DIM = LRU_WIDTH // LRU_BLOCKS
LRU_CONV = 4
LRU_C = 8.0
D_FF = 256 * ((8 * D_MODEL // 3 + 255) // 256)
FFN_CONV = 3
EPS = 1e-6
IN_WIDTHS = (HG_WIDTH, HG_WIDTH, HG_WIDTH, HG_WIDTH, LRU_WIDTH, LRU_WIDTH)
IN_TOTAL = sum(IN_WIDTHS)
IN_SPLIT = tuple(int(v) for v in np.cumsum(IN_WIDTHS)[:-1])

kernel_name = 'hymba_hgrn2_rglru_convffn_block'


def rmsnorm(x, w):
    x32 = x.astype(jnp.float32)
    y = x32 * lax.rsqrt(jnp.mean(x32 * x32, axis=-1, keepdims=True) + EPS)
    return (y * w.astype(jnp.float32)).astype(x.dtype)


def causal_dwconv(x, w, b):
    width = w.shape[0]
    seq = x.shape[1]
    xp = jnp.pad(x, ((0, 0), (width - 1, 0), (0, 0)))
    y = b + xp[:, 0:seq, :] * w[0]
    for j in range(1, width):
        y = y + xp[:, j:j + seq, :] * w[j]
    return y


def hgrn2_chunk_scan(q, k, v, g):
    bsz, seq, nh, dk = q.shape
    dv = v.shape[-1]
    n = seq // HG_CHUNK

    def to_chunks(t):
        return t.reshape(bsz, n, HG_CHUNK, nh, t.shape[-1]).transpose(1, 0, 3, 2, 4)

    causal = jnp.tril(jnp.ones((HG_CHUNK, HG_CHUNK), dtype=bool))[:, :, None]

    def step(state, inp):
        qc, kc, vc, gc = inp
        b = jnp.cumsum(gc, axis=2)
        o_inter = jnp.einsum('bhtk,bhkv->bhtv', qc * jnp.exp(b), state)
        diff = b[:, :, :, None, :] - b[:, :, None, :, :]
        decay = jnp.exp(jnp.where(causal, diff, -jnp.inf))
        scores = jnp.einsum('bhtk,bhtsk,bhsk->bhts', qc, decay, kc)
        o = o_inter + jnp.einsum('bhts,bhsv->bhtv', scores, vc)
        b_last = b[:, :, -1:, :]
        state = (jnp.exp(b_last[:, :, 0, :])[..., None] * state
                 + jnp.einsum('bhsk,bhsv->bhkv', kc * jnp.exp(b_last - b), vc))
        return state, o

    init = jnp.zeros((bsz, nh, dk, dv), jnp.float32)
    _, o = lax.scan(step, init, (to_chunks(q), to_chunks(k), to_chunks(v), to_chunks(g)))
    return o.transpose(1, 0, 3, 2, 4).reshape(bsz, seq, nh, dv)


def hgrn2_group(q_raw, f_raw, i_raw, g_raw, lb, norm_w):
    bsz, seq, _ = q_raw.shape
    q = jax.nn.silu(q_raw.astype(jnp.float32))
    f = lb + (1.0 - lb) * jax.nn.sigmoid(f_raw.astype(jnp.float32))
    k = 1.0 - f
    logf = jnp.log(f)
    v = i_raw.astype(jnp.float32)
    shp = (bsz, seq, HG_HEADS, HG_HEAD_DIM)
    o = hgrn2_chunk_scan(q.reshape(shp), k.reshape(shp), v.reshape(shp), logf.reshape(shp))
    o = rmsnorm(o, norm_w.reshape(HG_HEADS, HG_HEAD_DIM))
    return o.reshape(bsz, seq, HG_WIDTH) * jax.nn.silu(g_raw.astype(jnp.float32))


def _lin_combine(c1, c2):
    a1, b1 = c1
    a2, b2 = c2
    return a1 * a2, a2 * b1 + b2


def rglru_group(x_raw, y_raw, conv_w, conv_b, wa, ba, wx, bx, lam):
    bsz, seq, _ = x_raw.shape
    xb = causal_dwconv(x_raw, conv_w, conv_b).astype(jnp.float32)
    xblk = xb.reshape(bsz, seq, LRU_BLOCKS, LRU_BLOCK_DIM)
    r = jax.nn.sigmoid(jnp.einsum('bsnd,nde->bsne', xblk, wa).reshape(bsz, seq, LRU_WIDTH) + ba)
    i = jax.nn.sigmoid(jnp.einsum('bsnd,nde->bsne', xblk, wx).reshape(bsz, seq, LRU_WIDTH) + bx)
    log_a = -LRU_C * r * jax.nn.softplus(-lam.astype(jnp.float32))
    a = jnp.exp(log_a)
    mult = jnp.sqrt(-jnp.expm1(2.0 * log_a))
    mult = mult.at[:, 0].set(1.0)
    u = xb * i * mult
    _, h = lax.associative_scan(_lin_combine, (a, u), axis=1)
    return h * jax.nn.gelu(y_raw.astype(jnp.float32))


def _fwd_setup_inputs(seed: int = 0) -> dict:
    key = jax.random.key(seed)
    ks = jax.random.split(key, 20)
    f32 = jnp.float32
    nrm = lambda k, shp, s: (jax.random.normal(k, shp, f32) * s)
    x = jax.random.normal(ks[0], (BATCH, SEQ, D_MODEL), f32)
    ln1_w = 1.0 + nrm(ks[1], (DEPTH, D_MODEL), 0.02)
    w_in = nrm(ks[2], (DEPTH, D_MODEL, IN_TOTAL), D_MODEL ** -0.5)
    lb_gamma = nrm(ks[3], (DEPTH + 1, HG_WIDTH), 0.5)
    hg_norm_w = 1.0 + nrm(ks[4], (DEPTH, HG_WIDTH), 0.02)
    lru_conv_w = nrm(ks[5], (DEPTH, LRU_CONV, LRU_WIDTH), LRU_CONV ** -0.5)
    lru_conv_b = nrm(ks[6], (DEPTH, LRU_WIDTH), 0.02)
    lru_wa = nrm(ks[7], (DEPTH, LRU_BLOCKS, LRU_BLOCK_DIM, LRU_BLOCK_DIM), LRU_BLOCK_DIM ** -0.5)
    lru_ba = nrm(ks[8], (DEPTH, LRU_WIDTH), 0.1)
    lru_wx = nrm(ks[9], (DEPTH, LRU_BLOCKS, LRU_BLOCK_DIM, LRU_BLOCK_DIM), LRU_BLOCK_DIM ** -0.5)
    lru_bx = nrm(ks[10], (DEPTH, LRU_WIDTH), 0.1)
    a_c = jax.random.uniform(ks[11], (DEPTH, LRU_WIDTH), f32, 0.9, 0.999)
    a0 = a_c ** (1.0 / LRU_C)
    lru_lambda = jnp.log(a0) - jnp.log1p(-a0)
    lru_norm_w = 1.0 + nrm(ks[12], (DEPTH, LRU_WIDTH), 0.02)
    w_out = nrm(ks[13], (DEPTH, D_MIX, D_MODEL), D_MIX ** -0.5)
    ln2_w = 1.0 + nrm(ks[14], (DEPTH, D_MODEL), 0.02)
    ffn_w_up = nrm(ks[15], (DEPTH, D_MODEL, 2 * D_FF), D_MODEL ** -0.5)
    ffn_conv_w = nrm(ks[16], (DEPTH, FFN_CONV, 2 * D_FF), FFN_CONV ** -0.5)
    ffn_conv_b = nrm(ks[17], (DEPTH, 2 * D_FF), 0.02)
    ffn_w_down = nrm(ks[18], (DEPTH, D_FF, D_MODEL), D_FF ** -0.5)
    final_norm_w = 1.0 + nrm(ks[19], (D_MODEL,), 0.02)
    return {'x': x, 'ln1_w': ln1_w, 'w_in': w_in, 'lb_gamma': lb_gamma,
            'hg_norm_w': hg_norm_w, 'lru_conv_w': lru_conv_w, 'lru_conv_b': lru_conv_b,
            'lru_wa': lru_wa, 'lru_ba': lru_ba, 'lru_wx': lru_wx, 'lru_bx': lru_bx,
            'lru_lambda': lru_lambda, 'lru_norm_w': lru_norm_w, 'w_out': w_out,
            'ln2_w': ln2_w, 'ffn_w_up': ffn_w_up, 'ffn_conv_w': ffn_conv_w,
            'ffn_conv_b': ffn_conv_b, 'ffn_w_down': ffn_w_down, 'final_norm_w': final_norm_w}


def _fwd_reference(x, ln1_w, w_in, lb_gamma, hg_norm_w, lru_conv_w, lru_conv_b, lru_wa, lru_ba,
              lru_wx, lru_bx, lru_lambda, lru_norm_w, w_out, ln2_w, ffn_w_up, ffn_conv_w,
              ffn_conv_b, ffn_w_down, final_norm_w):
    lb_all = jnp.cumsum(jax.nn.softmax(lb_gamma.astype(jnp.float32), axis=0), axis=0)
    h = x
    for l in range(DEPTH):
        hn = rmsnorm(h, ln1_w[l])
        proj = jnp.einsum('bsd,de->bse', hn, w_in[l])
        q_r, f_r, i_r, g_r, x_r, y_r = jnp.split(proj, IN_SPLIT, axis=-1)
        o_hg = hgrn2_group(q_r, f_r, i_r, g_r, lb_all[l], hg_norm_w[l])
        o_lru = rglru_group(x_r, y_r, lru_conv_w[l], lru_conv_b[l], lru_wa[l], lru_ba[l],
                            lru_wx[l], lru_bx[l], lru_lambda[l])
        o_lru = rmsnorm(o_lru, lru_norm_w[l])
        mix = jnp.concatenate([o_hg, o_lru], axis=-1).astype(h.dtype)
        h = h + jnp.einsum('bse,ed->bsd', mix, w_out[l])
        hn = rmsnorm(h, ln2_w[l])
        up = jnp.einsum('bsd,df->bsf', hn, ffn_w_up[l])
        up = causal_dwconv(up, ffn_conv_w[l], ffn_conv_b[l])
        gate, val = jnp.split(up, [D_FF], axis=-1)
        h = h + jnp.einsum('bsf,fd->bsd', jax.nn.silu(gate) * val, ffn_w_down[l])
    return rmsnorm(h, final_norm_w)


import jax as _jax
import jax.numpy as _jnp

TWIN_FORMAT = 'train_step'
FWD_PARAMS = ['x', 'ln1_w', 'w_in', 'lb_gamma', 'hg_norm_w', 'lru_conv_w', 'lru_conv_b', 'lru_wa', 'lru_ba', 'lru_wx', 'lru_bx', 'lru_lambda', 'lru_norm_w', 'w_out', 'ln2_w', 'ffn_w_up', 'ffn_conv_w', 'ffn_conv_b', 'ffn_w_down', 'final_norm_w']
TWIN_WEIGHTS = ['ln1_w', 'w_in', 'lb_gamma', 'hg_norm_w', 'lru_conv_w', 'lru_conv_b', 'lru_wa', 'lru_ba', 'lru_wx', 'lru_bx', 'lru_lambda', 'lru_norm_w', 'w_out', 'ln2_w', 'ffn_w_up', 'ffn_conv_w', 'ffn_conv_b', 'ffn_w_down', 'final_norm_w']
TWIN_DIFF_INPUT = 'x'
TWIN_INPUTS = ['x', 'ln1_w', 'w_in', 'lb_gamma', 'hg_norm_w', 'lru_conv_w', 'lru_conv_b', 'lru_wa', 'lru_ba', 'lru_wx', 'lru_bx', 'lru_lambda', 'lru_norm_w', 'w_out', 'ln2_w', 'ffn_w_up', 'ffn_conv_w', 'ffn_conv_b', 'ffn_w_down', 'final_norm_w', 'loss_target', 'm_ln1_w', 'm_w_in', 'm_lb_gamma', 'm_hg_norm_w', 'm_lru_conv_w', 'm_lru_conv_b', 'm_lru_wa', 'm_lru_ba', 'm_lru_wx', 'm_lru_bx', 'm_lru_lambda', 'm_lru_norm_w', 'm_w_out', 'm_ln2_w', 'm_ffn_w_up', 'm_ffn_conv_w', 'm_ffn_conv_b', 'm_ffn_w_down', 'm_final_norm_w', 'v_ln1_w', 'v_w_in', 'v_lb_gamma', 'v_hg_norm_w', 'v_lru_conv_w', 'v_lru_conv_b', 'v_lru_wa', 'v_lru_ba', 'v_lru_wx', 'v_lru_bx', 'v_lru_lambda', 'v_lru_norm_w', 'v_w_out', 'v_ln2_w', 'v_ffn_w_up', 'v_ffn_conv_w', 'v_ffn_conv_b', 'v_ffn_w_down', 'v_final_norm_w']
TWIN_OUTPUTS = ['loss', 'grad_x', 'grad_ln1_w', 'grad_w_in', 'grad_lb_gamma', 'grad_hg_norm_w', 'grad_lru_conv_w', 'grad_lru_conv_b', 'grad_lru_wa', 'grad_lru_ba', 'grad_lru_wx', 'grad_lru_bx', 'grad_lru_lambda', 'grad_lru_norm_w', 'grad_w_out', 'grad_ln2_w', 'grad_ffn_w_up', 'grad_ffn_conv_w', 'grad_ffn_conv_b', 'grad_ffn_w_down', 'grad_final_norm_w', 'delta_ln1_w', 'delta_w_in', 'delta_lb_gamma', 'delta_hg_norm_w', 'delta_lru_conv_w', 'delta_lru_conv_b', 'delta_lru_wa', 'delta_lru_ba', 'delta_lru_wx', 'delta_lru_bx', 'delta_lru_lambda', 'delta_lru_norm_w', 'delta_w_out', 'delta_ln2_w', 'delta_ffn_w_up', 'delta_ffn_conv_w', 'delta_ffn_conv_b', 'delta_ffn_w_down', 'delta_final_norm_w', 'new_m_ln1_w', 'new_m_w_in', 'new_m_lb_gamma', 'new_m_hg_norm_w', 'new_m_lru_conv_w', 'new_m_lru_conv_b', 'new_m_lru_wa', 'new_m_lru_ba', 'new_m_lru_wx', 'new_m_lru_bx', 'new_m_lru_lambda', 'new_m_lru_norm_w', 'new_m_w_out', 'new_m_ln2_w', 'new_m_ffn_w_up', 'new_m_ffn_conv_w', 'new_m_ffn_conv_b', 'new_m_ffn_w_down', 'new_m_final_norm_w', 'new_v_ln1_w', 'new_v_w_in', 'new_v_lb_gamma', 'new_v_hg_norm_w', 'new_v_lru_conv_w', 'new_v_lru_conv_b', 'new_v_lru_wa', 'new_v_lru_ba', 'new_v_lru_wx', 'new_v_lru_bx', 'new_v_lru_lambda', 'new_v_lru_norm_w', 'new_v_w_out', 'new_v_ln2_w', 'new_v_ffn_w_up', 'new_v_ffn_conv_w', 'new_v_ffn_conv_b', 'new_v_ffn_w_down', 'new_v_final_norm_w']
TWIN_LEAF_KINDS = {'loss': 'loss', 'grad_x': 'grad_x', 'grad_ln1_w': 'grad_w', 'grad_w_in': 'grad_w', 'grad_lb_gamma': 'grad_w', 'grad_hg_norm_w': 'grad_w', 'grad_lru_conv_w': 'grad_w', 'grad_lru_conv_b': 'grad_w', 'grad_lru_wa': 'grad_w', 'grad_lru_ba': 'grad_w', 'grad_lru_wx': 'grad_w', 'grad_lru_bx': 'grad_w', 'grad_lru_lambda': 'grad_w', 'grad_lru_norm_w': 'grad_w', 'grad_w_out': 'grad_w', 'grad_ln2_w': 'grad_w', 'grad_ffn_w_up': 'grad_w', 'grad_ffn_conv_w': 'grad_w', 'grad_ffn_conv_b': 'grad_w', 'grad_ffn_w_down': 'grad_w', 'grad_final_norm_w': 'grad_w', 'delta_ln1_w': 'delta_w', 'delta_w_in': 'delta_w', 'delta_lb_gamma': 'delta_w', 'delta_hg_norm_w': 'delta_w', 'delta_lru_conv_w': 'delta_w', 'delta_lru_conv_b': 'delta_w', 'delta_lru_wa': 'delta_w', 'delta_lru_ba': 'delta_w', 'delta_lru_wx': 'delta_w', 'delta_lru_bx': 'delta_w', 'delta_lru_lambda': 'delta_w', 'delta_lru_norm_w': 'delta_w', 'delta_w_out': 'delta_w', 'delta_ln2_w': 'delta_w', 'delta_ffn_w_up': 'delta_w', 'delta_ffn_conv_w': 'delta_w', 'delta_ffn_conv_b': 'delta_w', 'delta_ffn_w_down': 'delta_w', 'delta_final_norm_w': 'delta_w', 'new_m_ln1_w': 'new_m', 'new_m_w_in': 'new_m', 'new_m_lb_gamma': 'new_m', 'new_m_hg_norm_w': 'new_m', 'new_m_lru_conv_w': 'new_m', 'new_m_lru_conv_b': 'new_m', 'new_m_lru_wa': 'new_m', 'new_m_lru_ba': 'new_m', 'new_m_lru_wx': 'new_m', 'new_m_lru_bx': 'new_m', 'new_m_lru_lambda': 'new_m', 'new_m_lru_norm_w': 'new_m', 'new_m_w_out': 'new_m', 'new_m_ln2_w': 'new_m', 'new_m_ffn_w_up': 'new_m', 'new_m_ffn_conv_w': 'new_m', 'new_m_ffn_conv_b': 'new_m', 'new_m_ffn_w_down': 'new_m', 'new_m_final_norm_w': 'new_m', 'new_v_ln1_w': 'new_v', 'new_v_w_in': 'new_v', 'new_v_lb_gamma': 'new_v', 'new_v_hg_norm_w': 'new_v', 'new_v_lru_conv_w': 'new_v', 'new_v_lru_conv_b': 'new_v', 'new_v_lru_wa': 'new_v', 'new_v_lru_ba': 'new_v', 'new_v_lru_wx': 'new_v', 'new_v_lru_bx': 'new_v', 'new_v_lru_lambda': 'new_v', 'new_v_lru_norm_w': 'new_v', 'new_v_w_out': 'new_v', 'new_v_ln2_w': 'new_v', 'new_v_ffn_w_up': 'new_v', 'new_v_ffn_conv_w': 'new_v', 'new_v_ffn_conv_b': 'new_v', 'new_v_ffn_w_down': 'new_v', 'new_v_final_norm_w': 'new_v'}


def _forward(args):
    return _fwd_reference(*[args[k] for k in FWD_PARAMS])


def _output_shape():
    def fwd():
        inp = _fwd_setup_inputs(0)
        return _fwd_reference(*[inp[k] for k in FWD_PARAMS])
    out = _jax.eval_shape(fwd)
    return out.shape, out.dtype

N_MICROBATCH = 1
ADAM_LR = 0.001
ADAM_B1 = 0.9
ADAM_B2 = 0.999
ADAM_EPS = 1e-08
ADAM_WD = 0.01
ADAM_STEP = 10
PER_EXAMPLE_BATCH_AXIS = {'x': 0, 'loss_target': 0}
SHARED_INPUTS = []
_WEIGHT_DTYPES = {'ln1_w': _jnp.float32, 'w_in': _jnp.float32, 'lb_gamma': _jnp.float32, 'hg_norm_w': _jnp.float32, 'lru_conv_w': _jnp.float32, 'lru_conv_b': _jnp.float32, 'lru_wa': _jnp.float32, 'lru_ba': _jnp.float32, 'lru_wx': _jnp.float32, 'lru_bx': _jnp.float32, 'lru_lambda': _jnp.float32, 'lru_norm_w': _jnp.float32, 'w_out': _jnp.float32, 'ln2_w': _jnp.float32, 'ffn_w_up': _jnp.float32, 'ffn_conv_w': _jnp.float32, 'ffn_conv_b': _jnp.float32, 'ffn_w_down': _jnp.float32, 'final_norm_w': _jnp.float32}
MOMENT_SCALE = {'ln1_w': 4.458112e-02, 'w_in': 2.584523e-02, 'lb_gamma': 1.967140e-03, 'hg_norm_w': 2.302406e-02, 'lru_conv_w': 4.001864e-02, 'lru_conv_b': 3.988689e-01, 'lru_wa': 1.188168e-02, 'lru_ba': 9.988184e-03, 'lru_wx': 2.096120e-02, 'lru_bx': 1.394757e-02, 'lru_lambda': 1.952022e-02, 'lru_norm_w': 3.736002e-02, 'w_out': 3.094901e-02, 'ln2_w': 2.699441e-02, 'ffn_w_up': 1.170491e-02, 'ffn_conv_w': 1.178139e-02, 'ffn_conv_b': 1.182533e-02, 'ffn_w_down': 1.890859e-02, 'final_norm_w': 8.002690e+00}


def _to_microbatches(a, axis):
    t = _jnp.moveaxis(a, axis, 0)
    t = t.reshape((N_MICROBATCH, t.shape[0] // N_MICROBATCH) + t.shape[1:])
    return _jnp.moveaxis(t, 1, axis + 1)


def setup_inputs(seed: int = 0) -> dict:
    inp = _fwd_setup_inputs(seed)
    key = _jax.random.fold_in(_jax.random.key(seed), 7919)
    shape, _ = _output_shape()
    out = dict(inp)
    out["loss_target"] = _jax.random.normal(_jax.random.fold_in(key, 0), shape, _jnp.float32)
    for i, name in enumerate(TWIN_WEIGHTS):
        w = inp[name].astype(_jnp.float32)
        if MOMENT_SCALE is None:
            s = _jnp.sqrt(_jnp.mean(_jnp.square(w)) + 1e-30)
        else:
            s = MOMENT_SCALE[name]
        km, kv = _jax.random.split(_jax.random.fold_in(key, i + 1))
        out[name] = w
        out["m_" + name] = s * _jax.random.normal(km, w.shape, _jnp.float32)
        out["v_" + name] = (s * s) * _jax.random.uniform(kv, w.shape, _jnp.float32, 0.5, 1.5)
    if N_MICROBATCH > 1:
        for name, axis in PER_EXAMPLE_BATCH_AXIS.items():
            out[name] = _to_microbatches(out[name], axis)
    return {'x': out['x'], 'ln1_w': out['ln1_w'], 'w_in': out['w_in'], 'lb_gamma': out['lb_gamma'], 'hg_norm_w': out['hg_norm_w'], 'lru_conv_w': out['lru_conv_w'], 'lru_conv_b': out['lru_conv_b'], 'lru_wa': out['lru_wa'], 'lru_ba': out['lru_ba'], 'lru_wx': out['lru_wx'], 'lru_bx': out['lru_bx'], 'lru_lambda': out['lru_lambda'], 'lru_norm_w': out['lru_norm_w'], 'w_out': out['w_out'], 'ln2_w': out['ln2_w'], 'ffn_w_up': out['ffn_w_up'], 'ffn_conv_w': out['ffn_conv_w'], 'ffn_conv_b': out['ffn_conv_b'], 'ffn_w_down': out['ffn_w_down'], 'final_norm_w': out['final_norm_w'], 'loss_target': out['loss_target'], 'm_ln1_w': out['m_ln1_w'], 'm_w_in': out['m_w_in'], 'm_lb_gamma': out['m_lb_gamma'], 'm_hg_norm_w': out['m_hg_norm_w'], 'm_lru_conv_w': out['m_lru_conv_w'], 'm_lru_conv_b': out['m_lru_conv_b'], 'm_lru_wa': out['m_lru_wa'], 'm_lru_ba': out['m_lru_ba'], 'm_lru_wx': out['m_lru_wx'], 'm_lru_bx': out['m_lru_bx'], 'm_lru_lambda': out['m_lru_lambda'], 'm_lru_norm_w': out['m_lru_norm_w'], 'm_w_out': out['m_w_out'], 'm_ln2_w': out['m_ln2_w'], 'm_ffn_w_up': out['m_ffn_w_up'], 'm_ffn_conv_w': out['m_ffn_conv_w'], 'm_ffn_conv_b': out['m_ffn_conv_b'], 'm_ffn_w_down': out['m_ffn_w_down'], 'm_final_norm_w': out['m_final_norm_w'], 'v_ln1_w': out['v_ln1_w'], 'v_w_in': out['v_w_in'], 'v_lb_gamma': out['v_lb_gamma'], 'v_hg_norm_w': out['v_hg_norm_w'], 'v_lru_conv_w': out['v_lru_conv_w'], 'v_lru_conv_b': out['v_lru_conv_b'], 'v_lru_wa': out['v_lru_wa'], 'v_lru_ba': out['v_lru_ba'], 'v_lru_wx': out['v_lru_wx'], 'v_lru_bx': out['v_lru_bx'], 'v_lru_lambda': out['v_lru_lambda'], 'v_lru_norm_w': out['v_lru_norm_w'], 'v_w_out': out['v_w_out'], 'v_ln2_w': out['v_ln2_w'], 'v_ffn_w_up': out['v_ffn_w_up'], 'v_ffn_conv_w': out['v_ffn_conv_w'], 'v_ffn_conv_b': out['v_ffn_conv_b'], 'v_ffn_w_down': out['v_ffn_w_down'], 'v_final_norm_w': out['v_final_norm_w']}


def _loss(weights, diff, rest, loss_target):
    with _jax.named_scope("forward"):
        args = {**rest, TWIN_DIFF_INPUT: diff, **{k: w.astype(_WEIGHT_DTYPES[k]) for k, w in weights.items()}}
        y = _forward(args)
    with _jax.named_scope("loss_head"):
        err = _jnp.square(y.astype(_jnp.float32) - loss_target)
        return 0.5 * _jnp.sum(_jnp.mean(err, axis=-1)) if err.ndim else 0.5 * err


def _adamw(w, g, m, v):
    m = ADAM_B1 * m + (1.0 - ADAM_B1) * g
    v = ADAM_B2 * v + (1.0 - ADAM_B2) * _jnp.square(g)
    m_hat = m / (1.0 - ADAM_B1 ** ADAM_STEP)
    v_hat = v / (1.0 - ADAM_B2 ** ADAM_STEP)
    delta = -ADAM_LR * (m_hat / (_jnp.sqrt(v_hat) + ADAM_EPS) + ADAM_WD * w)
    return delta, m, v


def reference(x, ln1_w, w_in, lb_gamma, hg_norm_w, lru_conv_w, lru_conv_b, lru_wa, lru_ba, lru_wx, lru_bx, lru_lambda, lru_norm_w, w_out, ln2_w, ffn_w_up, ffn_conv_w, ffn_conv_b, ffn_w_down, final_norm_w, loss_target, m_ln1_w, m_w_in, m_lb_gamma, m_hg_norm_w, m_lru_conv_w, m_lru_conv_b, m_lru_wa, m_lru_ba, m_lru_wx, m_lru_bx, m_lru_lambda, m_lru_norm_w, m_w_out, m_ln2_w, m_ffn_w_up, m_ffn_conv_w, m_ffn_conv_b, m_ffn_w_down, m_final_norm_w, v_ln1_w, v_w_in, v_lb_gamma, v_hg_norm_w, v_lru_conv_w, v_lru_conv_b, v_lru_wa, v_lru_ba, v_lru_wx, v_lru_bx, v_lru_lambda, v_lru_norm_w, v_w_out, v_ln2_w, v_ffn_w_up, v_ffn_conv_w, v_ffn_conv_b, v_ffn_w_down, v_final_norm_w):
    given = dict(x=x, ln1_w=ln1_w, w_in=w_in, lb_gamma=lb_gamma, hg_norm_w=hg_norm_w, lru_conv_w=lru_conv_w, lru_conv_b=lru_conv_b, lru_wa=lru_wa, lru_ba=lru_ba, lru_wx=lru_wx, lru_bx=lru_bx, lru_lambda=lru_lambda, lru_norm_w=lru_norm_w, w_out=w_out, ln2_w=ln2_w, ffn_w_up=ffn_w_up, ffn_conv_w=ffn_conv_w, ffn_conv_b=ffn_conv_b, ffn_w_down=ffn_w_down, final_norm_w=final_norm_w, loss_target=loss_target, m_ln1_w=m_ln1_w, m_w_in=m_w_in, m_lb_gamma=m_lb_gamma, m_hg_norm_w=m_hg_norm_w, m_lru_conv_w=m_lru_conv_w, m_lru_conv_b=m_lru_conv_b, m_lru_wa=m_lru_wa, m_lru_ba=m_lru_ba, m_lru_wx=m_lru_wx, m_lru_bx=m_lru_bx, m_lru_lambda=m_lru_lambda, m_lru_norm_w=m_lru_norm_w, m_w_out=m_w_out, m_ln2_w=m_ln2_w, m_ffn_w_up=m_ffn_w_up, m_ffn_conv_w=m_ffn_conv_w, m_ffn_conv_b=m_ffn_conv_b, m_ffn_w_down=m_ffn_w_down, m_final_norm_w=m_final_norm_w, v_ln1_w=v_ln1_w, v_w_in=v_w_in, v_lb_gamma=v_lb_gamma, v_hg_norm_w=v_hg_norm_w, v_lru_conv_w=v_lru_conv_w, v_lru_conv_b=v_lru_conv_b, v_lru_wa=v_lru_wa, v_lru_ba=v_lru_ba, v_lru_wx=v_lru_wx, v_lru_bx=v_lru_bx, v_lru_lambda=v_lru_lambda, v_lru_norm_w=v_lru_norm_w, v_w_out=v_w_out, v_ln2_w=v_ln2_w, v_ffn_w_up=v_ffn_w_up, v_ffn_conv_w=v_ffn_conv_w, v_ffn_conv_b=v_ffn_conv_b, v_ffn_w_down=v_ffn_w_down, v_final_norm_w=v_final_norm_w)
    weights = {n: given[n] for n in TWIN_WEIGHTS}
    shared = {n: given[n] for n in SHARED_INPUTS}
    per_example = {n: given[n] for n in ['x']}
    grad_fn = _jax.value_and_grad(_loss, argnums=(0, 1))

    def one_microbatch(ex, loss_target):
        ex = dict(ex)
        diff = ex.pop(TWIN_DIFF_INPUT)
        return grad_fn(weights, diff, {**shared, **ex}, loss_target)

    if N_MICROBATCH == 1:
        loss, (grad_w, grad_x) = one_microbatch(per_example, given["loss_target"])
    else:
        def body(carry, xs):
            loss_sum, grad_sum = carry
            l_k, (gw_k, gx_k) = one_microbatch(xs[0], xs[1])
            with _jax.named_scope("update"):
                return (loss_sum + l_k, _jax.tree.map(_jnp.add, grad_sum, gw_k)), gx_k

        init = (_jnp.zeros((), _jnp.float32), _jax.tree.map(_jnp.zeros_like, weights))
        (loss, grad_w), grad_x = _jax.lax.scan(body, init, (per_example, given["loss_target"]))
    with _jax.named_scope("update"):
        delta_w, new_m, new_v = {}, {}, {}
        for n in TWIN_WEIGHTS:
            delta_w[n], new_m[n], new_v[n] = _adamw(weights[n], grad_w[n], given["m_" + n], given["v_" + n])
    return (loss, grad_x, *[grad_w[n] for n in TWIN_WEIGHTS], *[delta_w[n] for n in TWIN_WEIGHTS],
            *[new_m[n] for n in TWIN_WEIGHTS], *[new_v[n] for n in TWIN_WEIGHTS])
```

```python
import functools
import math

import numpy as np
import jax
import jax.numpy as jnp
from jax import lax
from jax.experimental import pallas as pl
from jax.experimental.pallas import tpu as pltpu

F32 = jnp.float32
MXU_DTYPE = jnp.bfloat16
ACT_DTYPE = jnp.bfloat16

EPS = 1e-6
HEAD_DIM = 128
CHUNK = 64
LEVEL_HALVES = (32, 16, 8, 4, 2, 1)
LRU_CONV = 4
FFN_CONV = 3
LRU_C = 8.0
ADAM_LR, ADAM_B1, ADAM_B2, ADAM_EPS, ADAM_WD, ADAM_STEP = 0.001, 0.9, 0.999, 1e-08, 0.01, 10

V7X_LANES = 128
V7X_VMEM_BUDGET = 56 << 20

NN = (((1,), (0,)), ((), ()))
NT = (((1,), (1,)), ((), ()))
TN = (((0,), (0,)), ((), ()))
MESH = pl.DeviceIdType.MESH
HBM_SPEC = pl.BlockSpec(memory_space=pl.ANY)


def _pcall(kern, **kw):
    return pl.pallas_call(kern, **kw)


def _params(sem=None, vmem=None):
    kw = {}
    if sem is not None:
        kw["dimension_semantics"] = sem
    if vmem is not None:
        kw["vmem_limit_bytes"] = int(min(max(vmem, 16 << 20), V7X_VMEM_BUDGET))
    return pltpu.CompilerParams(**kw)


def _dot(a, b, dims=NN):
    return lax.dot_general(a.astype(MXU_DTYPE), b.astype(MXU_DTYPE), dims, preferred_element_type=F32)


def _tile(dim, pref, align):
    t = min(pref, dim) // align * align
    while t >= align:
        if dim % t == 0:
            return t
        t -= align
    return dim


def _sigmoid(x):
    return 1.0 / (1.0 + jnp.exp(-x))


def _silu_and_grad(x):
    s = _sigmoid(x)
    return x * s, s * (1.0 + x * (1.0 - s))


def _gelu_and_grad(x):
    k0, k1 = math.sqrt(2.0 / math.pi), 0.044715
    t = jnp.tanh(k0 * (x + k1 * x * x * x))
    g = 0.5 * x * (1.0 + t)
    dg = 0.5 * (1.0 + t) + 0.5 * x * (1.0 - t * t) * k0 * (1.0 + 3.0 * k1 * x * x)
    return g, dg


def _one_minus_exp(x):
    p = x * (1.0 + x * (0.5 + x * (1.0 / 6.0 + x * (1.0 / 24.0 + x * (1.0 / 120.0)))))
    return jnp.where(x > -0.05, -p, 1.0 - jnp.exp(x))


def _rows(n):
    return lax.broadcasted_iota(jnp.int32, (n, 1), 0)


def _matmul(a, b, mode, out_dtype, tm, tn, tk, add=None, name="mm"):
    if mode == "TN":
        K, M = a.shape
    else:
        M, K = a.shape
    N = b.shape[0] if mode == "NT" else b.shape[1]
    tm, tn = _tile(M, tm, V7X_LANES), _tile(N, tn, V7X_LANES)
    tk = _tile(K, tk, V7X_LANES)
    nk = K // tk
    dims = {"NN": NN, "NT": NT, "TN": TN}[mode]
    a_spec = (pl.BlockSpec((tk, tm), lambda i, j, k: (k, i)) if mode == "TN"
              else pl.BlockSpec((tm, tk), lambda i, j, k: (i, k)))
    b_spec = (pl.BlockSpec((tn, tk), lambda i, j, k: (j, k)) if mode == "NT"
              else pl.BlockSpec((tk, tn), lambda i, j, k: (k, j)))
    o_spec = pl.BlockSpec((tm, tn), lambda i, j, k: (i, j))
    has_add = add is not None

    def kern(*refs):
        if has_add:
            a_ref, b_ref, add_ref, o_ref, acc_ref = refs
        else:
            a_ref, b_ref, o_ref, acc_ref = refs
        k = pl.program_id(2)

        @pl.when(k == 0)
        def _():
            acc_ref[...] = jnp.zeros_like(acc_ref)

        acc_ref[...] += _dot(a_ref[...], b_ref[...], dims)

        @pl.when(k == nk - 1)
        def _():
            r = acc_ref[...]
            if has_add:
                r = r + add_ref[...]
            o_ref[...] = r.astype(out_dtype)

    ab = jnp.dtype(a.dtype).itemsize
    ob = jnp.dtype(out_dtype).itemsize
    vmem = 2 * (tm * tk + tk * tn) * ab + tm * tn * (4 + 2 * ob + (8 if has_add else 0)) + (4 << 20)
    ins = [a, b] + ([add] if has_add else [])
    in_specs = [a_spec, b_spec] + ([o_spec] if has_add else [])
    return _pcall(
        kern, name=name, grid=(M // tm, N // tn, nk),
        in_specs=in_specs, out_specs=o_spec,
        out_shape=jax.ShapeDtypeStruct((M, N), out_dtype),
        scratch_shapes=[pltpu.VMEM((tm, tn), F32)],
        compiler_params=_params(("parallel", "parallel", "arbitrary"), vmem),
    )(*ins)


def _cast(w, dtype, name):
    R, C = w.shape
    tr = _tile(R, 256, 16)

    def kern(w_ref, o_ref):
        o_ref[...] = w_ref[...].astype(dtype)

    return _pcall(kern, name=name, grid=(R // tr,),
                  in_specs=[pl.BlockSpec((tr, C), lambda i: (i, 0))],
                  out_specs=pl.BlockSpec((tr, C), lambda i: (i, 0)),
                  out_shape=jax.ShapeDtypeStruct((R, C), dtype),
                  compiler_params=_params(("parallel",), 6 * tr * C * 4))(w)


def _rms_fwd(x, w, name):
    T, D = x.shape
    tm = _tile(T, 256, 16)

    def kern(x_ref, w_ref, o_ref):
        xv = x_ref[...]
        r = lax.rsqrt(jnp.mean(xv * xv, axis=-1, keepdims=True) + EPS)
        o_ref[...] = (xv * r * w_ref[...]).astype(ACT_DTYPE)

    return _pcall(kern, name=name, grid=(T // tm,),
                  in_specs=[pl.BlockSpec((tm, D), lambda i: (i, 0)), pl.BlockSpec((1, D), lambda i: (0, 0))],
                  out_specs=pl.BlockSpec((tm, D), lambda i: (i, 0)),
                  out_shape=jax.ShapeDtypeStruct((T, D), ACT_DTYPE),
                  compiler_params=_params(("parallel",), 8 * tm * D * 4))(x, w)


def _rms_bwd(x, w, g, g_col, res, want_act, name):
    T, D = x.shape
    tm = _tile(T, 256, 16)
    has_res = res is not None

    def kern(*refs):
        refs = list(refs)
        x_ref, w_ref, g_ref = refs[:3]
        res_ref = refs[3] if has_res else None
        outs = refs[3 + has_res:]
        dx_ref = outs[0]
        dxa_ref = outs[1] if want_act else None
        dw_ref = outs[-1]
        i = pl.program_id(0)
        xv = x_ref[...]
        gv = g_ref[...].astype(F32)
        r = lax.rsqrt(jnp.mean(xv * xv, axis=-1, keepdims=True) + EPS)
        gw = gv * w_ref[...]
        dx = r * gw - xv * (r * r * r) * jnp.mean(gw * xv, axis=-1, keepdims=True)
        if has_res:
            dx = dx + res_ref[...]
        dx_ref[...] = dx
        if want_act:
            dxa_ref[...] = dx.astype(ACT_DTYPE)

        @pl.when(i == 0)
        def _():
            dw_ref[...] = jnp.zeros_like(dw_ref)

        dw_ref[...] += jnp.sum(gv * xv * r, axis=0, keepdims=True)

    row = pl.BlockSpec((tm, D), lambda i: (i, 0))
    vec = pl.BlockSpec((1, D), lambda i: (0, 0))
    in_specs = [row, vec, pl.BlockSpec((tm, D), lambda i: (i, g_col))] + ([row] if has_res else [])
    out_specs = [row] + ([row] if want_act else []) + [vec]
    out_shape = ([jax.ShapeDtypeStruct((T, D), F32)]
                 + ([jax.ShapeDtypeStruct((T, D), ACT_DTYPE)] if want_act else [])
                 + [jax.ShapeDtypeStruct((1, D), F32)])
    ins = [x, w, g] + ([res] if has_res else [])
    return _pcall(kern, name=name, grid=(T // tm,), in_specs=in_specs, out_specs=out_specs,
                  out_shape=out_shape, compiler_params=_params(("arbitrary",), 14 * tm * D * 4))(*ins)


def _loss_bwd(h, target, w, name):
    T, D = h.shape
    tm = _tile(T, 256, 16)

    def kern(h_ref, t_ref, w_ref, dh_ref, dha_ref, dw_ref, loss_ref):
        i = pl.program_id(0)
        hv = h_ref[...]
        r = lax.rsqrt(jnp.mean(hv * hv, axis=-1, keepdims=True) + EPS)
        e = hv * r * w_ref[...] - t_ref[...]
        dy = e * (1.0 / D)
        gw = dy * w_ref[...]
        dh = r * gw - hv * (r * r * r) * jnp.mean(gw * hv, axis=-1, keepdims=True)
        dh_ref[...] = dh
        dha_ref[...] = dh.astype(ACT_DTYPE)

        @pl.when(i == 0)
        def _():
            dw_ref[...] = jnp.zeros_like(dw_ref)
            loss_ref[...] = jnp.zeros_like(loss_ref)

        dw_ref[...] += jnp.sum(dy * hv * r, axis=0, keepdims=True)
        part = 0.5 * jnp.sum(jnp.mean(e * e, axis=-1, keepdims=True), axis=0, keepdims=True)
        loss_ref[...] += jnp.broadcast_to(part, loss_ref.shape)

    row = pl.BlockSpec((tm, D), lambda i: (i, 0))
    vec = pl.BlockSpec((1, D), lambda i: (0, 0))
    return _pcall(kern, name=name, grid=(T // tm,), in_specs=[row, row, vec],
                  out_specs=[row, row, vec, pl.BlockSpec((8, V7X_LANES), lambda i: (0, 0))],
                  out_shape=[jax.ShapeDtypeStruct((T, D), F32), jax.ShapeDtypeStruct((T, D), ACT_DTYPE),
                             jax.ShapeDtypeStruct((1, D), F32), jax.ShapeDtypeStruct((8, V7X_LANES), F32)],
                  compiler_params=_params(("arbitrary",), 14 * tm * D * 4))(h, target, w)


def _conv(x, w_ref, b, width):
    S = x.shape[0]
    row = _rows(S)
    y = b + x * w_ref[pl.ds(width - 1, 1), :]
    for j in range(width - 1):
        sh = width - 1 - j
        y = y + jnp.where(row >= sh, pltpu.roll(x, sh, 0), 0.0) * w_ref[pl.ds(j, 1), :]
    return y


def _conv_t(dy, w_ref, width):
    S = dy.shape[0]
    row = _rows(S)
    dx = dy * w_ref[pl.ds(width - 1, 1), :]
    for j in range(width - 1):
        sh = width - 1 - j
        dx = dx + jnp.where(row < S - sh, pltpu.roll(dy, S - sh, 0), 0.0) * w_ref[pl.ds(j, 1), :]
    return dx


def _conv_dw_rows(x, dy, width):
    S = x.shape[0]
    row = _rows(S)
    out = []
    for j in range(width):
        sh = width - 1 - j
        xs = x if sh == 0 else jnp.where(row >= sh, pltpu.roll(x, sh, 0), 0.0)
        out.append(jnp.sum(xs * dy, axis=0, keepdims=True))
    return out


def _ffn_act(up, cw, cb, B, name):
    T, F2 = up.shape
    S, F = T // B, F2 // 2
    tw = _tile(F, 256, V7X_LANES)
    nt = F // tw

    def kern(g_ref, v_ref, wg_ref, wv_ref, bg_ref, bv_ref, o_ref):
        gc = _conv(g_ref[...], wg_ref, bg_ref[...], FFN_CONV)
        vc = _conv(v_ref[...], wv_ref, bv_ref[...], FFN_CONV)
        o_ref[...] = (gc * _sigmoid(gc) * vc).astype(ACT_DTYPE)

    blk = lambda off: pl.BlockSpec((S, tw), lambda b, i: (b, off + i))
    wblk = lambda off: pl.BlockSpec((FFN_CONV, tw), lambda b, i: (0, off + i))
    bblk = lambda off: pl.BlockSpec((1, tw), lambda b, i: (0, off + i))
    return _pcall(kern, name=name, grid=(B, nt),
                  in_specs=[blk(0), blk(nt), wblk(0), wblk(nt), bblk(0), bblk(nt)],
                  out_specs=pl.BlockSpec((S, tw), lambda b, i: (b, i)),
                  out_shape=jax.ShapeDtypeStruct((T, F), ACT_DTYPE),
                  compiler_params=_params(("parallel", "parallel"), 16 * S * tw * 4))(up, up, cw, cw, cb, cb)


def _ffn_act_bwd(up, cw, cb, d_act, B, name):
    T, F2 = up.shape
    S, F = T // B, F2 // 2
    tw = _tile(F, 256, V7X_LANES)
    nt = F // tw

    def kern(s_ref, p_ref, ws_ref, wp_ref, bs_ref, bp_ref, da_ref, du_ref, dcw_ref, dcb_ref):
        t, b = pl.program_id(0), pl.program_id(1)
        sc = _conv(s_ref[...], ws_ref, bs_ref[...], FFN_CONV)
        pc = _conv(p_ref[...], wp_ref, bp_ref[...], FFN_CONV)
        da = da_ref[...].astype(F32)
        _, dsilu_self = _silu_and_grad(sc)
        silu_partner = pc * _sigmoid(pc)
        d = da * jnp.where(t < nt, pc * dsilu_self, silu_partner)
        du_ref[...] = _conv_t(d, ws_ref, FFN_CONV).astype(ACT_DTYPE)

        @pl.when(b == 0)
        def _():
            dcw_ref[...] = jnp.zeros_like(dcw_ref)
            dcb_ref[...] = jnp.zeros_like(dcb_ref)

        for j, rj in enumerate(_conv_dw_rows(s_ref[...], d, FFN_CONV)):
            dcw_ref[pl.ds(j, 1), :] += rj
        dcb_ref[...] += jnp.sum(d, axis=0, keepdims=True)

    partner = lambda t: (t + nt) % (2 * nt)
    return _pcall(
        kern, name=name, grid=(2 * nt, B),
        in_specs=[pl.BlockSpec((S, tw), lambda t, b: (b, t)),
                  pl.BlockSpec((S, tw), lambda t, b: (b, partner(t))),
                  pl.BlockSpec((FFN_CONV, tw), lambda t, b: (0, t)),
                  pl.BlockSpec((FFN_CONV, tw), lambda t, b: (0, partner(t))),
                  pl.BlockSpec((1, tw), lambda t, b: (0, t)),
                  pl.BlockSpec((1, tw), lambda t, b: (0, partner(t))),
                  pl.BlockSpec((S, tw), lambda t, b: (b, t % nt))],
        out_specs=[pl.BlockSpec((S, tw), lambda t, b: (b, t)),
                   pl.BlockSpec((FFN_CONV, tw), lambda t, b: (0, t)),
                   pl.BlockSpec((1, tw), lambda t, b: (0, t))],
        out_shape=[jax.ShapeDtypeStruct((T, F2), ACT_DTYPE), jax.ShapeDtypeStruct((FFN_CONV, F2), F32),
                   jax.ShapeDtypeStruct((1, F2), F32)],
        compiler_params=_params(("parallel", "arbitrary"), 24 * S * tw * 4),
    )(up, up, cw, cw, cb, cb, d_act)


def _hgrn_tables():
    C = CHUNK
    t = np.arange(C)
    mats = [(t[:, None] >= t[None, :]).astype(np.float32)]
    masks = []
    gsum = [(t[:, None] <= t[None, :]).astype(np.float32), (t[:, None] > t[None, :]).astype(np.float32)]
    for hs in LEVEL_HALVES:
        m = (t // (2 * hs)) * 2 * hs + hs
        later = t >= m
        d = np.zeros((C, C), np.float32)
        for i in range(C):
            if later[i]:
                d[i, m[i]:i + 1] = 1.0
            else:
                d[i, i + 1:m[i]] = -1.0
        mats.append(d)
        same = (t[:, None] // (2 * hs)) == (t[None, :] // (2 * hs))
        masks.append((same & later[:, None] & (~later)[None, :]).astype(np.float32))
        gsum.append((same & later[:, None] & (t[None, :] >= t[:, None])).astype(np.float32))
        gsum.append((same & (~later)[:, None] & (t[None, :] < t[:, None])).astype(np.float32))
    return np.concatenate(mats, 0), np.stack(masks, 0), np.concatenate(gsum, 1)


def _split_dot(mat, v):
    hi = v.astype(MXU_DTYPE)
    lo = (v - hi.astype(F32)).astype(MXU_DTYPE)
    r = _dot(mat, jnp.concatenate([hi, lo], axis=1))
    n = v.shape[1]
    return r[:, :n] + r[:, n:]


def _hgrn_gates(qr, fr, lb, mc):
    C = CHUNK
    q, dq_dqr = _silu_and_grad(qr)
    sf = _sigmoid(fr)
    f = lb + (1.0 - lb) * sf
    k = 1.0 - f
    dall = _split_dot(mc, jnp.log(f))
    b = dall[0:C]
    dl = [dall[C * (l + 1):C * (l + 2)] for l in range(len(LEVEL_HALVES))]
    eq = [jnp.exp(jnp.minimum(d, 0.0)) for d in dl]
    ek = [jnp.exp(jnp.minimum(-d, 0.0)) for d in dl]
    return q, dq_dqr, sf, f, k, b, eq, ek


def _hgrn_scores(q, k, eq, ek, masks_ref):
    p = jnp.where(_rows(CHUNK) == lax.broadcasted_iota(jnp.int32, (1, CHUNK), 1),
                  jnp.sum(q * k, axis=-1, keepdims=True), 0.0)
    for l in range(len(LEVEL_HALVES)):
        p = p + masks_ref[l] * _dot(q * eq[l], k * ek[l], NT)
    return p


def _hgrn_fwd(proj, lb_gamma, norm_w, B, HW, name):
    T = proj.shape[0]
    S, H, C = T // B, HW // HEAD_DIM, CHUNK
    NC = S // C
    mc_np, masks_np, _ = _hgrn_tables()
    mc, masks = jnp.asarray(mc_np, MXU_DTYPE), jnp.asarray(masks_np, F32)

    def kern(q_ref, f_ref, i_ref, g_ref, lbg_ref, nw_ref, mc_ref, masks_ref, oraw_ref, o_ref, st_ref):
        g0, g1 = lbg_ref[pl.ds(0, 1), :], lbg_ref[pl.ds(1, 1), :]
        mx = jnp.maximum(g0, g1)
        e0, e1 = jnp.exp(g0 - mx), jnp.exp(g1 - mx)
        lb = e0 / (e0 + e1)
        nw = nw_ref[...]
        mcv = mc_ref[...]

        def body(n, st):
            rows = pl.ds(pl.multiple_of(n * C, C), C)
            st_ref[n] = st
            q, _, _, _, k, b, eq, ek = _hgrn_gates(q_ref[rows, :], f_ref[rows, :], lb, mcv)
            v = i_ref[rows, :]
            o = _dot(q * jnp.exp(b), st, NT) + _dot(_hgrn_scores(q, k, eq, ek, masks_ref), v)
            b_last = b[C - 1:C]
            st = st * jnp.exp(b_last) + _dot(v, k * jnp.exp(b_last - b), TN)
            oraw_ref[rows, :] = o
            r = lax.rsqrt(jnp.mean(o * o, axis=-1, keepdims=True) + EPS)
            gate, _ = _silu_and_grad(g_ref[rows, :])
            o_ref[rows, :] = (o * r * nw * gate).astype(ACT_DTYPE)
            return st

        lax.fori_loop(0, NC, body, jnp.zeros((HEAD_DIM, HEAD_DIM), F32))

    col = lambda off: pl.BlockSpec((S, HEAD_DIM), lambda b, h: (b, off + h))
    return _pcall(
        kern, name=name, grid=(B, H),
        in_specs=[col(0), col(H), col(2 * H), col(3 * H),
                  pl.BlockSpec((2, HEAD_DIM), lambda b, h: (0, h)),
                  pl.BlockSpec((1, HEAD_DIM), lambda b, h: (0, h)),
                  pl.BlockSpec(mc.shape, lambda b, h: (0, 0)),
                  pl.BlockSpec(masks.shape, lambda b, h: (0, 0, 0))],
        out_specs=[col(0), col(0),
                   pl.BlockSpec((None, None, NC, HEAD_DIM, HEAD_DIM), lambda b, h: (b, h, 0, 0, 0))],
        out_shape=[jax.ShapeDtypeStruct((T, HW), F32), jax.ShapeDtypeStruct((T, HW), ACT_DTYPE),
                   jax.ShapeDtypeStruct((B, H, NC, HEAD_DIM, HEAD_DIM), F32)],
        compiler_params=_params(("parallel", "parallel"), 16 * S * HEAD_DIM * 4 + (8 << 20)),
    )(proj, proj, proj, proj, lb_gamma, norm_w, mc, masks)


def _hgrn_bwd(proj, lb_gamma, norm_w, o_raw, states, d_mix, B, HW, name):
    T = proj.shape[0]
    S, H, C = T // B, HW // HEAD_DIM, CHUNK
    NC = S // C
    mc_np, masks_np, gsum_np = _hgrn_tables()
    mc, masks, gsum = jnp.asarray(mc_np, MXU_DTYPE), jnp.asarray(masks_np, F32), jnp.asarray(gsum_np, MXU_DTYPE)
    nl = len(LEVEL_HALVES)

    def kern(q_ref, f_ref, i_ref, g_ref, lbg_ref, nw_ref, mc_ref, masks_ref, gsum_ref, oraw_ref, st_ref, do_ref,
             dq_ref, df_ref, di_ref, dg_ref, dlbg_ref, dnw_ref):
        bi = pl.program_id(1)
        g0, g1 = lbg_ref[pl.ds(0, 1), :], lbg_ref[pl.ds(1, 1), :]
        mx = jnp.maximum(g0, g1)
        e0, e1 = jnp.exp(g0 - mx), jnp.exp(g1 - mx)
        lb = e0 / (e0 + e1)
        nw = nw_ref[...]
        mcv, gsumv = mc_ref[...], gsum_ref[...]

        def body(it, carry):
            dst, dlb, dnw = carry
            n = NC - 1 - it
            rows = pl.ds(pl.multiple_of(n * C, C), C)
            qr, fr, v = q_ref[rows, :], f_ref[rows, :], i_ref[rows, :]
            q, dq_dqr, sf, f, k, b, eq, ek = _hgrn_gates(qr, fr, lb, mcv)
            o = oraw_ref[rows, :]
            dout = do_ref[rows, :].astype(F32)
            gate, dgate = _silu_and_grad(g_ref[rows, :])
            r = lax.rsqrt(jnp.mean(o * o, axis=-1, keepdims=True) + EPS)
            dg_ref[rows, :] = (dout * o * r * nw * dgate).astype(ACT_DTYPE)
            don = dout * gate
            dnw = dnw + jnp.sum(don * o * r, axis=0, keepdims=True)
            gw = don * nw
            do = r * gw - o * (r * r * r) * jnp.mean(gw * o, axis=-1, keepdims=True)
            st_prev = st_ref[n]
            eb = jnp.exp(b)
            b_last = b[C - 1:C]
            ebl = jnp.exp(b_last - b)
            p = _hgrn_scores(q, k, eq, ek, masks_ref)
            dp = _dot(do, v, NT)
            dpd = jnp.sum(do * v, axis=-1, keepdims=True)
            dq_state = _dot(do, st_prev) * eb
            dk_state = _dot(v, dst) * ebl
            dq = dq_state + dpd * k
            dk = dk_state + dpd * q
            pairs = [q * dq_state, k * dk_state]
            for l in range(nl):
                mdp = masks_ref[l] * dp
                dql = _dot(mdp, k * ek[l]) * eq[l]
                dkl = _dot(mdp, q * eq[l], TN) * ek[l]
                dq, dk = dq + dql, dk + dkl
                pairs += [q * dql, k * dkl]
            dv = _dot(p, do, TN) + _dot(k * ebl, dst, NT)
            through = jnp.exp(b_last) * jnp.sum(dst * st_prev, axis=0, keepdims=True)
            dlg = _split_dot(gsumv, jnp.concatenate(pairs, axis=0)) + through
            dst = dst * jnp.exp(b_last) + _dot(do, q * eb, TN)
            dq_ref[rows, :] = (dq * dq_dqr).astype(ACT_DTYPE)
            dfv = dlg / f - dk
            df_ref[rows, :] = (dfv * (1.0 - lb) * sf * (1.0 - sf)).astype(ACT_DTYPE)
            di_ref[rows, :] = dv.astype(ACT_DTYPE)
            dlb = dlb + jnp.sum(dfv * (1.0 - sf), axis=0, keepdims=True)
            return dst, dlb, dnw

        zrow = jnp.zeros((1, HEAD_DIM), F32)
        _, dlb, dnw = lax.fori_loop(0, NC, body, (jnp.zeros((HEAD_DIM, HEAD_DIM), F32), zrow, zrow))

        @pl.when(bi == 0)
        def _():
            dlbg_ref[...] = jnp.zeros_like(dlbg_ref)
            dnw_ref[...] = jnp.zeros_like(dnw_ref)

        dg0 = dlb * lb * (1.0 - lb)
        dlbg_ref[pl.ds(0, 1), :] += dg0
        dlbg_ref[pl.ds(1, 1), :] += -dg0
        dnw_ref[...] += dnw

    col = lambda off: pl.BlockSpec((S, HEAD_DIM), lambda h, b: (b, off + h))
    full = lambda a: pl.BlockSpec(a.shape, lambda h, b: (0,) * a.ndim)
    part = jax.ShapeDtypeStruct((T, HW), ACT_DTYPE)
    return _pcall(
        kern, name=name, grid=(H, B),
        in_specs=[col(0), col(H), col(2 * H), col(3 * H),
                  pl.BlockSpec((2, HEAD_DIM), lambda h, b: (0, h)),
                  pl.BlockSpec((1, HEAD_DIM), lambda h, b: (0, h)),
                  full(mc), full(masks), full(gsum), col(0),
                  pl.BlockSpec((None, None, NC, HEAD_DIM, HEAD_DIM), lambda h, b: (b, h, 0, 0, 0)),
                  col(0)],
        out_specs=[col(0), col(0), col(0), col(0),
                   pl.BlockSpec((2, HEAD_DIM), lambda h, b: (0, h)),
                   pl.BlockSpec((1, HEAD_DIM), lambda h, b: (0, h))],
        out_shape=[part, part, part, part, jax.ShapeDtypeStruct((2, HW), F32), jax.ShapeDtypeStruct((1, HW), F32)],
        compiler_params=_params(("parallel", "arbitrary"), 24 * S * HEAD_DIM * 4 + (8 << 20)),
    )(proj, proj, proj, proj, lb_gamma, norm_w, mc, masks, gsum, o_raw, states, d_mix)


def _lru_gates(xr, cw_ref, cb, wa, ba, wx, bx, lam):
    S = xr.shape[0]
    xb = _conv(xr, cw_ref, cb, LRU_CONV)
    r = _sigmoid(_dot(xb, wa) + ba)
    ig = _sigmoid(_dot(xb, wx) + bx)
    sp = jnp.maximum(-lam, 0.0) + jnp.log(1.0 + jnp.exp(-jnp.abs(lam)))
    la = -LRU_C * r * sp
    a = jnp.exp(la)
    mult = jnp.where(_rows(S) == 0, 1.0, jnp.sqrt(_one_minus_exp(2.0 * la)))
    return xb, r, ig, sp, a, mult


def _scan_rows(a_ref, u_ref, h_ref, reverse):
    S, W = a_ref.shape
    nb = S // 8
    row = _rows(8)

    def body(it, carry):
        blk = nb - 1 - it if reverse else it
        rows = pl.ds(pl.multiple_of(blk * 8, 8), 8)
        a, u = a_ref[rows, :], u_ref[rows, :]
        for d in (1, 2, 4):
            sh = 8 - d if reverse else d
            keep = (row < 8 - d) if reverse else (row >= d)
            u = u + jnp.where(keep, a * pltpu.roll(u, sh, 0), 0.0)
            a = jnp.where(keep, a * pltpu.roll(a, sh, 0), a)
        h = u + a * carry
        h_ref[rows, :] = h
        return h[0:1] if reverse else h[7:8]

    lax.fori_loop(0, nb, body, jnp.zeros((1, W), F32))


def _lru_fwd(proj, cw, cb, wa, ba, wx, bx, lam, B, HW, LW, name):
    T = proj.shape[0]
    S, NB = T // B, LW // HEAD_DIM
    xoff, yoff = 4 * HW // HEAD_DIM, 4 * HW // HEAD_DIM + NB

    def kern(x_ref, y_ref, cw_ref, cb_ref, wa_ref, ba_ref, wx_ref, bx_ref, lam_ref, h_ref, z_ref, a_s, u_s):
        xb, _, ig, _, a, mult = _lru_gates(x_ref[...], cw_ref, cb_ref[...], wa_ref[...], ba_ref[...],
                                           wx_ref[...], bx_ref[...], lam_ref[...])
        a_s[...] = a
        u_s[...] = xb * ig * mult
        _scan_rows(a_s, u_s, h_ref, False)
        gy, _ = _gelu_and_grad(y_ref[...])
        z_ref[...] = h_ref[...] * gy

    blk = lambda off: pl.BlockSpec((S, HEAD_DIM), lambda b, n: (b, off + n))
    vec = pl.BlockSpec((1, HEAD_DIM), lambda b, n: (0, n))
    mat = pl.BlockSpec((None, HEAD_DIM, HEAD_DIM), lambda b, n: (n, 0, 0))
    return _pcall(
        kern, name=name, grid=(B, NB),
        in_specs=[blk(xoff), blk(yoff), pl.BlockSpec((LRU_CONV, HEAD_DIM), lambda b, n: (0, n)),
                  vec, mat, vec, mat, vec, vec],
        out_specs=[blk(0), blk(0)],
        out_shape=[jax.ShapeDtypeStruct((T, LW), F32), jax.ShapeDtypeStruct((T, LW), F32)],
        scratch_shapes=[pltpu.VMEM((S, HEAD_DIM), F32), pltpu.VMEM((S, HEAD_DIM), F32)],
        compiler_params=_params(("parallel", "parallel"), 24 * S * HEAD_DIM * 4),
    )(proj, proj, cw, cb, wa, ba, wx, bx, lam)


def _lru_bwd(proj, cw, cb, wa, ba, wx, bx, lam, h, dz, B, HW, LW, name):
    T = proj.shape[0]
    S, NB = T // B, LW // HEAD_DIM
    xoff, yoff = 4 * HW // HEAD_DIM, 4 * HW // HEAD_DIM + NB

    def kern(x_ref, y_ref, cw_ref, cb_ref, wa_ref, ba_ref, wx_ref, bx_ref, lam_ref, h_ref, dz_ref,
             dx_ref, dy_ref, dwa_ref, dwx_ref, dba_ref, dbx_ref, dlam_ref, dcw_ref, dcb_ref, a_s, u_s, dh_s):
        bi = pl.program_id(1)
        row = _rows(S)
        xr, lam = x_ref[...], lam_ref[...]
        wa, wx = wa_ref[...], wx_ref[...]
        xb, r, ig, sp, a, mult = _lru_gates(xr, cw_ref, cb_ref[...], wa, ba_ref[...], wx, bx_ref[...], lam)
        hv, dz = h_ref[...], dz_ref[...]
        gy, dgy = _gelu_and_grad(y_ref[...])
        dy_ref[...] = (dz * hv * dgy).astype(ACT_DTYPE)
        a_s[...] = jnp.where(row < S - 1, pltpu.roll(a, S - 1, 0), 0.0)
        u_s[...] = dz * gy
        _scan_rows(a_s, u_s, dh_s, True)
        dh = dh_s[...]
        h_prev = jnp.where(row >= 1, pltpu.roll(hv, 1, 0), 0.0)
        d_ig = dh * xb * mult
        d_mult = jnp.where(row == 0, 0.0, dh * xb * ig)
        dxb = dh * ig * mult
        dla = dh * h_prev * a - d_mult * (a * a) / mult
        dpre_r = dla * (-LRU_C * sp) * r * (1.0 - r)
        dpre_i = d_ig * ig * (1.0 - ig)
        dxb = dxb + _dot(dpre_r, wa, NT) + _dot(dpre_i, wx, NT)
        dx_ref[...] = _conv_t(dxb, cw_ref, LRU_CONV).astype(ACT_DTYPE)

        @pl.when(bi == 0)
        def _():
            for ref in (dwa_ref, dwx_ref, dba_ref, dbx_ref, dlam_ref, dcw_ref, dcb_ref):
                ref[...] = jnp.zeros_like(ref)

        dwa_ref[...] += _dot(xb, dpre_r, TN)
        dwx_ref[...] += _dot(xb, dpre_i, TN)
        dba_ref[...] += jnp.sum(dpre_r, axis=0, keepdims=True)
        dbx_ref[...] += jnp.sum(dpre_i, axis=0, keepdims=True)
        dsp = jnp.sum(dla * (-LRU_C) * r, axis=0, keepdims=True)
        dlam_ref[...] += -dsp * _sigmoid(-lam)
        for j, rj in enumerate(_conv_dw_rows(xr, dxb, LRU_CONV)):
            dcw_ref[pl.ds(j, 1), :] += rj
        dcb_ref[...] += jnp.sum(dxb, axis=0, keepdims=True)

    blk = lambda off: pl.BlockSpec((S, HEAD_DIM), lambda n, b: (b, off + n))
    vec = pl.BlockSpec((1, HEAD_DIM), lambda n, b: (0, n))
    mat = pl.BlockSpec((None, HEAD_DIM, HEAD_DIM), lambda n, b: (n, 0, 0))
    cwb = pl.BlockSpec((LRU_CONV, HEAD_DIM), lambda n, b: (0, n))
    part = jax.ShapeDtypeStruct((T, LW), ACT_DTYPE)
    vshape = jax.ShapeDtypeStruct((1, LW), F32)
    mshape = jax.ShapeDtypeStruct((NB, HEAD_DIM, HEAD_DIM), F32)
    return _pcall(
        kern, name=name, grid=(NB, B),
        in_specs=[blk(xoff), blk(yoff), cwb, vec, mat, vec, mat, vec, vec, blk(0), blk(0)],
        out_specs=[blk(0), blk(0), mat, mat, vec, vec, vec, cwb, vec],
        out_shape=[part, part, mshape, mshape, vshape, vshape, vshape,
                   jax.ShapeDtypeStruct((LRU_CONV, LW), F32), vshape],
        scratch_shapes=[pltpu.VMEM((S, HEAD_DIM), F32)] * 3,
        compiler_params=_params(("parallel", "arbitrary"), 40 * S * HEAD_DIM * 4),
    )(proj, proj, cw, cb, wa, ba, wx, bx, lam, h, dz)


def _pos():
    return lax.axis_index("x"), lax.axis_index("y"), lax.axis_index("c")


def _other_chips(x, y):
    return [(1 - x, y), (x, 1 - y), (1 - x, 1 - y)]


def _remote(src, dst, send_sems, recv_sems, k, to):
    return pltpu.make_async_remote_copy(src_ref=src, dst_ref=dst, send_sem=send_sems.at[k],
                                        recv_sem=recv_sems.at[k], device_id=to, device_id_type=MESH)


def _allgather_big(shard, col_sharded, name):
    R, C = shard.shape
    Rh = R // 2
    out_shape = (R, 4 * C) if col_sharded else (4 * R, C)

    def body(s_ref, o_ref, send_sems, recv_sems, local_sem):
        x, y, c = _pos()
        me, sib, jme = (x, y, c), (x, y, 1 - c), 2 * x + y

        def win(j, h=None):
            if col_sharded:
                rows = pl.ds(0, R) if h is None else pl.ds(h * Rh, Rh)
                return o_ref.at[rows, pl.ds(j * C, C)]
            return o_ref.at[pl.ds(j * R, R) if h is None else pl.ds(j * R + h * Rh, Rh), :]

        mine = pltpu.make_async_copy(s_ref, win(jme), local_sem)
        mine.start()
        chips = _other_chips(x, y)
        sends = [_remote(s_ref.at[pl.ds(c * Rh, Rh), :], win(jme, c), send_sems, recv_sems, k, (cx, cy, c))
                 for k, (cx, cy) in enumerate(chips)]
        for cp in sends:
            cp.start()
        passed = []
        for k, (cx, cy) in enumerate(chips):
            w = win(2 * cx + cy, c)
            _remote(w, w, send_sems, recv_sems, k, me).wait_recv()
            cp = _remote(w, w, send_sems, recv_sems, 3 + k, sib)
            cp.start()
            passed.append(cp)
        for k, (cx, cy) in enumerate(chips):
            w = win(2 * cx + cy, 1 - c)
            _remote(w, w, send_sems, recv_sems, 3 + k, me).wait_recv()
        for cp in sends + passed:
            cp.wait_send()
        mine.wait()

    return _pcall(body, name=name, in_specs=[HBM_SPEC], out_specs=HBM_SPEC,
                  out_shape=jax.ShapeDtypeStruct(out_shape, shard.dtype),
                  scratch_shapes=[pltpu.SemaphoreType.DMA((6,)), pltpu.SemaphoreType.DMA((6,)),
                                  pltpu.SemaphoreType.DMA(())])(shard)


def _pair_exchange(g4, name):
    J, _, Rh, W = g4.shape

    def body(g_ref, p_ref, send_sems, recv_sems):
        x, y, c = _pos()
        cp = _remote(g_ref.at[pl.ds(0, J), 1 - c], p_ref, send_sems, recv_sems, 0, (x, y, 1 - c))
        cp.start()
        cp.wait()

    return _pcall(body, name=name, in_specs=[HBM_SPEC], out_specs=HBM_SPEC,
                  out_shape=jax.ShapeDtypeStruct((J, Rh, W), g4.dtype),
                  scratch_shapes=[pltpu.SemaphoreType.DMA((1,)), pltpu.SemaphoreType.DMA((1,))])(g4)


def _pair_add(g4, p, name):
    J, _, Rh, W = g4.shape
    tr = _tile(Rh, 256, 16)
    tw = _tile(W, 2048, V7X_LANES)
    c = lax.axis_index("c").astype(jnp.int32).reshape(1)

    def kern(c_ref, g_ref, p_ref, o_ref):
        o_ref[...] = (g_ref[...].astype(F32) + p_ref[...].astype(F32)).astype(ACT_DTYPE)

    grid_spec = pltpu.PrefetchScalarGridSpec(
        num_scalar_prefetch=1, grid=(J, Rh // tr, W // tw),
        in_specs=[pl.BlockSpec((None, None, tr, tw), lambda j, i, w, cr: (j, cr[0], i, w)),
                  pl.BlockSpec((None, tr, tw), lambda j, i, w, cr: (j, i, w))],
        out_specs=pl.BlockSpec((None, tr, tw), lambda j, i, w, cr: (j, i, w)))
    return _pcall(kern, name=name, grid_spec=grid_spec,
                  out_shape=jax.ShapeDtypeStruct((J, Rh, W), ACT_DTYPE),
                  compiler_params=_params(("parallel", "parallel", "parallel"), 12 * tr * tw * 4))(c, g4, p)


def _scatter_chip_sums(cs, col_sharded, name):
    J, Rh, W = cs.shape
    C = W // 4 if col_sharded else W

    def body(cs_ref, o_ref, send_sems, recv_sems, local_sem):
        x, y, c = _pos()
        me, sib, ime = (x, y, c), (x, y, 1 - c), 2 * x + y

        def piece(j):
            return cs_ref.at[0, :, pl.ds(j * C, C)] if col_sharded else cs_ref.at[j]

        mine = pltpu.make_async_copy(piece(ime), o_ref.at[ime, c], local_sem)
        mine.start()
        chips = _other_chips(x, y)
        sends = [_remote(piece(ime), o_ref.at[ime, c], send_sems, recv_sems, 0, sib)]
        sends += [_remote(piece(2 * cx + cy), o_ref.at[ime, c], send_sems, recv_sems, 1 + k, (cx, cy, c))
                  for k, (cx, cy) in enumerate(chips)]
        for cp in sends:
            cp.start()
        passed = []
        for k, (cx, cy) in enumerate(chips):
            w = o_ref.at[2 * cx + cy, c]
            _remote(w, w, send_sems, recv_sems, 1 + k, me).wait_recv()
            cp = _remote(w, w, send_sems, recv_sems, 4 + k, sib)
            cp.start()
            passed.append(cp)
        w = o_ref.at[ime, 1 - c]
        _remote(w, w, send_sems, recv_sems, 0, me).wait_recv()
        for k, (cx, cy) in enumerate(chips):
            w = o_ref.at[2 * cx + cy, 1 - c]
            _remote(w, w, send_sems, recv_sems, 4 + k, me).wait_recv()
        for cp in sends + passed:
            cp.wait_send()
        mine.wait()

    return _pcall(body, name=name, in_specs=[HBM_SPEC], out_specs=HBM_SPEC,
                  out_shape=jax.ShapeDtypeStruct((4, 2, Rh, C), cs.dtype),
                  scratch_shapes=[pltpu.SemaphoreType.DMA((7,)), pltpu.SemaphoreType.DMA((7,)),
                                  pltpu.SemaphoreType.DMA(())])(cs)


def _gather_small(buf, name):
    rows = buf.shape[0]

    def body(b_ref, o_ref, send_sems, recv_sems):
        x, y, c = _pos()
        jme = 2 * x + y
        chips = _other_chips(x, y)
        o_ref[jme] = b_ref[...]
        sends = [_remote(b_ref, o_ref.at[jme], send_sems, recv_sems, k, (cx, cy, c))
                 for k, (cx, cy) in enumerate(chips)]
        for cp in sends:
            cp.start()
        for k, (cx, cy) in enumerate(chips):
            w = o_ref.at[2 * cx + cy]
            _remote(w, w, send_sems, recv_sems, k, (x, y, c)).wait_recv()
        for cp in sends:
            cp.wait_send()

    vm = pl.BlockSpec(memory_space=pltpu.VMEM)
    return _pcall(body, name=name, in_specs=[vm], out_specs=vm,
                  out_shape=jax.ShapeDtypeStruct((4, rows, V7X_LANES), buf.dtype),
                  scratch_shapes=[pltpu.SemaphoreType.DMA((3,)), pltpu.SemaphoreType.DMA((3,))],
                  compiler_params=_params(None, 16 * rows * V7X_LANES * 4))(buf)


def _allreduce_small(buf, name):
    rows = buf.shape[0]

    def body(b_ref, o_ref, slots, send_sems, recv_sems):
        x, y, c = _pos()
        me = 4 * x + 2 * y + c
        slots[me] = b_ref[...]
        sends = []
        for k in range(1, 8):
            kx, ky, kc = (k >> 2) & 1, (k >> 1) & 1, k & 1
            to = (x ^ kx, y ^ ky, c ^ kc)
            sends.append(_remote(b_ref, slots.at[me], send_sems, recv_sems, k - 1, to))
        for cp in sends:
            cp.start()
        for k in range(1, 8):
            kx, ky, kc = (k >> 2) & 1, (k >> 1) & 1, k & 1
            w = slots.at[4 * (x ^ kx) + 2 * (y ^ ky) + (c ^ kc)]
            _remote(w, w, send_sems, recv_sems, k - 1, (x, y, c)).wait_recv()
        for cp in sends:
            cp.wait_send()
        acc = slots[0]
        for d in range(1, 8):
            acc = acc + slots[d]
        o_ref[...] = acc

    vm = pl.BlockSpec(memory_space=pltpu.VMEM)
    return _pcall(body, name=name, in_specs=[vm], out_specs=vm,
                  out_shape=jax.ShapeDtypeStruct(buf.shape, buf.dtype),
                  scratch_shapes=[pltpu.VMEM((8, rows, V7X_LANES), buf.dtype),
                                  pltpu.SemaphoreType.DMA((7,)), pltpu.SemaphoreType.DMA((7,))],
                  compiler_params=_params(None, 14 * rows * V7X_LANES * 4))(buf)


def _adamw_math(w, g, m, v):
    m = ADAM_B1 * m + (1.0 - ADAM_B1) * g
    v = ADAM_B2 * v + (1.0 - ADAM_B2) * (g * g)
    m_hat = m / (1.0 - ADAM_B1 ** ADAM_STEP)
    v_hat = v / (1.0 - ADAM_B2 ** ADAM_STEP)
    delta = -ADAM_LR * (m_hat / (jnp.sqrt(v_hat) + ADAM_EPS) + ADAM_WD * w)
    return delta, m, v


def _adamw_big(w, m, v, slots, name):
    R, C = w.shape
    tr = _tile(R, 32, 16)

    def kern(w_ref, m_ref, v_ref, s_ref, g_ref, d_ref, mo_ref, vo_ref):
        g = s_ref[0].astype(F32)
        for i in range(1, 4):
            g = g + s_ref[i].astype(F32)
        d, mn, vn = _adamw_math(w_ref[...], g, m_ref[...], v_ref[...])
        g_ref[...], d_ref[...], mo_ref[...], vo_ref[...] = g, d, mn, vn

    row = pl.BlockSpec((tr, C), lambda i: (i, 0))
    shp = jax.ShapeDtypeStruct((R, C), F32)
    return _pcall(kern, name=name, grid=(R // tr,),
                  in_specs=[row, row, row, pl.BlockSpec((4, tr, C), lambda i: (0, i, 0))],
                  out_specs=[row] * 4, out_shape=[shp] * 4,
                  compiler_params=_params(("parallel",), 36 * tr * C * 4))(w, m, v, slots)


def _adamw_small(w, g, m, v, name):
    def kern(w_ref, g_ref, m_ref, v_ref, d_ref, mo_ref, vo_ref):
        d_ref[...], mo_ref[...], vo_ref[...] = _adamw_math(w_ref[...], g_ref[...], m_ref[...], v_ref[...])

    vm = pl.BlockSpec(memory_space=pltpu.VMEM)
    shp = jax.ShapeDtypeStruct(w.shape, F32)
    return _pcall(kern, name=name, in_specs=[vm] * 4, out_specs=[vm] * 3, out_shape=[shp] * 3,
                  compiler_params=_params(None, 10 * w.size * 4))(w, g, m, v)


def _pack(arrs):
    flat = jnp.concatenate([a.reshape(-1).astype(F32) for a in arrs])
    n = flat.shape[0]
    rows = -(-n // (8 * V7X_LANES)) * 8
    return jnp.pad(flat, (0, rows * V7X_LANES - n)).reshape(rows, V7X_LANES)


def _unpack(buf, shapes):
    flat = buf.reshape(-1)
    out, off = [], 0
    for s in shapes:
        n = int(np.prod(s))
        out.append(flat[off:off + n].reshape(s))
        off += n
    return out


def _reduce_and_update(gfull, col_sharded, w, m, v, tag):
    R, C = w.shape
    if col_sharded:
        g4 = gfull.reshape(1, 2, R // 2, 4 * C)
    else:
        g4 = gfull.reshape(4, 2, R // 2, C)
    p = _pair_exchange(g4, "pair_exchange_" + tag)
    cs = _pair_add(g4, p, "pair_add_" + tag)
    slots = _scatter_chip_sums(cs, col_sharded, "scatter_sums_" + tag)
    return _adamw_big(w, m, v, slots.reshape(4, R, C), "adamw_" + tag)


def kernel(x, ln1_w, w_in, lb_gamma, hg_norm_w, lru_conv_w, lru_conv_b, lru_wa, lru_ba, lru_wx, lru_bx, lru_lambda, lru_norm_w, w_out, ln2_w, ffn_w_up, ffn_conv_w, ffn_conv_b, ffn_w_down, final_norm_w, loss_target, m_ln1_w, m_w_in, m_lb_gamma, m_hg_norm_w, m_lru_conv_w, m_lru_conv_b, m_lru_wa, m_lru_ba, m_lru_wx, m_lru_bx, m_lru_lambda, m_lru_norm_w, m_w_out, m_ln2_w, m_ffn_w_up, m_ffn_conv_w, m_ffn_conv_b, m_ffn_w_down, m_final_norm_w, v_ln1_w, v_w_in, v_lb_gamma, v_hg_norm_w, v_lru_conv_w, v_lru_conv_b, v_lru_wa, v_lru_ba, v_lru_wx, v_lru_bx, v_lru_lambda, v_lru_norm_w, v_w_out, v_ln2_w, v_ffn_w_up, v_ffn_conv_w, v_ffn_conv_b, v_ffn_w_down, v_final_norm_w):
    B, S, D = x.shape
    T = B * S
    HW = lb_gamma.shape[1]
    LW = lru_conv_b.shape[1]
    assert S % CHUNK == 0 and HW % HEAD_DIM == 0 and lru_wa.shape[2] == HEAD_DIM
    x2 = x.reshape(T, D)
    tgt = loss_target.reshape(T, D)
    jchip = 2 * lax.axis_index("x") + lax.axis_index("y")

    W_in = _allgather_big(_cast(w_in[0], ACT_DTYPE, "cast_w_in"), True, "gather_w_in")
    W_out = _allgather_big(_cast(w_out[0], ACT_DTYPE, "cast_w_out"), False, "gather_w_out")
    W_up = _allgather_big(_cast(ffn_w_up[0], ACT_DTYPE, "cast_w_up"), True, "gather_w_up")
    W_down = _allgather_big(_cast(ffn_w_down[0], ACT_DTYPE, "cast_w_down"), False, "gather_w_down")
    conv_shapes = [lru_conv_w[0].shape, ffn_conv_w[0].shape]
    convs = _gather_small(_pack([lru_conv_w[0], ffn_conv_w[0]]), "gather_conv_w")
    per_chip = [_unpack(convs[j], conv_shapes) for j in range(4)]
    lcw = jnp.concatenate([pc[0] for pc in per_chip], axis=1)
    fcw = jnp.concatenate([pc[1] for pc in per_chip], axis=1)

    hn1 = _rms_fwd(x2, ln1_w, "rms1")
    proj = _matmul(hn1, W_in, "NN", F32, 1024, 1024, 1024, name="mm_proj")
    o_raw, o_hg, states = _hgrn_fwd(proj, lb_gamma, hg_norm_w, B, HW, "hgrn_fwd")
    h_lru, z = _lru_fwd(proj, lcw, lru_conv_b, lru_wa[0], lru_ba, lru_wx[0], lru_bx, lru_lambda, B, HW, LW, "lru_fwd")
    o_lru = _rms_fwd(z, lru_norm_w, "rms_lru")
    mix = jnp.concatenate([o_hg, o_lru], axis=1)
    h1 = _matmul(mix, W_out, "NN", F32, 1024, 1024, 1024, add=x2, name="mm_out")
    hn2 = _rms_fwd(h1, ln2_w, "rms2")
    up = _matmul(hn2, W_up, "NN", F32, 2048, 512, 1024, name="mm_up")
    act = _ffn_act(up, fcw, ffn_conv_b, B, "ffn_act")
    h2 = _matmul(act, W_down, "NN", F32, 2048, 1024, 256, add=h1, name="mm_down")

    dh2, dh2a, d_final_w, loss_part = _loss_bwd(h2, tgt, final_norm_w.reshape(1, D), "loss_bwd")
    d_act = _matmul(dh2a, W_down, "NT", ACT_DTYPE, 512, 5504, 512, name="mm_d_act")
    g_down = _matmul(act, dh2a, "TN", ACT_DTYPE, 256, 4096, 1024, name="mm_g_down")
    d_up, d_fcw, d_fcb = _ffn_act_bwd(up, fcw, ffn_conv_b, d_act, B, "ffn_act_bwd")
    d_hn2 = _matmul(d_up, W_up, "NT", F32, 2048, 1024, 512, name="mm_d_hn2")
    g_up = _matmul(hn2, d_up, "TN", ACT_DTYPE, 2048, 512, 1024, name="mm_g_up")
    dh1, dh1a, d_ln2 = _rms_bwd(h1, ln2_w, d_hn2, 0, dh2, True, "rms2_bwd")
    d_mix = _matmul(dh1a, W_out, "NT", F32, 1024, 1024, 1024, name="mm_d_mix")
    g_out = _matmul(mix, dh1a, "TN", ACT_DTYPE, 1024, 1024, 1024, name="mm_g_out")
    dz, d_lru_norm = _rms_bwd(z, lru_norm_w, d_mix, HW // LW, None, False, "rms_lru_bwd")
    (d_xr, d_yr, d_wa, d_wx, d_ba, d_bx, d_lam, d_lcw, d_lcb) = _lru_bwd(
        proj, lcw, lru_conv_b, lru_wa[0], lru_ba, lru_wx[0], lru_bx, lru_lambda, h_lru, dz, B, HW, LW, "lru_bwd")
    d_q, d_f, d_i, d_g, d_lbg, d_hgw = _hgrn_bwd(proj, lb_gamma, hg_norm_w, o_raw, states, d_mix, B, HW, "hgrn_bwd")
    d_proj = jnp.concatenate([d_q, d_f, d_i, d_g, d_xr, d_yr], axis=1)
    d_hn1 = _matmul(d_proj, W_in, "NT", F32, 1024, 1024, 1024, name="mm_d_hn1")
    g_in = _matmul(hn1, d_proj, "TN", ACT_DTYPE, 1024, 1024, 1024, name="mm_g_in")
    dx, d_ln1 = _rms_bwd(x2, ln1_w, d_hn1, 0, dh1, False, "rms1_bwd")

    big = {
        "w_in": _reduce_and_update(g_in, True, w_in[0], m_w_in[0], v_w_in[0], "w_in"),
        "w_out": _reduce_and_update(g_out, False, w_out[0], m_w_out[0], v_w_out[0], "w_out"),
        "ffn_w_up": _reduce_and_update(g_up, True, ffn_w_up[0], m_ffn_w_up[0], v_ffn_w_up[0], "w_up"),
        "ffn_w_down": _reduce_and_update(g_down, False, ffn_w_down[0], m_ffn_w_down[0], v_ffn_w_down[0], "w_down"),
    }

    small_names = ["ln1_w", "lb_gamma", "hg_norm_w", "lru_conv_w", "lru_conv_b", "lru_wa", "lru_ba", "lru_wx",
                   "lru_bx", "lru_lambda", "lru_norm_w", "ln2_w", "ffn_conv_w", "ffn_conv_b", "final_norm_w"]
    small_grads = [d_ln1, d_lbg, d_hgw, d_lcw, d_lcb, d_wa, d_ba, d_wx, d_bx, d_lam, d_lru_norm, d_ln2,
                   d_fcw, d_fcb, d_final_w]
    red = _allreduce_small(_pack([loss_part[0:1, 0:1]] + small_grads), "allreduce_small")
    red = _unpack(red, [(1, 1)] + [g.shape for g in small_grads])
    loss = red[0].reshape(())
    gs = dict(zip(small_names, red[1:]))
    nlc, nfc = lru_conv_w.shape[2], ffn_conv_w.shape[2]
    gs["lru_conv_w"] = lax.dynamic_slice_in_dim(gs["lru_conv_w"], jchip * nlc, nlc, axis=1)
    gs["ffn_conv_w"] = lax.dynamic_slice_in_dim(gs["ffn_conv_w"], jchip * nfc, nfc, axis=1)
    args = dict(ln1_w=(ln1_w, m_ln1_w, v_ln1_w), lb_gamma=(lb_gamma, m_lb_gamma, v_lb_gamma),
                hg_norm_w=(hg_norm_w, m_hg_norm_w, v_hg_norm_w), lru_conv_w=(lru_conv_w, m_lru_conv_w, v_lru_conv_w),
                lru_conv_b=(lru_conv_b, m_lru_conv_b, v_lru_conv_b), lru_wa=(lru_wa, m_lru_wa, v_lru_wa),
                lru_ba=(lru_ba, m_lru_ba, v_lru_ba), lru_wx=(lru_wx, m_lru_wx, v_lru_wx),
                lru_bx=(lru_bx, m_lru_bx, v_lru_bx), lru_lambda=(lru_lambda, m_lru_lambda, v_lru_lambda),
                lru_norm_w=(lru_norm_w, m_lru_norm_w, v_lru_norm_w), ln2_w=(ln2_w, m_ln2_w, v_ln2_w),
                ffn_conv_w=(ffn_conv_w, m_ffn_conv_w, v_ffn_conv_w), ffn_conv_b=(ffn_conv_b, m_ffn_conv_b, v_ffn_conv_b),
                final_norm_w=(final_norm_w, m_final_norm_w, v_final_norm_w))
    shapes = [args[n][0].shape for n in small_names]
    upd = _adamw_small(_pack([args[n][0] for n in small_names]), _pack([gs[n] for n in small_names]),
                       _pack([args[n][1] for n in small_names]), _pack([args[n][2] for n in small_names]), "adamw_small")
    s_delta, s_m, s_v = (dict(zip(small_names, _unpack(u, shapes))) for u in upd)

    order = ["ln1_w", "w_in", "lb_gamma", "hg_norm_w", "lru_conv_w", "lru_conv_b", "lru_wa", "lru_ba", "lru_wx",
             "lru_bx", "lru_lambda", "lru_norm_w", "w_out", "ln2_w", "ffn_w_up", "ffn_conv_w", "ffn_conv_b",
             "ffn_w_down", "final_norm_w"]
    full_shape = dict(w_in=w_in.shape, w_out=w_out.shape, ffn_w_up=ffn_w_up.shape, ffn_w_down=ffn_w_down.shape)
    grads, deltas, new_m, new_v = [], [], [], []
    for n in order:
        if n in big:
            g, d, mn, vn = (t.reshape(full_shape[n]) for t in big[n])
        else:
            g, d, mn, vn = gs[n].reshape(args[n][0].shape), s_delta[n], s_m[n], s_v[n]
        grads.append(g), deltas.append(d), new_m.append(mn), new_v.append(vn)
    return (loss, dx.reshape(B, S, D), *grads, *deltas, *new_m, *new_v)
```

```python
import functools
import math

import numpy as np
import jax
import jax.numpy as jnp
from jax import lax
from jax.experimental import pallas as pl
from jax.experimental.pallas import tpu as pltpu

F32 = jnp.float32
MXU_DTYPE = jnp.bfloat16
ACT_DTYPE = jnp.bfloat16

EPS = 1e-6
HEAD_DIM = 128
CHUNK = 64
LEVEL_HALVES = (32, 16, 8, 4, 2, 1)
LRU_CONV = 4
FFN_CONV = 3
LRU_C = 8.0
ADAM_LR, ADAM_B1, ADAM_B2, ADAM_EPS, ADAM_WD, ADAM_STEP = 0.001, 0.9, 0.999, 1e-08, 0.01, 10

V7X_LANES = 128
V7X_VMEM_BUDGET = 56 << 20

NN = (((1,), (0,)), ((), ()))
NT = (((1,), (1,)), ((), ()))
TN = (((0,), (0,)), ((), ()))
MESH = pl.DeviceIdType.MESH
HBM_SPEC = pl.BlockSpec(memory_space=pl.ANY)


def _pcall(kern, **kw):
    return pl.pallas_call(kern, **kw)


def _params(sem=None, vmem=None):
    kw = {}
    if sem is not None:
        kw["dimension_semantics"] = sem
    if vmem is not None:
        kw["vmem_limit_bytes"] = int(min(max(vmem, 16 << 20), V7X_VMEM_BUDGET))
    return pltpu.CompilerParams(**kw)


def _dot(a, b, dims=NN):
    return lax.dot_general(a.astype(MXU_DTYPE), b.astype(MXU_DTYPE), dims, preferred_element_type=F32)


def _tile(dim, pref, align):
    t = min(pref, dim) // align * align
    while t >= align:
        if dim % t == 0:
            return t
        t -= align
    return dim


def _sigmoid(x):
    return 1.0 / (1.0 + jnp.exp(-x))


def _silu_and_grad(x):
    s = _sigmoid(x)
    return x * s, s * (1.0 + x * (1.0 - s))


def _gelu_and_grad(x):
    k0, k1 = math.sqrt(2.0 / math.pi), 0.044715
    t = jnp.tanh(k0 * (x + k1 * x * x * x))
    g = 0.5 * x * (1.0 + t)
    dg = 0.5 * (1.0 + t) + 0.5 * x * (1.0 - t * t) * k0 * (1.0 + 3.0 * k1 * x * x)
    return g, dg


def _one_minus_exp(x):
    p = x * (1.0 + x * (0.5 + x * (1.0 / 6.0 + x * (1.0 / 24.0 + x * (1.0 / 120.0)))))
    return jnp.where(x > -0.05, -p, 1.0 - jnp.exp(x))


def _rows(n):
    return lax.broadcasted_iota(jnp.int32, (n, 1), 0)


def _matmul(a, b, mode, out_dtype, tm, tn, tk, add=None, after=None, name="mm"):
    if mode == "TN":
        K, M = a.shape
    else:
        M, K = a.shape
    N = b.shape[0] if mode == "NT" else b.shape[1]
    tm, tn = _tile(M, tm, V7X_LANES), _tile(N, tn, V7X_LANES)
    tk = _tile(K, tk, V7X_LANES)
    nk = K // tk
    dims = {"NN": NN, "NT": NT, "TN": TN}[mode]
    a_spec = (pl.BlockSpec((tk, tm), lambda i, j, k: (k, i)) if mode == "TN"
              else pl.BlockSpec((tm, tk), lambda i, j, k: (i, k)))
    b_spec = (pl.BlockSpec((tn, tk), lambda i, j, k: (j, k)) if mode == "NT"
              else pl.BlockSpec((tk, tn), lambda i, j, k: (k, j)))
    o_spec = pl.BlockSpec((tm, tn), lambda i, j, k: (i, j))
    has_add = add is not None

    def kern(*refs):
        a_ref, b_ref = refs[:2]
        add_ref = refs[2] if has_add else None
        o_ref, acc_ref = refs[-2:]
        k = pl.program_id(2)

        @pl.when(k == 0)
        def _():
            acc_ref[...] = jnp.zeros_like(acc_ref)

        acc_ref[...] += _dot(a_ref[...], b_ref[...], dims)

        @pl.when(k == nk - 1)
        def _():
            r = acc_ref[...]
            if has_add:
                r = r + add_ref[...]
            o_ref[...] = r.astype(out_dtype)

    ab = jnp.dtype(a.dtype).itemsize
    ob = jnp.dtype(out_dtype).itemsize
    vmem = 2 * (tm * tk + tk * tn) * ab + tm * tn * (4 + 2 * ob + (8 if has_add else 0)) + (4 << 20)
    ins = [a, b] + ([add] if has_add else []) + ([after] if after is not None else [])
    in_specs = [a_spec, b_spec] + ([o_spec] if has_add else []) + ([HBM_SPEC] if after is not None else [])
    return _pcall(
        kern, name=name, grid=(M // tm, N // tn, nk),
        in_specs=in_specs, out_specs=o_spec,
        out_shape=jax.ShapeDtypeStruct((M, N), out_dtype),
        scratch_shapes=[pltpu.VMEM((tm, tn), F32)],
        compiler_params=_params(("parallel", "parallel", "arbitrary"), vmem),
    )(*ins)


def _cast(w, dtype, name):
    R, C = w.shape
    tr = _tile(R, 256, 16)

    def kern(w_ref, o_ref):
        o_ref[...] = w_ref[...].astype(dtype)

    return _pcall(kern, name=name, grid=(R // tr,),
                  in_specs=[pl.BlockSpec((tr, C), lambda i: (i, 0))],
                  out_specs=pl.BlockSpec((tr, C), lambda i: (i, 0)),
                  out_shape=jax.ShapeDtypeStruct((R, C), dtype),
                  compiler_params=_params(("parallel",), 6 * tr * C * 4))(w)


def _rms_fwd(x, w, name):
    T, D = x.shape
    tm = _tile(T, 256, 16)

    def kern(x_ref, w_ref, o_ref):
        xv = x_ref[...]
        r = lax.rsqrt(jnp.mean(xv * xv, axis=-1, keepdims=True) + EPS)
        o_ref[...] = (xv * r * w_ref[...]).astype(ACT_DTYPE)

    return _pcall(kern, name=name, grid=(T // tm,),
                  in_specs=[pl.BlockSpec((tm, D), lambda i: (i, 0)), pl.BlockSpec((1, D), lambda i: (0, 0))],
                  out_specs=pl.BlockSpec((tm, D), lambda i: (i, 0)),
                  out_shape=jax.ShapeDtypeStruct((T, D), ACT_DTYPE),
                  compiler_params=_params(("parallel",), 8 * tm * D * 4))(x, w)


def _rms_bwd(x, w, g, g_col, res, want_act, name, after=None):
    T, D = x.shape
    tm = _tile(T, 256, 16)
    has_res = res is not None

    def kern(*refs):
        refs = list(refs)
        x_ref, w_ref, g_ref = refs[:3]
        res_ref = refs[3] if has_res else None
        outs = refs[3 + has_res + (after is not None):]
        dx_ref = outs[0]
        dxa_ref = outs[1] if want_act else None
        dw_ref = outs[-1]
        i = pl.program_id(0)
        xv = x_ref[...]
        gv = g_ref[...].astype(F32)
        r = lax.rsqrt(jnp.mean(xv * xv, axis=-1, keepdims=True) + EPS)
        gw = gv * w_ref[...]
        dx = r * gw - xv * (r * r * r) * jnp.mean(gw * xv, axis=-1, keepdims=True)
        if has_res:
            dx = dx + res_ref[...]
        dx_ref[...] = dx
        if want_act:
            dxa_ref[...] = dx.astype(ACT_DTYPE)

        @pl.when(i == 0)
        def _():
            dw_ref[...] = jnp.zeros_like(dw_ref)

        dw_ref[...] += jnp.sum(gv * xv * r, axis=0, keepdims=True)

    row = pl.BlockSpec((tm, D), lambda i: (i, 0))
    vec = pl.BlockSpec((1, D), lambda i: (0, 0))
    in_specs = ([row, vec, pl.BlockSpec((tm, D), lambda i: (i, g_col))] + ([row] if has_res else [])
                + ([HBM_SPEC] if after is not None else []))
    out_specs = [row] + ([row] if want_act else []) + [vec]
    out_shape = ([jax.ShapeDtypeStruct((T, D), F32)]
                 + ([jax.ShapeDtypeStruct((T, D), ACT_DTYPE)] if want_act else [])
                 + [jax.ShapeDtypeStruct((1, D), F32)])
    ins = [x, w, g] + ([res] if has_res else []) + ([after] if after is not None else [])
    return _pcall(kern, name=name, grid=(T // tm,), in_specs=in_specs, out_specs=out_specs,
                  out_shape=out_shape, compiler_params=_params(("arbitrary",), 14 * tm * D * 4))(*ins)


def _loss_bwd(h, target, w, name):
    T, D = h.shape
    tm = _tile(T, 256, 16)

    def kern(h_ref, t_ref, w_ref, dh_ref, dha_ref, dw_ref, loss_ref):
        i = pl.program_id(0)
        hv = h_ref[...]
        r = lax.rsqrt(jnp.mean(hv * hv, axis=-1, keepdims=True) + EPS)
        e = hv * r * w_ref[...] - t_ref[...]
        dy = e * (1.0 / D)
        gw = dy * w_ref[...]
        dh = r * gw - hv * (r * r * r) * jnp.mean(gw * hv, axis=-1, keepdims=True)
        dh_ref[...] = dh
        dha_ref[...] = dh.astype(ACT_DTYPE)

        @pl.when(i == 0)
        def _():
            dw_ref[...] = jnp.zeros_like(dw_ref)
            loss_ref[...] = jnp.zeros_like(loss_ref)

        dw_ref[...] += jnp.sum(dy * hv * r, axis=0, keepdims=True)
        part = 0.5 * jnp.sum(jnp.mean(e * e, axis=-1, keepdims=True), axis=0, keepdims=True)
        loss_ref[...] += jnp.broadcast_to(part, loss_ref.shape)

    row = pl.BlockSpec((tm, D), lambda i: (i, 0))
    vec = pl.BlockSpec((1, D), lambda i: (0, 0))
    return _pcall(kern, name=name, grid=(T // tm,), in_specs=[row, row, vec],
                  out_specs=[row, row, vec, pl.BlockSpec((8, V7X_LANES), lambda i: (0, 0))],
                  out_shape=[jax.ShapeDtypeStruct((T, D), F32), jax.ShapeDtypeStruct((T, D), ACT_DTYPE),
                             jax.ShapeDtypeStruct((1, D), F32), jax.ShapeDtypeStruct((8, V7X_LANES), F32)],
                  compiler_params=_params(("arbitrary",), 14 * tm * D * 4))(h, target, w)


def _conv(x, w_ref, b, width):
    S = x.shape[0]
    row = _rows(S)
    y = b + x * w_ref[pl.ds(width - 1, 1), :]
    for j in range(width - 1):
        sh = width - 1 - j
        y = y + jnp.where(row >= sh, pltpu.roll(x, sh, 0), 0.0) * w_ref[pl.ds(j, 1), :]
    return y


def _conv_t(dy, w_ref, width):
    S = dy.shape[0]
    row = _rows(S)
    dx = dy * w_ref[pl.ds(width - 1, 1), :]
    for j in range(width - 1):
        sh = width - 1 - j
        dx = dx + jnp.where(row < S - sh, pltpu.roll(dy, S - sh, 0), 0.0) * w_ref[pl.ds(j, 1), :]
    return dx


def _conv_dw_rows(x, dy, width):
    S = x.shape[0]
    row = _rows(S)
    out = []
    for j in range(width):
        sh = width - 1 - j
        xs = x if sh == 0 else jnp.where(row >= sh, pltpu.roll(x, sh, 0), 0.0)
        out.append(jnp.sum(xs * dy, axis=0, keepdims=True))
    return out


def _ffn_act(up, cw, cb, B, name):
    T, F2 = up.shape
    S, F = T // B, F2 // 2
    tw = _tile(F, 256, V7X_LANES)
    nt = F // tw

    def kern(g_ref, v_ref, wg_ref, wv_ref, bg_ref, bv_ref, o_ref):
        gc = _conv(g_ref[...], wg_ref, bg_ref[...], FFN_CONV)
        vc = _conv(v_ref[...], wv_ref, bv_ref[...], FFN_CONV)
        o_ref[...] = (gc * _sigmoid(gc) * vc).astype(ACT_DTYPE)

    blk = lambda off: pl.BlockSpec((S, tw), lambda b, i: (b, off + i))
    wblk = lambda off: pl.BlockSpec((FFN_CONV, tw), lambda b, i: (0, off + i))
    bblk = lambda off: pl.BlockSpec((1, tw), lambda b, i: (0, off + i))
    return _pcall(kern, name=name, grid=(B, nt),
                  in_specs=[blk(0), blk(nt), wblk(0), wblk(nt), bblk(0), bblk(nt)],
                  out_specs=pl.BlockSpec((S, tw), lambda b, i: (b, i)),
                  out_shape=jax.ShapeDtypeStruct((T, F), ACT_DTYPE),
                  compiler_params=_params(("parallel", "parallel"), 16 * S * tw * 4))(up, up, cw, cw, cb, cb)


def _ffn_act_bwd(up, cw, cb, d_act, B, name):
    T, F2 = up.shape
    S, F = T // B, F2 // 2
    tw = _tile(F, 256, V7X_LANES)
    nt = F // tw

    def kern(s_ref, p_ref, ws_ref, wp_ref, bs_ref, bp_ref, da_ref, du_ref, dcw_ref, dcb_ref):
        t, b = pl.program_id(0), pl.program_id(1)
        sc = _conv(s_ref[...], ws_ref, bs_ref[...], FFN_CONV)
        pc = _conv(p_ref[...], wp_ref, bp_ref[...], FFN_CONV)
        da = da_ref[...].astype(F32)
        _, dsilu_self = _silu_and_grad(sc)
        silu_partner = pc * _sigmoid(pc)
        d = da * jnp.where(t < nt, pc * dsilu_self, silu_partner)
        du_ref[...] = _conv_t(d, ws_ref, FFN_CONV).astype(ACT_DTYPE)

        @pl.when(b == 0)
        def _():
            dcw_ref[...] = jnp.zeros_like(dcw_ref)
            dcb_ref[...] = jnp.zeros_like(dcb_ref)

        for j, rj in enumerate(_conv_dw_rows(s_ref[...], d, FFN_CONV)):
            dcw_ref[pl.ds(j, 1), :] += rj
        dcb_ref[...] += jnp.sum(d, axis=0, keepdims=True)

    partner = lambda t: (t + nt) % (2 * nt)
    return _pcall(
        kern, name=name, grid=(2 * nt, B),
        in_specs=[pl.BlockSpec((S, tw), lambda t, b: (b, t)),
                  pl.BlockSpec((S, tw), lambda t, b: (b, partner(t))),
                  pl.BlockSpec((FFN_CONV, tw), lambda t, b: (0, t)),
                  pl.BlockSpec((FFN_CONV, tw), lambda t, b: (0, partner(t))),
                  pl.BlockSpec((1, tw), lambda t, b: (0, t)),
                  pl.BlockSpec((1, tw), lambda t, b: (0, partner(t))),
                  pl.BlockSpec((S, tw), lambda t, b: (b, t % nt))],
        out_specs=[pl.BlockSpec((S, tw), lambda t, b: (b, t)),
                   pl.BlockSpec((FFN_CONV, tw), lambda t, b: (0, t)),
                   pl.BlockSpec((1, tw), lambda t, b: (0, t))],
        out_shape=[jax.ShapeDtypeStruct((T, F2), ACT_DTYPE), jax.ShapeDtypeStruct((FFN_CONV, F2), F32),
                   jax.ShapeDtypeStruct((1, F2), F32)],
        compiler_params=_params(("parallel", "arbitrary"), 24 * S * tw * 4),
    )(up, up, cw, cw, cb, cb, d_act)


def _hgrn_tables():
    C = CHUNK
    t = np.arange(C)
    mats = [(t[:, None] >= t[None, :]).astype(np.float32)]
    masks = []
    gsum = [(t[:, None] <= t[None, :]).astype(np.float32), (t[:, None] > t[None, :]).astype(np.float32)]
    for hs in LEVEL_HALVES:
        m = (t // (2 * hs)) * 2 * hs + hs
        later = t >= m
        d = np.zeros((C, C), np.float32)
        for i in range(C):
            if later[i]:
                d[i, m[i]:i + 1] = 1.0
            else:
                d[i, i + 1:m[i]] = -1.0
        mats.append(d)
        same = (t[:, None] // (2 * hs)) == (t[None, :] // (2 * hs))
        masks.append((same & later[:, None] & (~later)[None, :]).astype(np.float32))
        gsum.append((same & later[:, None] & (t[None, :] >= t[:, None])).astype(np.float32))
        gsum.append((same & (~later)[:, None] & (t[None, :] < t[:, None])).astype(np.float32))
    return np.concatenate(mats, 0), np.stack(masks, 0), np.concatenate(gsum, 1)


def _split_dot(mat, v):
    hi = v.astype(MXU_DTYPE)
    lo = (v - hi.astype(F32)).astype(MXU_DTYPE)
    r = _dot(mat, jnp.concatenate([hi, lo], axis=1))
    n = v.shape[1]
    return r[:, :n] + r[:, n:]


def _hgrn_gates(qr, fr, lb, mc):
    C = CHUNK
    q, dq_dqr = _silu_and_grad(qr)
    sf = _sigmoid(fr)
    f = lb + (1.0 - lb) * sf
    k = 1.0 - f
    dall = _split_dot(mc, jnp.log(f))
    b = dall[0:C]
    dl = [dall[C * (l + 1):C * (l + 2)] for l in range(len(LEVEL_HALVES))]
    eq = [jnp.exp(jnp.minimum(d, 0.0)) for d in dl]
    ek = [jnp.exp(jnp.minimum(-d, 0.0)) for d in dl]
    return q, dq_dqr, sf, f, k, b, eq, ek


def _hgrn_scores(q, k, eq, ek, masks_ref):
    p = jnp.where(_rows(CHUNK) == lax.broadcasted_iota(jnp.int32, (1, CHUNK), 1),
                  jnp.sum(q * k, axis=-1, keepdims=True), 0.0)
    for l in range(len(LEVEL_HALVES)):
        p = p + masks_ref[l] * _dot(q * eq[l], k * ek[l], NT)
    return p


def _hgrn_fwd(proj, lb_gamma, norm_w, B, HW, name):
    T = proj.shape[0]
    S, H, C = T // B, HW // HEAD_DIM, CHUNK
    NC = S // C
    mc_np, masks_np, _ = _hgrn_tables()
    mc, masks = jnp.asarray(mc_np, MXU_DTYPE), jnp.asarray(masks_np, F32)

    def kern(q_ref, f_ref, i_ref, g_ref, lbg_ref, nw_ref, mc_ref, masks_ref, oraw_ref, o_ref, st_ref):
        g0, g1 = lbg_ref[pl.ds(0, 1), :], lbg_ref[pl.ds(1, 1), :]
        mx = jnp.maximum(g0, g1)
        e0, e1 = jnp.exp(g0 - mx), jnp.exp(g1 - mx)
        lb = e0 / (e0 + e1)
        nw = nw_ref[...]
        mcv = mc_ref[...]

        def body(n, st):
            rows = pl.ds(pl.multiple_of(n * C, C), C)
            st_ref[n] = st
            q, _, _, _, k, b, eq, ek = _hgrn_gates(q_ref[rows, :], f_ref[rows, :], lb, mcv)
            v = i_ref[rows, :]
            o = _dot(q * jnp.exp(b), st, NT) + _dot(_hgrn_scores(q, k, eq, ek, masks_ref), v)
            b_last = b[C - 1:C]
            st = st * jnp.exp(b_last) + _dot(v, k * jnp.exp(b_last - b), TN)
            oraw_ref[rows, :] = o
            r = lax.rsqrt(jnp.mean(o * o, axis=-1, keepdims=True) + EPS)
            gate, _ = _silu_and_grad(g_ref[rows, :])
            o_ref[rows, :] = (o * r * nw * gate).astype(ACT_DTYPE)
            return st

        lax.fori_loop(0, NC, body, jnp.zeros((HEAD_DIM, HEAD_DIM), F32))

    col = lambda off: pl.BlockSpec((S, HEAD_DIM), lambda b, h: (b, off + h))
    return _pcall(
        kern, name=name, grid=(B, H),
        in_specs=[col(0), col(H), col(2 * H), col(3 * H),
                  pl.BlockSpec((2, HEAD_DIM), lambda b, h: (0, h)),
                  pl.BlockSpec((1, HEAD_DIM), lambda b, h: (0, h)),
                  pl.BlockSpec(mc.shape, lambda b, h: (0, 0)),
                  pl.BlockSpec(masks.shape, lambda b, h: (0, 0, 0))],
        out_specs=[col(0), col(0),
                   pl.BlockSpec((None, None, NC, HEAD_DIM, HEAD_DIM), lambda b, h: (b, h, 0, 0, 0))],
        out_shape=[jax.ShapeDtypeStruct((T, HW), F32), jax.ShapeDtypeStruct((T, HW), ACT_DTYPE),
                   jax.ShapeDtypeStruct((B, H, NC, HEAD_DIM, HEAD_DIM), F32)],
        compiler_params=_params(("parallel", "parallel"), 16 * S * HEAD_DIM * 4 + (8 << 20)),
    )(proj, proj, proj, proj, lb_gamma, norm_w, mc, masks)


def _hgrn_bwd(proj, lb_gamma, norm_w, o_raw, states, d_mix, B, HW, name):
    T = proj.shape[0]
    S, H, C = T // B, HW // HEAD_DIM, CHUNK
    NC = S // C
    mc_np, masks_np, gsum_np = _hgrn_tables()
    mc, masks, gsum = jnp.asarray(mc_np, MXU_DTYPE), jnp.asarray(masks_np, F32), jnp.asarray(gsum_np, MXU_DTYPE)
    nl = len(LEVEL_HALVES)

    def kern(q_ref, f_ref, i_ref, g_ref, lbg_ref, nw_ref, mc_ref, masks_ref, gsum_ref, oraw_ref, st_ref, do_ref,
             dq_ref, df_ref, di_ref, dg_ref, dlbg_ref, dnw_ref):
        bi = pl.program_id(1)
        g0, g1 = lbg_ref[pl.ds(0, 1), :], lbg_ref[pl.ds(1, 1), :]
        mx = jnp.maximum(g0, g1)
        e0, e1 = jnp.exp(g0 - mx), jnp.exp(g1 - mx)
        lb = e0 / (e0 + e1)
        nw = nw_ref[...]
        mcv, gsumv = mc_ref[...], gsum_ref[...]

        def body(it, carry):
            dst, dlb, dnw = carry
            n = NC - 1 - it
            rows = pl.ds(pl.multiple_of(n * C, C), C)
            qr, fr, v = q_ref[rows, :], f_ref[rows, :], i_ref[rows, :]
            q, dq_dqr, sf, f, k, b, eq, ek = _hgrn_gates(qr, fr, lb, mcv)
            o = oraw_ref[rows, :]
            dout = do_ref[rows, :].astype(F32)
            gate, dgate = _silu_and_grad(g_ref[rows, :])
            r = lax.rsqrt(jnp.mean(o * o, axis=-1, keepdims=True) + EPS)
            dg_ref[rows, :] = (dout * o * r * nw * dgate).astype(ACT_DTYPE)
            don = dout * gate
            dnw = dnw + jnp.sum(don * o * r, axis=0, keepdims=True)
            gw = don * nw
            do = r * gw - o * (r * r * r) * jnp.mean(gw * o, axis=-1, keepdims=True)
            st_prev = st_ref[n]
            eb = jnp.exp(b)
            b_last = b[C - 1:C]
            ebl = jnp.exp(b_last - b)
            p = _hgrn_scores(q, k, eq, ek, masks_ref)
            dp = _dot(do, v, NT)
            dpd = jnp.sum(do * v, axis=-1, keepdims=True)
            dq_state = _dot(do, st_prev) * eb
            dk_state = _dot(v, dst) * ebl
            dq = dq_state + dpd * k
            dk = dk_state + dpd * q
            pairs = [q * dq_state, k * dk_state]
            for l in range(nl):
                mdp = masks_ref[l] * dp
                dql = _dot(mdp, k * ek[l]) * eq[l]
                dkl = _dot(mdp, q * eq[l], TN) * ek[l]
                dq, dk = dq + dql, dk + dkl
                pairs += [q * dql, k * dkl]
            dv = _dot(p, do, TN) + _dot(k * ebl, dst, NT)
            through = jnp.exp(b_last) * jnp.sum(dst * st_prev, axis=0, keepdims=True)
            dlg = _split_dot(gsumv, jnp.concatenate(pairs, axis=0)) + through
            dst = dst * jnp.exp(b_last) + _dot(do, q * eb, TN)
            dq_ref[rows, :] = (dq * dq_dqr).astype(ACT_DTYPE)
            dfv = dlg / f - dk
            df_ref[rows, :] = (dfv * (1.0 - lb) * sf * (1.0 - sf)).astype(ACT_DTYPE)
            di_ref[rows, :] = dv.astype(ACT_DTYPE)
            dlb = dlb + jnp.sum(dfv * (1.0 - sf), axis=0, keepdims=True)
            return dst, dlb, dnw

        zrow = jnp.zeros((1, HEAD_DIM), F32)
        _, dlb, dnw = lax.fori_loop(0, NC, body, (jnp.zeros((HEAD_DIM, HEAD_DIM), F32), zrow, zrow))

        @pl.when(bi == 0)
        def _():
            dlbg_ref[...] = jnp.zeros_like(dlbg_ref)
            dnw_ref[...] = jnp.zeros_like(dnw_ref)

        dg0 = dlb * lb * (1.0 - lb)
        dlbg_ref[pl.ds(0, 1), :] += dg0
        dlbg_ref[pl.ds(1, 1), :] += -dg0
        dnw_ref[...] += dnw

    col = lambda off: pl.BlockSpec((S, HEAD_DIM), lambda h, b: (b, off + h))
    full = lambda a: pl.BlockSpec(a.shape, lambda h, b: (0,) * a.ndim)
    part = jax.ShapeDtypeStruct((T, HW), ACT_DTYPE)
    return _pcall(
        kern, name=name, grid=(H, B),
        in_specs=[col(0), col(H), col(2 * H), col(3 * H),
                  pl.BlockSpec((2, HEAD_DIM), lambda h, b: (0, h)),
                  pl.BlockSpec((1, HEAD_DIM), lambda h, b: (0, h)),
                  full(mc), full(masks), full(gsum), col(0),
                  pl.BlockSpec((None, None, NC, HEAD_DIM, HEAD_DIM), lambda h, b: (b, h, 0, 0, 0)),
                  col(0)],
        out_specs=[col(0), col(0), col(0), col(0),
                   pl.BlockSpec((2, HEAD_DIM), lambda h, b: (0, h)),
                   pl.BlockSpec((1, HEAD_DIM), lambda h, b: (0, h))],
        out_shape=[part, part, part, part, jax.ShapeDtypeStruct((2, HW), F32), jax.ShapeDtypeStruct((1, HW), F32)],
        compiler_params=_params(("parallel", "arbitrary"), 24 * S * HEAD_DIM * 4 + (8 << 20)),
    )(proj, proj, proj, proj, lb_gamma, norm_w, mc, masks, gsum, o_raw, states, d_mix)


def _lru_gates(xr, cw_ref, cb, wa, ba, wx, bx, lam):
    S = xr.shape[0]
    xb = _conv(xr, cw_ref, cb, LRU_CONV)
    r = _sigmoid(_dot(xb, wa) + ba)
    ig = _sigmoid(_dot(xb, wx) + bx)
    sp = jnp.maximum(-lam, 0.0) + jnp.log(1.0 + jnp.exp(-jnp.abs(lam)))
    la = -LRU_C * r * sp
    a = jnp.exp(la)
    mult = jnp.where(_rows(S) == 0, 1.0, jnp.sqrt(_one_minus_exp(2.0 * la)))
    return xb, r, ig, sp, a, mult


def _scan_rows(a_ref, u_ref, h_ref, reverse):
    S, W = a_ref.shape
    nb = S // 8
    row = _rows(8)

    def body(it, carry):
        blk = nb - 1 - it if reverse else it
        rows = pl.ds(pl.multiple_of(blk * 8, 8), 8)
        a, u = a_ref[rows, :], u_ref[rows, :]
        for d in (1, 2, 4):
            sh = 8 - d if reverse else d
            keep = (row < 8 - d) if reverse else (row >= d)
            u = u + jnp.where(keep, a * pltpu.roll(u, sh, 0), 0.0)
            a = jnp.where(keep, a * pltpu.roll(a, sh, 0), a)
        h = u + a * carry
        h_ref[rows, :] = h
        return h[0:1] if reverse else h[7:8]

    lax.fori_loop(0, nb, body, jnp.zeros((1, W), F32))


def _lru_fwd(proj, cw, cb, wa, ba, wx, bx, lam, B, HW, LW, name):
    T = proj.shape[0]
    S, NB = T // B, LW // HEAD_DIM
    xoff, yoff = 4 * HW // HEAD_DIM, 4 * HW // HEAD_DIM + NB

    def kern(x_ref, y_ref, cw_ref, cb_ref, wa_ref, ba_ref, wx_ref, bx_ref, lam_ref, h_ref, z_ref, a_s, u_s):
        xb, _, ig, _, a, mult = _lru_gates(x_ref[...], cw_ref, cb_ref[...], wa_ref[...], ba_ref[...],
                                           wx_ref[...], bx_ref[...], lam_ref[...])
        a_s[...] = a
        u_s[...] = xb * ig * mult
        _scan_rows(a_s, u_s, h_ref, False)
        gy, _ = _gelu_and_grad(y_ref[...])
        z_ref[...] = h_ref[...] * gy

    blk = lambda off: pl.BlockSpec((S, HEAD_DIM), lambda b, n: (b, off + n))
    vec = pl.BlockSpec((1, HEAD_DIM), lambda b, n: (0, n))
    mat = pl.BlockSpec((None, HEAD_DIM, HEAD_DIM), lambda b, n: (n, 0, 0))
    return _pcall(
        kern, name=name, grid=(B, NB),
        in_specs=[blk(xoff), blk(yoff), pl.BlockSpec((LRU_CONV, HEAD_DIM), lambda b, n: (0, n)),
                  vec, mat, vec, mat, vec, vec],
        out_specs=[blk(0), blk(0)],
        out_shape=[jax.ShapeDtypeStruct((T, LW), F32), jax.ShapeDtypeStruct((T, LW), F32)],
        scratch_shapes=[pltpu.VMEM((S, HEAD_DIM), F32), pltpu.VMEM((S, HEAD_DIM), F32)],
        compiler_params=_params(("parallel", "parallel"), 24 * S * HEAD_DIM * 4),
    )(proj, proj, cw, cb, wa, ba, wx, bx, lam)


def _lru_bwd(proj, cw, cb, wa, ba, wx, bx, lam, h, dz, B, HW, LW, name):
    T = proj.shape[0]
    S, NB = T // B, LW // HEAD_DIM
    xoff, yoff = 4 * HW // HEAD_DIM, 4 * HW // HEAD_DIM + NB

    def kern(x_ref, y_ref, cw_ref, cb_ref, wa_ref, ba_ref, wx_ref, bx_ref, lam_ref, h_ref, dz_ref,
             dx_ref, dy_ref, dwa_ref, dwx_ref, dba_ref, dbx_ref, dlam_ref, dcw_ref, dcb_ref, a_s, u_s, dh_s):
        bi = pl.program_id(1)
        row = _rows(S)
        xr, lam = x_ref[...], lam_ref[...]
        wa, wx = wa_ref[...], wx_ref[...]
        xb, r, ig, sp, a, mult = _lru_gates(xr, cw_ref, cb_ref[...], wa, ba_ref[...], wx, bx_ref[...], lam)
        hv, dz = h_ref[...], dz_ref[...]
        gy, dgy = _gelu_and_grad(y_ref[...])
        dy_ref[...] = (dz * hv * dgy).astype(ACT_DTYPE)
        a_s[...] = jnp.where(row < S - 1, pltpu.roll(a, S - 1, 0), 0.0)
        u_s[...] = dz * gy
        _scan_rows(a_s, u_s, dh_s, True)
        dh = dh_s[...]
        h_prev = jnp.where(row >= 1, pltpu.roll(hv, 1, 0), 0.0)
        d_ig = dh * xb * mult
        d_mult = jnp.where(row == 0, 0.0, dh * xb * ig)
        dxb = dh * ig * mult
        dla = dh * h_prev * a - d_mult * (a * a) / mult
        dpre_r = dla * (-LRU_C * sp) * r * (1.0 - r)
        dpre_i = d_ig * ig * (1.0 - ig)
        dxb = dxb + _dot(dpre_r, wa, NT) + _dot(dpre_i, wx, NT)
        dx_ref[...] = _conv_t(dxb, cw_ref, LRU_CONV).astype(ACT_DTYPE)

        @pl.when(bi == 0)
        def _():
            for ref in (dwa_ref, dwx_ref, dba_ref, dbx_ref, dlam_ref, dcw_ref, dcb_ref):
                ref[...] = jnp.zeros_like(ref)

        dwa_ref[...] += _dot(xb, dpre_r, TN)
        dwx_ref[...] += _dot(xb, dpre_i, TN)
        dba_ref[...] += jnp.sum(dpre_r, axis=0, keepdims=True)
        dbx_ref[...] += jnp.sum(dpre_i, axis=0, keepdims=True)
        dsp = jnp.sum(dla * (-LRU_C) * r, axis=0, keepdims=True)
        dlam_ref[...] += -dsp * _sigmoid(-lam)
        for j, rj in enumerate(_conv_dw_rows(xr, dxb, LRU_CONV)):
            dcw_ref[pl.ds(j, 1), :] += rj
        dcb_ref[...] += jnp.sum(dxb, axis=0, keepdims=True)

    blk = lambda off: pl.BlockSpec((S, HEAD_DIM), lambda n, b: (b, off + n))
    vec = pl.BlockSpec((1, HEAD_DIM), lambda n, b: (0, n))
    mat = pl.BlockSpec((None, HEAD_DIM, HEAD_DIM), lambda n, b: (n, 0, 0))
    cwb = pl.BlockSpec((LRU_CONV, HEAD_DIM), lambda n, b: (0, n))
    part = jax.ShapeDtypeStruct((T, LW), ACT_DTYPE)
    vshape = jax.ShapeDtypeStruct((1, LW), F32)
    mshape = jax.ShapeDtypeStruct((NB, HEAD_DIM, HEAD_DIM), F32)
    return _pcall(
        kern, name=name, grid=(NB, B),
        in_specs=[blk(xoff), blk(yoff), cwb, vec, mat, vec, mat, vec, vec, blk(0), blk(0)],
        out_specs=[blk(0), blk(0), mat, mat, vec, vec, vec, cwb, vec],
        out_shape=[part, part, mshape, mshape, vshape, vshape, vshape,
                   jax.ShapeDtypeStruct((LRU_CONV, LW), F32), vshape],
        scratch_shapes=[pltpu.VMEM((S, HEAD_DIM), F32)] * 3,
        compiler_params=_params(("parallel", "arbitrary"), 40 * S * HEAD_DIM * 4),
    )(proj, proj, cw, cb, wa, ba, wx, bx, lam, h, dz)


def _pos():
    return lax.axis_index("x"), lax.axis_index("y"), lax.axis_index("c")


def _other_chips(x, y):
    return [(1 - x, y), (x, 1 - y), (1 - x, 1 - y)]


def _remote(src, dst, send_sems, recv_sems, k, to):
    return pltpu.make_async_remote_copy(src_ref=src, dst_ref=dst, send_sem=send_sems.at[k],
                                        recv_sem=recv_sems.at[k], device_id=to, device_id_type=MESH)


HBM_BLK = pl.BlockSpec(memory_space=pltpu.HBM)
SEM_BLK = pl.BlockSpec(memory_space=pltpu.SEMAPHORE)
VMEM_BLK = pl.BlockSpec(memory_space=pltpu.VMEM)
DATAFLOW = pltpu.SideEffectType.DATAFLOW_SIDE_EFFECTING
TOKEN = jax.ShapeDtypeStruct((8, V7X_LANES), F32)


def _in_hbm(a):
    return pltpu.with_memory_space_constraint(a, pltpu.HBM)


def _gather_win(o_ref, R, C, col_sharded):
    Rh = R // 2

    def win(j, h=None):
        if col_sharded:
            rows = pl.ds(0, R) if h is None else pl.ds(h * Rh, Rh)
            return o_ref.at[rows, pl.ds(j * C, C)]
        return o_ref.at[pl.ds(j * R, R) if h is None else pl.ds(j * R + h * Rh, Rh), :]

    return win


def _gather_start(shard, col_sharded, token, name):
    R, C = shard.shape
    Rh = R // 2
    full = (R, 4 * C) if col_sharded else (4 * R, C)

    def body(s_ref, land_ref, tok_ref, send_sems, recv_sems, s_thru, land_thru, tok_out):
        x, y, c = _pos()
        win = _gather_win(land_ref, R, C, col_sharded)
        for k, (cx, cy) in enumerate(_other_chips(x, y)):
            _remote(s_ref.at[pl.ds(c * Rh, Rh), :], win(2 * x + y, c), send_sems, recv_sems, k, (cx, cy, c)).start()
        tok_out[...] = tok_ref[...]

    return _pcall(
        body, name=name,
        out_shape=(pltpu.SemaphoreType.DMA((3,)), pltpu.SemaphoreType.DMA((3,)),
                   pltpu.HBM(shard.shape, shard.dtype), pltpu.HBM(full, shard.dtype), TOKEN),
        in_specs=(HBM_BLK, HBM_BLK, VMEM_BLK), out_specs=(SEM_BLK, SEM_BLK, HBM_BLK, HBM_BLK, VMEM_BLK),
        input_output_aliases={0: 2, 1: 3},
        compiler_params=pltpu.CompilerParams(has_side_effects=DATAFLOW),
    )(_in_hbm(shard), _in_hbm(lax.empty(full, shard.dtype)), token)


def _gather_wait(started, after, col_sharded, name):
    send_sems, recv_sems, s_thru, land_thru, _ = started
    R, C = s_thru.shape
    Rh = R // 2

    def body(s_ref, land_ref, send_sems, recv_sems, after_ref, s_dead, got_ref):
        x, y, c = _pos()
        win = _gather_win(land_ref, R, C, col_sharded)
        for k, (cx, cy) in enumerate(_other_chips(x, y)):
            cp = _remote(s_ref.at[pl.ds(c * Rh, Rh), :], win(2 * cx + cy, c), send_sems, recv_sems, k, (cx, cy, c))
            cp.wait_send()
            cp.wait_recv()

    return _pcall(
        body, name=name,
        out_shape=(pltpu.HBM(s_thru.shape, s_thru.dtype), pltpu.HBM(land_thru.shape, land_thru.dtype)),
        in_specs=(HBM_BLK, HBM_BLK, SEM_BLK, SEM_BLK, HBM_SPEC), out_specs=(HBM_BLK, HBM_BLK),
        input_output_aliases={0: 0, 1: 1},
        compiler_params=pltpu.CompilerParams(has_side_effects=DATAFLOW),
    )(s_thru, land_thru, send_sems, recv_sems, after)


def _gather_pass_on(shard, landed, col_sharded, name):
    R, C = shard.shape

    def body(s_ref, in_ref, o_ref, send_sems, recv_sems, local_sem):
        x, y, c = _pos()
        src, dst = _gather_win(in_ref, R, C, col_sharded), _gather_win(o_ref, R, C, col_sharded)
        mine = pltpu.make_async_copy(s_ref, dst(2 * x + y), local_sem)
        mine.start()
        chips = _other_chips(x, y)
        passed = [_remote(src(2 * cx + cy, c), dst(2 * cx + cy, c), send_sems, recv_sems, k, (x, y, 1 - c))
                  for k, (cx, cy) in enumerate(chips)]
        for cp in passed:
            cp.start()
        for k, (cx, cy) in enumerate(chips):
            w = dst(2 * cx + cy, 1 - c)
            _remote(w, w, send_sems, recv_sems, k, (x, y, c)).wait_recv()
        for cp in passed:
            cp.wait_send()
        mine.wait()

    return _pcall(body, name=name, in_specs=[HBM_SPEC, HBM_SPEC], out_specs=HBM_SPEC,
                  out_shape=jax.ShapeDtypeStruct(landed.shape, landed.dtype), input_output_aliases={1: 0},
                  scratch_shapes=[pltpu.SemaphoreType.DMA((3,)), pltpu.SemaphoreType.DMA((3,)),
                                  pltpu.SemaphoreType.DMA(())])(shard, landed)


def _pair_exchange(g4, name):
    J, _, Rh, W = g4.shape

    def body(g_ref, p_ref, send_sems, recv_sems):
        x, y, c = _pos()
        cp = _remote(g_ref.at[pl.ds(0, J), 1 - c], p_ref, send_sems, recv_sems, 0, (x, y, 1 - c))
        cp.start()
        cp.wait()

    return _pcall(body, name=name, in_specs=[HBM_SPEC], out_specs=HBM_SPEC,
                  out_shape=jax.ShapeDtypeStruct((J, Rh, W), g4.dtype),
                  scratch_shapes=[pltpu.SemaphoreType.DMA((1,)), pltpu.SemaphoreType.DMA((1,))])(g4)


def _pair_add(g4, p, name):
    J, _, Rh, W = g4.shape
    tr = _tile(Rh, 256, 16)
    tw = _tile(W, 2048, V7X_LANES)
    c = lax.axis_index("c").astype(jnp.int32).reshape(1)

    def kern(c_ref, g_ref, p_ref, o_ref):
        o_ref[...] = (g_ref[...].astype(F32) + p_ref[...].astype(F32)).astype(ACT_DTYPE)

    grid_spec = pltpu.PrefetchScalarGridSpec(
        num_scalar_prefetch=1, grid=(J, Rh // tr, W // tw),
        in_specs=[pl.BlockSpec((None, None, tr, tw), lambda j, i, w, cr: (j, cr[0], i, w)),
                  pl.BlockSpec((None, tr, tw), lambda j, i, w, cr: (j, i, w))],
        out_specs=pl.BlockSpec((None, tr, tw), lambda j, i, w, cr: (j, i, w)))
    return _pcall(kern, name=name, grid_spec=grid_spec,
                  out_shape=jax.ShapeDtypeStruct((J, Rh, W), ACT_DTYPE),
                  compiler_params=_params(("parallel", "parallel", "parallel"), 12 * tr * tw * 4))(c, g4, p)


def _chip_sum_piece(cs_ref, C, col_sharded):
    return lambda j: cs_ref.at[0, :, pl.ds(j * C, C)] if col_sharded else cs_ref.at[j]


def _scatter_start(cs, col_sharded, token, name):
    J, Rh, W = cs.shape
    C = W // 4 if col_sharded else W
    slots = (4, 2, Rh, C)

    def body(cs_ref, land_ref, tok_ref, send_sems, recv_sems, cs_thru, land_thru, tok_out):
        x, y, c = _pos()
        piece = _chip_sum_piece(cs_ref, C, col_sharded)
        for k, (cx, cy) in enumerate(_other_chips(x, y)):
            _remote(piece(2 * cx + cy), land_ref.at[2 * x + y, c], send_sems, recv_sems, k, (cx, cy, c)).start()
        tok_out[...] = tok_ref[...]

    return _pcall(
        body, name=name,
        out_shape=(pltpu.SemaphoreType.DMA((3,)), pltpu.SemaphoreType.DMA((3,)),
                   pltpu.HBM(cs.shape, cs.dtype), pltpu.HBM(slots, cs.dtype), TOKEN),
        in_specs=(HBM_BLK, HBM_BLK, VMEM_BLK), out_specs=(SEM_BLK, SEM_BLK, HBM_BLK, HBM_BLK, VMEM_BLK),
        input_output_aliases={0: 2, 1: 3},
        compiler_params=pltpu.CompilerParams(has_side_effects=DATAFLOW),
    )(_in_hbm(cs), _in_hbm(lax.empty(slots, cs.dtype)), token)


def _scatter_wait(started, after, col_sharded, name):
    send_sems, recv_sems, cs_thru, land_thru, _ = started
    C = land_thru.shape[3]

    def body(cs_ref, land_ref, send_sems, recv_sems, after_ref, cs_dead, got_ref):
        x, y, c = _pos()
        piece = _chip_sum_piece(cs_ref, C, col_sharded)
        for k, (cx, cy) in enumerate(_other_chips(x, y)):
            cp = _remote(piece(2 * cx + cy), land_ref.at[2 * cx + cy, c], send_sems, recv_sems, k, (cx, cy, c))
            cp.wait_send()
            cp.wait_recv()

    return _pcall(
        body, name=name,
        out_shape=(pltpu.HBM(cs_thru.shape, cs_thru.dtype), pltpu.HBM(land_thru.shape, land_thru.dtype)),
        in_specs=(HBM_BLK, HBM_BLK, SEM_BLK, SEM_BLK, HBM_SPEC), out_specs=(HBM_BLK, HBM_BLK),
        input_output_aliases={0: 0, 1: 1},
        compiler_params=pltpu.CompilerParams(has_side_effects=DATAFLOW),
    )(cs_thru, land_thru, send_sems, recv_sems, after)


def _scatter_pass_on(cs, landed, col_sharded, name):
    C = landed.shape[3]

    def body(cs_ref, in_ref, o_ref, send_sems, recv_sems, local_sem):
        x, y, c = _pos()
        me, sib, ime = (x, y, c), (x, y, 1 - c), 2 * x + y
        piece = _chip_sum_piece(cs_ref, C, col_sharded)
        mine = pltpu.make_async_copy(piece(ime), o_ref.at[ime, c], local_sem)
        mine.start()
        chips = _other_chips(x, y)
        sends = [_remote(piece(ime), o_ref.at[ime, c], send_sems, recv_sems, 0, sib)]
        sends += [_remote(in_ref.at[2 * cx + cy, c], o_ref.at[2 * cx + cy, c], send_sems, recv_sems, 1 + k, sib)
                  for k, (cx, cy) in enumerate(chips)]
        for cp in sends:
            cp.start()
        w = o_ref.at[ime, 1 - c]
        _remote(w, w, send_sems, recv_sems, 0, me).wait_recv()
        for k, (cx, cy) in enumerate(chips):
            w = o_ref.at[2 * cx + cy, 1 - c]
            _remote(w, w, send_sems, recv_sems, 1 + k, me).wait_recv()
        for cp in sends:
            cp.wait_send()
        mine.wait()

    return _pcall(body, name=name, in_specs=[HBM_SPEC, HBM_SPEC], out_specs=HBM_SPEC,
                  out_shape=jax.ShapeDtypeStruct(landed.shape, landed.dtype), input_output_aliases={1: 0},
                  scratch_shapes=[pltpu.SemaphoreType.DMA((4,)), pltpu.SemaphoreType.DMA((4,)),
                                  pltpu.SemaphoreType.DMA(())])(cs, landed)


def _gather_small(buf, name):
    rows = buf.shape[0]

    def body(b_ref, o_ref, send_sems, recv_sems):
        x, y, c = _pos()
        jme = 2 * x + y
        chips = _other_chips(x, y)
        o_ref[jme] = b_ref[...]
        sends = [_remote(b_ref, o_ref.at[jme], send_sems, recv_sems, k, (cx, cy, c))
                 for k, (cx, cy) in enumerate(chips)]
        for cp in sends:
            cp.start()
        for k, (cx, cy) in enumerate(chips):
            w = o_ref.at[2 * cx + cy]
            _remote(w, w, send_sems, recv_sems, k, (x, y, c)).wait_recv()
        for cp in sends:
            cp.wait_send()

    vm = pl.BlockSpec(memory_space=pltpu.VMEM)
    return _pcall(body, name=name, in_specs=[vm], out_specs=vm,
                  out_shape=jax.ShapeDtypeStruct((4, rows, V7X_LANES), buf.dtype),
                  scratch_shapes=[pltpu.SemaphoreType.DMA((3,)), pltpu.SemaphoreType.DMA((3,))],
                  compiler_params=_params(None, 16 * rows * V7X_LANES * 4))(buf)


def _allreduce_small(buf, name):
    rows = buf.shape[0]

    def body(b_ref, o_ref, slots, send_sems, recv_sems):
        x, y, c = _pos()
        me = 4 * x + 2 * y + c
        slots[me] = b_ref[...]
        sends = []
        for k in range(1, 8):
            kx, ky, kc = (k >> 2) & 1, (k >> 1) & 1, k & 1
            to = (x ^ kx, y ^ ky, c ^ kc)
            sends.append(_remote(b_ref, slots.at[me], send_sems, recv_sems, k - 1, to))
        for cp in sends:
            cp.start()
        for k in range(1, 8):
            kx, ky, kc = (k >> 2) & 1, (k >> 1) & 1, k & 1
            w = slots.at[4 * (x ^ kx) + 2 * (y ^ ky) + (c ^ kc)]
            _remote(w, w, send_sems, recv_sems, k - 1, (x, y, c)).wait_recv()
        for cp in sends:
            cp.wait_send()
        acc = slots[0]
        for d in range(1, 8):
            acc = acc + slots[d]
        o_ref[...] = acc

    vm = pl.BlockSpec(memory_space=pltpu.VMEM)
    return _pcall(body, name=name, in_specs=[vm], out_specs=vm,
                  out_shape=jax.ShapeDtypeStruct(buf.shape, buf.dtype),
                  scratch_shapes=[pltpu.VMEM((8, rows, V7X_LANES), buf.dtype),
                                  pltpu.SemaphoreType.DMA((7,)), pltpu.SemaphoreType.DMA((7,))],
                  compiler_params=_params(None, 14 * rows * V7X_LANES * 4))(buf)


def _adamw_math(w, g, m, v):
    m = ADAM_B1 * m + (1.0 - ADAM_B1) * g
    v = ADAM_B2 * v + (1.0 - ADAM_B2) * (g * g)
    m_hat = m / (1.0 - ADAM_B1 ** ADAM_STEP)
    v_hat = v / (1.0 - ADAM_B2 ** ADAM_STEP)
    delta = -ADAM_LR * (m_hat / (jnp.sqrt(v_hat) + ADAM_EPS) + ADAM_WD * w)
    return delta, m, v


def _adamw_big(w, m, v, slots, name):
    R, C = w.shape
    tr = _tile(R, 32, 16)

    def kern(w_ref, m_ref, v_ref, s_ref, g_ref, d_ref, mo_ref, vo_ref):
        g = s_ref[0].astype(F32)
        for i in range(1, 4):
            g = g + s_ref[i].astype(F32)
        d, mn, vn = _adamw_math(w_ref[...], g, m_ref[...], v_ref[...])
        g_ref[...], d_ref[...], mo_ref[...], vo_ref[...] = g, d, mn, vn

    row = pl.BlockSpec((tr, C), lambda i: (i, 0))
    shp = jax.ShapeDtypeStruct((R, C), F32)
    return _pcall(kern, name=name, grid=(R // tr,),
                  in_specs=[row, row, row, pl.BlockSpec((4, tr, C), lambda i: (0, i, 0))],
                  out_specs=[row] * 4, out_shape=[shp] * 4,
                  compiler_params=_params(("parallel",), 36 * tr * C * 4))(w, m, v, slots)


def _adamw_small(w, g, m, v, name):
    def kern(w_ref, g_ref, m_ref, v_ref, d_ref, mo_ref, vo_ref):
        d_ref[...], mo_ref[...], vo_ref[...] = _adamw_math(w_ref[...], g_ref[...], m_ref[...], v_ref[...])

    vm = pl.BlockSpec(memory_space=pltpu.VMEM)
    shp = jax.ShapeDtypeStruct(w.shape, F32)
    return _pcall(kern, name=name, in_specs=[vm] * 4, out_specs=[vm] * 3, out_shape=[shp] * 3,
                  compiler_params=_params(None, 10 * w.size * 4))(w, g, m, v)


def _pack(arrs):
    flat = jnp.concatenate([a.reshape(-1).astype(F32) for a in arrs])
    n = flat.shape[0]
    rows = -(-n // (8 * V7X_LANES)) * 8
    return jnp.pad(flat, (0, rows * V7X_LANES - n)).reshape(rows, V7X_LANES)


def _unpack(buf, shapes):
    flat = buf.reshape(-1)
    out, off = [], 0
    for s in shapes:
        n = int(np.prod(s))
        out.append(flat[off:off + n].reshape(s))
        off += n
    return out


def _reduce_start(gfull, col_sharded, shard_shape, token, tag):
    R, C = shard_shape
    if col_sharded:
        g4 = gfull.reshape(1, 2, R // 2, 4 * C)
    else:
        g4 = gfull.reshape(4, 2, R // 2, C)
    p = _pair_exchange(g4, "pair_exchange_" + tag)
    cs = _pair_add(g4, p, "pair_add_" + tag)
    return _scatter_start(cs, col_sharded, token, "scatter_start_" + tag)


def _reduce_finish(started, after, col_sharded, w, m, v, tag):
    R, C = w.shape
    cs, landed = _scatter_wait(started, after, col_sharded, "scatter_wait_" + tag)
    slots = _scatter_pass_on(cs, landed, col_sharded, "scatter_pass_on_" + tag)
    return _adamw_big(w, m, v, slots.reshape(4, R, C), "adamw_" + tag)


def kernel(x, ln1_w, w_in, lb_gamma, hg_norm_w, lru_conv_w, lru_conv_b, lru_wa, lru_ba, lru_wx, lru_bx, lru_lambda, lru_norm_w, w_out, ln2_w, ffn_w_up, ffn_conv_w, ffn_conv_b, ffn_w_down, final_norm_w, loss_target, m_ln1_w, m_w_in, m_lb_gamma, m_hg_norm_w, m_lru_conv_w, m_lru_conv_b, m_lru_wa, m_lru_ba, m_lru_wx, m_lru_bx, m_lru_lambda, m_lru_norm_w, m_w_out, m_ln2_w, m_ffn_w_up, m_ffn_conv_w, m_ffn_conv_b, m_ffn_w_down, m_final_norm_w, v_ln1_w, v_w_in, v_lb_gamma, v_hg_norm_w, v_lru_conv_w, v_lru_conv_b, v_lru_wa, v_lru_ba, v_lru_wx, v_lru_bx, v_lru_lambda, v_lru_norm_w, v_w_out, v_ln2_w, v_ffn_w_up, v_ffn_conv_w, v_ffn_conv_b, v_ffn_w_down, v_final_norm_w):
    B, S, D = x.shape
    T = B * S
    HW = lb_gamma.shape[1]
    LW = lru_conv_b.shape[1]
    assert S % CHUNK == 0 and HW % HEAD_DIM == 0 and lru_wa.shape[2] == HEAD_DIM
    x2 = x.reshape(T, D)
    tgt = loss_target.reshape(T, D)
    jchip = 2 * lax.axis_index("x") + lax.axis_index("y")

    shards = dict(w_in=_cast(w_in[0], ACT_DTYPE, "cast_w_in"), w_out=_cast(w_out[0], ACT_DTYPE, "cast_w_out"),
                  w_up=_cast(ffn_w_up[0], ACT_DTYPE, "cast_w_up"), w_down=_cast(ffn_w_down[0], ACT_DTYPE, "cast_w_down"))
    col_of = dict(w_in=True, w_out=False, w_up=True, w_down=False)
    started, token = {}, jnp.zeros(TOKEN.shape, F32)
    for n in ("w_in", "w_out", "w_up", "w_down"):
        started[n] = _gather_start(shards[n], col_of[n], token, "gather_start_" + n)
        token = started[n][4]

    def gathered(n, after):
        shard, landed = _gather_wait(started[n], after, col_of[n], "gather_wait_" + n)
        return _gather_pass_on(shard, landed, col_of[n], "gather_pass_on_" + n)

    W_in = gathered("w_in", token)
    conv_shapes = [lru_conv_w[0].shape, ffn_conv_w[0].shape]
    convs = _gather_small(_pack([lru_conv_w[0], ffn_conv_w[0]]), "gather_conv_w")
    per_chip = [_unpack(convs[j], conv_shapes) for j in range(4)]
    lcw = jnp.concatenate([pc[0] for pc in per_chip], axis=1)
    fcw = jnp.concatenate([pc[1] for pc in per_chip], axis=1)

    hn1 = _rms_fwd(x2, ln1_w, "rms1")
    proj = _matmul(hn1, W_in, "NN", F32, 1024, 1024, 1024, name="mm_proj")
    o_raw, o_hg, states = _hgrn_fwd(proj, lb_gamma, hg_norm_w, B, HW, "hgrn_fwd")
    h_lru, z = _lru_fwd(proj, lcw, lru_conv_b, lru_wa[0], lru_ba, lru_wx[0], lru_bx, lru_lambda, B, HW, LW, "lru_fwd")
    o_lru = _rms_fwd(z, lru_norm_w, "rms_lru")
    mix = jnp.concatenate([o_hg, o_lru], axis=1)
    W_out = gathered("w_out", mix)
    h1 = _matmul(mix, W_out, "NN", F32, 1024, 1024, 1024, add=x2, name="mm_out")
    hn2 = _rms_fwd(h1, ln2_w, "rms2")
    W_up = gathered("w_up", hn2)
    up = _matmul(hn2, W_up, "NN", F32, 2048, 512, 1024, name="mm_up")
    act = _ffn_act(up, fcw, ffn_conv_b, B, "ffn_act")
    W_down = gathered("w_down", act)
    h2 = _matmul(act, W_down, "NN", F32, 2048, 1024, 256, add=h1, name="mm_down")

    dh2, dh2a, d_final_w, loss_part = _loss_bwd(h2, tgt, final_norm_w.reshape(1, D), "loss_bwd")
    g_down = _matmul(act, dh2a, "TN", ACT_DTYPE, 256, 4096, 1024, name="mm_g_down")
    red_down = _reduce_start(g_down, False, ffn_w_down[0].shape, token, "w_down")
    d_act = _matmul(dh2a, W_down, "NT", ACT_DTYPE, 512, 5504, 512, after=red_down[4], name="mm_d_act")
    d_up, d_fcw, d_fcb = _ffn_act_bwd(up, fcw, ffn_conv_b, d_act, B, "ffn_act_bwd")
    g_up = _matmul(hn2, d_up, "TN", ACT_DTYPE, 2048, 512, 1024, name="mm_g_up")
    red_up = _reduce_start(g_up, True, ffn_w_up[0].shape, red_down[4], "w_up")
    d_hn2 = _matmul(d_up, W_up, "NT", F32, 2048, 1024, 512, after=red_up[4], name="mm_d_hn2")
    dh1, dh1a, d_ln2 = _rms_bwd(h1, ln2_w, d_hn2, 0, dh2, True, "rms2_bwd")
    g_out = _matmul(mix, dh1a, "TN", ACT_DTYPE, 1024, 1024, 1024, name="mm_g_out")
    red_out = _reduce_start(g_out, False, w_out[0].shape, red_up[4], "w_out")
    d_mix = _matmul(dh1a, W_out, "NT", F32, 1024, 1024, 1024, after=red_out[4], name="mm_d_mix")
    dz, d_lru_norm = _rms_bwd(z, lru_norm_w, d_mix, HW // LW, None, False, "rms_lru_bwd")
    (d_xr, d_yr, d_wa, d_wx, d_ba, d_bx, d_lam, d_lcw, d_lcb) = _lru_bwd(
        proj, lcw, lru_conv_b, lru_wa[0], lru_ba, lru_wx[0], lru_bx, lru_lambda, h_lru, dz, B, HW, LW, "lru_bwd")
    d_q, d_f, d_i, d_g, d_lbg, d_hgw = _hgrn_bwd(proj, lb_gamma, hg_norm_w, o_raw, states, d_mix, B, HW, "hgrn_bwd")
    d_proj = jnp.concatenate([d_q, d_f, d_i, d_g, d_xr, d_yr], axis=1)
    g_in = _matmul(hn1, d_proj, "TN", ACT_DTYPE, 1024, 1024, 1024, name="mm_g_in")
    red_in = _reduce_start(g_in, True, w_in[0].shape, red_out[4], "w_in")
    d_hn1 = _matmul(d_proj, W_in, "NT", F32, 1024, 1024, 1024, after=red_in[4], name="mm_d_hn1")
    dx, d_ln1 = _rms_bwd(x2, ln1_w, d_hn1, 0, dh1, False, "rms1_bwd")

    big = {}
    big["ffn_w_down"] = _reduce_finish(red_down, dx, False, ffn_w_down[0], m_ffn_w_down[0], v_ffn_w_down[0], "w_down")
    big["ffn_w_up"] = _reduce_finish(red_up, big["ffn_w_down"][1], True, ffn_w_up[0], m_ffn_w_up[0], v_ffn_w_up[0], "w_up")
    big["w_out"] = _reduce_finish(red_out, big["ffn_w_up"][1], False, w_out[0], m_w_out[0], v_w_out[0], "w_out")
    big["w_in"] = _reduce_finish(red_in, big["w_out"][1], True, w_in[0], m_w_in[0], v_w_in[0], "w_in")

    small_names = ["ln1_w", "lb_gamma", "hg_norm_w", "lru_conv_w", "lru_conv_b", "lru_wa", "lru_ba", "lru_wx",
                   "lru_bx", "lru_lambda", "lru_norm_w", "ln2_w", "ffn_conv_w", "ffn_conv_b", "final_norm_w"]
    small_grads = [d_ln1, d_lbg, d_hgw, d_lcw, d_lcb, d_wa, d_ba, d_wx, d_bx, d_lam, d_lru_norm, d_ln2,
                   d_fcw, d_fcb, d_final_w]
    red = _allreduce_small(_pack([loss_part[0:1, 0:1]] + small_grads), "allreduce_small")
    red = _unpack(red, [(1, 1)] + [g.shape for g in small_grads])
    loss = red[0].reshape(())
    gs = dict(zip(small_names, red[1:]))
    nlc, nfc = lru_conv_w.shape[2], ffn_conv_w.shape[2]
    gs["lru_conv_w"] = lax.dynamic_slice_in_dim(gs["lru_conv_w"], jchip * nlc, nlc, axis=1)
    gs["ffn_conv_w"] = lax.dynamic_slice_in_dim(gs["ffn_conv_w"], jchip * nfc, nfc, axis=1)
    args = dict(ln1_w=(ln1_w, m_ln1_w, v_ln1_w), lb_gamma=(lb_gamma, m_lb_gamma, v_lb_gamma),
                hg_norm_w=(hg_norm_w, m_hg_norm_w, v_hg_norm_w), lru_conv_w=(lru_conv_w, m_lru_conv_w, v_lru_conv_w),
                lru_conv_b=(lru_conv_b, m_lru_conv_b, v_lru_conv_b), lru_wa=(lru_wa, m_lru_wa, v_lru_wa),
                lru_ba=(lru_ba, m_lru_ba, v_lru_ba), lru_wx=(lru_wx, m_lru_wx, v_lru_wx),
                lru_bx=(lru_bx, m_lru_bx, v_lru_bx), lru_lambda=(lru_lambda, m_lru_lambda, v_lru_lambda),
                lru_norm_w=(lru_norm_w, m_lru_norm_w, v_lru_norm_w), ln2_w=(ln2_w, m_ln2_w, v_ln2_w),
                ffn_conv_w=(ffn_conv_w, m_ffn_conv_w, v_ffn_conv_w), ffn_conv_b=(ffn_conv_b, m_ffn_conv_b, v_ffn_conv_b),
                final_norm_w=(final_norm_w, m_final_norm_w, v_final_norm_w))
    shapes = [args[n][0].shape for n in small_names]
    upd = _adamw_small(_pack([args[n][0] for n in small_names]), _pack([gs[n] for n in small_names]),
                       _pack([args[n][1] for n in small_names]), _pack([args[n][2] for n in small_names]), "adamw_small")
    s_delta, s_m, s_v = (dict(zip(small_names, _unpack(u, shapes))) for u in upd)

    order = ["ln1_w", "w_in", "lb_gamma", "hg_norm_w", "lru_conv_w", "lru_conv_b", "lru_wa", "lru_ba", "lru_wx",
             "lru_bx", "lru_lambda", "lru_norm_w", "w_out", "ln2_w", "ffn_w_up", "ffn_conv_w", "ffn_conv_b",
             "ffn_w_down", "final_norm_w"]
    full_shape = dict(w_in=w_in.shape, w_out=w_out.shape, ffn_w_up=ffn_w_up.shape, ffn_w_down=ffn_w_down.shape)
    grads, deltas, new_m, new_v = [], [], [], []
    for n in order:
        if n in big:
            g, d, mn, vn = (t.reshape(full_shape[n]) for t in big[n])
        else:
            g, d, mn, vn = gs[n].reshape(args[n][0].shape), s_delta[n], s_m[n], s_v[n]
        grads.append(g), deltas.append(d), new_m.append(mn), new_v.append(vn)
    return (loss, dx.reshape(B, S, D), *grads, *deltas, *new_m, *new_v)
```

```python
import functools
import math

import numpy as np
import jax
import jax.numpy as jnp
from jax import lax
from jax.experimental import pallas as pl
from jax.experimental.pallas import tpu as pltpu

F32 = jnp.float32
MXU_DTYPE = jnp.bfloat16
ACT_DTYPE = jnp.bfloat16

EPS = 1e-6
HEAD_DIM = 128
CHUNK = 64
LEVEL_HALVES = (32, 16, 8, 4, 2, 1)
LRU_CONV = 4
FFN_CONV = 3
LRU_C = 8.0
ADAM_LR, ADAM_B1, ADAM_B2, ADAM_EPS, ADAM_WD, ADAM_STEP = 0.001, 0.9, 0.999, 1e-08, 0.01, 10

V7X_LANES = 128
V7X_VMEM_BUDGET = 56 << 20

NN = (((1,), (0,)), ((), ()))
NT = (((1,), (1,)), ((), ()))
TN = (((0,), (0,)), ((), ()))
MESH = pl.DeviceIdType.MESH
HBM_SPEC = pl.BlockSpec(memory_space=pl.ANY)


def _pcall(kern, **kw):
    return pl.pallas_call(kern, **kw)


def _params(sem=None, vmem=None):
    kw = {}
    if sem is not None:
        kw["dimension_semantics"] = sem
    if vmem is not None:
        kw["vmem_limit_bytes"] = int(min(max(vmem, 16 << 20), V7X_VMEM_BUDGET))
    return pltpu.CompilerParams(**kw)


def _dot(a, b, dims=NN):
    return lax.dot_general(a.astype(MXU_DTYPE), b.astype(MXU_DTYPE), dims, preferred_element_type=F32)


def _tile(dim, pref, align):
    t = min(pref, dim) // align * align
    while t >= align:
        if dim % t == 0:
            return t
        t -= align
    return dim


def _sigmoid(x):
    return 1.0 / (1.0 + jnp.exp(-x))


def _silu_and_grad(x):
    s = _sigmoid(x)
    return x * s, s * (1.0 + x * (1.0 - s))


def _gelu_and_grad(x):
    k0, k1 = math.sqrt(2.0 / math.pi), 0.044715
    t = jnp.tanh(k0 * (x + k1 * x * x * x))
    g = 0.5 * x * (1.0 + t)
    dg = 0.5 * (1.0 + t) + 0.5 * x * (1.0 - t * t) * k0 * (1.0 + 3.0 * k1 * x * x)
    return g, dg


def _one_minus_exp(x):
    p = x * (1.0 + x * (0.5 + x * (1.0 / 6.0 + x * (1.0 / 24.0 + x * (1.0 / 120.0)))))
    return jnp.where(x > -0.05, -p, 1.0 - jnp.exp(x))


def _rows(n):
    return lax.broadcasted_iota(jnp.int32, (n, 1), 0)


def _matmul(a, b, mode, out_dtype, tm, tn, tk, add=None, after=None, name="mm"):
    if mode == "TN":
        K, M = a.shape
    else:
        M, K = a.shape
    N = b.shape[0] if mode == "NT" else b.shape[1]
    tm, tn = _tile(M, tm, V7X_LANES), _tile(N, tn, V7X_LANES)
    tk = _tile(K, tk, V7X_LANES)
    nk = K // tk
    dims = {"NN": NN, "NT": NT, "TN": TN}[mode]
    a_spec = (pl.BlockSpec((tk, tm), lambda i, j, k: (k, i)) if mode == "TN"
              else pl.BlockSpec((tm, tk), lambda i, j, k: (i, k)))
    b_spec = (pl.BlockSpec((tn, tk), lambda i, j, k: (j, k)) if mode == "NT"
              else pl.BlockSpec((tk, tn), lambda i, j, k: (k, j)))
    o_spec = pl.BlockSpec((tm, tn), lambda i, j, k: (i, j))
    has_add = add is not None

    def kern(*refs):
        a_ref, b_ref = refs[:2]
        add_ref = refs[2] if has_add else None
        o_ref, acc_ref = refs[-2:]
        k = pl.program_id(2)

        @pl.when(k == 0)
        def _():
            acc_ref[...] = jnp.zeros_like(acc_ref)

        acc_ref[...] += _dot(a_ref[...], b_ref[...], dims)

        @pl.when(k == nk - 1)
        def _():
            r = acc_ref[...]
            if has_add:
                r = r + add_ref[...]
            o_ref[...] = r.astype(out_dtype)

    ab = jnp.dtype(a.dtype).itemsize
    ob = jnp.dtype(out_dtype).itemsize
    vmem = 2 * (tm * tk + tk * tn) * ab + tm * tn * (4 + 2 * ob + (8 if has_add else 0)) + (4 << 20)
    ins = [a, b] + ([add] if has_add else []) + ([after] if after is not None else [])
    in_specs = [a_spec, b_spec] + ([o_spec] if has_add else []) + ([HBM_SPEC] if after is not None else [])
    return _pcall(
        kern, name=name, grid=(M // tm, N // tn, nk),
        in_specs=in_specs, out_specs=o_spec,
        out_shape=jax.ShapeDtypeStruct((M, N), out_dtype),
        scratch_shapes=[pltpu.VMEM((tm, tn), F32)],
        compiler_params=_params(("parallel", "parallel", "arbitrary"), vmem),
    )(*ins)


def _rms_fwd(x, w, name):
    T, D = x.shape
    tm = _tile(T, 256, 16)

    def kern(x_ref, w_ref, o_ref):
        xv = x_ref[...]
        r = lax.rsqrt(jnp.mean(xv * xv, axis=-1, keepdims=True) + EPS)
        o_ref[...] = (xv * r * w_ref[...]).astype(ACT_DTYPE)

    return _pcall(kern, name=name, grid=(T // tm,),
                  in_specs=[pl.BlockSpec((tm, D), lambda i: (i, 0)), pl.BlockSpec((1, D), lambda i: (0, 0))],
                  out_specs=pl.BlockSpec((tm, D), lambda i: (i, 0)),
                  out_shape=jax.ShapeDtypeStruct((T, D), ACT_DTYPE),
                  compiler_params=_params(("parallel",), 8 * tm * D * 4))(x, w)


def _rms_bwd(x, w, g, g_col, res, want_act, name, after=None):
    T, D = x.shape
    tm = _tile(T, 256, 16)
    has_res = res is not None

    def kern(*refs):
        refs = list(refs)
        x_ref, w_ref, g_ref = refs[:3]
        res_ref = refs[3] if has_res else None
        outs = refs[3 + has_res + (after is not None):]
        dx_ref = outs[0]
        dxa_ref = outs[1] if want_act else None
        dw_ref = outs[-1]
        i = pl.program_id(0)
        xv = x_ref[...]
        gv = g_ref[...].astype(F32)
        r = lax.rsqrt(jnp.mean(xv * xv, axis=-1, keepdims=True) + EPS)
        gw = gv * w_ref[...]
        dx = r * gw - xv * (r * r * r) * jnp.mean(gw * xv, axis=-1, keepdims=True)
        if has_res:
            dx = dx + res_ref[...]
        dx_ref[...] = dx
        if want_act:
            dxa_ref[...] = dx.astype(ACT_DTYPE)

        @pl.when(i == 0)
        def _():
            dw_ref[...] = jnp.zeros_like(dw_ref)

        dw_ref[...] += jnp.sum(gv * xv * r, axis=0, keepdims=True)

    row = pl.BlockSpec((tm, D), lambda i: (i, 0))
    vec = pl.BlockSpec((1, D), lambda i: (0, 0))
    in_specs = ([row, vec, pl.BlockSpec((tm, D), lambda i: (i, g_col))] + ([row] if has_res else [])
                + ([HBM_SPEC] if after is not None else []))
    out_specs = [row] + ([row] if want_act else []) + [vec]
    out_shape = ([jax.ShapeDtypeStruct((T, D), F32)]
                 + ([jax.ShapeDtypeStruct((T, D), ACT_DTYPE)] if want_act else [])
                 + [jax.ShapeDtypeStruct((1, D), F32)])
    ins = [x, w, g] + ([res] if has_res else []) + ([after] if after is not None else [])
    return _pcall(kern, name=name, grid=(T // tm,), in_specs=in_specs, out_specs=out_specs,
                  out_shape=out_shape, compiler_params=_params(("arbitrary",), 14 * tm * D * 4))(*ins)


def _loss_bwd(h, target, w, name):
    T, D = h.shape
    tm = _tile(T, 256, 16)

    def kern(h_ref, t_ref, w_ref, dh_ref, dha_ref, dw_ref, loss_ref):
        i = pl.program_id(0)
        hv = h_ref[...]
        r = lax.rsqrt(jnp.mean(hv * hv, axis=-1, keepdims=True) + EPS)
        e = hv * r * w_ref[...] - t_ref[...]
        dy = e * (1.0 / D)
        gw = dy * w_ref[...]
        dh = r * gw - hv * (r * r * r) * jnp.mean(gw * hv, axis=-1, keepdims=True)
        dh_ref[...] = dh
        dha_ref[...] = dh.astype(ACT_DTYPE)

        @pl.when(i == 0)
        def _():
            dw_ref[...] = jnp.zeros_like(dw_ref)
            loss_ref[...] = jnp.zeros_like(loss_ref)

        dw_ref[...] += jnp.sum(dy * hv * r, axis=0, keepdims=True)
        part = 0.5 * jnp.sum(jnp.mean(e * e, axis=-1, keepdims=True), axis=0, keepdims=True)
        loss_ref[...] += jnp.broadcast_to(part, loss_ref.shape)

    row = pl.BlockSpec((tm, D), lambda i: (i, 0))
    vec = pl.BlockSpec((1, D), lambda i: (0, 0))
    return _pcall(kern, name=name, grid=(T // tm,), in_specs=[row, row, vec],
                  out_specs=[row, row, vec, pl.BlockSpec((8, V7X_LANES), lambda i: (0, 0))],
                  out_shape=[jax.ShapeDtypeStruct((T, D), F32), jax.ShapeDtypeStruct((T, D), ACT_DTYPE),
                             jax.ShapeDtypeStruct((1, D), F32), jax.ShapeDtypeStruct((8, V7X_LANES), F32)],
                  compiler_params=_params(("arbitrary",), 14 * tm * D * 4))(h, target, w)


def _conv(x, w_ref, b, width):
    S = x.shape[0]
    row = _rows(S)
    y = b + x * w_ref[pl.ds(width - 1, 1), :]
    for j in range(width - 1):
        sh = width - 1 - j
        y = y + jnp.where(row >= sh, pltpu.roll(x, sh, 0), 0.0) * w_ref[pl.ds(j, 1), :]
    return y


def _conv_t(dy, w_ref, width):
    S = dy.shape[0]
    row = _rows(S)
    dx = dy * w_ref[pl.ds(width - 1, 1), :]
    for j in range(width - 1):
        sh = width - 1 - j
        dx = dx + jnp.where(row < S - sh, pltpu.roll(dy, S - sh, 0), 0.0) * w_ref[pl.ds(j, 1), :]
    return dx


def _conv_dw_rows(x, dy, width):
    S = x.shape[0]
    row = _rows(S)
    out = []
    for j in range(width):
        sh = width - 1 - j
        xs = x if sh == 0 else jnp.where(row >= sh, pltpu.roll(x, sh, 0), 0.0)
        out.append(jnp.sum(xs * dy, axis=0, keepdims=True))
    return out


def _ffn_act(up, cw, cb, B, name):
    T, F2 = up.shape
    S, F = T // B, F2 // 2
    tw = _tile(F, 256, V7X_LANES)
    nt = F // tw

    def kern(g_ref, v_ref, wg_ref, wv_ref, bg_ref, bv_ref, o_ref):
        gc = _conv(g_ref[...], wg_ref, bg_ref[...], FFN_CONV)
        vc = _conv(v_ref[...], wv_ref, bv_ref[...], FFN_CONV)
        o_ref[...] = (gc * _sigmoid(gc) * vc).astype(ACT_DTYPE)

    blk = lambda off: pl.BlockSpec((S, tw), lambda b, i: (b, off + i))
    wblk = lambda off: pl.BlockSpec((FFN_CONV, tw), lambda b, i: (0, off + i))
    bblk = lambda off: pl.BlockSpec((1, tw), lambda b, i: (0, off + i))
    return _pcall(kern, name=name, grid=(B, nt),
                  in_specs=[blk(0), blk(nt), wblk(0), wblk(nt), bblk(0), bblk(nt)],
                  out_specs=pl.BlockSpec((S, tw), lambda b, i: (b, i)),
                  out_shape=jax.ShapeDtypeStruct((T, F), ACT_DTYPE),
                  compiler_params=_params(("parallel", "parallel"), 16 * S * tw * 4))(up, up, cw, cw, cb, cb)


def _ffn_act_bwd(up, cw, cb, d_act, B, name):
    T, F2 = up.shape
    S, F = T // B, F2 // 2
    tw = _tile(F, 256, V7X_LANES)
    nt = F // tw

    def kern(s_ref, p_ref, ws_ref, wp_ref, bs_ref, bp_ref, da_ref, du_ref, dcw_ref, dcb_ref):
        t, b = pl.program_id(0), pl.program_id(1)
        sc = _conv(s_ref[...], ws_ref, bs_ref[...], FFN_CONV)
        pc = _conv(p_ref[...], wp_ref, bp_ref[...], FFN_CONV)
        da = da_ref[...].astype(F32)
        _, dsilu_self = _silu_and_grad(sc)
        silu_partner = pc * _sigmoid(pc)
        d = da * jnp.where(t < nt, pc * dsilu_self, silu_partner)
        du_ref[...] = _conv_t(d, ws_ref, FFN_CONV).astype(ACT_DTYPE)

        @pl.when(b == 0)
        def _():
            dcw_ref[...] = jnp.zeros_like(dcw_ref)
            dcb_ref[...] = jnp.zeros_like(dcb_ref)

        for j, rj in enumerate(_conv_dw_rows(s_ref[...], d, FFN_CONV)):
            dcw_ref[pl.ds(j, 1), :] += rj
        dcb_ref[...] += jnp.sum(d, axis=0, keepdims=True)

    partner = lambda t: (t + nt) % (2 * nt)
    return _pcall(
        kern, name=name, grid=(2 * nt, B),
        in_specs=[pl.BlockSpec((S, tw), lambda t, b: (b, t)),
                  pl.BlockSpec((S, tw), lambda t, b: (b, partner(t))),
                  pl.BlockSpec((FFN_CONV, tw), lambda t, b: (0, t)),
                  pl.BlockSpec((FFN_CONV, tw), lambda t, b: (0, partner(t))),
                  pl.BlockSpec((1, tw), lambda t, b: (0, t)),
                  pl.BlockSpec((1, tw), lambda t, b: (0, partner(t))),
                  pl.BlockSpec((S, tw), lambda t, b: (b, t % nt))],
        out_specs=[pl.BlockSpec((S, tw), lambda t, b: (b, t)),
                   pl.BlockSpec((FFN_CONV, tw), lambda t, b: (0, t)),
                   pl.BlockSpec((1, tw), lambda t, b: (0, t))],
        out_shape=[jax.ShapeDtypeStruct((T, F2), ACT_DTYPE), jax.ShapeDtypeStruct((FFN_CONV, F2), F32),
                   jax.ShapeDtypeStruct((1, F2), F32)],
        compiler_params=_params(("parallel", "arbitrary"), 24 * S * tw * 4),
    )(up, up, cw, cw, cb, cb, d_act)


def _hgrn_tables():
    C = CHUNK
    t = np.arange(C)
    mats = [(t[:, None] >= t[None, :]).astype(np.float32)]
    masks = []
    gsum = [(t[:, None] <= t[None, :]).astype(np.float32), (t[:, None] > t[None, :]).astype(np.float32)]
    for hs in LEVEL_HALVES:
        m = (t // (2 * hs)) * 2 * hs + hs
        later = t >= m
        d = np.zeros((C, C), np.float32)
        for i in range(C):
            if later[i]:
                d[i, m[i]:i + 1] = 1.0
            else:
                d[i, i + 1:m[i]] = -1.0
        mats.append(d)
        same = (t[:, None] // (2 * hs)) == (t[None, :] // (2 * hs))
        masks.append((same & later[:, None] & (~later)[None, :]).astype(np.float32))
        gsum.append((same & later[:, None] & (t[None, :] >= t[:, None])).astype(np.float32))
        gsum.append((same & (~later)[:, None] & (t[None, :] < t[:, None])).astype(np.float32))
    return np.concatenate(mats, 0), np.stack(masks, 0), np.concatenate(gsum, 1)


def _split_dot(mat, v):
    hi = v.astype(MXU_DTYPE)
    lo = (v - hi.astype(F32)).astype(MXU_DTYPE)
    r = _dot(mat, jnp.concatenate([hi, lo], axis=1))
    n = v.shape[1]
    return r[:, :n] + r[:, n:]


def _hgrn_gates(qr, fr, lb, mc):
    C = CHUNK
    q, dq_dqr = _silu_and_grad(qr)
    sf = _sigmoid(fr)
    f = lb + (1.0 - lb) * sf
    k = 1.0 - f
    dall = _split_dot(mc, jnp.log(f))
    b = dall[0:C]
    dl = [dall[C * (l + 1):C * (l + 2)] for l in range(len(LEVEL_HALVES))]
    eq = [jnp.exp(jnp.minimum(d, 0.0)) for d in dl]
    ek = [jnp.exp(jnp.minimum(-d, 0.0)) for d in dl]
    return q, dq_dqr, sf, f, k, b, eq, ek


def _hgrn_scores(q, k, eq, ek, masks_ref):
    p = jnp.where(_rows(CHUNK) == lax.broadcasted_iota(jnp.int32, (1, CHUNK), 1),
                  jnp.sum(q * k, axis=-1, keepdims=True), 0.0)
    for l in range(len(LEVEL_HALVES)):
        p = p + masks_ref[l] * _dot(q * eq[l], k * ek[l], NT)
    return p


def _hgrn_fwd(proj, lb_gamma, norm_w, B, HW, name):
    T = proj.shape[0]
    S, H, C = T // B, HW // HEAD_DIM, CHUNK
    NC = S // C
    mc_np, masks_np, _ = _hgrn_tables()
    mc, masks = jnp.asarray(mc_np, MXU_DTYPE), jnp.asarray(masks_np, F32)

    def kern(q_ref, f_ref, i_ref, g_ref, lbg_ref, nw_ref, mc_ref, masks_ref, oraw_ref, o_ref, st_ref):
        g0, g1 = lbg_ref[pl.ds(0, 1), :], lbg_ref[pl.ds(1, 1), :]
        mx = jnp.maximum(g0, g1)
        e0, e1 = jnp.exp(g0 - mx), jnp.exp(g1 - mx)
        lb = e0 / (e0 + e1)
        nw = nw_ref[...]
        mcv = mc_ref[...]

        def body(n, st):
            rows = pl.ds(pl.multiple_of(n * C, C), C)
            st_ref[n] = st
            q, _, _, _, k, b, eq, ek = _hgrn_gates(q_ref[rows, :], f_ref[rows, :], lb, mcv)
            v = i_ref[rows, :]
            o = _dot(q * jnp.exp(b), st, NT) + _dot(_hgrn_scores(q, k, eq, ek, masks_ref), v)
            b_last = b[C - 1:C]
            st = st * jnp.exp(b_last) + _dot(v, k * jnp.exp(b_last - b), TN)
            oraw_ref[rows, :] = o
            r = lax.rsqrt(jnp.mean(o * o, axis=-1, keepdims=True) + EPS)
            gate, _ = _silu_and_grad(g_ref[rows, :])
            o_ref[rows, :] = (o * r * nw * gate).astype(ACT_DTYPE)
            return st

        lax.fori_loop(0, NC, body, jnp.zeros((HEAD_DIM, HEAD_DIM), F32))

    col = lambda off: pl.BlockSpec((S, HEAD_DIM), lambda b, h: (b, off + h))
    return _pcall(
        kern, name=name, grid=(B, H),
        in_specs=[col(0), col(H), col(2 * H), col(3 * H),
                  pl.BlockSpec((2, HEAD_DIM), lambda b, h: (0, h)),
                  pl.BlockSpec((1, HEAD_DIM), lambda b, h: (0, h)),
                  pl.BlockSpec(mc.shape, lambda b, h: (0, 0)),
                  pl.BlockSpec(masks.shape, lambda b, h: (0, 0, 0))],
        out_specs=[col(0), col(0),
                   pl.BlockSpec((None, None, NC, HEAD_DIM, HEAD_DIM), lambda b, h: (b, h, 0, 0, 0))],
        out_shape=[jax.ShapeDtypeStruct((T, HW), F32), jax.ShapeDtypeStruct((T, HW), ACT_DTYPE),
                   jax.ShapeDtypeStruct((B, H, NC, HEAD_DIM, HEAD_DIM), F32)],
        compiler_params=_params(("parallel", "parallel"), 16 * S * HEAD_DIM * 4 + (8 << 20)),
    )(proj, proj, proj, proj, lb_gamma, norm_w, mc, masks)


def _hgrn_bwd(proj, lb_gamma, norm_w, o_raw, states, d_mix, B, HW, name):
    T = proj.shape[0]
    S, H, C = T // B, HW // HEAD_DIM, CHUNK
    NC = S // C
    mc_np, masks_np, gsum_np = _hgrn_tables()
    mc, masks, gsum = jnp.asarray(mc_np, MXU_DTYPE), jnp.asarray(masks_np, F32), jnp.asarray(gsum_np, MXU_DTYPE)
    nl = len(LEVEL_HALVES)

    def kern(q_ref, f_ref, i_ref, g_ref, lbg_ref, nw_ref, mc_ref, masks_ref, gsum_ref, oraw_ref, st_ref, do_ref,
             dq_ref, df_ref, di_ref, dg_ref, dlbg_ref, dnw_ref):
        bi = pl.program_id(1)
        g0, g1 = lbg_ref[pl.ds(0, 1), :], lbg_ref[pl.ds(1, 1), :]
        mx = jnp.maximum(g0, g1)
        e0, e1 = jnp.exp(g0 - mx), jnp.exp(g1 - mx)
        lb = e0 / (e0 + e1)
        nw = nw_ref[...]
        mcv, gsumv = mc_ref[...], gsum_ref[...]

        def body(it, carry):
            dst, dlb, dnw = carry
            n = NC - 1 - it
            rows = pl.ds(pl.multiple_of(n * C, C), C)
            qr, fr, v = q_ref[rows, :], f_ref[rows, :], i_ref[rows, :]
            q, dq_dqr, sf, f, k, b, eq, ek = _hgrn_gates(qr, fr, lb, mcv)
            o = oraw_ref[rows, :]
            dout = do_ref[rows, :].astype(F32)
            gate, dgate = _silu_and_grad(g_ref[rows, :])
            r = lax.rsqrt(jnp.mean(o * o, axis=-1, keepdims=True) + EPS)
            dg_ref[rows, :] = (dout * o * r * nw * dgate).astype(ACT_DTYPE)
            don = dout * gate
            dnw = dnw + jnp.sum(don * o * r, axis=0, keepdims=True)
            gw = don * nw
            do = r * gw - o * (r * r * r) * jnp.mean(gw * o, axis=-1, keepdims=True)
            st_prev = st_ref[n]
            eb = jnp.exp(b)
            b_last = b[C - 1:C]
            ebl = jnp.exp(b_last - b)
            p = _hgrn_scores(q, k, eq, ek, masks_ref)
            dp = _dot(do, v, NT)
            dpd = jnp.sum(do * v, axis=-1, keepdims=True)
            dq_state = _dot(do, st_prev) * eb
            dk_state = _dot(v, dst) * ebl
            dq = dq_state + dpd * k
            dk = dk_state + dpd * q
            pairs = [q * dq_state, k * dk_state]
            for l in range(nl):
                mdp = masks_ref[l] * dp
                dql = _dot(mdp, k * ek[l]) * eq[l]
                dkl = _dot(mdp, q * eq[l], TN) * ek[l]
                dq, dk = dq + dql, dk + dkl
                pairs += [q * dql, k * dkl]
            dv = _dot(p, do, TN) + _dot(k * ebl, dst, NT)
            through = jnp.exp(b_last) * jnp.sum(dst * st_prev, axis=0, keepdims=True)
            dlg = _split_dot(gsumv, jnp.concatenate(pairs, axis=0)) + through
            dst = dst * jnp.exp(b_last) + _dot(do, q * eb, TN)
            dq_ref[rows, :] = (dq * dq_dqr).astype(ACT_DTYPE)
            dfv = dlg / f - dk
            df_ref[rows, :] = (dfv * (1.0 - lb) * sf * (1.0 - sf)).astype(ACT_DTYPE)
            di_ref[rows, :] = dv.astype(ACT_DTYPE)
            dlb = dlb + jnp.sum(dfv * (1.0 - sf), axis=0, keepdims=True)
            return dst, dlb, dnw

        zrow = jnp.zeros((1, HEAD_DIM), F32)
        _, dlb, dnw = lax.fori_loop(0, NC, body, (jnp.zeros((HEAD_DIM, HEAD_DIM), F32), zrow, zrow))

        @pl.when(bi == 0)
        def _():
            dlbg_ref[...] = jnp.zeros_like(dlbg_ref)
            dnw_ref[...] = jnp.zeros_like(dnw_ref)

        dg0 = dlb * lb * (1.0 - lb)
        dlbg_ref[pl.ds(0, 1), :] += dg0
        dlbg_ref[pl.ds(1, 1), :] += -dg0
        dnw_ref[...] += dnw

    col = lambda off: pl.BlockSpec((S, HEAD_DIM), lambda h, b: (b, off + h))
    full = lambda a: pl.BlockSpec(a.shape, lambda h, b: (0,) * a.ndim)
    part = jax.ShapeDtypeStruct((T, HW), ACT_DTYPE)
    return _pcall(
        kern, name=name, grid=(H, B),
        in_specs=[col(0), col(H), col(2 * H), col(3 * H),
                  pl.BlockSpec((2, HEAD_DIM), lambda h, b: (0, h)),
                  pl.BlockSpec((1, HEAD_DIM), lambda h, b: (0, h)),
                  full(mc), full(masks), full(gsum), col(0),
                  pl.BlockSpec((None, None, NC, HEAD_DIM, HEAD_DIM), lambda h, b: (b, h, 0, 0, 0)),
                  col(0)],
        out_specs=[col(0), col(0), col(0), col(0),
                   pl.BlockSpec((2, HEAD_DIM), lambda h, b: (0, h)),
                   pl.BlockSpec((1, HEAD_DIM), lambda h, b: (0, h))],
        out_shape=[part, part, part, part, jax.ShapeDtypeStruct((2, HW), F32), jax.ShapeDtypeStruct((1, HW), F32)],
        compiler_params=_params(("parallel", "arbitrary"), 24 * S * HEAD_DIM * 4 + (8 << 20)),
    )(proj, proj, proj, proj, lb_gamma, norm_w, mc, masks, gsum, o_raw, states, d_mix)


def _lru_gates(xr, cw_ref, cb, wa, ba, wx, bx, lam):
    S = xr.shape[0]
    xb = _conv(xr, cw_ref, cb, LRU_CONV)
    r = _sigmoid(_dot(xb, wa) + ba)
    ig = _sigmoid(_dot(xb, wx) + bx)
    sp = jnp.maximum(-lam, 0.0) + jnp.log(1.0 + jnp.exp(-jnp.abs(lam)))
    la = -LRU_C * r * sp
    a = jnp.exp(la)
    mult = jnp.where(_rows(S) == 0, 1.0, jnp.sqrt(_one_minus_exp(2.0 * la)))
    return xb, r, ig, sp, a, mult


def _scan_rows(a_ref, u_ref, h_ref, reverse):
    S, W = a_ref.shape
    nb = S // 8
    row = _rows(8)

    def body(it, carry):
        blk = nb - 1 - it if reverse else it
        rows = pl.ds(pl.multiple_of(blk * 8, 8), 8)
        a, u = a_ref[rows, :], u_ref[rows, :]
        for d in (1, 2, 4):
            sh = 8 - d if reverse else d
            keep = (row < 8 - d) if reverse else (row >= d)
            u = u + jnp.where(keep, a * pltpu.roll(u, sh, 0), 0.0)
            a = jnp.where(keep, a * pltpu.roll(a, sh, 0), a)
        h = u + a * carry
        h_ref[rows, :] = h
        return h[0:1] if reverse else h[7:8]

    lax.fori_loop(0, nb, body, jnp.zeros((1, W), F32))


def _lru_fwd(proj, cw, cb, wa, ba, wx, bx, lam, B, HW, LW, name):
    T = proj.shape[0]
    S, NB = T // B, LW // HEAD_DIM
    xoff, yoff = 4 * HW // HEAD_DIM, 4 * HW // HEAD_DIM + NB

    def kern(x_ref, y_ref, cw_ref, cb_ref, wa_ref, ba_ref, wx_ref, bx_ref, lam_ref, h_ref, z_ref, a_s, u_s):
        xb, _, ig, _, a, mult = _lru_gates(x_ref[...], cw_ref, cb_ref[...], wa_ref[...], ba_ref[...],
                                           wx_ref[...], bx_ref[...], lam_ref[...])
        a_s[...] = a
        u_s[...] = xb * ig * mult
        _scan_rows(a_s, u_s, h_ref, False)
        gy, _ = _gelu_and_grad(y_ref[...])
        z_ref[...] = h_ref[...] * gy

    blk = lambda off: pl.BlockSpec((S, HEAD_DIM), lambda b, n: (b, off + n))
    vec = pl.BlockSpec((1, HEAD_DIM), lambda b, n: (0, n))
    mat = pl.BlockSpec((None, HEAD_DIM, HEAD_DIM), lambda b, n: (n, 0, 0))
    return _pcall(
        kern, name=name, grid=(B, NB),
        in_specs=[blk(xoff), blk(yoff), pl.BlockSpec((LRU_CONV, HEAD_DIM), lambda b, n: (0, n)),
                  vec, mat, vec, mat, vec, vec],
        out_specs=[blk(0), blk(0)],
        out_shape=[jax.ShapeDtypeStruct((T, LW), F32), jax.ShapeDtypeStruct((T, LW), F32)],
        scratch_shapes=[pltpu.VMEM((S, HEAD_DIM), F32), pltpu.VMEM((S, HEAD_DIM), F32)],
        compiler_params=_params(("parallel", "parallel"), 24 * S * HEAD_DIM * 4),
    )(proj, proj, cw, cb, wa, ba, wx, bx, lam)


def _lru_bwd(proj, cw, cb, wa, ba, wx, bx, lam, h, dz, B, HW, LW, name):
    T = proj.shape[0]
    S, NB = T // B, LW // HEAD_DIM
    xoff, yoff = 4 * HW // HEAD_DIM, 4 * HW // HEAD_DIM + NB

    def kern(x_ref, y_ref, cw_ref, cb_ref, wa_ref, ba_ref, wx_ref, bx_ref, lam_ref, h_ref, dz_ref,
             dx_ref, dy_ref, dwa_ref, dwx_ref, dba_ref, dbx_ref, dlam_ref, dcw_ref, dcb_ref, a_s, u_s, dh_s):
        bi = pl.program_id(1)
        row = _rows(S)
        xr, lam = x_ref[...], lam_ref[...]
        wa, wx = wa_ref[...], wx_ref[...]
        xb, r, ig, sp, a, mult = _lru_gates(xr, cw_ref, cb_ref[...], wa, ba_ref[...], wx, bx_ref[...], lam)
        hv, dz = h_ref[...], dz_ref[...]
        gy, dgy = _gelu_and_grad(y_ref[...])
        dy_ref[...] = (dz * hv * dgy).astype(ACT_DTYPE)
        a_s[...] = jnp.where(row < S - 1, pltpu.roll(a, S - 1, 0), 0.0)
        u_s[...] = dz * gy
        _scan_rows(a_s, u_s, dh_s, True)
        dh = dh_s[...]
        h_prev = jnp.where(row >= 1, pltpu.roll(hv, 1, 0), 0.0)
        d_ig = dh * xb * mult
        d_mult = jnp.where(row == 0, 0.0, dh * xb * ig)
        dxb = dh * ig * mult
        dla = dh * h_prev * a - d_mult * (a * a) / mult
        dpre_r = dla * (-LRU_C * sp) * r * (1.0 - r)
        dpre_i = d_ig * ig * (1.0 - ig)
        dxb = dxb + _dot(dpre_r, wa, NT) + _dot(dpre_i, wx, NT)
        dx_ref[...] = _conv_t(dxb, cw_ref, LRU_CONV).astype(ACT_DTYPE)

        @pl.when(bi == 0)
        def _():
            for ref in (dwa_ref, dwx_ref, dba_ref, dbx_ref, dlam_ref, dcw_ref, dcb_ref):
                ref[...] = jnp.zeros_like(ref)

        dwa_ref[...] += _dot(xb, dpre_r, TN)
        dwx_ref[...] += _dot(xb, dpre_i, TN)
        dba_ref[...] += jnp.sum(dpre_r, axis=0, keepdims=True)
        dbx_ref[...] += jnp.sum(dpre_i, axis=0, keepdims=True)
        dsp = jnp.sum(dla * (-LRU_C) * r, axis=0, keepdims=True)
        dlam_ref[...] += -dsp * _sigmoid(-lam)
        for j, rj in enumerate(_conv_dw_rows(xr, dxb, LRU_CONV)):
            dcw_ref[pl.ds(j, 1), :] += rj
        dcb_ref[...] += jnp.sum(dxb, axis=0, keepdims=True)

    blk = lambda off: pl.BlockSpec((S, HEAD_DIM), lambda n, b: (b, off + n))
    vec = pl.BlockSpec((1, HEAD_DIM), lambda n, b: (0, n))
    mat = pl.BlockSpec((None, HEAD_DIM, HEAD_DIM), lambda n, b: (n, 0, 0))
    cwb = pl.BlockSpec((LRU_CONV, HEAD_DIM), lambda n, b: (0, n))
    part = jax.ShapeDtypeStruct((T, LW), ACT_DTYPE)
    vshape = jax.ShapeDtypeStruct((1, LW), F32)
    mshape = jax.ShapeDtypeStruct((NB, HEAD_DIM, HEAD_DIM), F32)
    return _pcall(
        kern, name=name, grid=(NB, B),
        in_specs=[blk(xoff), blk(yoff), cwb, vec, mat, vec, mat, vec, vec, blk(0), blk(0)],
        out_specs=[blk(0), blk(0), mat, mat, vec, vec, vec, cwb, vec],
        out_shape=[part, part, mshape, mshape, vshape, vshape, vshape,
                   jax.ShapeDtypeStruct((LRU_CONV, LW), F32), vshape],
        scratch_shapes=[pltpu.VMEM((S, HEAD_DIM), F32)] * 3,
        compiler_params=_params(("parallel", "arbitrary"), 40 * S * HEAD_DIM * 4),
    )(proj, proj, cw, cb, wa, ba, wx, bx, lam, h, dz)


def _pos():
    return lax.axis_index("x"), lax.axis_index("y"), lax.axis_index("c")


def _other_chips(x, y):
    return [(1 - x, y), (x, 1 - y), (1 - x, 1 - y)]


def _remote(src, dst, send_sems, recv_sems, k, to):
    return pltpu.make_async_remote_copy(src_ref=src, dst_ref=dst, send_sem=send_sems.at[k],
                                        recv_sem=recv_sems.at[k], device_id=to, device_id_type=MESH)


HBM_BLK = pl.BlockSpec(memory_space=pltpu.HBM)
SEM_BLK = pl.BlockSpec(memory_space=pltpu.SEMAPHORE)
VMEM_BLK = pl.BlockSpec(memory_space=pltpu.VMEM)
DATAFLOW = pltpu.SideEffectType.DATAFLOW_SIDE_EFFECTING
TOKEN = jax.ShapeDtypeStruct((8, V7X_LANES), F32)


def _in_hbm(a):
    return pltpu.with_memory_space_constraint(a, pltpu.HBM)


def _gather_win(o_ref, R, C, col_sharded):
    Rh = R // 2

    def win(j, h=None):
        if col_sharded:
            rows = pl.ds(0, R) if h is None else pl.ds(h * Rh, Rh)
            return o_ref.at[rows, pl.ds(j * C, C)]
        return o_ref.at[pl.ds(j * R, R) if h is None else pl.ds(j * R + h * Rh, Rh), :]

    return win


def _cast_into_window(w, col_sharded, name):
    R, C = w.shape
    tr = _tile(R, 256, 16)
    nr = R // tr
    full = (R, 4 * C) if col_sharded else (4 * R, C)
    j = (2 * lax.axis_index("x") + lax.axis_index("y")).astype(jnp.int32).reshape(1)

    def kern(j_ref, w_ref, o_ref):
        o_ref[...] = w_ref[...].astype(ACT_DTYPE)

    out_map = (lambda i, jr: (i, jr[0])) if col_sharded else (lambda i, jr: (jr[0] * nr + i, 0))
    grid_spec = pltpu.PrefetchScalarGridSpec(
        num_scalar_prefetch=1, grid=(nr,),
        in_specs=[pl.BlockSpec((tr, C), lambda i, jr: (i, 0))], out_specs=pl.BlockSpec((tr, C), out_map))
    return _pcall(kern, name=name, grid_spec=grid_spec, out_shape=jax.ShapeDtypeStruct(full, ACT_DTYPE),
                  compiler_params=_params(("parallel",), 6 * tr * C * 4))(j, w)


def _gather_start(land, shard_shape, col_sharded, token, name):
    R, C = shard_shape

    def body(land_ref, tok_ref, send_sems, recv_sems, land_thru, tok_out):
        x, y, c = _pos()
        w = _gather_win(land_ref, R, C, col_sharded)(2 * x + y, c)
        for k, (cx, cy) in enumerate(_other_chips(x, y)):
            _remote(w, w, send_sems, recv_sems, k, (cx, cy, c)).start()
        tok_out[...] = tok_ref[...]

    return _pcall(
        body, name=name,
        out_shape=(pltpu.SemaphoreType.DMA((3,)), pltpu.SemaphoreType.DMA((3,)),
                   pltpu.HBM(land.shape, land.dtype), TOKEN),
        in_specs=(HBM_BLK, VMEM_BLK), out_specs=(SEM_BLK, SEM_BLK, HBM_BLK, VMEM_BLK),
        input_output_aliases={0: 2},
        compiler_params=pltpu.CompilerParams(has_side_effects=DATAFLOW),
    )(_in_hbm(land), token)


def _gather_wait(started, shard_shape, after, col_sharded, name):
    send_sems, recv_sems, land_thru, _ = started
    R, C = shard_shape

    def body(land_ref, send_sems, recv_sems, after_ref, got_ref):
        x, y, c = _pos()
        win = _gather_win(land_ref, R, C, col_sharded)
        for k, (cx, cy) in enumerate(_other_chips(x, y)):
            cp = _remote(win(2 * x + y, c), win(2 * cx + cy, c), send_sems, recv_sems, k, (cx, cy, c))
            cp.wait_send()
            cp.wait_recv()

    return _pcall(
        body, name=name, out_shape=pltpu.HBM(land_thru.shape, land_thru.dtype),
        in_specs=(HBM_BLK, SEM_BLK, SEM_BLK, HBM_SPEC), out_specs=HBM_BLK, input_output_aliases={0: 0},
        compiler_params=pltpu.CompilerParams(has_side_effects=DATAFLOW),
    )(land_thru, send_sems, recv_sems, after)


def _gather_pass_on(landed, shard_shape, col_sharded, name):
    R, C = shard_shape

    def body(in_ref, o_ref, send_sems, recv_sems):
        x, y, c = _pos()
        src, dst = _gather_win(in_ref, R, C, col_sharded), _gather_win(o_ref, R, C, col_sharded)
        chips = _other_chips(x, y)
        passed = [_remote(src(2 * cx + cy, c), dst(2 * cx + cy, c), send_sems, recv_sems, k, (x, y, 1 - c))
                  for k, (cx, cy) in enumerate(chips)]
        for cp in passed:
            cp.start()
        for k, (cx, cy) in enumerate(chips):
            w = dst(2 * cx + cy, 1 - c)
            _remote(w, w, send_sems, recv_sems, k, (x, y, c)).wait_recv()
        for cp in passed:
            cp.wait_send()

    return _pcall(body, name=name, in_specs=[HBM_SPEC], out_specs=HBM_SPEC,
                  out_shape=jax.ShapeDtypeStruct(landed.shape, landed.dtype), input_output_aliases={0: 0},
                  scratch_shapes=[pltpu.SemaphoreType.DMA((3,)), pltpu.SemaphoreType.DMA((3,))])(landed)


def _pair_exchange(g4, name):
    J, _, Rh, W = g4.shape

    def body(g_ref, p_ref, send_sems, recv_sems):
        x, y, c = _pos()
        cp = _remote(g_ref.at[pl.ds(0, J), 1 - c], p_ref, send_sems, recv_sems, 0, (x, y, 1 - c))
        cp.start()
        cp.wait()

    return _pcall(body, name=name, in_specs=[HBM_SPEC], out_specs=HBM_SPEC,
                  out_shape=jax.ShapeDtypeStruct((J, Rh, W), g4.dtype),
                  scratch_shapes=[pltpu.SemaphoreType.DMA((1,)), pltpu.SemaphoreType.DMA((1,))])(g4)


def _pair_add(g4, p, name):
    J, _, Rh, W = g4.shape
    tr = _tile(Rh, 256, 16)
    tw = _tile(W, 2048, V7X_LANES)
    c = lax.axis_index("c").astype(jnp.int32).reshape(1)

    def kern(c_ref, g_ref, p_ref, o_ref):
        o_ref[...] = (g_ref[...].astype(F32) + p_ref[...].astype(F32)).astype(ACT_DTYPE)

    grid_spec = pltpu.PrefetchScalarGridSpec(
        num_scalar_prefetch=1, grid=(J, Rh // tr, W // tw),
        in_specs=[pl.BlockSpec((None, None, tr, tw), lambda j, i, w, cr: (j, cr[0], i, w)),
                  pl.BlockSpec((None, tr, tw), lambda j, i, w, cr: (j, i, w))],
        out_specs=pl.BlockSpec((None, tr, tw), lambda j, i, w, cr: (j, i, w)))
    return _pcall(kern, name=name, grid_spec=grid_spec,
                  out_shape=jax.ShapeDtypeStruct((J, Rh, W), ACT_DTYPE),
                  compiler_params=_params(("parallel", "parallel", "parallel"), 12 * tr * tw * 4))(c, g4, p)


def _chip_sum_piece(cs_ref, C, col_sharded):
    return lambda j: cs_ref.at[0, :, pl.ds(j * C, C)] if col_sharded else cs_ref.at[j]


def _pair_add_own(g4, p, col_sharded, name):
    J, _, Rh, W = g4.shape
    C = W // 4 if col_sharded else W
    tr = _tile(Rh, 256, 16)
    tw = _tile(C, 8192, V7X_LANES)
    nw = C // tw
    x, y, c = _pos()
    chip = (2 * x + y).astype(jnp.int32).reshape(1)
    core = c.astype(jnp.int32).reshape(1)

    def kern(j_ref, c_ref, g_ref, p_ref, o_ref):
        o_ref[...] = (g_ref[...].astype(F32) + p_ref[...].astype(F32)).astype(ACT_DTYPE)

    if col_sharded:
        g_map = lambda i, w, j, cc: (0, cc[0], i, j[0] * nw + w)
        p_map = lambda i, w, j, cc: (0, i, j[0] * nw + w)
    else:
        g_map = lambda i, w, j, cc: (j[0], cc[0], i, w)
        p_map = lambda i, w, j, cc: (j[0], i, w)
    grid_spec = pltpu.PrefetchScalarGridSpec(
        num_scalar_prefetch=2, grid=(Rh // tr, nw),
        in_specs=[pl.BlockSpec((None, None, tr, tw), g_map), pl.BlockSpec((None, tr, tw), p_map)],
        out_specs=pl.BlockSpec((None, None, tr, tw), lambda i, w, j, cc: (j[0], cc[0], i, w)))
    return _pcall(kern, name=name, grid_spec=grid_spec,
                  out_shape=jax.ShapeDtypeStruct((4, 2, Rh, C), ACT_DTYPE),
                  compiler_params=_params(("parallel", "parallel"), 12 * tr * tw * 4))(chip, core, g4, p)


def _scatter_start(cs, slots, col_sharded, token, name):
    C = slots.shape[3]

    def body(cs_ref, land_ref, tok_ref, send_sems, recv_sems, cs_thru, land_thru, tok_out):
        x, y, c = _pos()
        piece = _chip_sum_piece(cs_ref, C, col_sharded)
        for k, (cx, cy) in enumerate(_other_chips(x, y)):
            _remote(piece(2 * cx + cy), land_ref.at[2 * x + y, c], send_sems, recv_sems, k, (cx, cy, c)).start()
        tok_out[...] = tok_ref[...]

    return _pcall(
        body, name=name,
        out_shape=(pltpu.SemaphoreType.DMA((3,)), pltpu.SemaphoreType.DMA((3,)),
                   pltpu.HBM(cs.shape, cs.dtype), pltpu.HBM(slots.shape, cs.dtype), TOKEN),
        in_specs=(HBM_BLK, HBM_BLK, VMEM_BLK), out_specs=(SEM_BLK, SEM_BLK, HBM_BLK, HBM_BLK, VMEM_BLK),
        input_output_aliases={0: 2, 1: 3},
        compiler_params=pltpu.CompilerParams(has_side_effects=DATAFLOW),
    )(_in_hbm(cs), _in_hbm(slots), token)


def _scatter_wait(started, after, col_sharded, name):
    send_sems, recv_sems, cs_thru, land_thru, _ = started
    C = land_thru.shape[3]

    def body(cs_ref, land_ref, send_sems, recv_sems, after_ref, cs_dead, got_ref):
        x, y, c = _pos()
        piece = _chip_sum_piece(cs_ref, C, col_sharded)
        for k, (cx, cy) in enumerate(_other_chips(x, y)):
            cp = _remote(piece(2 * cx + cy), land_ref.at[2 * cx + cy, c], send_sems, recv_sems, k, (cx, cy, c))
            cp.wait_send()
            cp.wait_recv()

    return _pcall(
        body, name=name,
        out_shape=(pltpu.HBM(cs_thru.shape, cs_thru.dtype), pltpu.HBM(land_thru.shape, land_thru.dtype)),
        in_specs=(HBM_BLK, HBM_BLK, SEM_BLK, SEM_BLK, HBM_SPEC), out_specs=(HBM_BLK, HBM_BLK),
        input_output_aliases={0: 0, 1: 1},
        compiler_params=pltpu.CompilerParams(has_side_effects=DATAFLOW),
    )(cs_thru, land_thru, send_sems, recv_sems, after)


def _scatter_pass_on(landed, name):
    def body(in_ref, o_ref, send_sems, recv_sems):
        x, y, c = _pos()
        sends = [_remote(in_ref.at[i, c], o_ref.at[i, c], send_sems, recv_sems, i, (x, y, 1 - c)) for i in range(4)]
        for cp in sends:
            cp.start()
        for i in range(4):
            w = o_ref.at[i, 1 - c]
            _remote(w, w, send_sems, recv_sems, i, (x, y, c)).wait_recv()
        for cp in sends:
            cp.wait_send()

    return _pcall(body, name=name, in_specs=[HBM_SPEC], out_specs=HBM_SPEC,
                  out_shape=jax.ShapeDtypeStruct(landed.shape, landed.dtype), input_output_aliases={0: 0},
                  scratch_shapes=[pltpu.SemaphoreType.DMA((4,)), pltpu.SemaphoreType.DMA((4,))])(landed)


def _gather_small(buf, name):
    rows = buf.shape[0]

    def body(b_ref, o_ref, send_sems, recv_sems):
        x, y, c = _pos()
        jme = 2 * x + y
        chips = _other_chips(x, y)
        o_ref[jme] = b_ref[...]
        sends = [_remote(b_ref, o_ref.at[jme], send_sems, recv_sems, k, (cx, cy, c))
                 for k, (cx, cy) in enumerate(chips)]
        for cp in sends:
            cp.start()
        for k, (cx, cy) in enumerate(chips):
            w = o_ref.at[2 * cx + cy]
            _remote(w, w, send_sems, recv_sems, k, (x, y, c)).wait_recv()
        for cp in sends:
            cp.wait_send()

    vm = pl.BlockSpec(memory_space=pltpu.VMEM)
    return _pcall(body, name=name, in_specs=[vm], out_specs=vm,
                  out_shape=jax.ShapeDtypeStruct((4, rows, V7X_LANES), buf.dtype),
                  scratch_shapes=[pltpu.SemaphoreType.DMA((3,)), pltpu.SemaphoreType.DMA((3,))],
                  compiler_params=_params(None, 16 * rows * V7X_LANES * 4))(buf)


def _allreduce_small(buf, name):
    rows = buf.shape[0]

    def body(b_ref, o_ref, slots, send_sems, recv_sems):
        x, y, c = _pos()
        me = 4 * x + 2 * y + c
        slots[me] = b_ref[...]
        sends = []
        for k in range(1, 8):
            kx, ky, kc = (k >> 2) & 1, (k >> 1) & 1, k & 1
            to = (x ^ kx, y ^ ky, c ^ kc)
            sends.append(_remote(b_ref, slots.at[me], send_sems, recv_sems, k - 1, to))
        for cp in sends:
            cp.start()
        for k in range(1, 8):
            kx, ky, kc = (k >> 2) & 1, (k >> 1) & 1, k & 1
            w = slots.at[4 * (x ^ kx) + 2 * (y ^ ky) + (c ^ kc)]
            _remote(w, w, send_sems, recv_sems, k - 1, (x, y, c)).wait_recv()
        for cp in sends:
            cp.wait_send()
        acc = slots[0]
        for d in range(1, 8):
            acc = acc + slots[d]
        o_ref[...] = acc

    vm = pl.BlockSpec(memory_space=pltpu.VMEM)
    return _pcall(body, name=name, in_specs=[vm], out_specs=vm,
                  out_shape=jax.ShapeDtypeStruct(buf.shape, buf.dtype),
                  scratch_shapes=[pltpu.VMEM((8, rows, V7X_LANES), buf.dtype),
                                  pltpu.SemaphoreType.DMA((7,)), pltpu.SemaphoreType.DMA((7,))],
                  compiler_params=_params(None, 14 * rows * V7X_LANES * 4))(buf)


def _adamw_math(w, g, m, v):
    m = ADAM_B1 * m + (1.0 - ADAM_B1) * g
    v = ADAM_B2 * v + (1.0 - ADAM_B2) * (g * g)
    m_hat = m / (1.0 - ADAM_B1 ** ADAM_STEP)
    v_hat = v / (1.0 - ADAM_B2 ** ADAM_STEP)
    delta = -ADAM_LR * (m_hat / (jnp.sqrt(v_hat) + ADAM_EPS) + ADAM_WD * w)
    return delta, m, v


def _adamw_big(w, m, v, slots, name):
    R, C = w.shape
    tr = _tile(R, 32, 16)

    def kern(w_ref, m_ref, v_ref, s_ref, g_ref, d_ref, mo_ref, vo_ref):
        g = s_ref[0].astype(F32)
        for i in range(1, 4):
            g = g + s_ref[i].astype(F32)
        d, mn, vn = _adamw_math(w_ref[...], g, m_ref[...], v_ref[...])
        g_ref[...], d_ref[...], mo_ref[...], vo_ref[...] = g, d, mn, vn

    row = pl.BlockSpec((tr, C), lambda i: (i, 0))
    shp = jax.ShapeDtypeStruct((R, C), F32)
    return _pcall(kern, name=name, grid=(R // tr,),
                  in_specs=[row, row, row, pl.BlockSpec((4, tr, C), lambda i: (0, i, 0))],
                  out_specs=[row] * 4, out_shape=[shp] * 4,
                  compiler_params=_params(("parallel",), 36 * tr * C * 4))(w, m, v, slots)


def _adamw_small(w, g, m, v, name):
    def kern(w_ref, g_ref, m_ref, v_ref, d_ref, mo_ref, vo_ref):
        d_ref[...], mo_ref[...], vo_ref[...] = _adamw_math(w_ref[...], g_ref[...], m_ref[...], v_ref[...])

    vm = pl.BlockSpec(memory_space=pltpu.VMEM)
    shp = jax.ShapeDtypeStruct(w.shape, F32)
    return _pcall(kern, name=name, in_specs=[vm] * 4, out_specs=[vm] * 3, out_shape=[shp] * 3,
                  compiler_params=_params(None, 10 * w.size * 4))(w, g, m, v)


def _pack(arrs):
    flat = jnp.concatenate([a.reshape(-1).astype(F32) for a in arrs])
    n = flat.shape[0]
    rows = -(-n // (8 * V7X_LANES)) * 8
    return jnp.pad(flat, (0, rows * V7X_LANES - n)).reshape(rows, V7X_LANES)


def _unpack(buf, shapes):
    flat = buf.reshape(-1)
    out, off = [], 0
    for s in shapes:
        n = int(np.prod(s))
        out.append(flat[off:off + n].reshape(s))
        off += n
    return out


def _reduce_start(gfull, col_sharded, shard_shape, token, tag):
    R, C = shard_shape
    if col_sharded:
        g4 = gfull.reshape(1, 2, R // 2, 4 * C)
    else:
        g4 = gfull.reshape(4, 2, R // 2, C)
    p = _pair_exchange(g4, "pair_exchange_" + tag)
    cs = _pair_add(g4, p, "pair_add_" + tag)
    slots = _pair_add_own(g4, p, col_sharded, "pair_add_own_" + tag)
    return _scatter_start(cs, slots, col_sharded, token, "scatter_start_" + tag)


def _reduce_finish(started, after, col_sharded, w, m, v, tag):
    R, C = w.shape
    _, landed = _scatter_wait(started, after, col_sharded, "scatter_wait_" + tag)
    slots = _scatter_pass_on(landed, "scatter_pass_on_" + tag)
    return _adamw_big(w, m, v, slots.reshape(4, R, C), "adamw_" + tag)


def kernel(x, ln1_w, w_in, lb_gamma, hg_norm_w, lru_conv_w, lru_conv_b, lru_wa, lru_ba, lru_wx, lru_bx, lru_lambda, lru_norm_w, w_out, ln2_w, ffn_w_up, ffn_conv_w, ffn_conv_b, ffn_w_down, final_norm_w, loss_target, m_ln1_w, m_w_in, m_lb_gamma, m_hg_norm_w, m_lru_conv_w, m_lru_conv_b, m_lru_wa, m_lru_ba, m_lru_wx, m_lru_bx, m_lru_lambda, m_lru_norm_w, m_w_out, m_ln2_w, m_ffn_w_up, m_ffn_conv_w, m_ffn_conv_b, m_ffn_w_down, m_final_norm_w, v_ln1_w, v_w_in, v_lb_gamma, v_hg_norm_w, v_lru_conv_w, v_lru_conv_b, v_lru_wa, v_lru_ba, v_lru_wx, v_lru_bx, v_lru_lambda, v_lru_norm_w, v_w_out, v_ln2_w, v_ffn_w_up, v_ffn_conv_w, v_ffn_conv_b, v_ffn_w_down, v_final_norm_w):
    B, S, D = x.shape
    T = B * S
    HW = lb_gamma.shape[1]
    LW = lru_conv_b.shape[1]
    assert S % CHUNK == 0 and HW % HEAD_DIM == 0 and lru_wa.shape[2] == HEAD_DIM
    x2 = x.reshape(T, D)
    tgt = loss_target.reshape(T, D)
    jchip = 2 * lax.axis_index("x") + lax.axis_index("y")

    masters = dict(w_in=w_in[0], w_out=w_out[0], w_up=ffn_w_up[0], w_down=ffn_w_down[0])
    col_of = dict(w_in=True, w_out=False, w_up=True, w_down=False)
    started, token = {}, jnp.zeros(TOKEN.shape, F32)
    for n in ("w_in", "w_out", "w_up", "w_down"):
        land = _cast_into_window(masters[n], col_of[n], "cast_" + n)
        started[n] = _gather_start(land, masters[n].shape, col_of[n], token, "gather_start_" + n)
        token = started[n][3]

    def gathered(n, after):
        landed = _gather_wait(started[n], masters[n].shape, after, col_of[n], "gather_wait_" + n)
        return _gather_pass_on(landed, masters[n].shape, col_of[n], "gather_pass_on_" + n)

    W_in = gathered("w_in", token)
    conv_shapes = [lru_conv_w[0].shape, ffn_conv_w[0].shape]
    convs = _gather_small(_pack([lru_conv_w[0], ffn_conv_w[0]]), "gather_conv_w")
    per_chip = [_unpack(convs[j], conv_shapes) for j in range(4)]
    lcw = jnp.concatenate([pc[0] for pc in per_chip], axis=1)
    fcw = jnp.concatenate([pc[1] for pc in per_chip], axis=1)

    hn1 = _rms_fwd(x2, ln1_w, "rms1")
    proj = _matmul(hn1, W_in, "NN", F32, 1024, 1024, 1024, name="mm_proj")
    o_raw, o_hg, states = _hgrn_fwd(proj, lb_gamma, hg_norm_w, B, HW, "hgrn_fwd")
    h_lru, z = _lru_fwd(proj, lcw, lru_conv_b, lru_wa[0], lru_ba, lru_wx[0], lru_bx, lru_lambda, B, HW, LW, "lru_fwd")
    o_lru = _rms_fwd(z, lru_norm_w, "rms_lru")
    mix = jnp.concatenate([o_hg, o_lru], axis=1)
    W_out = gathered("w_out", mix)
    h1 = _matmul(mix, W_out, "NN", F32, 1024, 1024, 1024, add=x2, name="mm_out")
    hn2 = _rms_fwd(h1, ln2_w, "rms2")
    W_up = gathered("w_up", hn2)
    up = _matmul(hn2, W_up, "NN", F32, 2048, 512, 1024, name="mm_up")
    act = _ffn_act(up, fcw, ffn_conv_b, B, "ffn_act")
    W_down = gathered("w_down", act)
    h2 = _matmul(act, W_down, "NN", F32, 2048, 1024, 256, add=h1, name="mm_down")

    dh2, dh2a, d_final_w, loss_part = _loss_bwd(h2, tgt, final_norm_w.reshape(1, D), "loss_bwd")
    g_down = _matmul(act, dh2a, "TN", ACT_DTYPE, 256, 4096, 1024, name="mm_g_down")
    red_down = _reduce_start(g_down, False, ffn_w_down[0].shape, token, "w_down")
    d_act = _matmul(dh2a, W_down, "NT", ACT_DTYPE, 512, 5504, 512, after=red_down[4], name="mm_d_act")
    d_up, d_fcw, d_fcb = _ffn_act_bwd(up, fcw, ffn_conv_b, d_act, B, "ffn_act_bwd")
    g_up = _matmul(hn2, d_up, "TN", ACT_DTYPE, 2048, 512, 1024, name="mm_g_up")
    red_up = _reduce_start(g_up, True, ffn_w_up[0].shape, red_down[4], "w_up")
    d_hn2 = _matmul(d_up, W_up, "NT", F32, 2048, 1024, 512, after=red_up[4], name="mm_d_hn2")
    dh1, dh1a, d_ln2 = _rms_bwd(h1, ln2_w, d_hn2, 0, dh2, True, "rms2_bwd")
    g_out = _matmul(mix, dh1a, "TN", ACT_DTYPE, 1024, 1024, 1024, name="mm_g_out")
    red_out = _reduce_start(g_out, False, w_out[0].shape, red_up[4], "w_out")
    d_mix = _matmul(dh1a, W_out, "NT", F32, 1024, 1024, 1024, after=red_out[4], name="mm_d_mix")
    dz, d_lru_norm = _rms_bwd(z, lru_norm_w, d_mix, HW // LW, None, False, "rms_lru_bwd")
    (d_xr, d_yr, d_wa, d_wx, d_ba, d_bx, d_lam, d_lcw, d_lcb) = _lru_bwd(
        proj, lcw, lru_conv_b, lru_wa[0], lru_ba, lru_wx[0], lru_bx, lru_lambda, h_lru, dz, B, HW, LW, "lru_bwd")
    d_q, d_f, d_i, d_g, d_lbg, d_hgw = _hgrn_bwd(proj, lb_gamma, hg_norm_w, o_raw, states, d_mix, B, HW, "hgrn_bwd")
    d_proj = jnp.concatenate([d_q, d_f, d_i, d_g, d_xr, d_yr], axis=1)
    g_in = _matmul(hn1, d_proj, "TN", ACT_DTYPE, 1024, 1024, 1024, name="mm_g_in")
    red_in = _reduce_start(g_in, True, w_in[0].shape, red_out[4], "w_in")
    d_hn1 = _matmul(d_proj, W_in, "NT", F32, 1024, 1024, 1024, after=red_in[4], name="mm_d_hn1")
    dx, d_ln1 = _rms_bwd(x2, ln1_w, d_hn1, 0, dh1, False, "rms1_bwd")

    big = {}
    big["ffn_w_down"] = _reduce_finish(red_down, dx, False, ffn_w_down[0], m_ffn_w_down[0], v_ffn_w_down[0], "w_down")
    big["ffn_w_up"] = _reduce_finish(red_up, big["ffn_w_down"][1], True, ffn_w_up[0], m_ffn_w_up[0], v_ffn_w_up[0], "w_up")
    big["w_out"] = _reduce_finish(red_out, big["ffn_w_up"][1], False, w_out[0], m_w_out[0], v_w_out[0], "w_out")
    big["w_in"] = _reduce_finish(red_in, big["w_out"][1], True, w_in[0], m_w_in[0], v_w_in[0], "w_in")

    small_names = ["ln1_w", "lb_gamma", "hg_norm_w", "lru_conv_w", "lru_conv_b", "lru_wa", "lru_ba", "lru_wx",
                   "lru_bx", "lru_lambda", "lru_norm_w", "ln2_w", "ffn_conv_w", "ffn_conv_b", "final_norm_w"]
    small_grads = [d_ln1, d_lbg, d_hgw, d_lcw, d_lcb, d_wa, d_ba, d_wx, d_bx, d_lam, d_lru_norm, d_ln2,
                   d_fcw, d_fcb, d_final_w]
    red = _allreduce_small(_pack([loss_part[0:1, 0:1]] + small_grads), "allreduce_small")
    red = _unpack(red, [(1, 1)] + [g.shape for g in small_grads])
    loss = red[0].reshape(())
    gs = dict(zip(small_names, red[1:]))
    nlc, nfc = lru_conv_w.shape[2], ffn_conv_w.shape[2]
    gs["lru_conv_w"] = lax.dynamic_slice_in_dim(gs["lru_conv_w"], jchip * nlc, nlc, axis=1)
    gs["ffn_conv_w"] = lax.dynamic_slice_in_dim(gs["ffn_conv_w"], jchip * nfc, nfc, axis=1)
    args = dict(ln1_w=(ln1_w, m_ln1_w, v_ln1_w), lb_gamma=(lb_gamma, m_lb_gamma, v_lb_gamma),
                hg_norm_w=(hg_norm_w, m_hg_norm_w, v_hg_norm_w), lru_conv_w=(lru_conv_w, m_lru_conv_w, v_lru_conv_w),
                lru_conv_b=(lru_conv_b, m_lru_conv_b, v_lru_conv_b), lru_wa=(lru_wa, m_lru_wa, v_lru_wa),
                lru_ba=(lru_ba, m_lru_ba, v_lru_ba), lru_wx=(lru_wx, m_lru_wx, v_lru_wx),
                lru_bx=(lru_bx, m_lru_bx, v_lru_bx), lru_lambda=(lru_lambda, m_lru_lambda, v_lru_lambda),
                lru_norm_w=(lru_norm_w, m_lru_norm_w, v_lru_norm_w), ln2_w=(ln2_w, m_ln2_w, v_ln2_w),
                ffn_conv_w=(ffn_conv_w, m_ffn_conv_w, v_ffn_conv_w), ffn_conv_b=(ffn_conv_b, m_ffn_conv_b, v_ffn_conv_b),
                final_norm_w=(final_norm_w, m_final_norm_w, v_final_norm_w))
    shapes = [args[n][0].shape for n in small_names]
    upd = _adamw_small(_pack([args[n][0] for n in small_names]), _pack([gs[n] for n in small_names]),
                       _pack([args[n][1] for n in small_names]), _pack([args[n][2] for n in small_names]), "adamw_small")
    s_delta, s_m, s_v = (dict(zip(small_names, _unpack(u, shapes))) for u in upd)

    order = ["ln1_w", "w_in", "lb_gamma", "hg_norm_w", "lru_conv_w", "lru_conv_b", "lru_wa", "lru_ba", "lru_wx",
             "lru_bx", "lru_lambda", "lru_norm_w", "w_out", "ln2_w", "ffn_w_up", "ffn_conv_w", "ffn_conv_b",
             "ffn_w_down", "final_norm_w"]
    full_shape = dict(w_in=w_in.shape, w_out=w_out.shape, ffn_w_up=ffn_w_up.shape, ffn_w_down=ffn_w_down.shape)
    grads, deltas, new_m, new_v = [], [], [], []
    for n in order:
        if n in big:
            g, d, mn, vn = (t.reshape(full_shape[n]) for t in big[n])
        else:
            g, d, mn, vn = gs[n].reshape(args[n][0].shape), s_delta[n], s_m[n], s_v[n]
        grads.append(g), deltas.append(d), new_m.append(mn), new_v.append(vn)
    return (loss, dx.reshape(B, S, D), *grads, *deltas, *new_m, *new_v)
```

```python
import functools
import math

import numpy as np
import jax
import jax.numpy as jnp
from jax import lax
from jax.experimental import pallas as pl
from jax.experimental.pallas import tpu as pltpu

F32 = jnp.float32
MXU_DTYPE = jnp.bfloat16
ACT_DTYPE = jnp.bfloat16

EPS = 1e-6
HEAD_DIM = 128
CHUNK = 64
LEVEL_HALVES = (32, 16, 8, 4, 2, 1)
LRU_CONV = 4
FFN_CONV = 3
LRU_C = 8.0
ADAM_LR, ADAM_B1, ADAM_B2, ADAM_EPS, ADAM_WD, ADAM_STEP = 0.001, 0.9, 0.999, 1e-08, 0.01, 10

V7X_LANES = 128
V7X_VMEM_BUDGET = 56 << 20

NN = (((1,), (0,)), ((), ()))
NT = (((1,), (1,)), ((), ()))
TN = (((0,), (0,)), ((), ()))
MESH = pl.DeviceIdType.MESH
HBM_SPEC = pl.BlockSpec(memory_space=pl.ANY)


def _pcall(kern, **kw):
    return pl.pallas_call(kern, **kw)


def _params(sem=None, vmem=None):
    kw = {}
    if sem is not None:
        kw["dimension_semantics"] = sem
    if vmem is not None:
        kw["vmem_limit_bytes"] = int(min(max(vmem, 16 << 20), V7X_VMEM_BUDGET))
    return pltpu.CompilerParams(**kw)


def _dot(a, b, dims=NN):
    return lax.dot_general(a.astype(MXU_DTYPE), b.astype(MXU_DTYPE), dims, preferred_element_type=F32)


def _tile(dim, pref, align):
    t = min(pref, dim) // align * align
    while t >= align:
        if dim % t == 0:
            return t
        t -= align
    return dim


def _sigmoid(x):
    return 1.0 / (1.0 + jnp.exp(-x))


def _silu_and_grad(x):
    s = _sigmoid(x)
    return x * s, s * (1.0 + x * (1.0 - s))


def _gelu_and_grad(x):
    k0, k1 = math.sqrt(2.0 / math.pi), 0.044715
    t = jnp.tanh(k0 * (x + k1 * x * x * x))
    g = 0.5 * x * (1.0 + t)
    dg = 0.5 * (1.0 + t) + 0.5 * x * (1.0 - t * t) * k0 * (1.0 + 3.0 * k1 * x * x)
    return g, dg


def _one_minus_exp(x):
    p = x * (1.0 + x * (0.5 + x * (1.0 / 6.0 + x * (1.0 / 24.0 + x * (1.0 / 120.0)))))
    return jnp.where(x > -0.05, -p, 1.0 - jnp.exp(x))


def _rows(n):
    return lax.broadcasted_iota(jnp.int32, (n, 1), 0)


def _matmul(a, b, mode, out_dtype, tm, tn, tk, add=None, after=None, n_outer=False, name="mm"):
    if mode == "TN":
        K, M = a.shape
    else:
        M, K = a.shape
    N = b.shape[0] if mode == "NT" else b.shape[1]
    tm, tn = _tile(M, tm, V7X_LANES), _tile(N, tn, V7X_LANES)
    tk = _tile(K, tk, V7X_LANES)
    nk = K // tk
    dims = {"NN": NN, "NT": NT, "TN": TN}[mode]
    order = (lambda f: (lambda j, i, k: f(i, j, k))) if n_outer else (lambda f: f)
    a_spec = (pl.BlockSpec((tk, tm), order(lambda i, j, k: (k, i))) if mode == "TN"
              else pl.BlockSpec((tm, tk), order(lambda i, j, k: (i, k))))
    b_spec = (pl.BlockSpec((tn, tk), order(lambda i, j, k: (j, k))) if mode == "NT"
              else pl.BlockSpec((tk, tn), order(lambda i, j, k: (k, j))))
    o_spec = pl.BlockSpec((tm, tn), order(lambda i, j, k: (i, j)))
    has_add = add is not None

    def kern(*refs):
        a_ref, b_ref = refs[:2]
        add_ref = refs[2] if has_add else None

        def finish(r, o_ref):
            if has_add:
                r = r + add_ref[...]
            o_ref[...] = r.astype(out_dtype)

        if nk == 1:
            finish(_dot(a_ref[...], b_ref[...], dims), refs[-1])
            return
        o_ref, acc_ref = refs[-2:]
        k = pl.program_id(2)

        @pl.when(k == 0)
        def _():
            acc_ref[...] = jnp.zeros_like(acc_ref)

        acc_ref[...] += _dot(a_ref[...], b_ref[...], dims)

        @pl.when(k == nk - 1)
        def _():
            finish(acc_ref[...], o_ref)

    ab = jnp.dtype(a.dtype).itemsize
    ob = jnp.dtype(out_dtype).itemsize
    vmem = 2 * (tm * tk + tk * tn) * ab + tm * tn * (8 + 2 * ob + (8 if has_add else 0)) + (4 << 20)
    ins = [a, b] + ([add] if has_add else []) + ([after] if after is not None else [])
    in_specs = [a_spec, b_spec] + ([o_spec] if has_add else []) + ([HBM_SPEC] if after is not None else [])
    grid = (N // tn, M // tm, nk) if n_outer else (M // tm, N // tn, nk)
    return _pcall(
        kern, name=name, grid=grid,
        in_specs=in_specs, out_specs=o_spec,
        out_shape=jax.ShapeDtypeStruct((M, N), out_dtype),
        scratch_shapes=[pltpu.VMEM((tm, tn), F32)] if nk > 1 else [],
        compiler_params=_params(("parallel", "parallel", "arbitrary"), vmem),
    )(*ins)


def _rms_fwd(x, w, name):
    T, D = x.shape
    tm = _tile(T, 256, 16)

    def kern(x_ref, w_ref, o_ref):
        xv = x_ref[...]
        r = lax.rsqrt(jnp.mean(xv * xv, axis=-1, keepdims=True) + EPS)
        o_ref[...] = (xv * r * w_ref[...]).astype(ACT_DTYPE)

    return _pcall(kern, name=name, grid=(T // tm,),
                  in_specs=[pl.BlockSpec((tm, D), lambda i: (i, 0)), pl.BlockSpec((1, D), lambda i: (0, 0))],
                  out_specs=pl.BlockSpec((tm, D), lambda i: (i, 0)),
                  out_shape=jax.ShapeDtypeStruct((T, D), ACT_DTYPE),
                  compiler_params=_params(("parallel",), 8 * tm * D * 4))(x, w)


def _rms_bwd(x, w, g, g_col, res, want_act, name, after=None):
    T, D = x.shape
    tm = _tile(T, 256, 16)
    has_res = res is not None

    def kern(*refs):
        refs = list(refs)
        x_ref, w_ref, g_ref = refs[:3]
        res_ref = refs[3] if has_res else None
        outs = refs[3 + has_res + (after is not None):]
        dx_ref = outs[0]
        dxa_ref = outs[1] if want_act else None
        dw_ref = outs[-1]
        i = pl.program_id(0)
        xv = x_ref[...]
        gv = g_ref[...].astype(F32)
        r = lax.rsqrt(jnp.mean(xv * xv, axis=-1, keepdims=True) + EPS)
        gw = gv * w_ref[...]
        dx = r * gw - xv * (r * r * r) * jnp.mean(gw * xv, axis=-1, keepdims=True)
        if has_res:
            dx = dx + res_ref[...]
        dx_ref[...] = dx
        if want_act:
            dxa_ref[...] = dx.astype(ACT_DTYPE)

        @pl.when(i == 0)
        def _():
            dw_ref[...] = jnp.zeros_like(dw_ref)

        dw_ref[...] += jnp.sum(gv * xv * r, axis=0, keepdims=True)

    row = pl.BlockSpec((tm, D), lambda i: (i, 0))
    vec = pl.BlockSpec((1, D), lambda i: (0, 0))
    in_specs = ([row, vec, pl.BlockSpec((tm, D), lambda i: (i, g_col))] + ([row] if has_res else [])
                + ([HBM_SPEC] if after is not None else []))
    out_specs = [row] + ([row] if want_act else []) + [vec]
    out_shape = ([jax.ShapeDtypeStruct((T, D), F32)]
                 + ([jax.ShapeDtypeStruct((T, D), ACT_DTYPE)] if want_act else [])
                 + [jax.ShapeDtypeStruct((1, D), F32)])
    ins = [x, w, g] + ([res] if has_res else []) + ([after] if after is not None else [])
    return _pcall(kern, name=name, grid=(T // tm,), in_specs=in_specs, out_specs=out_specs,
                  out_shape=out_shape, compiler_params=_params(("arbitrary",), 14 * tm * D * 4))(*ins)


def _loss_bwd(h, target, w, name):
    T, D = h.shape
    tm = _tile(T, 256, 16)

    def kern(h_ref, t_ref, w_ref, dh_ref, dha_ref, dw_ref, loss_ref):
        i = pl.program_id(0)
        hv = h_ref[...]
        r = lax.rsqrt(jnp.mean(hv * hv, axis=-1, keepdims=True) + EPS)
        e = hv * r * w_ref[...] - t_ref[...]
        dy = e * (1.0 / D)
        gw = dy * w_ref[...]
        dh = r * gw - hv * (r * r * r) * jnp.mean(gw * hv, axis=-1, keepdims=True)
        dh_ref[...] = dh
        dha_ref[...] = dh.astype(ACT_DTYPE)

        @pl.when(i == 0)
        def _():
            dw_ref[...] = jnp.zeros_like(dw_ref)
            loss_ref[...] = jnp.zeros_like(loss_ref)

        dw_ref[...] += jnp.sum(dy * hv * r, axis=0, keepdims=True)
        part = 0.5 * jnp.sum(jnp.mean(e * e, axis=-1, keepdims=True), axis=0, keepdims=True)
        loss_ref[...] += jnp.broadcast_to(part, loss_ref.shape)

    row = pl.BlockSpec((tm, D), lambda i: (i, 0))
    vec = pl.BlockSpec((1, D), lambda i: (0, 0))
    return _pcall(kern, name=name, grid=(T // tm,), in_specs=[row, row, vec],
                  out_specs=[row, row, vec, pl.BlockSpec((8, V7X_LANES), lambda i: (0, 0))],
                  out_shape=[jax.ShapeDtypeStruct((T, D), F32), jax.ShapeDtypeStruct((T, D), ACT_DTYPE),
                             jax.ShapeDtypeStruct((1, D), F32), jax.ShapeDtypeStruct((8, V7X_LANES), F32)],
                  compiler_params=_params(("arbitrary",), 14 * tm * D * 4))(h, target, w)


def _conv(x, w_ref, b, width):
    S = x.shape[0]
    row = _rows(S)
    y = b + x * w_ref[pl.ds(width - 1, 1), :]
    for j in range(width - 1):
        sh = width - 1 - j
        y = y + jnp.where(row >= sh, pltpu.roll(x, sh, 0), 0.0) * w_ref[pl.ds(j, 1), :]
    return y


def _conv_t(dy, w_ref, width):
    S = dy.shape[0]
    row = _rows(S)
    dx = dy * w_ref[pl.ds(width - 1, 1), :]
    for j in range(width - 1):
        sh = width - 1 - j
        dx = dx + jnp.where(row < S - sh, pltpu.roll(dy, S - sh, 0), 0.0) * w_ref[pl.ds(j, 1), :]
    return dx


def _conv_dw_rows(x, dy, width):
    S = x.shape[0]
    row = _rows(S)
    out = []
    for j in range(width):
        sh = width - 1 - j
        xs = x if sh == 0 else jnp.where(row >= sh, pltpu.roll(x, sh, 0), 0.0)
        out.append(jnp.sum(xs * dy, axis=0, keepdims=True))
    return out


def _ffn_act(up, cw, cb, B, name):
    T, F2 = up.shape
    S, F = T // B, F2 // 2
    tw = _tile(F, 256, V7X_LANES)
    nt = F // tw

    def kern(g_ref, v_ref, wg_ref, wv_ref, bg_ref, bv_ref, o_ref):
        gc = _conv(g_ref[...], wg_ref, bg_ref[...], FFN_CONV)
        vc = _conv(v_ref[...], wv_ref, bv_ref[...], FFN_CONV)
        o_ref[...] = (gc * _sigmoid(gc) * vc).astype(ACT_DTYPE)

    blk = lambda off: pl.BlockSpec((S, tw), lambda b, i: (b, off + i))
    wblk = lambda off: pl.BlockSpec((FFN_CONV, tw), lambda b, i: (0, off + i))
    bblk = lambda off: pl.BlockSpec((1, tw), lambda b, i: (0, off + i))
    return _pcall(kern, name=name, grid=(B, nt),
                  in_specs=[blk(0), blk(nt), wblk(0), wblk(nt), bblk(0), bblk(nt)],
                  out_specs=pl.BlockSpec((S, tw), lambda b, i: (b, i)),
                  out_shape=jax.ShapeDtypeStruct((T, F), ACT_DTYPE),
                  compiler_params=_params(("parallel", "parallel"), 16 * S * tw * 4))(up, up, cw, cw, cb, cb)


def _ffn_act_bwd(up, cw, cb, d_act, B, name):
    T, F2 = up.shape
    S, F = T // B, F2 // 2
    tw = _tile(F, 256, V7X_LANES)
    nt = F // tw

    def kern(s_ref, p_ref, ws_ref, wp_ref, bs_ref, bp_ref, da_ref, du_ref, dcw_ref, dcb_ref):
        t, b = pl.program_id(0), pl.program_id(1)
        sc = _conv(s_ref[...], ws_ref, bs_ref[...], FFN_CONV)
        pc = _conv(p_ref[...], wp_ref, bp_ref[...], FFN_CONV)
        da = da_ref[...].astype(F32)
        _, dsilu_self = _silu_and_grad(sc)
        silu_partner = pc * _sigmoid(pc)
        d = da * jnp.where(t < nt, pc * dsilu_self, silu_partner)
        du_ref[...] = _conv_t(d, ws_ref, FFN_CONV).astype(ACT_DTYPE)

        @pl.when(b == 0)
        def _():
            dcw_ref[...] = jnp.zeros_like(dcw_ref)
            dcb_ref[...] = jnp.zeros_like(dcb_ref)

        for j, rj in enumerate(_conv_dw_rows(s_ref[...], d, FFN_CONV)):
            dcw_ref[pl.ds(j, 1), :] += rj
        dcb_ref[...] += jnp.sum(d, axis=0, keepdims=True)

    partner = lambda t: (t + nt) % (2 * nt)
    return _pcall(
        kern, name=name, grid=(2 * nt, B),
        in_specs=[pl.BlockSpec((S, tw), lambda t, b: (b, t)),
                  pl.BlockSpec((S, tw), lambda t, b: (b, partner(t))),
                  pl.BlockSpec((FFN_CONV, tw), lambda t, b: (0, t)),
                  pl.BlockSpec((FFN_CONV, tw), lambda t, b: (0, partner(t))),
                  pl.BlockSpec((1, tw), lambda t, b: (0, t)),
                  pl.BlockSpec((1, tw), lambda t, b: (0, partner(t))),
                  pl.BlockSpec((S, tw), lambda t, b: (b, t % nt))],
        out_specs=[pl.BlockSpec((S, tw), lambda t, b: (b, t)),
                   pl.BlockSpec((FFN_CONV, tw), lambda t, b: (0, t)),
                   pl.BlockSpec((1, tw), lambda t, b: (0, t))],
        out_shape=[jax.ShapeDtypeStruct((T, F2), ACT_DTYPE), jax.ShapeDtypeStruct((FFN_CONV, F2), F32),
                   jax.ShapeDtypeStruct((1, F2), F32)],
        compiler_params=_params(("parallel", "arbitrary"), 24 * S * tw * 4),
    )(up, up, cw, cw, cb, cb, d_act)


def _hgrn_tables():
    C = CHUNK
    t = np.arange(C)
    mats = [(t[:, None] >= t[None, :]).astype(np.float32)]
    masks = []
    gsum = [(t[:, None] <= t[None, :]).astype(np.float32), (t[:, None] > t[None, :]).astype(np.float32)]
    for hs in LEVEL_HALVES:
        m = (t // (2 * hs)) * 2 * hs + hs
        later = t >= m
        d = np.zeros((C, C), np.float32)
        for i in range(C):
            if later[i]:
                d[i, m[i]:i + 1] = 1.0
            else:
                d[i, i + 1:m[i]] = -1.0
        mats.append(d)
        same = (t[:, None] // (2 * hs)) == (t[None, :] // (2 * hs))
        masks.append((same & later[:, None] & (~later)[None, :]).astype(np.float32))
        gsum.append((same & later[:, None] & (t[None, :] >= t[:, None])).astype(np.float32))
        gsum.append((same & (~later)[:, None] & (t[None, :] < t[:, None])).astype(np.float32))
    return np.concatenate(mats, 0), np.stack(masks, 0), np.concatenate(gsum, 1)


def _split_dot(mat, v):
    hi = v.astype(MXU_DTYPE)
    lo = (v - hi.astype(F32)).astype(MXU_DTYPE)
    r = _dot(mat, jnp.concatenate([hi, lo], axis=1))
    n = v.shape[1]
    return r[:, :n] + r[:, n:]


def _hgrn_gates(qr, fr, lb, mc):
    C = CHUNK
    q, dq_dqr = _silu_and_grad(qr)
    sf = _sigmoid(fr)
    f = lb + (1.0 - lb) * sf
    k = 1.0 - f
    dall = _split_dot(mc, jnp.log(f))
    b = dall[0:C]
    dl = [dall[C * (l + 1):C * (l + 2)] for l in range(len(LEVEL_HALVES))]
    eq = [jnp.exp(jnp.minimum(d, 0.0)) for d in dl]
    ek = [jnp.exp(jnp.minimum(-d, 0.0)) for d in dl]
    return q, dq_dqr, sf, f, k, b, eq, ek


def _hgrn_scores(q, k, eq, ek, masks_ref):
    p = jnp.where(_rows(CHUNK) == lax.broadcasted_iota(jnp.int32, (1, CHUNK), 1),
                  jnp.sum(q * k, axis=-1, keepdims=True), 0.0)
    for l in range(len(LEVEL_HALVES)):
        p = p + masks_ref[l] * _dot(q * eq[l], k * ek[l], NT)
    return p


def _hgrn_fwd(proj, lb_gamma, norm_w, B, HW, name):
    T = proj.shape[0]
    S, H, C = T // B, HW // HEAD_DIM, CHUNK
    NC = S // C
    mc_np, masks_np, _ = _hgrn_tables()
    mc, masks = jnp.asarray(mc_np, MXU_DTYPE), jnp.asarray(masks_np, F32)

    def kern(q_ref, f_ref, i_ref, g_ref, lbg_ref, nw_ref, mc_ref, masks_ref, oraw_ref, o_ref, st_ref):
        g0, g1 = lbg_ref[pl.ds(0, 1), :], lbg_ref[pl.ds(1, 1), :]
        mx = jnp.maximum(g0, g1)
        e0, e1 = jnp.exp(g0 - mx), jnp.exp(g1 - mx)
        lb = e0 / (e0 + e1)
        nw = nw_ref[...]
        mcv = mc_ref[...]

        def body(n, st):
            rows = pl.ds(pl.multiple_of(n * C, C), C)
            st_ref[n] = st
            q, _, _, _, k, b, eq, ek = _hgrn_gates(q_ref[rows, :], f_ref[rows, :], lb, mcv)
            v = i_ref[rows, :]
            o = _dot(q * jnp.exp(b), st, NT) + _dot(_hgrn_scores(q, k, eq, ek, masks_ref), v)
            b_last = b[C - 1:C]
            st = st * jnp.exp(b_last) + _dot(v, k * jnp.exp(b_last - b), TN)
            oraw_ref[rows, :] = o
            r = lax.rsqrt(jnp.mean(o * o, axis=-1, keepdims=True) + EPS)
            gate, _ = _silu_and_grad(g_ref[rows, :])
            o_ref[rows, :] = (o * r * nw * gate).astype(ACT_DTYPE)
            return st

        lax.fori_loop(0, NC, body, jnp.zeros((HEAD_DIM, HEAD_DIM), F32))

    col = lambda off: pl.BlockSpec((S, HEAD_DIM), lambda b, h: (b, off + h))
    return _pcall(
        kern, name=name, grid=(B, H),
        in_specs=[col(0), col(H), col(2 * H), col(3 * H),
                  pl.BlockSpec((2, HEAD_DIM), lambda b, h: (0, h)),
                  pl.BlockSpec((1, HEAD_DIM), lambda b, h: (0, h)),
                  pl.BlockSpec(mc.shape, lambda b, h: (0, 0)),
                  pl.BlockSpec(masks.shape, lambda b, h: (0, 0, 0))],
        out_specs=[col(0), col(0),
                   pl.BlockSpec((None, None, NC, HEAD_DIM, HEAD_DIM), lambda b, h: (b, h, 0, 0, 0))],
        out_shape=[jax.ShapeDtypeStruct((T, HW), F32), jax.ShapeDtypeStruct((T, HW), ACT_DTYPE),
                   jax.ShapeDtypeStruct((B, H, NC, HEAD_DIM, HEAD_DIM), F32)],
        compiler_params=_params(("parallel", "parallel"), 16 * S * HEAD_DIM * 4 + (8 << 20)),
    )(proj, proj, proj, proj, lb_gamma, norm_w, mc, masks)


def _hgrn_bwd(proj, lb_gamma, norm_w, o_raw, states, d_mix, B, HW, name):
    T = proj.shape[0]
    S, H, C = T // B, HW // HEAD_DIM, CHUNK
    NC = S // C
    mc_np, masks_np, gsum_np = _hgrn_tables()
    mc, masks, gsum = jnp.asarray(mc_np, MXU_DTYPE), jnp.asarray(masks_np, F32), jnp.asarray(gsum_np, MXU_DTYPE)
    nl = len(LEVEL_HALVES)

    def kern(q_ref, f_ref, i_ref, g_ref, lbg_ref, nw_ref, mc_ref, masks_ref, gsum_ref, oraw_ref, st_ref, do_ref,
             dq_ref, df_ref, di_ref, dg_ref, dlbg_ref, dnw_ref):
        bi = pl.program_id(1)
        g0, g1 = lbg_ref[pl.ds(0, 1), :], lbg_ref[pl.ds(1, 1), :]
        mx = jnp.maximum(g0, g1)
        e0, e1 = jnp.exp(g0 - mx), jnp.exp(g1 - mx)
        lb = e0 / (e0 + e1)
        nw = nw_ref[...]
        mcv, gsumv = mc_ref[...], gsum_ref[...]

        def body(it, carry):
            dst, dlb, dnw = carry
            n = NC - 1 - it
            rows = pl.ds(pl.multiple_of(n * C, C), C)
            qr, fr, v = q_ref[rows, :], f_ref[rows, :], i_ref[rows, :]
            q, dq_dqr, sf, f, k, b, eq, ek = _hgrn_gates(qr, fr, lb, mcv)
            o = oraw_ref[rows, :]
            dout = do_ref[rows, :].astype(F32)
            gate, dgate = _silu_and_grad(g_ref[rows, :])
            r = lax.rsqrt(jnp.mean(o * o, axis=-1, keepdims=True) + EPS)
            dg_ref[rows, :] = (dout * o * r * nw * dgate).astype(ACT_DTYPE)
            don = dout * gate
            dnw = dnw + jnp.sum(don * o * r, axis=0, keepdims=True)
            gw = don * nw
            do = r * gw - o * (r * r * r) * jnp.mean(gw * o, axis=-1, keepdims=True)
            st_prev = st_ref[n]
            eb = jnp.exp(b)
            b_last = b[C - 1:C]
            ebl = jnp.exp(b_last - b)
            p = _hgrn_scores(q, k, eq, ek, masks_ref)
            dp = _dot(do, v, NT)
            dpd = jnp.sum(do * v, axis=-1, keepdims=True)
            dq_state = _dot(do, st_prev) * eb
            dk_state = _dot(v, dst) * ebl
            dq = dq_state + dpd * k
            dk = dk_state + dpd * q
            pairs = [q * dq_state, k * dk_state]
            for l in range(nl):
                mdp = masks_ref[l] * dp
                dql = _dot(mdp, k * ek[l]) * eq[l]
                dkl = _dot(mdp, q * eq[l], TN) * ek[l]
                dq, dk = dq + dql, dk + dkl
                pairs += [q * dql, k * dkl]
            dv = _dot(p, do, TN) + _dot(k * ebl, dst, NT)
            through = jnp.exp(b_last) * jnp.sum(dst * st_prev, axis=0, keepdims=True)
            dlg = _split_dot(gsumv, jnp.concatenate(pairs, axis=0)) + through
            dst = dst * jnp.exp(b_last) + _dot(do, q * eb, TN)
            dq_ref[rows, :] = (dq * dq_dqr).astype(ACT_DTYPE)
            dfv = dlg / f - dk
            df_ref[rows, :] = (dfv * (1.0 - lb) * sf * (1.0 - sf)).astype(ACT_DTYPE)
            di_ref[rows, :] = dv.astype(ACT_DTYPE)
            dlb = dlb + jnp.sum(dfv * (1.0 - sf), axis=0, keepdims=True)
            return dst, dlb, dnw

        zrow = jnp.zeros((1, HEAD_DIM), F32)
        _, dlb, dnw = lax.fori_loop(0, NC, body, (jnp.zeros((HEAD_DIM, HEAD_DIM), F32), zrow, zrow))

        @pl.when(bi == 0)
        def _():
            dlbg_ref[...] = jnp.zeros_like(dlbg_ref)
            dnw_ref[...] = jnp.zeros_like(dnw_ref)

        dg0 = dlb * lb * (1.0 - lb)
        dlbg_ref[pl.ds(0, 1), :] += dg0
        dlbg_ref[pl.ds(1, 1), :] += -dg0
        dnw_ref[...] += dnw

    col = lambda off: pl.BlockSpec((S, HEAD_DIM), lambda h, b: (b, off + h))
    full = lambda a: pl.BlockSpec(a.shape, lambda h, b: (0,) * a.ndim)
    part = jax.ShapeDtypeStruct((T, HW), ACT_DTYPE)
    return _pcall(
        kern, name=name, grid=(H, B),
        in_specs=[col(0), col(H), col(2 * H), col(3 * H),
                  pl.BlockSpec((2, HEAD_DIM), lambda h, b: (0, h)),
                  pl.BlockSpec((1, HEAD_DIM), lambda h, b: (0, h)),
                  full(mc), full(masks), full(gsum), col(0),
                  pl.BlockSpec((None, None, NC, HEAD_DIM, HEAD_DIM), lambda h, b: (b, h, 0, 0, 0)),
                  col(0)],
        out_specs=[col(0), col(0), col(0), col(0),
                   pl.BlockSpec((2, HEAD_DIM), lambda h, b: (0, h)),
                   pl.BlockSpec((1, HEAD_DIM), lambda h, b: (0, h))],
        out_shape=[part, part, part, part, jax.ShapeDtypeStruct((2, HW), F32), jax.ShapeDtypeStruct((1, HW), F32)],
        compiler_params=_params(("parallel", "arbitrary"), 24 * S * HEAD_DIM * 4 + (8 << 20)),
    )(proj, proj, proj, proj, lb_gamma, norm_w, mc, masks, gsum, o_raw, states, d_mix)


def _lru_gates(xr, cw_ref, cb, wa, ba, wx, bx, lam):
    S = xr.shape[0]
    xb = _conv(xr, cw_ref, cb, LRU_CONV)
    r = _sigmoid(_dot(xb, wa) + ba)
    ig = _sigmoid(_dot(xb, wx) + bx)
    sp = jnp.maximum(-lam, 0.0) + jnp.log(1.0 + jnp.exp(-jnp.abs(lam)))
    la = -LRU_C * r * sp
    a = jnp.exp(la)
    mult = jnp.where(_rows(S) == 0, 1.0, jnp.sqrt(_one_minus_exp(2.0 * la)))
    return xb, r, ig, sp, a, mult


def _scan_rows(a_ref, u_ref, h_ref, reverse):
    S, W = a_ref.shape
    nb = S // 8
    row = _rows(8)

    def body(it, carry):
        blk = nb - 1 - it if reverse else it
        rows = pl.ds(pl.multiple_of(blk * 8, 8), 8)
        a, u = a_ref[rows, :], u_ref[rows, :]
        for d in (1, 2, 4):
            sh = 8 - d if reverse else d
            keep = (row < 8 - d) if reverse else (row >= d)
            u = u + jnp.where(keep, a * pltpu.roll(u, sh, 0), 0.0)
            a = jnp.where(keep, a * pltpu.roll(a, sh, 0), a)
        h = u + a * carry
        h_ref[rows, :] = h
        return h[0:1] if reverse else h[7:8]

    lax.fori_loop(0, nb, body, jnp.zeros((1, W), F32))


def _lru_fwd(proj, cw, cb, wa, ba, wx, bx, lam, B, HW, LW, name):
    T = proj.shape[0]
    S, NB = T // B, LW // HEAD_DIM
    xoff, yoff = 4 * HW // HEAD_DIM, 4 * HW // HEAD_DIM + NB

    def kern(x_ref, y_ref, cw_ref, cb_ref, wa_ref, ba_ref, wx_ref, bx_ref, lam_ref, h_ref, z_ref, a_s, u_s):
        xb, _, ig, _, a, mult = _lru_gates(x_ref[...], cw_ref, cb_ref[...], wa_ref[...], ba_ref[...],
                                           wx_ref[...], bx_ref[...], lam_ref[...])
        a_s[...] = a
        u_s[...] = xb * ig * mult
        _scan_rows(a_s, u_s, h_ref, False)
        gy, _ = _gelu_and_grad(y_ref[...])
        z_ref[...] = h_ref[...] * gy

    blk = lambda off: pl.BlockSpec((S, HEAD_DIM), lambda b, n: (b, off + n))
    vec = pl.BlockSpec((1, HEAD_DIM), lambda b, n: (0, n))
    mat = pl.BlockSpec((None, HEAD_DIM, HEAD_DIM), lambda b, n: (n, 0, 0))
    return _pcall(
        kern, name=name, grid=(B, NB),
        in_specs=[blk(xoff), blk(yoff), pl.BlockSpec((LRU_CONV, HEAD_DIM), lambda b, n: (0, n)),
                  vec, mat, vec, mat, vec, vec],
        out_specs=[blk(0), blk(0)],
        out_shape=[jax.ShapeDtypeStruct((T, LW), F32), jax.ShapeDtypeStruct((T, LW), F32)],
        scratch_shapes=[pltpu.VMEM((S, HEAD_DIM), F32), pltpu.VMEM((S, HEAD_DIM), F32)],
        compiler_params=_params(("parallel", "parallel"), 24 * S * HEAD_DIM * 4),
    )(proj, proj, cw, cb, wa, ba, wx, bx, lam)


def _lru_bwd(proj, cw, cb, wa, ba, wx, bx, lam, h, dz, B, HW, LW, name):
    T = proj.shape[0]
    S, NB = T // B, LW // HEAD_DIM
    xoff, yoff = 4 * HW // HEAD_DIM, 4 * HW // HEAD_DIM + NB

    def kern(x_ref, y_ref, cw_ref, cb_ref, wa_ref, ba_ref, wx_ref, bx_ref, lam_ref, h_ref, dz_ref,
             dx_ref, dy_ref, dwa_ref, dwx_ref, dba_ref, dbx_ref, dlam_ref, dcw_ref, dcb_ref, a_s, u_s, dh_s):
        bi = pl.program_id(1)
        row = _rows(S)
        xr, lam = x_ref[...], lam_ref[...]
        wa, wx = wa_ref[...], wx_ref[...]
        xb, r, ig, sp, a, mult = _lru_gates(xr, cw_ref, cb_ref[...], wa, ba_ref[...], wx, bx_ref[...], lam)
        hv, dz = h_ref[...], dz_ref[...]
        gy, dgy = _gelu_and_grad(y_ref[...])
        dy_ref[...] = (dz * hv * dgy).astype(ACT_DTYPE)
        a_s[...] = jnp.where(row < S - 1, pltpu.roll(a, S - 1, 0), 0.0)
        u_s[...] = dz * gy
        _scan_rows(a_s, u_s, dh_s, True)
        dh = dh_s[...]
        h_prev = jnp.where(row >= 1, pltpu.roll(hv, 1, 0), 0.0)
        d_ig = dh * xb * mult
        d_mult = jnp.where(row == 0, 0.0, dh * xb * ig)
        dxb = dh * ig * mult
        dla = dh * h_prev * a - d_mult * (a * a) / mult
        dpre_r = dla * (-LRU_C * sp) * r * (1.0 - r)
        dpre_i = d_ig * ig * (1.0 - ig)
        dxb = dxb + _dot(dpre_r, wa, NT) + _dot(dpre_i, wx, NT)
        dx_ref[...] = _conv_t(dxb, cw_ref, LRU_CONV).astype(ACT_DTYPE)

        @pl.when(bi == 0)
        def _():
            for ref in (dwa_ref, dwx_ref, dba_ref, dbx_ref, dlam_ref, dcw_ref, dcb_ref):
                ref[...] = jnp.zeros_like(ref)

        dwa_ref[...] += _dot(xb, dpre_r, TN)
        dwx_ref[...] += _dot(xb, dpre_i, TN)
        dba_ref[...] += jnp.sum(dpre_r, axis=0, keepdims=True)
        dbx_ref[...] += jnp.sum(dpre_i, axis=0, keepdims=True)
        dsp = jnp.sum(dla * (-LRU_C) * r, axis=0, keepdims=True)
        dlam_ref[...] += -dsp * _sigmoid(-lam)
        for j, rj in enumerate(_conv_dw_rows(xr, dxb, LRU_CONV)):
            dcw_ref[pl.ds(j, 1), :] += rj
        dcb_ref[...] += jnp.sum(dxb, axis=0, keepdims=True)

    blk = lambda off: pl.BlockSpec((S, HEAD_DIM), lambda n, b: (b, off + n))
    vec = pl.BlockSpec((1, HEAD_DIM), lambda n, b: (0, n))
    mat = pl.BlockSpec((None, HEAD_DIM, HEAD_DIM), lambda n, b: (n, 0, 0))
    cwb = pl.BlockSpec((LRU_CONV, HEAD_DIM), lambda n, b: (0, n))
    part = jax.ShapeDtypeStruct((T, LW), ACT_DTYPE)
    vshape = jax.ShapeDtypeStruct((1, LW), F32)
    mshape = jax.ShapeDtypeStruct((NB, HEAD_DIM, HEAD_DIM), F32)
    return _pcall(
        kern, name=name, grid=(NB, B),
        in_specs=[blk(xoff), blk(yoff), cwb, vec, mat, vec, mat, vec, vec, blk(0), blk(0)],
        out_specs=[blk(0), blk(0), mat, mat, vec, vec, vec, cwb, vec],
        out_shape=[part, part, mshape, mshape, vshape, vshape, vshape,
                   jax.ShapeDtypeStruct((LRU_CONV, LW), F32), vshape],
        scratch_shapes=[pltpu.VMEM((S, HEAD_DIM), F32)] * 3,
        compiler_params=_params(("parallel", "arbitrary"), 40 * S * HEAD_DIM * 4),
    )(proj, proj, cw, cb, wa, ba, wx, bx, lam, h, dz)


def _pos():
    return lax.axis_index("x"), lax.axis_index("y"), lax.axis_index("c")


def _other_chips(x, y):
    return [(1 - x, y), (x, 1 - y), (1 - x, 1 - y)]


def _remote(src, dst, send_sems, recv_sems, k, to):
    return pltpu.make_async_remote_copy(src_ref=src, dst_ref=dst, send_sem=send_sems.at[k],
                                        recv_sem=recv_sems.at[k], device_id=to, device_id_type=MESH)


HBM_BLK = pl.BlockSpec(memory_space=pltpu.HBM)
SEM_BLK = pl.BlockSpec(memory_space=pltpu.SEMAPHORE)
VMEM_BLK = pl.BlockSpec(memory_space=pltpu.VMEM)
DATAFLOW = pltpu.SideEffectType.DATAFLOW_SIDE_EFFECTING
TOKEN = jax.ShapeDtypeStruct((8, V7X_LANES), F32)


def _in_hbm(a):
    return pltpu.with_memory_space_constraint(a, pltpu.HBM)


def _gather_win(o_ref, R, C, col_sharded):
    Rh = R // 2

    def win(j, h=None):
        if col_sharded:
            rows = pl.ds(0, R) if h is None else pl.ds(h * Rh, Rh)
            return o_ref.at[rows, pl.ds(j * C, C)]
        return o_ref.at[pl.ds(j * R, R) if h is None else pl.ds(j * R + h * Rh, Rh), :]

    return win


def _cast_into_window(w, col_sharded, after, name):
    R, C = w.shape
    tr = _tile(R, 256, 16)
    nr = R // tr
    full = (R, 4 * C) if col_sharded else (4 * R, C)
    j = (2 * lax.axis_index("x") + lax.axis_index("y")).astype(jnp.int32).reshape(1)

    def kern(j_ref, w_ref, after_ref, o_ref):
        o_ref[...] = w_ref[...].astype(ACT_DTYPE)

    out_map = (lambda i, jr: (i, jr[0])) if col_sharded else (lambda i, jr: (jr[0] * nr + i, 0))
    grid_spec = pltpu.PrefetchScalarGridSpec(
        num_scalar_prefetch=1, grid=(nr,),
        in_specs=[pl.BlockSpec((tr, C), lambda i, jr: (i, 0)), HBM_SPEC], out_specs=pl.BlockSpec((tr, C), out_map))
    return _pcall(kern, name=name, grid_spec=grid_spec, out_shape=jax.ShapeDtypeStruct(full, ACT_DTYPE),
                  compiler_params=_params(("parallel",), 6 * tr * C * 4))(j, w, after)


def _gather_start(land, shard_shape, col_sharded, token, name):
    R, C = shard_shape

    def body(land_ref, tok_ref, send_sems, recv_sems, land_thru, tok_out):
        x, y, c = _pos()
        w = _gather_win(land_ref, R, C, col_sharded)(2 * x + y, c)
        for k, (cx, cy) in enumerate(_other_chips(x, y)):
            _remote(w, w, send_sems, recv_sems, k, (cx, cy, c)).start()
        tok_out[...] = tok_ref[...]

    return _pcall(
        body, name=name,
        out_shape=(pltpu.SemaphoreType.DMA((3,)), pltpu.SemaphoreType.DMA((3,)),
                   pltpu.HBM(land.shape, land.dtype), TOKEN),
        in_specs=(HBM_BLK, VMEM_BLK), out_specs=(SEM_BLK, SEM_BLK, HBM_BLK, VMEM_BLK),
        input_output_aliases={0: 2},
        compiler_params=pltpu.CompilerParams(has_side_effects=DATAFLOW),
    )(_in_hbm(land), token)


def _gather_wait(started, shard_shape, after, col_sharded, name):
    send_sems, recv_sems, land_thru, _ = started
    R, C = shard_shape

    def body(land_ref, send_sems, recv_sems, after_ref, got_ref):
        x, y, c = _pos()
        win = _gather_win(land_ref, R, C, col_sharded)
        for k, (cx, cy) in enumerate(_other_chips(x, y)):
            cp = _remote(win(2 * x + y, c), win(2 * cx + cy, c), send_sems, recv_sems, k, (cx, cy, c))
            cp.wait_send()
            cp.wait_recv()

    return _pcall(
        body, name=name, out_shape=pltpu.HBM(land_thru.shape, land_thru.dtype),
        in_specs=(HBM_BLK, SEM_BLK, SEM_BLK, HBM_SPEC), out_specs=HBM_BLK, input_output_aliases={0: 0},
        compiler_params=pltpu.CompilerParams(has_side_effects=DATAFLOW),
    )(land_thru, send_sems, recv_sems, after)


def _gather_pass_on(landed, shard_shape, col_sharded, name):
    R, C = shard_shape

    def body(in_ref, o_ref, send_sems, recv_sems):
        x, y, c = _pos()
        src, dst = _gather_win(in_ref, R, C, col_sharded), _gather_win(o_ref, R, C, col_sharded)
        chips = _other_chips(x, y)
        passed = [_remote(src(2 * cx + cy, c), dst(2 * cx + cy, c), send_sems, recv_sems, k, (x, y, 1 - c))
                  for k, (cx, cy) in enumerate(chips)]
        for cp in passed:
            cp.start()
        for k, (cx, cy) in enumerate(chips):
            w = dst(2 * cx + cy, 1 - c)
            _remote(w, w, send_sems, recv_sems, k, (x, y, c)).wait_recv()
        for cp in passed:
            cp.wait_send()

    return _pcall(body, name=name, in_specs=[HBM_SPEC], out_specs=HBM_SPEC,
                  out_shape=jax.ShapeDtypeStruct(landed.shape, landed.dtype), input_output_aliases={0: 0},
                  scratch_shapes=[pltpu.SemaphoreType.DMA((3,)), pltpu.SemaphoreType.DMA((3,))])(landed)


def _pair_exchange(g4, name):
    J, _, Rh, W = g4.shape

    def body(g_ref, p_ref, send_sems, recv_sems):
        x, y, c = _pos()
        cp = _remote(g_ref.at[pl.ds(0, J), 1 - c], p_ref, send_sems, recv_sems, 0, (x, y, 1 - c))
        cp.start()
        cp.wait()

    return _pcall(body, name=name, in_specs=[HBM_SPEC], out_specs=HBM_SPEC,
                  out_shape=jax.ShapeDtypeStruct((J, Rh, W), g4.dtype),
                  scratch_shapes=[pltpu.SemaphoreType.DMA((1,)), pltpu.SemaphoreType.DMA((1,))])(g4)


def _pair_add(g4, p, name):
    J, _, Rh, W = g4.shape
    tr = _tile(Rh, 256, 16)
    tw = _tile(W, 2048, V7X_LANES)
    c = lax.axis_index("c").astype(jnp.int32).reshape(1)

    def kern(c_ref, g_ref, p_ref, o_ref):
        o_ref[...] = (g_ref[...].astype(F32) + p_ref[...].astype(F32)).astype(ACT_DTYPE)

    grid_spec = pltpu.PrefetchScalarGridSpec(
        num_scalar_prefetch=1, grid=(J, Rh // tr, W // tw),
        in_specs=[pl.BlockSpec((None, None, tr, tw), lambda j, i, w, cr: (j, cr[0], i, w)),
                  pl.BlockSpec((None, tr, tw), lambda j, i, w, cr: (j, i, w))],
        out_specs=pl.BlockSpec((None, tr, tw), lambda j, i, w, cr: (j, i, w)))
    return _pcall(kern, name=name, grid_spec=grid_spec,
                  out_shape=jax.ShapeDtypeStruct((J, Rh, W), ACT_DTYPE),
                  compiler_params=_params(("parallel", "parallel", "parallel"), 12 * tr * tw * 4))(c, g4, p)


def _chip_sum_piece(cs_ref, C, col_sharded):
    return lambda j: cs_ref.at[0, :, pl.ds(j * C, C)] if col_sharded else cs_ref.at[j]


def _pair_add_own(g4, p, col_sharded, name):
    J, _, Rh, W = g4.shape
    C = W // 4 if col_sharded else W
    tr = _tile(Rh, 256, 16)
    tw = _tile(C, 8192, V7X_LANES)
    nw = C // tw
    x, y, c = _pos()
    chip = (2 * x + y).astype(jnp.int32).reshape(1)
    core = c.astype(jnp.int32).reshape(1)

    def kern(j_ref, c_ref, g_ref, p_ref, o_ref):
        o_ref[...] = (g_ref[...].astype(F32) + p_ref[...].astype(F32)).astype(ACT_DTYPE)

    if col_sharded:
        g_map = lambda i, w, j, cc: (0, cc[0], i, j[0] * nw + w)
        p_map = lambda i, w, j, cc: (0, i, j[0] * nw + w)
    else:
        g_map = lambda i, w, j, cc: (j[0], cc[0], i, w)
        p_map = lambda i, w, j, cc: (j[0], i, w)
    grid_spec = pltpu.PrefetchScalarGridSpec(
        num_scalar_prefetch=2, grid=(Rh // tr, nw),
        in_specs=[pl.BlockSpec((None, None, tr, tw), g_map), pl.BlockSpec((None, tr, tw), p_map)],
        out_specs=pl.BlockSpec((None, None, tr, tw), lambda i, w, j, cc: (j[0], cc[0], i, w)))
    return _pcall(kern, name=name, grid_spec=grid_spec,
                  out_shape=jax.ShapeDtypeStruct((4, 2, Rh, C), ACT_DTYPE),
                  compiler_params=_params(("parallel", "parallel"), 12 * tr * tw * 4))(chip, core, g4, p)


def _scatter_start(cs, slots, col_sharded, token, name):
    C = slots.shape[3]

    def body(cs_ref, land_ref, tok_ref, send_sems, recv_sems, cs_thru, land_thru, tok_out):
        x, y, c = _pos()
        piece = _chip_sum_piece(cs_ref, C, col_sharded)
        for k, (cx, cy) in enumerate(_other_chips(x, y)):
            _remote(piece(2 * cx + cy), land_ref.at[2 * x + y, c], send_sems, recv_sems, k, (cx, cy, c)).start()
        tok_out[...] = tok_ref[...]

    return _pcall(
        body, name=name,
        out_shape=(pltpu.SemaphoreType.DMA((3,)), pltpu.SemaphoreType.DMA((3,)),
                   pltpu.HBM(cs.shape, cs.dtype), pltpu.HBM(slots.shape, cs.dtype), TOKEN),
        in_specs=(HBM_BLK, HBM_BLK, VMEM_BLK), out_specs=(SEM_BLK, SEM_BLK, HBM_BLK, HBM_BLK, VMEM_BLK),
        input_output_aliases={0: 2, 1: 3},
        compiler_params=pltpu.CompilerParams(has_side_effects=DATAFLOW),
    )(_in_hbm(cs), _in_hbm(slots), token)


def _scatter_wait(started, after, col_sharded, name):
    send_sems, recv_sems, cs_thru, land_thru, _ = started
    C = land_thru.shape[3]

    def body(cs_ref, land_ref, send_sems, recv_sems, after_ref, cs_dead, got_ref):
        x, y, c = _pos()
        piece = _chip_sum_piece(cs_ref, C, col_sharded)
        for k, (cx, cy) in enumerate(_other_chips(x, y)):
            cp = _remote(piece(2 * cx + cy), land_ref.at[2 * cx + cy, c], send_sems, recv_sems, k, (cx, cy, c))
            cp.wait_send()
            cp.wait_recv()

    return _pcall(
        body, name=name,
        out_shape=(pltpu.HBM(cs_thru.shape, cs_thru.dtype), pltpu.HBM(land_thru.shape, land_thru.dtype)),
        in_specs=(HBM_BLK, HBM_BLK, SEM_BLK, SEM_BLK, HBM_SPEC), out_specs=(HBM_BLK, HBM_BLK),
        input_output_aliases={0: 0, 1: 1},
        compiler_params=pltpu.CompilerParams(has_side_effects=DATAFLOW),
    )(cs_thru, land_thru, send_sems, recv_sems, after)


def _scatter_pass_on(landed, name):
    def body(in_ref, o_ref, send_sems, recv_sems):
        x, y, c = _pos()
        sends = [_remote(in_ref.at[i, c], o_ref.at[i, c], send_sems, recv_sems, i, (x, y, 1 - c)) for i in range(4)]
        for cp in sends:
            cp.start()
        for i in range(4):
            w = o_ref.at[i, 1 - c]
            _remote(w, w, send_sems, recv_sems, i, (x, y, c)).wait_recv()
        for cp in sends:
            cp.wait_send()

    return _pcall(body, name=name, in_specs=[HBM_SPEC], out_specs=HBM_SPEC,
                  out_shape=jax.ShapeDtypeStruct(landed.shape, landed.dtype), input_output_aliases={0: 0},
                  scratch_shapes=[pltpu.SemaphoreType.DMA((4,)), pltpu.SemaphoreType.DMA((4,))])(landed)


def _gather_small(buf, name):
    rows = buf.shape[0]

    def body(b_ref, o_ref, send_sems, recv_sems):
        x, y, c = _pos()
        jme = 2 * x + y
        chips = _other_chips(x, y)
        o_ref[jme] = b_ref[...]
        sends = [_remote(b_ref, o_ref.at[jme], send_sems, recv_sems, k, (cx, cy, c))
                 for k, (cx, cy) in enumerate(chips)]
        for cp in sends:
            cp.start()
        for k, (cx, cy) in enumerate(chips):
            w = o_ref.at[2 * cx + cy]
            _remote(w, w, send_sems, recv_sems, k, (x, y, c)).wait_recv()
        for cp in sends:
            cp.wait_send()

    vm = pl.BlockSpec(memory_space=pltpu.VMEM)
    return _pcall(body, name=name, in_specs=[vm], out_specs=vm,
                  out_shape=jax.ShapeDtypeStruct((4, rows, V7X_LANES), buf.dtype),
                  scratch_shapes=[pltpu.SemaphoreType.DMA((3,)), pltpu.SemaphoreType.DMA((3,))],
                  compiler_params=_params(None, 16 * rows * V7X_LANES * 4))(buf)


def _allreduce_small(buf, name):
    rows = buf.shape[0]

    def body(b_ref, o_ref, slots, send_sems, recv_sems):
        x, y, c = _pos()
        me = 4 * x + 2 * y + c
        slots[me] = b_ref[...]
        sends = []
        for k in range(1, 8):
            kx, ky, kc = (k >> 2) & 1, (k >> 1) & 1, k & 1
            to = (x ^ kx, y ^ ky, c ^ kc)
            sends.append(_remote(b_ref, slots.at[me], send_sems, recv_sems, k - 1, to))
        for cp in sends:
            cp.start()
        for k in range(1, 8):
            kx, ky, kc = (k >> 2) & 1, (k >> 1) & 1, k & 1
            w = slots.at[4 * (x ^ kx) + 2 * (y ^ ky) + (c ^ kc)]
            _remote(w, w, send_sems, recv_sems, k - 1, (x, y, c)).wait_recv()
        for cp in sends:
            cp.wait_send()
        acc = slots[0]
        for d in range(1, 8):
            acc = acc + slots[d]
        o_ref[...] = acc

    vm = pl.BlockSpec(memory_space=pltpu.VMEM)
    return _pcall(body, name=name, in_specs=[vm], out_specs=vm,
                  out_shape=jax.ShapeDtypeStruct(buf.shape, buf.dtype),
                  scratch_shapes=[pltpu.VMEM((8, rows, V7X_LANES), buf.dtype),
                                  pltpu.SemaphoreType.DMA((7,)), pltpu.SemaphoreType.DMA((7,))],
                  compiler_params=_params(None, 14 * rows * V7X_LANES * 4))(buf)


def _adamw_math(w, g, m, v):
    m = ADAM_B1 * m + (1.0 - ADAM_B1) * g
    v = ADAM_B2 * v + (1.0 - ADAM_B2) * (g * g)
    m_hat = m / (1.0 - ADAM_B1 ** ADAM_STEP)
    v_hat = v / (1.0 - ADAM_B2 ** ADAM_STEP)
    delta = -ADAM_LR * (m_hat / (jnp.sqrt(v_hat) + ADAM_EPS) + ADAM_WD * w)
    return delta, m, v


def _adamw_big(w, m, v, slots, name):
    R, C = w.shape
    tr = _tile(R, 32, 16)

    def kern(w_ref, m_ref, v_ref, s_ref, g_ref, d_ref, mo_ref, vo_ref):
        g = s_ref[0].astype(F32)
        for i in range(1, 4):
            g = g + s_ref[i].astype(F32)
        d, mn, vn = _adamw_math(w_ref[...], g, m_ref[...], v_ref[...])
        g_ref[...], d_ref[...], mo_ref[...], vo_ref[...] = g, d, mn, vn

    row = pl.BlockSpec((tr, C), lambda i: (i, 0))
    shp = jax.ShapeDtypeStruct((R, C), F32)
    return _pcall(kern, name=name, grid=(R // tr,),
                  in_specs=[row, row, row, pl.BlockSpec((4, tr, C), lambda i: (0, i, 0))],
                  out_specs=[row] * 4, out_shape=[shp] * 4,
                  compiler_params=_params(("parallel",), 36 * tr * C * 4))(w, m, v, slots)


def _adamw_small(w, g, m, v, name):
    def kern(w_ref, g_ref, m_ref, v_ref, d_ref, mo_ref, vo_ref):
        d_ref[...], mo_ref[...], vo_ref[...] = _adamw_math(w_ref[...], g_ref[...], m_ref[...], v_ref[...])

    vm = pl.BlockSpec(memory_space=pltpu.VMEM)
    shp = jax.ShapeDtypeStruct(w.shape, F32)
    return _pcall(kern, name=name, in_specs=[vm] * 4, out_specs=[vm] * 3, out_shape=[shp] * 3,
                  compiler_params=_params(None, 10 * w.size * 4))(w, g, m, v)


def _pack(arrs):
    flat = jnp.concatenate([a.reshape(-1).astype(F32) for a in arrs])
    n = flat.shape[0]
    rows = -(-n // (8 * V7X_LANES)) * 8
    return jnp.pad(flat, (0, rows * V7X_LANES - n)).reshape(rows, V7X_LANES)


def _unpack(buf, shapes):
    flat = buf.reshape(-1)
    out, off = [], 0
    for s in shapes:
        n = int(np.prod(s))
        out.append(flat[off:off + n].reshape(s))
        off += n
    return out


def _reduce_start(gfull, col_sharded, shard_shape, token, tag):
    R, C = shard_shape
    if col_sharded:
        g4 = gfull.reshape(1, 2, R // 2, 4 * C)
    else:
        g4 = gfull.reshape(4, 2, R // 2, C)
    p = _pair_exchange(g4, "pair_exchange_" + tag)
    cs = _pair_add(g4, p, "pair_add_" + tag)
    slots = _pair_add_own(g4, p, col_sharded, "pair_add_own_" + tag)
    return _scatter_start(cs, slots, col_sharded, token, "scatter_start_" + tag)


def _reduce_finish(started, after, col_sharded, w, m, v, tag):
    R, C = w.shape
    _, landed = _scatter_wait(started, after, col_sharded, "scatter_wait_" + tag)
    slots = _scatter_pass_on(landed, "scatter_pass_on_" + tag)
    return _adamw_big(w, m, v, slots.reshape(4, R, C), "adamw_" + tag)


def kernel(x, ln1_w, w_in, lb_gamma, hg_norm_w, lru_conv_w, lru_conv_b, lru_wa, lru_ba, lru_wx, lru_bx, lru_lambda, lru_norm_w, w_out, ln2_w, ffn_w_up, ffn_conv_w, ffn_conv_b, ffn_w_down, final_norm_w, loss_target, m_ln1_w, m_w_in, m_lb_gamma, m_hg_norm_w, m_lru_conv_w, m_lru_conv_b, m_lru_wa, m_lru_ba, m_lru_wx, m_lru_bx, m_lru_lambda, m_lru_norm_w, m_w_out, m_ln2_w, m_ffn_w_up, m_ffn_conv_w, m_ffn_conv_b, m_ffn_w_down, m_final_norm_w, v_ln1_w, v_w_in, v_lb_gamma, v_hg_norm_w, v_lru_conv_w, v_lru_conv_b, v_lru_wa, v_lru_ba, v_lru_wx, v_lru_bx, v_lru_lambda, v_lru_norm_w, v_w_out, v_ln2_w, v_ffn_w_up, v_ffn_conv_w, v_ffn_conv_b, v_ffn_w_down, v_final_norm_w):
    B, S, D = x.shape
    T = B * S
    HW = lb_gamma.shape[1]
    LW = lru_conv_b.shape[1]
    assert S % CHUNK == 0 and HW % HEAD_DIM == 0 and lru_wa.shape[2] == HEAD_DIM
    x2 = x.reshape(T, D)
    tgt = loss_target.reshape(T, D)
    jchip = 2 * lax.axis_index("x") + lax.axis_index("y")

    conv_shapes = [lru_conv_w[0].shape, ffn_conv_w[0].shape]
    convs = _gather_small(_pack([lru_conv_w[0], ffn_conv_w[0]]), "gather_conv_w")
    per_chip = [_unpack(convs[j], conv_shapes) for j in range(4)]
    lcw = jnp.concatenate([pc[0] for pc in per_chip], axis=1)
    fcw = jnp.concatenate([pc[1] for pc in per_chip], axis=1)
    masters = dict(w_in=w_in[0], w_out=w_out[0], w_up=ffn_w_up[0], w_down=ffn_w_down[0])
    col_of = dict(w_in=True, w_out=False, w_up=True, w_down=False)
    started, token, after = {}, jnp.zeros(TOKEN.shape, F32), convs
    for n in ("w_in", "w_out", "w_up", "w_down"):
        land = _cast_into_window(masters[n], col_of[n], after, "cast_" + n)
        started[n] = _gather_start(land, masters[n].shape, col_of[n], token, "gather_start_" + n)
        token = after = started[n][3]

    def gathered(n, after):
        landed = _gather_wait(started[n], masters[n].shape, after, col_of[n], "gather_wait_" + n)
        return _gather_pass_on(landed, masters[n].shape, col_of[n], "gather_pass_on_" + n)

    W_in = gathered("w_in", token)

    hn1 = _rms_fwd(x2, ln1_w, "rms1")
    proj = _matmul(hn1, W_in, "NN", F32, 1024, 512, 4096, name="mm_proj")
    o_raw, o_hg, states = _hgrn_fwd(proj, lb_gamma, hg_norm_w, B, HW, "hgrn_fwd")
    h_lru, z = _lru_fwd(proj, lcw, lru_conv_b, lru_wa[0], lru_ba, lru_wx[0], lru_bx, lru_lambda, B, HW, LW, "lru_fwd")
    o_lru = _rms_fwd(z, lru_norm_w, "rms_lru")
    mix = jnp.concatenate([o_hg, o_lru], axis=1)
    W_out = gathered("w_out", mix)
    h1 = _matmul(mix, W_out, "NN", F32, 1024, 512, 4096, add=x2, name="mm_out")
    hn2 = _rms_fwd(h1, ln2_w, "rms2")
    W_up = gathered("w_up", hn2)
    up = _matmul(hn2, W_up, "NN", F32, 1024, 512, 4096, name="mm_up")
    act = _ffn_act(up, fcw, ffn_conv_b, B, "ffn_act")
    W_down = gathered("w_down", act)
    h2 = _matmul(act, W_down, "NN", F32, 2048, 1024, 256, add=h1, name="mm_down")

    dh2, dh2a, d_final_w, loss_part = _loss_bwd(h2, tgt, final_norm_w.reshape(1, D), "loss_bwd")
    g_down = _matmul(act, dh2a, "TN", ACT_DTYPE, 256, 2048, 4096, n_outer=True, name="mm_g_down")
    red_down = _reduce_start(g_down, False, ffn_w_down[0].shape, token, "w_down")
    d_act = _matmul(dh2a, W_down, "NT", ACT_DTYPE, 512, 5504, 512, after=red_down[4], name="mm_d_act")
    d_up, d_fcw, d_fcb = _ffn_act_bwd(up, fcw, ffn_conv_b, d_act, B, "ffn_act_bwd")
    g_up = _matmul(hn2, d_up, "TN", ACT_DTYPE, 1024, 512, 4096, name="mm_g_up")
    red_up = _reduce_start(g_up, True, ffn_w_up[0].shape, red_down[4], "w_up")
    d_hn2 = _matmul(d_up, W_up, "NT", F32, 2048, 1024, 512, after=red_up[4], name="mm_d_hn2")
    dh1, dh1a, d_ln2 = _rms_bwd(h1, ln2_w, d_hn2, 0, dh2, True, "rms2_bwd")
    g_out = _matmul(mix, dh1a, "TN", ACT_DTYPE, 1024, 512, 4096, name="mm_g_out")
    red_out = _reduce_start(g_out, False, w_out[0].shape, red_up[4], "w_out")
    d_mix = _matmul(dh1a, W_out, "NT", F32, 1024, 512, 4096, after=red_out[4], name="mm_d_mix")
    dz, d_lru_norm = _rms_bwd(z, lru_norm_w, d_mix, HW // LW, None, False, "rms_lru_bwd")
    (d_xr, d_yr, d_wa, d_wx, d_ba, d_bx, d_lam, d_lcw, d_lcb) = _lru_bwd(
        proj, lcw, lru_conv_b, lru_wa[0], lru_ba, lru_wx[0], lru_bx, lru_lambda, h_lru, dz, B, HW, LW, "lru_bwd")
    d_q, d_f, d_i, d_g, d_lbg, d_hgw = _hgrn_bwd(proj, lb_gamma, hg_norm_w, o_raw, states, d_mix, B, HW, "hgrn_bwd")
    d_proj = jnp.concatenate([d_q, d_f, d_i, d_g, d_xr, d_yr], axis=1)
    g_in = _matmul(hn1, d_proj, "TN", ACT_DTYPE, 1024, 512, 4096, name="mm_g_in")
    red_in = _reduce_start(g_in, True, w_in[0].shape, red_out[4], "w_in")
    d_hn1 = _matmul(d_proj, W_in, "NT", F32, 1024, 1024, 1024, after=red_in[4], name="mm_d_hn1")
    dx, d_ln1 = _rms_bwd(x2, ln1_w, d_hn1, 0, dh1, False, "rms1_bwd")

    big = {}
    big["ffn_w_down"] = _reduce_finish(red_down, dx, False, ffn_w_down[0], m_ffn_w_down[0], v_ffn_w_down[0], "w_down")
    big["ffn_w_up"] = _reduce_finish(red_up, big["ffn_w_down"][1], True, ffn_w_up[0], m_ffn_w_up[0], v_ffn_w_up[0], "w_up")
    big["w_out"] = _reduce_finish(red_out, big["ffn_w_up"][1], False, w_out[0], m_w_out[0], v_w_out[0], "w_out")
    big["w_in"] = _reduce_finish(red_in, big["w_out"][1], True, w_in[0], m_w_in[0], v_w_in[0], "w_in")

    small_names = ["ln1_w", "lb_gamma", "hg_norm_w", "lru_conv_w", "lru_conv_b", "lru_wa", "lru_ba", "lru_wx",
                   "lru_bx", "lru_lambda", "lru_norm_w", "ln2_w", "ffn_conv_w", "ffn_conv_b", "final_norm_w"]
    small_grads = [d_ln1, d_lbg, d_hgw, d_lcw, d_lcb, d_wa, d_ba, d_wx, d_bx, d_lam, d_lru_norm, d_ln2,
                   d_fcw, d_fcb, d_final_w]
    red = _allreduce_small(_pack([loss_part[0:1, 0:1]] + small_grads), "allreduce_small")
    red = _unpack(red, [(1, 1)] + [g.shape for g in small_grads])
    loss = red[0].reshape(())
    gs = dict(zip(small_names, red[1:]))
    nlc, nfc = lru_conv_w.shape[2], ffn_conv_w.shape[2]
    gs["lru_conv_w"] = lax.dynamic_slice_in_dim(gs["lru_conv_w"], jchip * nlc, nlc, axis=1)
    gs["ffn_conv_w"] = lax.dynamic_slice_in_dim(gs["ffn_conv_w"], jchip * nfc, nfc, axis=1)
    args = dict(ln1_w=(ln1_w, m_ln1_w, v_ln1_w), lb_gamma=(lb_gamma, m_lb_gamma, v_lb_gamma),
                hg_norm_w=(hg_norm_w, m_hg_norm_w, v_hg_norm_w), lru_conv_w=(lru_conv_w, m_lru_conv_w, v_lru_conv_w),
                lru_conv_b=(lru_conv_b, m_lru_conv_b, v_lru_conv_b), lru_wa=(lru_wa, m_lru_wa, v_lru_wa),
                lru_ba=(lru_ba, m_lru_ba, v_lru_ba), lru_wx=(lru_wx, m_lru_wx, v_lru_wx),
                lru_bx=(lru_bx, m_lru_bx, v_lru_bx), lru_lambda=(lru_lambda, m_lru_lambda, v_lru_lambda),
                lru_norm_w=(lru_norm_w, m_lru_norm_w, v_lru_norm_w), ln2_w=(ln2_w, m_ln2_w, v_ln2_w),
                ffn_conv_w=(ffn_conv_w, m_ffn_conv_w, v_ffn_conv_w), ffn_conv_b=(ffn_conv_b, m_ffn_conv_b, v_ffn_conv_b),
                final_norm_w=(final_norm_w, m_final_norm_w, v_final_norm_w))
    shapes = [args[n][0].shape for n in small_names]
    upd = _adamw_small(_pack([args[n][0] for n in small_names]), _pack([gs[n] for n in small_names]),
                       _pack([args[n][1] for n in small_names]), _pack([args[n][2] for n in small_names]), "adamw_small")
    s_delta, s_m, s_v = (dict(zip(small_names, _unpack(u, shapes))) for u in upd)

    order = ["ln1_w", "w_in", "lb_gamma", "hg_norm_w", "lru_conv_w", "lru_conv_b", "lru_wa", "lru_ba", "lru_wx",
             "lru_bx", "lru_lambda", "lru_norm_w", "w_out", "ln2_w", "ffn_w_up", "ffn_conv_w", "ffn_conv_b",
             "ffn_w_down", "final_norm_w"]
    full_shape = dict(w_in=w_in.shape, w_out=w_out.shape, ffn_w_up=ffn_w_up.shape, ffn_w_down=ffn_w_down.shape)
    grads, deltas, new_m, new_v = [], [], [], []
    for n in order:
        if n in big:
            g, d, mn, vn = (t.reshape(full_shape[n]) for t in big[n])
        else:
            g, d, mn, vn = gs[n].reshape(args[n][0].shape), s_delta[n], s_m[n], s_v[n]
        grads.append(g), deltas.append(d), new_m.append(mn), new_v.append(vn)
    return (loss, dx.reshape(B, S, D), *grads, *deltas, *new_m, *new_v)
```

```python
import functools
import math

import numpy as np
import jax
import jax.numpy as jnp
from jax import lax
from jax.experimental import pallas as pl
from jax.experimental.pallas import tpu as pltpu

F32 = jnp.float32
MXU_DTYPE = jnp.bfloat16
ACT_DTYPE = jnp.bfloat16

EPS = 1e-6
HEAD_DIM = 128
CHUNK = 64
LEVEL_HALVES = (32, 16, 8, 4, 2, 1)
LRU_CONV = 4
FFN_CONV = 3
LRU_C = 8.0
ADAM_LR, ADAM_B1, ADAM_B2, ADAM_EPS, ADAM_WD, ADAM_STEP = 0.001, 0.9, 0.999, 1e-08, 0.01, 10

V7X_LANES = 128
V7X_VMEM_BUDGET = 56 << 20

NN = (((1,), (0,)), ((), ()))
NT = (((1,), (1,)), ((), ()))
TN = (((0,), (0,)), ((), ()))
MESH = pl.DeviceIdType.MESH
HBM_SPEC = pl.BlockSpec(memory_space=pl.ANY)


def _pcall(kern, **kw):
    return pl.pallas_call(kern, **kw)


def _params(sem=None, vmem=None):
    kw = {}
    if sem is not None:
        kw["dimension_semantics"] = sem
    if vmem is not None:
        kw["vmem_limit_bytes"] = int(min(max(vmem, 16 << 20), V7X_VMEM_BUDGET))
    return pltpu.CompilerParams(**kw)


def _dot(a, b, dims=NN):
    return lax.dot_general(a.astype(MXU_DTYPE), b.astype(MXU_DTYPE), dims, preferred_element_type=F32)


def _tile(dim, pref, align):
    t = min(pref, dim) // align * align
    while t >= align:
        if dim % t == 0:
            return t
        t -= align
    return dim


def _sigmoid(x):
    return 1.0 / (1.0 + jnp.exp(-x))


def _silu_and_grad(x):
    s = _sigmoid(x)
    return x * s, s * (1.0 + x * (1.0 - s))


def _gelu_and_grad(x):
    k0, k1 = math.sqrt(2.0 / math.pi), 0.044715
    t = jnp.tanh(k0 * (x + k1 * x * x * x))
    g = 0.5 * x * (1.0 + t)
    dg = 0.5 * (1.0 + t) + 0.5 * x * (1.0 - t * t) * k0 * (1.0 + 3.0 * k1 * x * x)
    return g, dg


def _one_minus_exp(x):
    p = x * (1.0 + x * (0.5 + x * (1.0 / 6.0 + x * (1.0 / 24.0 + x * (1.0 / 120.0)))))
    return jnp.where(x > -0.05, -p, 1.0 - jnp.exp(x))


def _rows(n):
    return lax.broadcasted_iota(jnp.int32, (n, 1), 0)


def _matmul(a, b, mode, out_dtype, tm, tn, tk, add=None, after=None, n_outer=False, name="mm"):
    if mode == "TN":
        K, M = a.shape
    else:
        M, K = a.shape
    N = b.shape[0] if mode == "NT" else b.shape[1]
    tm, tn = _tile(M, tm, V7X_LANES), _tile(N, tn, V7X_LANES)
    tk = _tile(K, tk, V7X_LANES)
    nk = K // tk
    dims = {"NN": NN, "NT": NT, "TN": TN}[mode]
    order = (lambda f: (lambda j, i, k: f(i, j, k))) if n_outer else (lambda f: f)
    a_spec = (pl.BlockSpec((tk, tm), order(lambda i, j, k: (k, i))) if mode == "TN"
              else pl.BlockSpec((tm, tk), order(lambda i, j, k: (i, k))))
    b_spec = (pl.BlockSpec((tn, tk), order(lambda i, j, k: (j, k))) if mode == "NT"
              else pl.BlockSpec((tk, tn), order(lambda i, j, k: (k, j))))
    o_spec = pl.BlockSpec((tm, tn), order(lambda i, j, k: (i, j)))
    has_add = add is not None

    def kern(*refs):
        a_ref, b_ref = refs[:2]
        add_ref = refs[2] if has_add else None

        def finish(r, o_ref):
            if has_add:
                r = r + add_ref[...]
            o_ref[...] = r.astype(out_dtype)

        if nk == 1:
            finish(_dot(a_ref[...], b_ref[...], dims), refs[-1])
            return
        o_ref, acc_ref = refs[-2:]
        k = pl.program_id(2)

        @pl.when(k == 0)
        def _():
            acc_ref[...] = jnp.zeros_like(acc_ref)

        acc_ref[...] += _dot(a_ref[...], b_ref[...], dims)

        @pl.when(k == nk - 1)
        def _():
            finish(acc_ref[...], o_ref)

    ab = jnp.dtype(a.dtype).itemsize
    ob = jnp.dtype(out_dtype).itemsize
    vmem = 2 * (tm * tk + tk * tn) * ab + tm * tn * (8 + 2 * ob + (8 if has_add else 0)) + (4 << 20)
    ins = [a, b] + ([add] if has_add else []) + ([after] if after is not None else [])
    in_specs = [a_spec, b_spec] + ([o_spec] if has_add else []) + ([HBM_SPEC] if after is not None else [])
    grid = (N // tn, M // tm, nk) if n_outer else (M // tm, N // tn, nk)
    return _pcall(
        kern, name=name, grid=grid,
        in_specs=in_specs, out_specs=o_spec,
        out_shape=jax.ShapeDtypeStruct((M, N), out_dtype),
        scratch_shapes=[pltpu.VMEM((tm, tn), F32)] if nk > 1 else [],
        compiler_params=_params(("parallel", "parallel", "arbitrary"), vmem),
    )(*ins)


def _rms_fwd(x, w, name):
    T, D = x.shape
    tm = _tile(T, 256, 16)

    def kern(x_ref, w_ref, o_ref):
        xv = x_ref[...]
        r = lax.rsqrt(jnp.mean(xv * xv, axis=-1, keepdims=True) + EPS)
        o_ref[...] = (xv * r * w_ref[...]).astype(ACT_DTYPE)

    return _pcall(kern, name=name, grid=(T // tm,),
                  in_specs=[pl.BlockSpec((tm, D), lambda i: (i, 0)), pl.BlockSpec((1, D), lambda i: (0, 0))],
                  out_specs=pl.BlockSpec((tm, D), lambda i: (i, 0)),
                  out_shape=jax.ShapeDtypeStruct((T, D), ACT_DTYPE),
                  compiler_params=_params(("parallel",), 8 * tm * D * 4))(x, w)


def _rms_bwd(x, w, g, g_col, res, want_act, name, after=None):
    T, D = x.shape
    tm = _tile(T, 256, 16)
    has_res = res is not None

    def kern(*refs):
        refs = list(refs)
        x_ref, w_ref, g_ref = refs[:3]
        res_ref = refs[3] if has_res else None
        outs = refs[3 + has_res + (after is not None):]
        dx_ref = outs[0]
        dxa_ref = outs[1] if want_act else None
        dw_ref = outs[-1]
        i = pl.program_id(0)
        xv = x_ref[...]
        gv = g_ref[...].astype(F32)
        r = lax.rsqrt(jnp.mean(xv * xv, axis=-1, keepdims=True) + EPS)
        gw = gv * w_ref[...]
        dx = r * gw - xv * (r * r * r) * jnp.mean(gw * xv, axis=-1, keepdims=True)
        if has_res:
            dx = dx + res_ref[...]
        dx_ref[...] = dx
        if want_act:
            dxa_ref[...] = dx.astype(ACT_DTYPE)

        @pl.when(i == 0)
        def _():
            dw_ref[...] = jnp.zeros_like(dw_ref)

        dw_ref[...] += jnp.sum(gv * xv * r, axis=0, keepdims=True)

    row = pl.BlockSpec((tm, D), lambda i: (i, 0))
    vec = pl.BlockSpec((1, D), lambda i: (0, 0))
    in_specs = ([row, vec, pl.BlockSpec((tm, D), lambda i: (i, g_col))] + ([row] if has_res else [])
                + ([HBM_SPEC] if after is not None else []))
    out_specs = [row] + ([row] if want_act else []) + [vec]
    out_shape = ([jax.ShapeDtypeStruct((T, D), F32)]
                 + ([jax.ShapeDtypeStruct((T, D), ACT_DTYPE)] if want_act else [])
                 + [jax.ShapeDtypeStruct((1, D), F32)])
    ins = [x, w, g] + ([res] if has_res else []) + ([after] if after is not None else [])
    return _pcall(kern, name=name, grid=(T // tm,), in_specs=in_specs, out_specs=out_specs,
                  out_shape=out_shape, compiler_params=_params(("arbitrary",), 14 * tm * D * 4))(*ins)


def _loss_bwd(h, target, w, name):
    T, D = h.shape
    tm = _tile(T, 256, 16)

    def kern(h_ref, t_ref, w_ref, dh_ref, dha_ref, dw_ref, loss_ref):
        i = pl.program_id(0)
        hv = h_ref[...]
        r = lax.rsqrt(jnp.mean(hv * hv, axis=-1, keepdims=True) + EPS)
        e = hv * r * w_ref[...] - t_ref[...]
        dy = e * (1.0 / D)
        gw = dy * w_ref[...]
        dh = r * gw - hv * (r * r * r) * jnp.mean(gw * hv, axis=-1, keepdims=True)
        dh_ref[...] = dh
        dha_ref[...] = dh.astype(ACT_DTYPE)

        @pl.when(i == 0)
        def _():
            dw_ref[...] = jnp.zeros_like(dw_ref)
            loss_ref[...] = jnp.zeros_like(loss_ref)

        dw_ref[...] += jnp.sum(dy * hv * r, axis=0, keepdims=True)
        part = 0.5 * jnp.sum(jnp.mean(e * e, axis=-1, keepdims=True), axis=0, keepdims=True)
        loss_ref[...] += jnp.broadcast_to(part, loss_ref.shape)

    row = pl.BlockSpec((tm, D), lambda i: (i, 0))
    vec = pl.BlockSpec((1, D), lambda i: (0, 0))
    return _pcall(kern, name=name, grid=(T // tm,), in_specs=[row, row, vec],
                  out_specs=[row, row, vec, pl.BlockSpec((8, V7X_LANES), lambda i: (0, 0))],
                  out_shape=[jax.ShapeDtypeStruct((T, D), F32), jax.ShapeDtypeStruct((T, D), ACT_DTYPE),
                             jax.ShapeDtypeStruct((1, D), F32), jax.ShapeDtypeStruct((8, V7X_LANES), F32)],
                  compiler_params=_params(("arbitrary",), 14 * tm * D * 4))(h, target, w)


def _conv(x, w_ref, b, width):
    S = x.shape[0]
    row = _rows(S)
    y = b + x * w_ref[pl.ds(width - 1, 1), :]
    for j in range(width - 1):
        sh = width - 1 - j
        y = y + jnp.where(row >= sh, pltpu.roll(x, sh, 0), 0.0) * w_ref[pl.ds(j, 1), :]
    return y


def _conv_bwd(x, dy, w_ref, width):
    S = dy.shape[0]
    row = _rows(S)
    dx = dy * w_ref[pl.ds(width - 1, 1), :]
    dw = [None] * (width - 1) + [jnp.sum(x * dy, axis=0, keepdims=True)]
    for j in range(width - 1):
        sh = width - 1 - j
        dys = jnp.where(row < S - sh, pltpu.roll(dy, S - sh, 0), 0.0)
        dx = dx + dys * w_ref[pl.ds(j, 1), :]
        dw[j] = jnp.sum(x * dys, axis=0, keepdims=True)
    return dx, dw


def _ffn_act(up, cw, cb, B, name):
    T, F2 = up.shape
    S, F = T // B, F2 // 2
    tw = _tile(F, 256, V7X_LANES)
    nt = F // tw

    def kern(g_ref, v_ref, wg_ref, wv_ref, bg_ref, bv_ref, o_ref, mg_ref, mv_ref):
        gc = _conv(g_ref[...], wg_ref, bg_ref[...], FFN_CONV)
        vc = _conv(v_ref[...], wv_ref, bv_ref[...], FFN_CONV)
        silu, dsilu = _silu_and_grad(gc)
        o_ref[...] = (silu * vc).astype(ACT_DTYPE)
        mg_ref[...] = (vc * dsilu).astype(ACT_DTYPE)
        mv_ref[...] = silu.astype(ACT_DTYPE)

    blk = lambda off: pl.BlockSpec((S, tw), lambda b, i: (b, off + i))
    wblk = lambda off: pl.BlockSpec((FFN_CONV, tw), lambda b, i: (0, off + i))
    bblk = lambda off: pl.BlockSpec((1, tw), lambda b, i: (0, off + i))
    half = jax.ShapeDtypeStruct((T, F), ACT_DTYPE)
    return _pcall(kern, name=name, grid=(B, nt),
                  in_specs=[blk(0), blk(nt), wblk(0), wblk(nt), bblk(0), bblk(nt)],
                  out_specs=[blk(0), blk(0), blk(0)], out_shape=[half, half, half],
                  compiler_params=_params(("parallel", "parallel"), 20 * S * tw * 4))(up, up, cw, cw, cb, cb)


def _ffn_act_bwd(up, cw, mg, mv, d_act, B, name):
    T, F2 = up.shape
    S, F = T // B, F2 // 2
    tw = _tile(F, 256, V7X_LANES)
    nt = F // tw

    def kern(s_ref, ws_ref, mg_ref, mv_ref, da_ref, du_ref, dcw_ref, dcb_ref):
        t, b = pl.program_id(0), pl.program_id(1)
        mult = jnp.where(t < nt, mg_ref[...], mv_ref[...])
        d = da_ref[...].astype(F32) * mult.astype(F32)
        dx, dw = _conv_bwd(s_ref[...], d, ws_ref, FFN_CONV)
        du_ref[...] = dx.astype(ACT_DTYPE)

        @pl.when(b == 0)
        def _():
            dcw_ref[...] = jnp.zeros_like(dcw_ref)
            dcb_ref[...] = jnp.zeros_like(dcb_ref)

        for j, rj in enumerate(dw):
            dcw_ref[pl.ds(j, 1), :] += rj
        dcb_ref[...] += jnp.sum(d, axis=0, keepdims=True)

    own = lambda t, b: (b, t % nt)
    return _pcall(
        kern, name=name, grid=(2 * nt, B),
        in_specs=[pl.BlockSpec((S, tw), lambda t, b: (b, t)),
                  pl.BlockSpec((FFN_CONV, tw), lambda t, b: (0, t)),
                  pl.BlockSpec((S, tw), own), pl.BlockSpec((S, tw), own), pl.BlockSpec((S, tw), own)],
        out_specs=[pl.BlockSpec((S, tw), lambda t, b: (b, t)),
                   pl.BlockSpec((FFN_CONV, tw), lambda t, b: (0, t)),
                   pl.BlockSpec((1, tw), lambda t, b: (0, t))],
        out_shape=[jax.ShapeDtypeStruct((T, F2), ACT_DTYPE), jax.ShapeDtypeStruct((FFN_CONV, F2), F32),
                   jax.ShapeDtypeStruct((1, F2), F32)],
        compiler_params=_params(("parallel", "arbitrary"), 20 * S * tw * 4),
    )(up, cw, mg, mv, d_act)


def _hgrn_tables():
    C = CHUNK
    t = np.arange(C)
    mats = [(t[:, None] >= t[None, :]).astype(np.float32)]
    masks = []
    gsum = [(t[:, None] <= t[None, :]).astype(np.float32), (t[:, None] > t[None, :]).astype(np.float32)]
    for hs in LEVEL_HALVES:
        m = (t // (2 * hs)) * 2 * hs + hs
        later = t >= m
        d = np.zeros((C, C), np.float32)
        for i in range(C):
            if later[i]:
                d[i, m[i]:i + 1] = 1.0
            else:
                d[i, i + 1:m[i]] = -1.0
        mats.append(d)
        same = (t[:, None] // (2 * hs)) == (t[None, :] // (2 * hs))
        masks.append((same & later[:, None] & (~later)[None, :]).astype(np.float32))
        gsum.append((same & later[:, None] & (t[None, :] >= t[:, None])).astype(np.float32))
        gsum.append((same & (~later)[:, None] & (t[None, :] < t[:, None])).astype(np.float32))
    return np.concatenate(mats, 0), np.stack(masks, 0), np.concatenate(gsum, 1)


def _split_dot(mat, v):
    hi = v.astype(MXU_DTYPE)
    lo = (v - hi.astype(F32)).astype(MXU_DTYPE)
    r = _dot(mat, jnp.concatenate([hi, lo], axis=1))
    n = v.shape[1]
    return r[:, :n] + r[:, n:]


def _hgrn_gates(qr, fr, lb, mc):
    C = CHUNK
    q, dq_dqr = _silu_and_grad(qr)
    sf = _sigmoid(fr)
    f = lb + (1.0 - lb) * sf
    k = 1.0 - f
    dall = _split_dot(mc, jnp.log(f))
    b = dall[0:C]
    dl = [dall[C * (l + 1):C * (l + 2)] for l in range(len(LEVEL_HALVES))]
    eq = [jnp.exp(jnp.minimum(d, 0.0)) for d in dl]
    ek = [jnp.exp(jnp.minimum(-d, 0.0)) for d in dl]
    return q, dq_dqr, sf, f, k, b, eq, ek


def _hgrn_scores(q, k, eq, ek, masks_ref):
    p = jnp.where(_rows(CHUNK) == lax.broadcasted_iota(jnp.int32, (1, CHUNK), 1),
                  jnp.sum(q * k, axis=-1, keepdims=True), 0.0)
    for l in range(len(LEVEL_HALVES)):
        p = p + masks_ref[l] * _dot(q * eq[l], k * ek[l], NT)
    return p


def _hgrn_fwd(proj, lb_gamma, norm_w, B, HW, name):
    T = proj.shape[0]
    S, H, C = T // B, HW // HEAD_DIM, CHUNK
    NC = S // C
    mc_np, masks_np, _ = _hgrn_tables()
    mc, masks = jnp.asarray(mc_np, MXU_DTYPE), jnp.asarray(masks_np, F32)

    def kern(q_ref, f_ref, i_ref, g_ref, lbg_ref, nw_ref, mc_ref, masks_ref, oraw_ref, o_ref, st_ref):
        g0, g1 = lbg_ref[pl.ds(0, 1), :], lbg_ref[pl.ds(1, 1), :]
        mx = jnp.maximum(g0, g1)
        e0, e1 = jnp.exp(g0 - mx), jnp.exp(g1 - mx)
        lb = e0 / (e0 + e1)
        nw = nw_ref[...]
        mcv = mc_ref[...]

        def body(n, sts):
            out = []
            for s, st in enumerate(sts):
                rows = pl.ds(pl.multiple_of(s * S + n * C, C), C)
                st_ref[s, n] = st
                q, _, _, _, k, b, eq, ek = _hgrn_gates(q_ref[rows, :], f_ref[rows, :], lb, mcv)
                v = i_ref[rows, :]
                o = _dot(q * jnp.exp(b), st, NT) + _dot(_hgrn_scores(q, k, eq, ek, masks_ref), v)
                b_last = b[C - 1:C]
                out.append(st * jnp.exp(b_last) + _dot(v, k * jnp.exp(b_last - b), TN))
                oraw_ref[rows, :] = o
                r = lax.rsqrt(jnp.mean(o * o, axis=-1, keepdims=True) + EPS)
                gate, _ = _silu_and_grad(g_ref[rows, :])
                o_ref[rows, :] = (o * r * nw * gate).astype(ACT_DTYPE)
            return tuple(out)

        lax.fori_loop(0, NC, body, tuple(jnp.zeros((HEAD_DIM, HEAD_DIM), F32) for _ in range(B)))

    col = lambda off: pl.BlockSpec((T, HEAD_DIM), lambda h: (0, off + h))
    return _pcall(
        kern, name=name, grid=(H,),
        in_specs=[col(0), col(H), col(2 * H), col(3 * H),
                  pl.BlockSpec((2, HEAD_DIM), lambda h: (0, h)),
                  pl.BlockSpec((1, HEAD_DIM), lambda h: (0, h)),
                  pl.BlockSpec(mc.shape, lambda h: (0, 0)),
                  pl.BlockSpec(masks.shape, lambda h: (0, 0, 0))],
        out_specs=[col(0), col(0),
                   pl.BlockSpec((B, None, NC, HEAD_DIM, HEAD_DIM), lambda h: (0, h, 0, 0, 0))],
        out_shape=[jax.ShapeDtypeStruct((T, HW), F32), jax.ShapeDtypeStruct((T, HW), ACT_DTYPE),
                   jax.ShapeDtypeStruct((B, H, NC, HEAD_DIM, HEAD_DIM), F32)],
        compiler_params=_params(("parallel",), 20 * T * HEAD_DIM * 4 + (8 << 20)),
    )(proj, proj, proj, proj, lb_gamma, norm_w, mc, masks)


def _hgrn_bwd(proj, lb_gamma, norm_w, o_raw, states, d_mix, B, HW, name):
    T = proj.shape[0]
    S, H, C = T // B, HW // HEAD_DIM, CHUNK
    NC = S // C
    mc_np, masks_np, gsum_np = _hgrn_tables()
    mc, masks, gsum = jnp.asarray(mc_np, MXU_DTYPE), jnp.asarray(masks_np, F32), jnp.asarray(gsum_np, MXU_DTYPE)
    nl = len(LEVEL_HALVES)

    def kern(q_ref, f_ref, i_ref, g_ref, lbg_ref, nw_ref, mc_ref, masks_ref, gsum_ref, oraw_ref, st_ref, do_ref,
             dq_ref, df_ref, di_ref, dg_ref, dlbg_ref, dnw_ref):
        g0, g1 = lbg_ref[pl.ds(0, 1), :], lbg_ref[pl.ds(1, 1), :]
        mx = jnp.maximum(g0, g1)
        e0, e1 = jnp.exp(g0 - mx), jnp.exp(g1 - mx)
        lb = e0 / (e0 + e1)
        nw = nw_ref[...]
        mcv, gsumv = mc_ref[...], gsum_ref[...]

        def chunk(s, n, dst, dlb, dnw):
            rows = pl.ds(pl.multiple_of(s * S + n * C, C), C)
            qr, fr, v = q_ref[rows, :], f_ref[rows, :], i_ref[rows, :]
            q, dq_dqr, sf, f, k, b, eq, ek = _hgrn_gates(qr, fr, lb, mcv)
            o = oraw_ref[rows, :]
            dout = do_ref[rows, :].astype(F32)
            gate, dgate = _silu_and_grad(g_ref[rows, :])
            r = lax.rsqrt(jnp.mean(o * o, axis=-1, keepdims=True) + EPS)
            dg_ref[rows, :] = (dout * o * r * nw * dgate).astype(ACT_DTYPE)
            don = dout * gate
            dnw = dnw + jnp.sum(don * o * r, axis=0, keepdims=True)
            gw = don * nw
            do = r * gw - o * (r * r * r) * jnp.mean(gw * o, axis=-1, keepdims=True)
            st_prev = st_ref[s, n]
            eb = jnp.exp(b)
            b_last = b[C - 1:C]
            ebl = jnp.exp(b_last - b)
            p = _hgrn_scores(q, k, eq, ek, masks_ref)
            dp = _dot(do, v, NT)
            dpd = jnp.sum(do * v, axis=-1, keepdims=True)
            dq_state = _dot(do, st_prev) * eb
            dk_state = _dot(v, dst) * ebl
            dq = dq_state + dpd * k
            dk = dk_state + dpd * q
            pairs = [q * dq_state, k * dk_state]
            for l in range(nl):
                mdp = masks_ref[l] * dp
                dql = _dot(mdp, k * ek[l]) * eq[l]
                dkl = _dot(mdp, q * eq[l], TN) * ek[l]
                dq, dk = dq + dql, dk + dkl
                pairs += [q * dql, k * dkl]
            dv = _dot(p, do, TN) + _dot(k * ebl, dst, NT)
            through = jnp.exp(b_last) * jnp.sum(dst * st_prev, axis=0, keepdims=True)
            dlg = _split_dot(gsumv, jnp.concatenate(pairs, axis=0)) + through
            dst = dst * jnp.exp(b_last) + _dot(do, q * eb, TN)
            dq_ref[rows, :] = (dq * dq_dqr).astype(ACT_DTYPE)
            dfv = dlg / f - dk
            df_ref[rows, :] = (dfv * (1.0 - lb) * sf * (1.0 - sf)).astype(ACT_DTYPE)
            di_ref[rows, :] = dv.astype(ACT_DTYPE)
            dlb = dlb + jnp.sum(dfv * (1.0 - sf), axis=0, keepdims=True)
            return dst, dlb, dnw

        def body(it, carry):
            dsts, dlb, dnw = carry
            out = []
            for s, dst in enumerate(dsts):
                dst, dlb, dnw = chunk(s, NC - 1 - it, dst, dlb, dnw)
                out.append(dst)
            return tuple(out), dlb, dnw

        zrow = jnp.zeros((1, HEAD_DIM), F32)
        zst = tuple(jnp.zeros((HEAD_DIM, HEAD_DIM), F32) for _ in range(B))
        _, dlb, dnw = lax.fori_loop(0, NC, body, (zst, zrow, zrow))
        dg0 = dlb * lb * (1.0 - lb)
        dlbg_ref[pl.ds(0, 1), :] = dg0
        dlbg_ref[pl.ds(1, 1), :] = -dg0
        dnw_ref[...] = dnw

    col = lambda off: pl.BlockSpec((T, HEAD_DIM), lambda h: (0, off + h))
    full = lambda a: pl.BlockSpec(a.shape, lambda h: (0,) * a.ndim)
    part = jax.ShapeDtypeStruct((T, HW), ACT_DTYPE)
    return _pcall(
        kern, name=name, grid=(H,),
        in_specs=[col(0), col(H), col(2 * H), col(3 * H),
                  pl.BlockSpec((2, HEAD_DIM), lambda h: (0, h)),
                  pl.BlockSpec((1, HEAD_DIM), lambda h: (0, h)),
                  full(mc), full(masks), full(gsum), col(0),
                  pl.BlockSpec((B, None, NC, HEAD_DIM, HEAD_DIM), lambda h: (0, h, 0, 0, 0)),
                  col(0)],
        out_specs=[col(0), col(0), col(0), col(0),
                   pl.BlockSpec((2, HEAD_DIM), lambda h: (0, h)),
                   pl.BlockSpec((1, HEAD_DIM), lambda h: (0, h))],
        out_shape=[part, part, part, part, jax.ShapeDtypeStruct((2, HW), F32), jax.ShapeDtypeStruct((1, HW), F32)],
        compiler_params=_params(("parallel",), 28 * T * HEAD_DIM * 4 + (8 << 20)),
    )(proj, proj, proj, proj, lb_gamma, norm_w, mc, masks, gsum, o_raw, states, d_mix)


def _lru_gates(xr, cw_ref, cb, wa, ba, wx, bx, lam):
    S = xr.shape[0]
    xb = _conv(xr, cw_ref, cb, LRU_CONV)
    r = _sigmoid(_dot(xb, wa) + ba)
    ig = _sigmoid(_dot(xb, wx) + bx)
    sp = jnp.maximum(-lam, 0.0) + jnp.log(1.0 + jnp.exp(-jnp.abs(lam)))
    la = -LRU_C * r * sp
    a = jnp.exp(la)
    mult = jnp.where(_rows(S) == 0, 1.0, jnp.sqrt(_one_minus_exp(2.0 * la)))
    return xb, r, ig, sp, a, mult


def _scan_rows(a_ref, u_ref, h_ref, reverse):
    S, W = a_ref.shape
    nb = S // 8
    row = _rows(8)

    def body(it, carry):
        blk = nb - 1 - it if reverse else it
        rows = pl.ds(pl.multiple_of(blk * 8, 8), 8)
        a, u = a_ref[rows, :], u_ref[rows, :]
        for d in (1, 2, 4):
            sh = 8 - d if reverse else d
            keep = (row < 8 - d) if reverse else (row >= d)
            u = u + jnp.where(keep, a * pltpu.roll(u, sh, 0), 0.0)
            a = jnp.where(keep, a * pltpu.roll(a, sh, 0), a)
        h = u + a * carry
        h_ref[rows, :] = h
        return h[0:1] if reverse else h[7:8]

    lax.fori_loop(0, nb, body, jnp.zeros((1, W), F32))


def _lru_fwd(proj, cw, cb, wa, ba, wx, bx, lam, B, HW, LW, name):
    T = proj.shape[0]
    S, NB = T // B, LW // HEAD_DIM
    xoff, yoff = 4 * HW // HEAD_DIM, 4 * HW // HEAD_DIM + NB

    def kern(x_ref, y_ref, cw_ref, cb_ref, wa_ref, ba_ref, wx_ref, bx_ref, lam_ref, h_ref, z_ref, a_s, u_s):
        xb, _, ig, _, a, mult = _lru_gates(x_ref[...], cw_ref, cb_ref[...], wa_ref[...], ba_ref[...],
                                           wx_ref[...], bx_ref[...], lam_ref[...])
        a_s[...] = a
        u_s[...] = xb * ig * mult
        _scan_rows(a_s, u_s, h_ref, False)
        gy, _ = _gelu_and_grad(y_ref[...])
        z_ref[...] = h_ref[...] * gy

    blk = lambda off: pl.BlockSpec((S, HEAD_DIM), lambda b, n: (b, off + n))
    vec = pl.BlockSpec((1, HEAD_DIM), lambda b, n: (0, n))
    mat = pl.BlockSpec((None, HEAD_DIM, HEAD_DIM), lambda b, n: (n, 0, 0))
    return _pcall(
        kern, name=name, grid=(B, NB),
        in_specs=[blk(xoff), blk(yoff), pl.BlockSpec((LRU_CONV, HEAD_DIM), lambda b, n: (0, n)),
                  vec, mat, vec, mat, vec, vec],
        out_specs=[blk(0), blk(0)],
        out_shape=[jax.ShapeDtypeStruct((T, LW), F32), jax.ShapeDtypeStruct((T, LW), F32)],
        scratch_shapes=[pltpu.VMEM((S, HEAD_DIM), F32), pltpu.VMEM((S, HEAD_DIM), F32)],
        compiler_params=_params(("parallel", "parallel"), 24 * S * HEAD_DIM * 4),
    )(proj, proj, cw, cb, wa, ba, wx, bx, lam)


def _lru_bwd(proj, cw, cb, wa, ba, wx, bx, lam, h, dz, B, HW, LW, name):
    T = proj.shape[0]
    S, NB = T // B, LW // HEAD_DIM
    xoff, yoff = 4 * HW // HEAD_DIM, 4 * HW // HEAD_DIM + NB

    def kern(x_ref, y_ref, cw_ref, cb_ref, wa_ref, ba_ref, wx_ref, bx_ref, lam_ref, h_ref, dz_ref,
             dx_ref, dy_ref, dwa_ref, dwx_ref, dba_ref, dbx_ref, dlam_ref, dcw_ref, dcb_ref, a_s, u_s, dh_s):
        bi = pl.program_id(1)
        row = _rows(S)
        xr, lam = x_ref[...], lam_ref[...]
        wa, wx = wa_ref[...], wx_ref[...]
        xb, r, ig, sp, a, mult = _lru_gates(xr, cw_ref, cb_ref[...], wa, ba_ref[...], wx, bx_ref[...], lam)
        hv, dz = h_ref[...], dz_ref[...]
        gy, dgy = _gelu_and_grad(y_ref[...])
        dy_ref[...] = (dz * hv * dgy).astype(ACT_DTYPE)
        a_s[...] = jnp.where(row < S - 1, pltpu.roll(a, S - 1, 0), 0.0)
        u_s[...] = dz * gy
        _scan_rows(a_s, u_s, dh_s, True)
        dh = dh_s[...]
        h_prev = jnp.where(row >= 1, pltpu.roll(hv, 1, 0), 0.0)
        d_ig = dh * xb * mult
        d_mult = jnp.where(row == 0, 0.0, dh * xb * ig)
        dxb = dh * ig * mult
        dla = dh * h_prev * a - d_mult * (a * a) / mult
        dpre_r = dla * (-LRU_C * sp) * r * (1.0 - r)
        dpre_i = d_ig * ig * (1.0 - ig)
        dxb = dxb + _dot(dpre_r, wa, NT) + _dot(dpre_i, wx, NT)
        dxr, dcw = _conv_bwd(xr, dxb, cw_ref, LRU_CONV)
        dx_ref[...] = dxr.astype(ACT_DTYPE)

        @pl.when(bi == 0)
        def _():
            for ref in (dwa_ref, dwx_ref, dba_ref, dbx_ref, dlam_ref, dcw_ref, dcb_ref):
                ref[...] = jnp.zeros_like(ref)

        dwa_ref[...] += _dot(xb, dpre_r, TN)
        dwx_ref[...] += _dot(xb, dpre_i, TN)
        dba_ref[...] += jnp.sum(dpre_r, axis=0, keepdims=True)
        dbx_ref[...] += jnp.sum(dpre_i, axis=0, keepdims=True)
        dsp = jnp.sum(dla * (-LRU_C) * r, axis=0, keepdims=True)
        dlam_ref[...] += -dsp * _sigmoid(-lam)
        for j, rj in enumerate(dcw):
            dcw_ref[pl.ds(j, 1), :] += rj
        dcb_ref[...] += jnp.sum(dxb, axis=0, keepdims=True)

    blk = lambda off: pl.BlockSpec((S, HEAD_DIM), lambda n, b: (b, off + n))
    vec = pl.BlockSpec((1, HEAD_DIM), lambda n, b: (0, n))
    mat = pl.BlockSpec((None, HEAD_DIM, HEAD_DIM), lambda n, b: (n, 0, 0))
    cwb = pl.BlockSpec((LRU_CONV, HEAD_DIM), lambda n, b: (0, n))
    part = jax.ShapeDtypeStruct((T, LW), ACT_DTYPE)
    vshape = jax.ShapeDtypeStruct((1, LW), F32)
    mshape = jax.ShapeDtypeStruct((NB, HEAD_DIM, HEAD_DIM), F32)
    return _pcall(
        kern, name=name, grid=(NB, B),
        in_specs=[blk(xoff), blk(yoff), cwb, vec, mat, vec, mat, vec, vec, blk(0), blk(0)],
        out_specs=[blk(0), blk(0), mat, mat, vec, vec, vec, cwb, vec],
        out_shape=[part, part, mshape, mshape, vshape, vshape, vshape,
                   jax.ShapeDtypeStruct((LRU_CONV, LW), F32), vshape],
        scratch_shapes=[pltpu.VMEM((S, HEAD_DIM), F32)] * 3,
        compiler_params=_params(("parallel", "arbitrary"), 40 * S * HEAD_DIM * 4),
    )(proj, proj, cw, cb, wa, ba, wx, bx, lam, h, dz)


def _pos():
    return lax.axis_index("x"), lax.axis_index("y"), lax.axis_index("c")


def _other_chips(x, y):
    return [(1 - x, y), (x, 1 - y), (1 - x, 1 - y)]


def _remote(src, dst, send_sems, recv_sems, k, to):
    return pltpu.make_async_remote_copy(src_ref=src, dst_ref=dst, send_sem=send_sems.at[k],
                                        recv_sem=recv_sems.at[k], device_id=to, device_id_type=MESH)


HBM_BLK = pl.BlockSpec(memory_space=pltpu.HBM)
SEM_BLK = pl.BlockSpec(memory_space=pltpu.SEMAPHORE)
VMEM_BLK = pl.BlockSpec(memory_space=pltpu.VMEM)
DATAFLOW = pltpu.SideEffectType.DATAFLOW_SIDE_EFFECTING
TOKEN = jax.ShapeDtypeStruct((8, V7X_LANES), F32)


def _in_hbm(a):
    return pltpu.with_memory_space_constraint(a, pltpu.HBM)


def _gather_win(o_ref, R, C, col_sharded):
    Rh = R // 2

    def win(j, h=None):
        if col_sharded:
            rows = pl.ds(0, R) if h is None else pl.ds(h * Rh, Rh)
            return o_ref.at[rows, pl.ds(j * C, C)]
        return o_ref.at[pl.ds(j * R, R) if h is None else pl.ds(j * R + h * Rh, Rh), :]

    return win


def _cast_into_window(w, col_sharded, after, name):
    R, C = w.shape
    tr = _tile(R, 256, 16)
    nr = R // tr
    full = (R, 4 * C) if col_sharded else (4 * R, C)
    j = (2 * lax.axis_index("x") + lax.axis_index("y")).astype(jnp.int32).reshape(1)

    def kern(j_ref, w_ref, after_ref, o_ref):
        o_ref[...] = w_ref[...].astype(ACT_DTYPE)

    out_map = (lambda i, jr: (i, jr[0])) if col_sharded else (lambda i, jr: (jr[0] * nr + i, 0))
    grid_spec = pltpu.PrefetchScalarGridSpec(
        num_scalar_prefetch=1, grid=(nr,),
        in_specs=[pl.BlockSpec((tr, C), lambda i, jr: (i, 0)), HBM_SPEC], out_specs=pl.BlockSpec((tr, C), out_map))
    return _pcall(kern, name=name, grid_spec=grid_spec, out_shape=jax.ShapeDtypeStruct(full, ACT_DTYPE),
                  compiler_params=_params(("parallel",), 6 * tr * C * 4))(j, w, after)


def _gather_start(land, shard_shape, col_sharded, token, name):
    R, C = shard_shape

    def body(land_ref, tok_ref, send_sems, recv_sems, land_thru, tok_out):
        x, y, c = _pos()
        w = _gather_win(land_ref, R, C, col_sharded)(2 * x + y, c)
        for k, (cx, cy) in enumerate(_other_chips(x, y)):
            _remote(w, w, send_sems, recv_sems, k, (cx, cy, c)).start()
        tok_out[...] = tok_ref[...]

    return _pcall(
        body, name=name,
        out_shape=(pltpu.SemaphoreType.DMA((3,)), pltpu.SemaphoreType.DMA((3,)),
                   pltpu.HBM(land.shape, land.dtype), TOKEN),
        in_specs=(HBM_BLK, VMEM_BLK), out_specs=(SEM_BLK, SEM_BLK, HBM_BLK, VMEM_BLK),
        input_output_aliases={0: 2},
        compiler_params=pltpu.CompilerParams(has_side_effects=DATAFLOW),
    )(_in_hbm(land), token)


def _gather_wait(started, shard_shape, after, col_sharded, name):
    send_sems, recv_sems, land_thru, _ = started
    R, C = shard_shape

    def body(land_ref, send_sems, recv_sems, after_ref, got_ref):
        x, y, c = _pos()
        win = _gather_win(land_ref, R, C, col_sharded)
        for k, (cx, cy) in enumerate(_other_chips(x, y)):
            cp = _remote(win(2 * x + y, c), win(2 * cx + cy, c), send_sems, recv_sems, k, (cx, cy, c))
            cp.wait_send()
            cp.wait_recv()

    return _pcall(
        body, name=name, out_shape=pltpu.HBM(land_thru.shape, land_thru.dtype),
        in_specs=(HBM_BLK, SEM_BLK, SEM_BLK, HBM_SPEC), out_specs=HBM_BLK, input_output_aliases={0: 0},
        compiler_params=pltpu.CompilerParams(has_side_effects=DATAFLOW),
    )(land_thru, send_sems, recv_sems, after)


def _gather_pass_on(landed, shard_shape, col_sharded, name):
    R, C = shard_shape

    def body(in_ref, o_ref, send_sems, recv_sems):
        x, y, c = _pos()
        src, dst = _gather_win(in_ref, R, C, col_sharded), _gather_win(o_ref, R, C, col_sharded)
        chips = _other_chips(x, y)
        passed = [_remote(src(2 * cx + cy, c), dst(2 * cx + cy, c), send_sems, recv_sems, k, (x, y, 1 - c))
                  for k, (cx, cy) in enumerate(chips)]
        for cp in passed:
            cp.start()
        for k, (cx, cy) in enumerate(chips):
            w = dst(2 * cx + cy, 1 - c)
            _remote(w, w, send_sems, recv_sems, k, (x, y, c)).wait_recv()
        for cp in passed:
            cp.wait_send()

    return _pcall(body, name=name, in_specs=[HBM_SPEC], out_specs=HBM_SPEC,
                  out_shape=jax.ShapeDtypeStruct(landed.shape, landed.dtype), input_output_aliases={0: 0},
                  scratch_shapes=[pltpu.SemaphoreType.DMA((3,)), pltpu.SemaphoreType.DMA((3,))])(landed)


def _pair_exchange(g4, name):
    J, _, Rh, W = g4.shape

    def body(g_ref, p_ref, send_sems, recv_sems):
        x, y, c = _pos()
        cp = _remote(g_ref.at[pl.ds(0, J), 1 - c], p_ref, send_sems, recv_sems, 0, (x, y, 1 - c))
        cp.start()
        cp.wait()

    return _pcall(body, name=name, in_specs=[HBM_SPEC], out_specs=HBM_SPEC,
                  out_shape=jax.ShapeDtypeStruct((J, Rh, W), g4.dtype),
                  scratch_shapes=[pltpu.SemaphoreType.DMA((1,)), pltpu.SemaphoreType.DMA((1,))])(g4)


def _pair_add(g4, p, name):
    J, _, Rh, W = g4.shape
    tr = _tile(Rh, 256, 16)
    tw = _tile(W, 2048, V7X_LANES)
    c = lax.axis_index("c").astype(jnp.int32).reshape(1)

    def kern(c_ref, g_ref, p_ref, o_ref):
        o_ref[...] = (g_ref[...].astype(F32) + p_ref[...].astype(F32)).astype(ACT_DTYPE)

    grid_spec = pltpu.PrefetchScalarGridSpec(
        num_scalar_prefetch=1, grid=(J, Rh // tr, W // tw),
        in_specs=[pl.BlockSpec((None, None, tr, tw), lambda j, i, w, cr: (j, cr[0], i, w)),
                  pl.BlockSpec((None, tr, tw), lambda j, i, w, cr: (j, i, w))],
        out_specs=pl.BlockSpec((None, tr, tw), lambda j, i, w, cr: (j, i, w)))
    return _pcall(kern, name=name, grid_spec=grid_spec,
                  out_shape=jax.ShapeDtypeStruct((J, Rh, W), ACT_DTYPE),
                  compiler_params=_params(("parallel", "parallel", "parallel"), 12 * tr * tw * 4))(c, g4, p)


def _chip_sum_piece(cs_ref, C, col_sharded):
    return lambda j: cs_ref.at[0, :, pl.ds(j * C, C)] if col_sharded else cs_ref.at[j]


def _pair_add_own(g4, p, col_sharded, name):
    J, _, Rh, W = g4.shape
    C = W // 4 if col_sharded else W
    tr = _tile(Rh, 256, 16)
    tw = _tile(C, 8192, V7X_LANES)
    nw = C // tw
    x, y, c = _pos()
    chip = (2 * x + y).astype(jnp.int32).reshape(1)
    core = c.astype(jnp.int32).reshape(1)

    def kern(j_ref, c_ref, g_ref, p_ref, o_ref):
        o_ref[...] = (g_ref[...].astype(F32) + p_ref[...].astype(F32)).astype(ACT_DTYPE)

    if col_sharded:
        g_map = lambda i, w, j, cc: (0, cc[0], i, j[0] * nw + w)
        p_map = lambda i, w, j, cc: (0, i, j[0] * nw + w)
    else:
        g_map = lambda i, w, j, cc: (j[0], cc[0], i, w)
        p_map = lambda i, w, j, cc: (j[0], i, w)
    grid_spec = pltpu.PrefetchScalarGridSpec(
        num_scalar_prefetch=2, grid=(Rh // tr, nw),
        in_specs=[pl.BlockSpec((None, None, tr, tw), g_map), pl.BlockSpec((None, tr, tw), p_map)],
        out_specs=pl.BlockSpec((None, None, tr, tw), lambda i, w, j, cc: (j[0], cc[0], i, w)))
    return _pcall(kern, name=name, grid_spec=grid_spec,
                  out_shape=jax.ShapeDtypeStruct((4, 2, Rh, C), ACT_DTYPE),
                  compiler_params=_params(("parallel", "parallel"), 12 * tr * tw * 4))(chip, core, g4, p)


def _scatter_start(cs, slots, col_sharded, token, name):
    C = slots.shape[3]

    def body(cs_ref, land_ref, tok_ref, send_sems, recv_sems, cs_thru, land_thru, tok_out):
        x, y, c = _pos()
        piece = _chip_sum_piece(cs_ref, C, col_sharded)
        for k, (cx, cy) in enumerate(_other_chips(x, y)):
            _remote(piece(2 * cx + cy), land_ref.at[2 * x + y, c], send_sems, recv_sems, k, (cx, cy, c)).start()
        tok_out[...] = tok_ref[...]

    return _pcall(
        body, name=name,
        out_shape=(pltpu.SemaphoreType.DMA((3,)), pltpu.SemaphoreType.DMA((3,)),
                   pltpu.HBM(cs.shape, cs.dtype), pltpu.HBM(slots.shape, cs.dtype), TOKEN),
        in_specs=(HBM_BLK, HBM_BLK, VMEM_BLK), out_specs=(SEM_BLK, SEM_BLK, HBM_BLK, HBM_BLK, VMEM_BLK),
        input_output_aliases={0: 2, 1: 3},
        compiler_params=pltpu.CompilerParams(has_side_effects=DATAFLOW),
    )(_in_hbm(cs), _in_hbm(slots), token)


def _scatter_wait(started, after, col_sharded, name):
    send_sems, recv_sems, cs_thru, land_thru, _ = started
    C = land_thru.shape[3]

    def body(cs_ref, land_ref, send_sems, recv_sems, after_ref, cs_dead, got_ref):
        x, y, c = _pos()
        piece = _chip_sum_piece(cs_ref, C, col_sharded)
        for k, (cx, cy) in enumerate(_other_chips(x, y)):
            cp = _remote(piece(2 * cx + cy), land_ref.at[2 * cx + cy, c], send_sems, recv_sems, k, (cx, cy, c))
            cp.wait_send()
            cp.wait_recv()

    return _pcall(
        body, name=name,
        out_shape=(pltpu.HBM(cs_thru.shape, cs_thru.dtype), pltpu.HBM(land_thru.shape, land_thru.dtype)),
        in_specs=(HBM_BLK, HBM_BLK, SEM_BLK, SEM_BLK, HBM_SPEC), out_specs=(HBM_BLK, HBM_BLK),
        input_output_aliases={0: 0, 1: 1},
        compiler_params=pltpu.CompilerParams(has_side_effects=DATAFLOW),
    )(cs_thru, land_thru, send_sems, recv_sems, after)


def _scatter_pass_on(landed, name):
    def body(in_ref, o_ref, send_sems, recv_sems):
        x, y, c = _pos()
        sends = [_remote(in_ref.at[i, c], o_ref.at[i, c], send_sems, recv_sems, i, (x, y, 1 - c)) for i in range(4)]
        for cp in sends:
            cp.start()
        for i in range(4):
            w = o_ref.at[i, 1 - c]
            _remote(w, w, send_sems, recv_sems, i, (x, y, c)).wait_recv()
        for cp in sends:
            cp.wait_send()

    return _pcall(body, name=name, in_specs=[HBM_SPEC], out_specs=HBM_SPEC,
                  out_shape=jax.ShapeDtypeStruct(landed.shape, landed.dtype), input_output_aliases={0: 0},
                  scratch_shapes=[pltpu.SemaphoreType.DMA((4,)), pltpu.SemaphoreType.DMA((4,))])(landed)


def _gather_small(buf, name):
    rows = buf.shape[0]

    def body(b_ref, o_ref, send_sems, recv_sems):
        x, y, c = _pos()
        jme = 2 * x + y
        chips = _other_chips(x, y)
        o_ref[jme] = b_ref[...]
        sends = [_remote(b_ref, o_ref.at[jme], send_sems, recv_sems, k, (cx, cy, c))
                 for k, (cx, cy) in enumerate(chips)]
        for cp in sends:
            cp.start()
        for k, (cx, cy) in enumerate(chips):
            w = o_ref.at[2 * cx + cy]
            _remote(w, w, send_sems, recv_sems, k, (x, y, c)).wait_recv()
        for cp in sends:
            cp.wait_send()

    vm = pl.BlockSpec(memory_space=pltpu.VMEM)
    return _pcall(body, name=name, in_specs=[vm], out_specs=vm,
                  out_shape=jax.ShapeDtypeStruct((4, rows, V7X_LANES), buf.dtype),
                  scratch_shapes=[pltpu.SemaphoreType.DMA((3,)), pltpu.SemaphoreType.DMA((3,))],
                  compiler_params=_params(None, 16 * rows * V7X_LANES * 4))(buf)


def _allreduce_small(buf, name):
    rows = buf.shape[0]

    def body(b_ref, o_ref, slots, send_sems, recv_sems):
        x, y, c = _pos()
        me = 4 * x + 2 * y + c
        slots[me] = b_ref[...]
        sends = []
        for k in range(1, 8):
            kx, ky, kc = (k >> 2) & 1, (k >> 1) & 1, k & 1
            to = (x ^ kx, y ^ ky, c ^ kc)
            sends.append(_remote(b_ref, slots.at[me], send_sems, recv_sems, k - 1, to))
        for cp in sends:
            cp.start()
        for k in range(1, 8):
            kx, ky, kc = (k >> 2) & 1, (k >> 1) & 1, k & 1
            w = slots.at[4 * (x ^ kx) + 2 * (y ^ ky) + (c ^ kc)]
            _remote(w, w, send_sems, recv_sems, k - 1, (x, y, c)).wait_recv()
        for cp in sends:
            cp.wait_send()
        acc = slots[0]
        for d in range(1, 8):
            acc = acc + slots[d]
        o_ref[...] = acc

    vm = pl.BlockSpec(memory_space=pltpu.VMEM)
    return _pcall(body, name=name, in_specs=[vm], out_specs=vm,
                  out_shape=jax.ShapeDtypeStruct(buf.shape, buf.dtype),
                  scratch_shapes=[pltpu.VMEM((8, rows, V7X_LANES), buf.dtype),
                                  pltpu.SemaphoreType.DMA((7,)), pltpu.SemaphoreType.DMA((7,))],
                  compiler_params=_params(None, 14 * rows * V7X_LANES * 4))(buf)


def _adamw_math(w, g, m, v):
    m = ADAM_B1 * m + (1.0 - ADAM_B1) * g
    v = ADAM_B2 * v + (1.0 - ADAM_B2) * (g * g)
    m_hat = m / (1.0 - ADAM_B1 ** ADAM_STEP)
    v_hat = v / (1.0 - ADAM_B2 ** ADAM_STEP)
    delta = -ADAM_LR * (m_hat / (jnp.sqrt(v_hat) + ADAM_EPS) + ADAM_WD * w)
    return delta, m, v


def _adamw_big(w, m, v, slots, name):
    R, C = w.shape
    tr = _tile(R, 32, 16)

    def kern(w_ref, m_ref, v_ref, s_ref, g_ref, d_ref, mo_ref, vo_ref):
        g = s_ref[0].astype(F32)
        for i in range(1, 4):
            g = g + s_ref[i].astype(F32)
        d, mn, vn = _adamw_math(w_ref[...], g, m_ref[...], v_ref[...])
        g_ref[...], d_ref[...], mo_ref[...], vo_ref[...] = g, d, mn, vn

    row = pl.BlockSpec((tr, C), lambda i: (i, 0))
    shp = jax.ShapeDtypeStruct((R, C), F32)
    return _pcall(kern, name=name, grid=(R // tr,),
                  in_specs=[row, row, row, pl.BlockSpec((4, tr, C), lambda i: (0, i, 0))],
                  out_specs=[row] * 4, out_shape=[shp] * 4,
                  compiler_params=_params(("parallel",), 36 * tr * C * 4))(w, m, v, slots)


def _adamw_small(w, g, m, v, name):
    def kern(w_ref, g_ref, m_ref, v_ref, d_ref, mo_ref, vo_ref):
        d_ref[...], mo_ref[...], vo_ref[...] = _adamw_math(w_ref[...], g_ref[...], m_ref[...], v_ref[...])

    vm = pl.BlockSpec(memory_space=pltpu.VMEM)
    shp = jax.ShapeDtypeStruct(w.shape, F32)
    return _pcall(kern, name=name, in_specs=[vm] * 4, out_specs=[vm] * 3, out_shape=[shp] * 3,
                  compiler_params=_params(None, 10 * w.size * 4))(w, g, m, v)


def _pack(arrs):
    flat = jnp.concatenate([a.reshape(-1).astype(F32) for a in arrs])
    n = flat.shape[0]
    rows = -(-n // (8 * V7X_LANES)) * 8
    return jnp.pad(flat, (0, rows * V7X_LANES - n)).reshape(rows, V7X_LANES)


def _unpack(buf, shapes):
    flat = buf.reshape(-1)
    out, off = [], 0
    for s in shapes:
        n = int(np.prod(s))
        out.append(flat[off:off + n].reshape(s))
        off += n
    return out


def _reduce_start(gfull, col_sharded, shard_shape, token, tag):
    R, C = shard_shape
    if col_sharded:
        g4 = gfull.reshape(1, 2, R // 2, 4 * C)
    else:
        g4 = gfull.reshape(4, 2, R // 2, C)
    p = _pair_exchange(g4, "pair_exchange_" + tag)
    cs = _pair_add(g4, p, "pair_add_" + tag)
    slots = _pair_add_own(g4, p, col_sharded, "pair_add_own_" + tag)
    return _scatter_start(cs, slots, col_sharded, token, "scatter_start_" + tag)


def _reduce_finish(started, after, col_sharded, w, m, v, tag):
    R, C = w.shape
    _, landed = _scatter_wait(started, after, col_sharded, "scatter_wait_" + tag)
    slots = _scatter_pass_on(landed, "scatter_pass_on_" + tag)
    return _adamw_big(w, m, v, slots.reshape(4, R, C), "adamw_" + tag)


def kernel(x, ln1_w, w_in, lb_gamma, hg_norm_w, lru_conv_w, lru_conv_b, lru_wa, lru_ba, lru_wx, lru_bx, lru_lambda, lru_norm_w, w_out, ln2_w, ffn_w_up, ffn_conv_w, ffn_conv_b, ffn_w_down, final_norm_w, loss_target, m_ln1_w, m_w_in, m_lb_gamma, m_hg_norm_w, m_lru_conv_w, m_lru_conv_b, m_lru_wa, m_lru_ba, m_lru_wx, m_lru_bx, m_lru_lambda, m_lru_norm_w, m_w_out, m_ln2_w, m_ffn_w_up, m_ffn_conv_w, m_ffn_conv_b, m_ffn_w_down, m_final_norm_w, v_ln1_w, v_w_in, v_lb_gamma, v_hg_norm_w, v_lru_conv_w, v_lru_conv_b, v_lru_wa, v_lru_ba, v_lru_wx, v_lru_bx, v_lru_lambda, v_lru_norm_w, v_w_out, v_ln2_w, v_ffn_w_up, v_ffn_conv_w, v_ffn_conv_b, v_ffn_w_down, v_final_norm_w):
    B, S, D = x.shape
    T = B * S
    HW = lb_gamma.shape[1]
    LW = lru_conv_b.shape[1]
    assert S % CHUNK == 0 and HW % HEAD_DIM == 0 and lru_wa.shape[2] == HEAD_DIM
    x2 = x.reshape(T, D)
    tgt = loss_target.reshape(T, D)
    jchip = 2 * lax.axis_index("x") + lax.axis_index("y")

    conv_shapes = [lru_conv_w[0].shape, ffn_conv_w[0].shape]
    convs = _gather_small(_pack([lru_conv_w[0], ffn_conv_w[0]]), "gather_conv_w")
    per_chip = [_unpack(convs[j], conv_shapes) for j in range(4)]
    lcw = jnp.concatenate([pc[0] for pc in per_chip], axis=1)
    fcw = jnp.concatenate([pc[1] for pc in per_chip], axis=1)
    masters = dict(w_in=w_in[0], w_out=w_out[0], w_up=ffn_w_up[0], w_down=ffn_w_down[0])
    col_of = dict(w_in=True, w_out=False, w_up=True, w_down=False)
    started, token, after = {}, jnp.zeros(TOKEN.shape, F32), convs
    for n in ("w_in", "w_out", "w_up", "w_down"):
        land = _cast_into_window(masters[n], col_of[n], after, "cast_" + n)
        started[n] = _gather_start(land, masters[n].shape, col_of[n], token, "gather_start_" + n)
        token = after = started[n][3]

    def gathered(n, after):
        landed = _gather_wait(started[n], masters[n].shape, after, col_of[n], "gather_wait_" + n)
        return _gather_pass_on(landed, masters[n].shape, col_of[n], "gather_pass_on_" + n)

    W_in = gathered("w_in", token)

    hn1 = _rms_fwd(x2, ln1_w, "rms1")
    proj = _matmul(hn1, W_in, "NN", F32, 1024, 512, 4096, name="mm_proj")
    o_raw, o_hg, states = _hgrn_fwd(proj, lb_gamma, hg_norm_w, B, HW, "hgrn_fwd")
    h_lru, z = _lru_fwd(proj, lcw, lru_conv_b, lru_wa[0], lru_ba, lru_wx[0], lru_bx, lru_lambda, B, HW, LW, "lru_fwd")
    o_lru = _rms_fwd(z, lru_norm_w, "rms_lru")
    mix = jnp.concatenate([o_hg, o_lru], axis=1)
    W_out = gathered("w_out", mix)
    h1 = _matmul(mix, W_out, "NN", F32, 1024, 512, 4096, add=x2, name="mm_out")
    hn2 = _rms_fwd(h1, ln2_w, "rms2")
    W_up = gathered("w_up", hn2)
    up = _matmul(hn2, W_up, "NN", F32, 1024, 512, 4096, name="mm_up")
    act, act_dg, act_dv = _ffn_act(up, fcw, ffn_conv_b, B, "ffn_act")
    W_down = gathered("w_down", act)
    h2 = _matmul(act, W_down, "NN", F32, 2048, 1024, 256, add=h1, name="mm_down")

    dh2, dh2a, d_final_w, loss_part = _loss_bwd(h2, tgt, final_norm_w.reshape(1, D), "loss_bwd")
    g_down = _matmul(act, dh2a, "TN", ACT_DTYPE, 256, 2048, 4096, n_outer=True, name="mm_g_down")
    red_down = _reduce_start(g_down, False, ffn_w_down[0].shape, token, "w_down")
    d_act = _matmul(dh2a, W_down, "NT", ACT_DTYPE, 512, 5504, 512, after=red_down[4], name="mm_d_act")
    d_up, d_fcw, d_fcb = _ffn_act_bwd(up, fcw, act_dg, act_dv, d_act, B, "ffn_act_bwd")
    g_up = _matmul(hn2, d_up, "TN", ACT_DTYPE, 1024, 512, 4096, name="mm_g_up")
    red_up = _reduce_start(g_up, True, ffn_w_up[0].shape, red_down[4], "w_up")
    d_hn2 = _matmul(d_up, W_up, "NT", F32, 2048, 1024, 512, after=red_up[4], name="mm_d_hn2")
    dh1, dh1a, d_ln2 = _rms_bwd(h1, ln2_w, d_hn2, 0, dh2, True, "rms2_bwd")
    g_out = _matmul(mix, dh1a, "TN", ACT_DTYPE, 1024, 512, 4096, name="mm_g_out")
    red_out = _reduce_start(g_out, False, w_out[0].shape, red_up[4], "w_out")
    d_mix = _matmul(dh1a, W_out, "NT", F32, 1024, 512, 4096, after=red_out[4], name="mm_d_mix")
    dz, d_lru_norm = _rms_bwd(z, lru_norm_w, d_mix, HW // LW, None, False, "rms_lru_bwd")
    (d_xr, d_yr, d_wa, d_wx, d_ba, d_bx, d_lam, d_lcw, d_lcb) = _lru_bwd(
        proj, lcw, lru_conv_b, lru_wa[0], lru_ba, lru_wx[0], lru_bx, lru_lambda, h_lru, dz, B, HW, LW, "lru_bwd")
    d_q, d_f, d_i, d_g, d_lbg, d_hgw = _hgrn_bwd(proj, lb_gamma, hg_norm_w, o_raw, states, d_mix, B, HW, "hgrn_bwd")
    d_proj = jnp.concatenate([d_q, d_f, d_i, d_g, d_xr, d_yr], axis=1)
    g_in = _matmul(hn1, d_proj, "TN", ACT_DTYPE, 1024, 512, 4096, name="mm_g_in")
    red_in = _reduce_start(g_in, True, w_in[0].shape, red_out[4], "w_in")
    d_hn1 = _matmul(d_proj, W_in, "NT", F32, 1024, 1024, 1024, after=red_in[4], name="mm_d_hn1")
    dx, d_ln1 = _rms_bwd(x2, ln1_w, d_hn1, 0, dh1, False, "rms1_bwd")

    big = {}
    big["ffn_w_down"] = _reduce_finish(red_down, dx, False, ffn_w_down[0], m_ffn_w_down[0], v_ffn_w_down[0], "w_down")
    big["ffn_w_up"] = _reduce_finish(red_up, big["ffn_w_down"][1], True, ffn_w_up[0], m_ffn_w_up[0], v_ffn_w_up[0], "w_up")
    big["w_out"] = _reduce_finish(red_out, big["ffn_w_up"][1], False, w_out[0], m_w_out[0], v_w_out[0], "w_out")
    big["w_in"] = _reduce_finish(red_in, big["w_out"][1], True, w_in[0], m_w_in[0], v_w_in[0], "w_in")

    small_names = ["ln1_w", "lb_gamma", "hg_norm_w", "lru_conv_w", "lru_conv_b", "lru_wa", "lru_ba", "lru_wx",
                   "lru_bx", "lru_lambda", "lru_norm_w", "ln2_w", "ffn_conv_w", "ffn_conv_b", "final_norm_w"]
    small_grads = [d_ln1, d_lbg, d_hgw, d_lcw, d_lcb, d_wa, d_ba, d_wx, d_bx, d_lam, d_lru_norm, d_ln2,
                   d_fcw, d_fcb, d_final_w]
    red = _allreduce_small(_pack([loss_part[0:1, 0:1]] + small_grads), "allreduce_small")
    red = _unpack(red, [(1, 1)] + [g.shape for g in small_grads])
    loss = red[0].reshape(())
    gs = dict(zip(small_names, red[1:]))
    nlc, nfc = lru_conv_w.shape[2], ffn_conv_w.shape[2]
    gs["lru_conv_w"] = lax.dynamic_slice_in_dim(gs["lru_conv_w"], jchip * nlc, nlc, axis=1)
    gs["ffn_conv_w"] = lax.dynamic_slice_in_dim(gs["ffn_conv_w"], jchip * nfc, nfc, axis=1)
    args = dict(ln1_w=(ln1_w, m_ln1_w, v_ln1_w), lb_gamma=(lb_gamma, m_lb_gamma, v_lb_gamma),
                hg_norm_w=(hg_norm_w, m_hg_norm_w, v_hg_norm_w), lru_conv_w=(lru_conv_w, m_lru_conv_w, v_lru_conv_w),
                lru_conv_b=(lru_conv_b, m_lru_conv_b, v_lru_conv_b), lru_wa=(lru_wa, m_lru_wa, v_lru_wa),
                lru_ba=(lru_ba, m_lru_ba, v_lru_ba), lru_wx=(lru_wx, m_lru_wx, v_lru_wx),
                lru_bx=(lru_bx, m_lru_bx, v_lru_bx), lru_lambda=(lru_lambda, m_lru_lambda, v_lru_lambda),
                lru_norm_w=(lru_norm_w, m_lru_norm_w, v_lru_norm_w), ln2_w=(ln2_w, m_ln2_w, v_ln2_w),
                ffn_conv_w=(ffn_conv_w, m_ffn_conv_w, v_ffn_conv_w), ffn_conv_b=(ffn_conv_b, m_ffn_conv_b, v_ffn_conv_b),
                final_norm_w=(final_norm_w, m_final_norm_w, v_final_norm_w))
    shapes = [args[n][0].shape for n in small_names]
    upd = _adamw_small(_pack([args[n][0] for n in small_names]), _pack([gs[n] for n in small_names]),
                       _pack([args[n][1] for n in small_names]), _pack([args[n][2] for n in small_names]), "adamw_small")
    s_delta, s_m, s_v = (dict(zip(small_names, _unpack(u, shapes))) for u in upd)

    order = ["ln1_w", "w_in", "lb_gamma", "hg_norm_w", "lru_conv_w", "lru_conv_b", "lru_wa", "lru_ba", "lru_wx",
             "lru_bx", "lru_lambda", "lru_norm_w", "w_out", "ln2_w", "ffn_w_up", "ffn_conv_w", "ffn_conv_b",
             "ffn_w_down", "final_norm_w"]
    full_shape = dict(w_in=w_in.shape, w_out=w_out.shape, ffn_w_up=ffn_w_up.shape, ffn_w_down=ffn_w_down.shape)
    grads, deltas, new_m, new_v = [], [], [], []
    for n in order:
        if n in big:
            g, d, mn, vn = (t.reshape(full_shape[n]) for t in big[n])
        else:
            g, d, mn, vn = gs[n].reshape(args[n][0].shape), s_delta[n], s_m[n], s_v[n]
        grads.append(g), deltas.append(d), new_m.append(mn), new_v.append(vn)
    return (loss, dx.reshape(B, S, D), *grads, *deltas, *new_m, *new_v)
```

```python
import functools
import math

import numpy as np
import jax
import jax.numpy as jnp
from jax import lax
from jax.experimental import pallas as pl
from jax.experimental.pallas import tpu as pltpu

F32 = jnp.float32
MXU_DTYPE = jnp.bfloat16
ACT_DTYPE = jnp.bfloat16

EPS = 1e-6
HEAD_DIM = 128
CHUNK = 64
LEVEL_HALVES = (32, 16, 8, 4, 2, 1)
LRU_CONV = 4
FFN_CONV = 3
LRU_C = 8.0
ADAM_LR, ADAM_B1, ADAM_B2, ADAM_EPS, ADAM_WD, ADAM_STEP = 0.001, 0.9, 0.999, 1e-08, 0.01, 10

V7X_LANES = 128
V7X_VMEM_BUDGET = 56 << 20

NN = (((1,), (0,)), ((), ()))
NT = (((1,), (1,)), ((), ()))
TN = (((0,), (0,)), ((), ()))
MESH = pl.DeviceIdType.MESH
HBM_SPEC = pl.BlockSpec(memory_space=pl.ANY)


def _pcall(kern, **kw):
    return pl.pallas_call(kern, **kw)


def _params(sem=None, vmem=None):
    kw = {}
    if sem is not None:
        kw["dimension_semantics"] = sem
    if vmem is not None:
        kw["vmem_limit_bytes"] = int(min(max(vmem, 16 << 20), V7X_VMEM_BUDGET))
    return pltpu.CompilerParams(**kw)


def _dot(a, b, dims=NN):
    return lax.dot_general(a.astype(MXU_DTYPE), b.astype(MXU_DTYPE), dims, preferred_element_type=F32)


def _tile(dim, pref, align):
    t = min(pref, dim) // align * align
    while t >= align:
        if dim % t == 0:
            return t
        t -= align
    return dim


def _sigmoid(x):
    return 1.0 / (1.0 + jnp.exp(-x))


def _silu_and_grad(x):
    s = _sigmoid(x)
    return x * s, s * (1.0 + x * (1.0 - s))


def _gelu_and_grad(x):
    k0, k1 = math.sqrt(2.0 / math.pi), 0.044715
    t = jnp.tanh(k0 * (x + k1 * x * x * x))
    g = 0.5 * x * (1.0 + t)
    dg = 0.5 * (1.0 + t) + 0.5 * x * (1.0 - t * t) * k0 * (1.0 + 3.0 * k1 * x * x)
    return g, dg


def _one_minus_exp(x):
    p = x * (1.0 + x * (0.5 + x * (1.0 / 6.0 + x * (1.0 / 24.0 + x * (1.0 / 120.0)))))
    return jnp.where(x > -0.05, -p, 1.0 - jnp.exp(x))


def _rows(n):
    return lax.broadcasted_iota(jnp.int32, (n, 1), 0)


def _matmul(a, b, mode, out_dtype, tm, tn, tk, add=None, after=None, n_outer=False, name="mm"):
    if mode == "TN":
        K, M = a.shape
    else:
        M, K = a.shape
    N = b.shape[0] if mode == "NT" else b.shape[1]
    tm, tn = _tile(M, tm, V7X_LANES), _tile(N, tn, V7X_LANES)
    tk = _tile(K, tk, V7X_LANES)
    nk = K // tk
    dims = {"NN": NN, "NT": NT, "TN": TN}[mode]
    order = (lambda f: (lambda j, i, k: f(i, j, k))) if n_outer else (lambda f: f)
    a_spec = (pl.BlockSpec((tk, tm), order(lambda i, j, k: (k, i))) if mode == "TN"
              else pl.BlockSpec((tm, tk), order(lambda i, j, k: (i, k))))
    b_spec = (pl.BlockSpec((tn, tk), order(lambda i, j, k: (j, k))) if mode == "NT"
              else pl.BlockSpec((tk, tn), order(lambda i, j, k: (k, j))))
    o_spec = pl.BlockSpec((tm, tn), order(lambda i, j, k: (i, j)))
    has_add = add is not None

    def kern(*refs):
        a_ref, b_ref = refs[:2]
        add_ref = refs[2] if has_add else None

        def finish(r, o_ref):
            if has_add:
                r = r + add_ref[...]
            o_ref[...] = r.astype(out_dtype)

        if nk == 1:
            finish(_dot(a_ref[...], b_ref[...], dims), refs[-1])
            return
        o_ref, acc_ref = refs[-2:]
        k = pl.program_id(2)

        @pl.when(k == 0)
        def _():
            acc_ref[...] = jnp.zeros_like(acc_ref)

        acc_ref[...] += _dot(a_ref[...], b_ref[...], dims)

        @pl.when(k == nk - 1)
        def _():
            finish(acc_ref[...], o_ref)

    ab = jnp.dtype(a.dtype).itemsize
    ob = jnp.dtype(out_dtype).itemsize
    vmem = 2 * (tm * tk + tk * tn) * ab + tm * tn * (8 + 2 * ob + (8 if has_add else 0)) + (4 << 20)
    ins = [a, b] + ([add] if has_add else []) + ([after] if after is not None else [])
    in_specs = [a_spec, b_spec] + ([o_spec] if has_add else []) + ([HBM_SPEC] if after is not None else [])
    grid = (N // tn, M // tm, nk) if n_outer else (M // tm, N // tn, nk)
    return _pcall(
        kern, name=name, grid=grid,
        in_specs=in_specs, out_specs=o_spec,
        out_shape=jax.ShapeDtypeStruct((M, N), out_dtype),
        scratch_shapes=[pltpu.VMEM((tm, tn), F32)] if nk > 1 else [],
        compiler_params=_params(("parallel", "parallel", "arbitrary"), vmem),
    )(*ins)


def _matmul_tn_half(a, b, half, add, tm, tn, name):
    K, M = a.shape
    N, Mh = b.shape[1], M // 2
    tm, tn = _tile(Mh, tm, V7X_LANES), _tile(N, tn, V7X_LANES)
    nm = Mh // tm
    has_add = add is not None

    def kern(h_ref, a_ref, b_ref, *rest):
        r = _dot(a_ref[...], b_ref[...], TN)
        if has_add:
            r = r + rest[0][...].astype(F32)
        rest[-1][...] = r.astype(ACT_DTYPE)

    blk = pl.BlockSpec((tm, tn), lambda i, j, h: (i, j))
    grid_spec = pltpu.PrefetchScalarGridSpec(
        num_scalar_prefetch=1, grid=(nm, N // tn),
        in_specs=[pl.BlockSpec((K, tm), lambda i, j, h: (0, h[0] * nm + i)),
                  pl.BlockSpec((K, tn), lambda i, j, h: (0, j))] + ([blk] if has_add else []),
        out_specs=blk)
    ab = jnp.dtype(a.dtype).itemsize
    vmem = 2 * K * (tm + tn) * ab + tm * tn * 16 + (4 << 20)
    return _pcall(kern, name=name, grid_spec=grid_spec, out_shape=jax.ShapeDtypeStruct((Mh, N), ACT_DTYPE),
                  compiler_params=_params(("parallel", "parallel"), vmem))(half, a, b, *([add] if has_add else []))


def _rms_fwd(x, w, name):
    T, D = x.shape
    tm = _tile(T, 256, 16)

    def kern(x_ref, w_ref, o_ref):
        xv = x_ref[...]
        r = lax.rsqrt(jnp.mean(xv * xv, axis=-1, keepdims=True) + EPS)
        o_ref[...] = (xv * r * w_ref[...]).astype(ACT_DTYPE)

    return _pcall(kern, name=name, grid=(T // tm,),
                  in_specs=[pl.BlockSpec((tm, D), lambda i: (i, 0)), pl.BlockSpec((1, D), lambda i: (0, 0))],
                  out_specs=pl.BlockSpec((tm, D), lambda i: (i, 0)),
                  out_shape=jax.ShapeDtypeStruct((T, D), ACT_DTYPE),
                  compiler_params=_params(("parallel",), 8 * tm * D * 4))(x, w)


def _rms_bwd(x, w, g, g_col, res, want_act, name, after=None):
    T, D = x.shape
    tm = _tile(T, 256, 16)
    has_res = res is not None

    def kern(*refs):
        refs = list(refs)
        x_ref, w_ref, g_ref = refs[:3]
        res_ref = refs[3] if has_res else None
        outs = refs[3 + has_res + (after is not None):]
        dx_ref = outs[0]
        dxa_ref = outs[1] if want_act else None
        dw_ref = outs[-1]
        i = pl.program_id(0)
        xv = x_ref[...]
        gv = g_ref[...].astype(F32)
        r = lax.rsqrt(jnp.mean(xv * xv, axis=-1, keepdims=True) + EPS)
        gw = gv * w_ref[...]
        dx = r * gw - xv * (r * r * r) * jnp.mean(gw * xv, axis=-1, keepdims=True)
        if has_res:
            dx = dx + res_ref[...]
        dx_ref[...] = dx
        if want_act:
            dxa_ref[...] = dx.astype(ACT_DTYPE)

        @pl.when(i == 0)
        def _():
            dw_ref[...] = jnp.zeros_like(dw_ref)

        dw_ref[...] += jnp.sum(gv * xv * r, axis=0, keepdims=True)

    row = pl.BlockSpec((tm, D), lambda i: (i, 0))
    vec = pl.BlockSpec((1, D), lambda i: (0, 0))
    in_specs = ([row, vec, pl.BlockSpec((tm, D), lambda i: (i, g_col))] + ([row] if has_res else [])
                + ([HBM_SPEC] if after is not None else []))
    out_specs = [row] + ([row] if want_act else []) + [vec]
    out_shape = ([jax.ShapeDtypeStruct((T, D), F32)]
                 + ([jax.ShapeDtypeStruct((T, D), ACT_DTYPE)] if want_act else [])
                 + [jax.ShapeDtypeStruct((1, D), F32)])
    ins = [x, w, g] + ([res] if has_res else []) + ([after] if after is not None else [])
    return _pcall(kern, name=name, grid=(T // tm,), in_specs=in_specs, out_specs=out_specs,
                  out_shape=out_shape, compiler_params=_params(("arbitrary",), 14 * tm * D * 4))(*ins)


def _loss_bwd(h, target, w, name):
    T, D = h.shape
    tm = _tile(T, 256, 16)

    def kern(h_ref, t_ref, w_ref, dh_ref, dha_ref, dw_ref, loss_ref):
        i = pl.program_id(0)
        hv = h_ref[...]
        r = lax.rsqrt(jnp.mean(hv * hv, axis=-1, keepdims=True) + EPS)
        e = hv * r * w_ref[...] - t_ref[...]
        dy = e * (1.0 / D)
        gw = dy * w_ref[...]
        dh = r * gw - hv * (r * r * r) * jnp.mean(gw * hv, axis=-1, keepdims=True)
        dh_ref[...] = dh
        dha_ref[...] = dh.astype(ACT_DTYPE)

        @pl.when(i == 0)
        def _():
            dw_ref[...] = jnp.zeros_like(dw_ref)
            loss_ref[...] = jnp.zeros_like(loss_ref)

        dw_ref[...] += jnp.sum(dy * hv * r, axis=0, keepdims=True)
        part = 0.5 * jnp.sum(jnp.mean(e * e, axis=-1, keepdims=True), axis=0, keepdims=True)
        loss_ref[...] += jnp.broadcast_to(part, loss_ref.shape)

    row = pl.BlockSpec((tm, D), lambda i: (i, 0))
    vec = pl.BlockSpec((1, D), lambda i: (0, 0))
    return _pcall(kern, name=name, grid=(T // tm,), in_specs=[row, row, vec],
                  out_specs=[row, row, vec, pl.BlockSpec((8, V7X_LANES), lambda i: (0, 0))],
                  out_shape=[jax.ShapeDtypeStruct((T, D), F32), jax.ShapeDtypeStruct((T, D), ACT_DTYPE),
                             jax.ShapeDtypeStruct((1, D), F32), jax.ShapeDtypeStruct((8, V7X_LANES), F32)],
                  compiler_params=_params(("arbitrary",), 14 * tm * D * 4))(h, target, w)


def _conv(x, w_ref, b, width):
    S = x.shape[0]
    row = _rows(S)
    y = b + x * w_ref[pl.ds(width - 1, 1), :]
    for j in range(width - 1):
        sh = width - 1 - j
        y = y + jnp.where(row >= sh, pltpu.roll(x, sh, 0), 0.0) * w_ref[pl.ds(j, 1), :]
    return y


def _conv_bwd(x, dy, w_ref, width):
    S = dy.shape[0]
    row = _rows(S)
    dx = dy * w_ref[pl.ds(width - 1, 1), :]
    dw = [None] * (width - 1) + [jnp.sum(x * dy, axis=0, keepdims=True)]
    for j in range(width - 1):
        sh = width - 1 - j
        dys = jnp.where(row < S - sh, pltpu.roll(dy, S - sh, 0), 0.0)
        dx = dx + dys * w_ref[pl.ds(j, 1), :]
        dw[j] = jnp.sum(x * dys, axis=0, keepdims=True)
    return dx, dw


def _ffn_act(up, cw, cb, B, name):
    T, F2 = up.shape
    S, F = T // B, F2 // 2
    tw = _tile(F, 256, V7X_LANES)
    nt = F // tw

    def kern(g_ref, v_ref, wg_ref, wv_ref, bg_ref, bv_ref, o_ref, mg_ref, mv_ref):
        gc = _conv(g_ref[...], wg_ref, bg_ref[...], FFN_CONV)
        vc = _conv(v_ref[...], wv_ref, bv_ref[...], FFN_CONV)
        silu, dsilu = _silu_and_grad(gc)
        o_ref[...] = (silu * vc).astype(ACT_DTYPE)
        mg_ref[...] = (vc * dsilu).astype(ACT_DTYPE)
        mv_ref[...] = silu.astype(ACT_DTYPE)

    blk = lambda off: pl.BlockSpec((S, tw), lambda b, i: (b, off + i))
    wblk = lambda off: pl.BlockSpec((FFN_CONV, tw), lambda b, i: (0, off + i))
    bblk = lambda off: pl.BlockSpec((1, tw), lambda b, i: (0, off + i))
    half = jax.ShapeDtypeStruct((T, F), ACT_DTYPE)
    return _pcall(kern, name=name, grid=(B, nt),
                  in_specs=[blk(0), blk(nt), wblk(0), wblk(nt), bblk(0), bblk(nt)],
                  out_specs=[blk(0), blk(0), blk(0)], out_shape=[half, half, half],
                  compiler_params=_params(("parallel", "parallel"), 20 * S * tw * 4))(up, up, cw, cw, cb, cb)


def _ffn_act_bwd(up, cw, mg, mv, d_act, B, name):
    T, F2 = up.shape
    S, F = T // B, F2 // 2
    tw = _tile(F, 256, V7X_LANES)
    nt = F // tw

    def kern(s_ref, ws_ref, mg_ref, mv_ref, da_ref, du_ref, dcw_ref, dcb_ref):
        t, b = pl.program_id(0), pl.program_id(1)
        mult = jnp.where(t < nt, mg_ref[...], mv_ref[...])
        d = da_ref[...].astype(F32) * mult.astype(F32)
        dx, dw = _conv_bwd(s_ref[...], d, ws_ref, FFN_CONV)
        du_ref[...] = dx.astype(ACT_DTYPE)

        @pl.when(b == 0)
        def _():
            dcw_ref[...] = jnp.zeros_like(dcw_ref)
            dcb_ref[...] = jnp.zeros_like(dcb_ref)

        for j, rj in enumerate(dw):
            dcw_ref[pl.ds(j, 1), :] += rj
        dcb_ref[...] += jnp.sum(d, axis=0, keepdims=True)

    own = lambda t, b: (b, t % nt)
    return _pcall(
        kern, name=name, grid=(2 * nt, B),
        in_specs=[pl.BlockSpec((S, tw), lambda t, b: (b, t)),
                  pl.BlockSpec((FFN_CONV, tw), lambda t, b: (0, t)),
                  pl.BlockSpec((S, tw), own), pl.BlockSpec((S, tw), own), pl.BlockSpec((S, tw), own)],
        out_specs=[pl.BlockSpec((S, tw), lambda t, b: (b, t)),
                   pl.BlockSpec((FFN_CONV, tw), lambda t, b: (0, t)),
                   pl.BlockSpec((1, tw), lambda t, b: (0, t))],
        out_shape=[jax.ShapeDtypeStruct((T, F2), ACT_DTYPE), jax.ShapeDtypeStruct((FFN_CONV, F2), F32),
                   jax.ShapeDtypeStruct((1, F2), F32)],
        compiler_params=_params(("parallel", "arbitrary"), 20 * S * tw * 4),
    )(up, cw, mg, mv, d_act)


def _hgrn_tables():
    C = CHUNK
    t = np.arange(C)
    mats = [(t[:, None] >= t[None, :]).astype(np.float32)]
    masks = []
    gsum = [(t[:, None] <= t[None, :]).astype(np.float32), (t[:, None] > t[None, :]).astype(np.float32)]
    for hs in LEVEL_HALVES:
        m = (t // (2 * hs)) * 2 * hs + hs
        later = t >= m
        d = np.zeros((C, C), np.float32)
        for i in range(C):
            if later[i]:
                d[i, m[i]:i + 1] = 1.0
            else:
                d[i, i + 1:m[i]] = -1.0
        mats.append(d)
        same = (t[:, None] // (2 * hs)) == (t[None, :] // (2 * hs))
        masks.append((same & later[:, None] & (~later)[None, :]).astype(np.float32))
        gsum.append((same & later[:, None] & (t[None, :] >= t[:, None])).astype(np.float32))
        gsum.append((same & (~later)[:, None] & (t[None, :] < t[:, None])).astype(np.float32))
    return np.concatenate(mats, 0), np.stack(masks, 0), np.concatenate(gsum, 1)


def _split_dot(mat, v):
    hi = v.astype(MXU_DTYPE)
    lo = (v - hi.astype(F32)).astype(MXU_DTYPE)
    r = _dot(mat, jnp.concatenate([hi, lo], axis=1))
    n = v.shape[1]
    return r[:, :n] + r[:, n:]


def _hgrn_gates(qr, fr, lb, mc):
    C = CHUNK
    q, dq_dqr = _silu_and_grad(qr)
    sf = _sigmoid(fr)
    f = lb + (1.0 - lb) * sf
    k = 1.0 - f
    dall = _split_dot(mc, jnp.log(f))
    b = dall[0:C]
    dl = [dall[C * (l + 1):C * (l + 2)] for l in range(len(LEVEL_HALVES))]
    eq = [jnp.exp(jnp.minimum(d, 0.0)) for d in dl]
    ek = [jnp.exp(jnp.minimum(-d, 0.0)) for d in dl]
    return q, dq_dqr, sf, f, k, b, eq, ek


def _hgrn_scores(q, k, eq, ek, masks_ref):
    p = jnp.where(_rows(CHUNK) == lax.broadcasted_iota(jnp.int32, (1, CHUNK), 1),
                  jnp.sum(q * k, axis=-1, keepdims=True), 0.0)
    for l in range(len(LEVEL_HALVES)):
        p = p + masks_ref[l] * _dot(q * eq[l], k * ek[l], NT)
    return p


def _hgrn_fwd(proj, lb_gamma, norm_w, B, HW, name):
    T = proj.shape[0]
    S, H, C = T // B, HW // HEAD_DIM, CHUNK
    NC = S // C
    mc_np, masks_np, _ = _hgrn_tables()
    mc, masks = jnp.asarray(mc_np, MXU_DTYPE), jnp.asarray(masks_np, F32)

    def kern(q_ref, f_ref, i_ref, g_ref, lbg_ref, nw_ref, mc_ref, masks_ref, oraw_ref, o_ref, st_ref):
        g0, g1 = lbg_ref[pl.ds(0, 1), :], lbg_ref[pl.ds(1, 1), :]
        mx = jnp.maximum(g0, g1)
        e0, e1 = jnp.exp(g0 - mx), jnp.exp(g1 - mx)
        lb = e0 / (e0 + e1)
        nw = nw_ref[...]
        mcv = mc_ref[...]

        def body(n, sts):
            out = []
            for s, st in enumerate(sts):
                rows = pl.ds(pl.multiple_of(s * S + n * C, C), C)
                st_ref[s, n] = st
                q, _, _, _, k, b, eq, ek = _hgrn_gates(q_ref[rows, :], f_ref[rows, :], lb, mcv)
                v = i_ref[rows, :]
                o = _dot(q * jnp.exp(b), st, NT) + _dot(_hgrn_scores(q, k, eq, ek, masks_ref), v)
                b_last = b[C - 1:C]
                out.append(st * jnp.exp(b_last) + _dot(v, k * jnp.exp(b_last - b), TN))
                oraw_ref[rows, :] = o
                r = lax.rsqrt(jnp.mean(o * o, axis=-1, keepdims=True) + EPS)
                gate, _ = _silu_and_grad(g_ref[rows, :])
                o_ref[rows, :] = (o * r * nw * gate).astype(ACT_DTYPE)
            return tuple(out)

        lax.fori_loop(0, NC, body, tuple(jnp.zeros((HEAD_DIM, HEAD_DIM), F32) for _ in range(B)))

    col = lambda off: pl.BlockSpec((T, HEAD_DIM), lambda h: (0, off + h))
    return _pcall(
        kern, name=name, grid=(H,),
        in_specs=[col(0), col(H), col(2 * H), col(3 * H),
                  pl.BlockSpec((2, HEAD_DIM), lambda h: (0, h)),
                  pl.BlockSpec((1, HEAD_DIM), lambda h: (0, h)),
                  pl.BlockSpec(mc.shape, lambda h: (0, 0)),
                  pl.BlockSpec(masks.shape, lambda h: (0, 0, 0))],
        out_specs=[col(0), col(0),
                   pl.BlockSpec((B, None, NC, HEAD_DIM, HEAD_DIM), lambda h: (0, h, 0, 0, 0))],
        out_shape=[jax.ShapeDtypeStruct((T, HW), F32), jax.ShapeDtypeStruct((T, HW), ACT_DTYPE),
                   jax.ShapeDtypeStruct((B, H, NC, HEAD_DIM, HEAD_DIM), F32)],
        compiler_params=_params(("parallel",), 20 * T * HEAD_DIM * 4 + (8 << 20)),
    )(proj, proj, proj, proj, lb_gamma, norm_w, mc, masks)


def _hgrn_bwd(proj, lb_gamma, norm_w, o_raw, states, d_mix, B, HW, name):
    T = proj.shape[0]
    S, H, C = T // B, HW // HEAD_DIM, CHUNK
    NC = S // C
    mc_np, masks_np, gsum_np = _hgrn_tables()
    mc, masks, gsum = jnp.asarray(mc_np, MXU_DTYPE), jnp.asarray(masks_np, F32), jnp.asarray(gsum_np, MXU_DTYPE)
    nl = len(LEVEL_HALVES)

    def kern(q_ref, f_ref, i_ref, g_ref, lbg_ref, nw_ref, mc_ref, masks_ref, gsum_ref, oraw_ref, st_ref, do_ref,
             dq_ref, df_ref, di_ref, dg_ref, dlbg_ref, dnw_ref):
        g0, g1 = lbg_ref[pl.ds(0, 1), :], lbg_ref[pl.ds(1, 1), :]
        mx = jnp.maximum(g0, g1)
        e0, e1 = jnp.exp(g0 - mx), jnp.exp(g1 - mx)
        lb = e0 / (e0 + e1)
        nw = nw_ref[...]
        mcv, gsumv = mc_ref[...], gsum_ref[...]

        def chunk(s, n, dst, dlb, dnw):
            rows = pl.ds(pl.multiple_of(s * S + n * C, C), C)
            qr, fr, v = q_ref[rows, :], f_ref[rows, :], i_ref[rows, :]
            q, dq_dqr, sf, f, k, b, eq, ek = _hgrn_gates(qr, fr, lb, mcv)
            o = oraw_ref[rows, :]
            dout = do_ref[rows, :].astype(F32)
            gate, dgate = _silu_and_grad(g_ref[rows, :])
            r = lax.rsqrt(jnp.mean(o * o, axis=-1, keepdims=True) + EPS)
            dg_ref[rows, :] = (dout * o * r * nw * dgate).astype(ACT_DTYPE)
            don = dout * gate
            dnw = dnw + jnp.sum(don * o * r, axis=0, keepdims=True)
            gw = don * nw
            do = r * gw - o * (r * r * r) * jnp.mean(gw * o, axis=-1, keepdims=True)
            st_prev = st_ref[s, n]
            eb = jnp.exp(b)
            b_last = b[C - 1:C]
            ebl = jnp.exp(b_last - b)
            p = _hgrn_scores(q, k, eq, ek, masks_ref)
            dp = _dot(do, v, NT)
            dpd = jnp.sum(do * v, axis=-1, keepdims=True)
            dq_state = _dot(do, st_prev) * eb
            dk_state = _dot(v, dst) * ebl
            dq = dq_state + dpd * k
            dk = dk_state + dpd * q
            pairs = [q * dq_state, k * dk_state]
            for l in range(nl):
                mdp = masks_ref[l] * dp
                dql = _dot(mdp, k * ek[l]) * eq[l]
                dkl = _dot(mdp, q * eq[l], TN) * ek[l]
                dq, dk = dq + dql, dk + dkl
                pairs += [q * dql, k * dkl]
            dv = _dot(p, do, TN) + _dot(k * ebl, dst, NT)
            through = jnp.exp(b_last) * jnp.sum(dst * st_prev, axis=0, keepdims=True)
            dlg = _split_dot(gsumv, jnp.concatenate(pairs, axis=0)) + through
            dst = dst * jnp.exp(b_last) + _dot(do, q * eb, TN)
            dq_ref[rows, :] = (dq * dq_dqr).astype(ACT_DTYPE)
            dfv = dlg / f - dk
            df_ref[rows, :] = (dfv * (1.0 - lb) * sf * (1.0 - sf)).astype(ACT_DTYPE)
            di_ref[rows, :] = dv.astype(ACT_DTYPE)
            dlb = dlb + jnp.sum(dfv * (1.0 - sf), axis=0, keepdims=True)
            return dst, dlb, dnw

        def body(it, carry):
            dsts, dlb, dnw = carry
            out = []
            for s, dst in enumerate(dsts):
                dst, dlb, dnw = chunk(s, NC - 1 - it, dst, dlb, dnw)
                out.append(dst)
            return tuple(out), dlb, dnw

        zrow = jnp.zeros((1, HEAD_DIM), F32)
        zst = tuple(jnp.zeros((HEAD_DIM, HEAD_DIM), F32) for _ in range(B))
        _, dlb, dnw = lax.fori_loop(0, NC, body, (zst, zrow, zrow))
        dg0 = dlb * lb * (1.0 - lb)
        dlbg_ref[pl.ds(0, 1), :] = dg0
        dlbg_ref[pl.ds(1, 1), :] = -dg0
        dnw_ref[...] = dnw

    col = lambda off: pl.BlockSpec((T, HEAD_DIM), lambda h: (0, off + h))
    full = lambda a: pl.BlockSpec(a.shape, lambda h: (0,) * a.ndim)
    part = jax.ShapeDtypeStruct((T, HW), ACT_DTYPE)
    return _pcall(
        kern, name=name, grid=(H,),
        in_specs=[col(0), col(H), col(2 * H), col(3 * H),
                  pl.BlockSpec((2, HEAD_DIM), lambda h: (0, h)),
                  pl.BlockSpec((1, HEAD_DIM), lambda h: (0, h)),
                  full(mc), full(masks), full(gsum), col(0),
                  pl.BlockSpec((B, None, NC, HEAD_DIM, HEAD_DIM), lambda h: (0, h, 0, 0, 0)),
                  col(0)],
        out_specs=[col(0), col(0), col(0), col(0),
                   pl.BlockSpec((2, HEAD_DIM), lambda h: (0, h)),
                   pl.BlockSpec((1, HEAD_DIM), lambda h: (0, h))],
        out_shape=[part, part, part, part, jax.ShapeDtypeStruct((2, HW), F32), jax.ShapeDtypeStruct((1, HW), F32)],
        compiler_params=_params(("parallel",), 28 * T * HEAD_DIM * 4 + (8 << 20)),
    )(proj, proj, proj, proj, lb_gamma, norm_w, mc, masks, gsum, o_raw, states, d_mix)


def _lru_gates(xr, cw_ref, cb, wa, ba, wx, bx, lam):
    S = xr.shape[0]
    xb = _conv(xr, cw_ref, cb, LRU_CONV)
    r = _sigmoid(_dot(xb, wa) + ba)
    ig = _sigmoid(_dot(xb, wx) + bx)
    sp = jnp.maximum(-lam, 0.0) + jnp.log(1.0 + jnp.exp(-jnp.abs(lam)))
    la = -LRU_C * r * sp
    a = jnp.exp(la)
    mult = jnp.where(_rows(S) == 0, 1.0, jnp.sqrt(_one_minus_exp(2.0 * la)))
    return xb, r, ig, sp, a, mult


def _scan_rows(a_ref, u_ref, h_ref, reverse):
    S, W = a_ref.shape
    nb = S // 8
    row = _rows(8)

    def body(it, carry):
        blk = nb - 1 - it if reverse else it
        rows = pl.ds(pl.multiple_of(blk * 8, 8), 8)
        a, u = a_ref[rows, :], u_ref[rows, :]
        for d in (1, 2, 4):
            sh = 8 - d if reverse else d
            keep = (row < 8 - d) if reverse else (row >= d)
            u = u + jnp.where(keep, a * pltpu.roll(u, sh, 0), 0.0)
            a = jnp.where(keep, a * pltpu.roll(a, sh, 0), a)
        h = u + a * carry
        h_ref[rows, :] = h
        return h[0:1] if reverse else h[7:8]

    lax.fori_loop(0, nb, body, jnp.zeros((1, W), F32))


def _lru_fwd(proj, cw, cb, wa, ba, wx, bx, lam, B, HW, LW, name):
    T = proj.shape[0]
    S, NB = T // B, LW // HEAD_DIM
    xoff, yoff = 4 * HW // HEAD_DIM, 4 * HW // HEAD_DIM + NB

    def kern(x_ref, y_ref, cw_ref, cb_ref, wa_ref, ba_ref, wx_ref, bx_ref, lam_ref, h_ref, z_ref, a_s, u_s):
        xb, _, ig, _, a, mult = _lru_gates(x_ref[...], cw_ref, cb_ref[...], wa_ref[...], ba_ref[...],
                                           wx_ref[...], bx_ref[...], lam_ref[...])
        a_s[...] = a
        u_s[...] = xb * ig * mult
        _scan_rows(a_s, u_s, h_ref, False)
        gy, _ = _gelu_and_grad(y_ref[...])
        z_ref[...] = h_ref[...] * gy

    blk = lambda off: pl.BlockSpec((S, HEAD_DIM), lambda b, n: (b, off + n))
    vec = pl.BlockSpec((1, HEAD_DIM), lambda b, n: (0, n))
    mat = pl.BlockSpec((None, HEAD_DIM, HEAD_DIM), lambda b, n: (n, 0, 0))
    return _pcall(
        kern, name=name, grid=(B, NB),
        in_specs=[blk(xoff), blk(yoff), pl.BlockSpec((LRU_CONV, HEAD_DIM), lambda b, n: (0, n)),
                  vec, mat, vec, mat, vec, vec],
        out_specs=[blk(0), blk(0)],
        out_shape=[jax.ShapeDtypeStruct((T, LW), F32), jax.ShapeDtypeStruct((T, LW), F32)],
        scratch_shapes=[pltpu.VMEM((S, HEAD_DIM), F32), pltpu.VMEM((S, HEAD_DIM), F32)],
        compiler_params=_params(("parallel", "parallel"), 24 * S * HEAD_DIM * 4),
    )(proj, proj, cw, cb, wa, ba, wx, bx, lam)


def _lru_bwd(proj, cw, cb, wa, ba, wx, bx, lam, h, dz, B, HW, LW, name):
    T = proj.shape[0]
    S, NB = T // B, LW // HEAD_DIM
    xoff, yoff = 4 * HW // HEAD_DIM, 4 * HW // HEAD_DIM + NB

    def kern(x_ref, y_ref, cw_ref, cb_ref, wa_ref, ba_ref, wx_ref, bx_ref, lam_ref, h_ref, dz_ref,
             dx_ref, dy_ref, dwa_ref, dwx_ref, dba_ref, dbx_ref, dlam_ref, dcw_ref, dcb_ref, a_s, u_s, dh_s):
        bi = pl.program_id(1)
        row = _rows(S)
        xr, lam = x_ref[...], lam_ref[...]
        wa, wx = wa_ref[...], wx_ref[...]
        xb, r, ig, sp, a, mult = _lru_gates(xr, cw_ref, cb_ref[...], wa, ba_ref[...], wx, bx_ref[...], lam)
        hv, dz = h_ref[...], dz_ref[...]
        gy, dgy = _gelu_and_grad(y_ref[...])
        dy_ref[...] = (dz * hv * dgy).astype(ACT_DTYPE)
        a_s[...] = jnp.where(row < S - 1, pltpu.roll(a, S - 1, 0), 0.0)
        u_s[...] = dz * gy
        _scan_rows(a_s, u_s, dh_s, True)
        dh = dh_s[...]
        h_prev = jnp.where(row >= 1, pltpu.roll(hv, 1, 0), 0.0)
        d_ig = dh * xb * mult
        d_mult = jnp.where(row == 0, 0.0, dh * xb * ig)
        dxb = dh * ig * mult
        dla = dh * h_prev * a - d_mult * (a * a) / mult
        dpre_r = dla * (-LRU_C * sp) * r * (1.0 - r)
        dpre_i = d_ig * ig * (1.0 - ig)
        dxb = dxb + _dot(dpre_r, wa, NT) + _dot(dpre_i, wx, NT)
        dxr, dcw = _conv_bwd(xr, dxb, cw_ref, LRU_CONV)
        dx_ref[...] = dxr.astype(ACT_DTYPE)

        @pl.when(bi == 0)
        def _():
            for ref in (dwa_ref, dwx_ref, dba_ref, dbx_ref, dlam_ref, dcw_ref, dcb_ref):
                ref[...] = jnp.zeros_like(ref)

        dwa_ref[...] += _dot(xb, dpre_r, TN)
        dwx_ref[...] += _dot(xb, dpre_i, TN)
        dba_ref[...] += jnp.sum(dpre_r, axis=0, keepdims=True)
        dbx_ref[...] += jnp.sum(dpre_i, axis=0, keepdims=True)
        dsp = jnp.sum(dla * (-LRU_C) * r, axis=0, keepdims=True)
        dlam_ref[...] += -dsp * _sigmoid(-lam)
        for j, rj in enumerate(dcw):
            dcw_ref[pl.ds(j, 1), :] += rj
        dcb_ref[...] += jnp.sum(dxb, axis=0, keepdims=True)

    blk = lambda off: pl.BlockSpec((S, HEAD_DIM), lambda n, b: (b, off + n))
    vec = pl.BlockSpec((1, HEAD_DIM), lambda n, b: (0, n))
    mat = pl.BlockSpec((None, HEAD_DIM, HEAD_DIM), lambda n, b: (n, 0, 0))
    cwb = pl.BlockSpec((LRU_CONV, HEAD_DIM), lambda n, b: (0, n))
    part = jax.ShapeDtypeStruct((T, LW), ACT_DTYPE)
    vshape = jax.ShapeDtypeStruct((1, LW), F32)
    mshape = jax.ShapeDtypeStruct((NB, HEAD_DIM, HEAD_DIM), F32)
    return _pcall(
        kern, name=name, grid=(NB, B),
        in_specs=[blk(xoff), blk(yoff), cwb, vec, mat, vec, mat, vec, vec, blk(0), blk(0)],
        out_specs=[blk(0), blk(0), mat, mat, vec, vec, vec, cwb, vec],
        out_shape=[part, part, mshape, mshape, vshape, vshape, vshape,
                   jax.ShapeDtypeStruct((LRU_CONV, LW), F32), vshape],
        scratch_shapes=[pltpu.VMEM((S, HEAD_DIM), F32)] * 3,
        compiler_params=_params(("parallel", "arbitrary"), 40 * S * HEAD_DIM * 4),
    )(proj, proj, cw, cb, wa, ba, wx, bx, lam, h, dz)


def _pos():
    return lax.axis_index("x"), lax.axis_index("y"), lax.axis_index("c")


def _other_chips(x, y):
    return [(1 - x, y), (x, 1 - y), (1 - x, 1 - y)]


def _remote(src, dst, send_sems, recv_sems, k, to):
    return pltpu.make_async_remote_copy(src_ref=src, dst_ref=dst, send_sem=send_sems.at[k],
                                        recv_sem=recv_sems.at[k], device_id=to, device_id_type=MESH)


HBM_BLK = pl.BlockSpec(memory_space=pltpu.HBM)
SEM_BLK = pl.BlockSpec(memory_space=pltpu.SEMAPHORE)
VMEM_BLK = pl.BlockSpec(memory_space=pltpu.VMEM)
DATAFLOW = pltpu.SideEffectType.DATAFLOW_SIDE_EFFECTING
TOKEN = jax.ShapeDtypeStruct((8, V7X_LANES), F32)


def _in_hbm(a):
    return pltpu.with_memory_space_constraint(a, pltpu.HBM)


def _gather_win(o_ref, R, C, col_sharded):
    Rh = R // 2

    def win(j, h=None):
        if col_sharded:
            rows = pl.ds(0, R) if h is None else pl.ds(h * Rh, Rh)
            return o_ref.at[rows, pl.ds(j * C, C)]
        return o_ref.at[pl.ds(j * R, R) if h is None else pl.ds(j * R + h * Rh, Rh), :]

    return win


def _cast_into_window(w, col_sharded, after, name):
    R, C = w.shape
    tr = _tile(R, 256, 16)
    nr = R // tr
    full = (R, 4 * C) if col_sharded else (4 * R, C)
    j = (2 * lax.axis_index("x") + lax.axis_index("y")).astype(jnp.int32).reshape(1)

    def kern(j_ref, w_ref, after_ref, o_ref):
        o_ref[...] = w_ref[...].astype(ACT_DTYPE)

    out_map = (lambda i, jr: (i, jr[0])) if col_sharded else (lambda i, jr: (jr[0] * nr + i, 0))
    grid_spec = pltpu.PrefetchScalarGridSpec(
        num_scalar_prefetch=1, grid=(nr,),
        in_specs=[pl.BlockSpec((tr, C), lambda i, jr: (i, 0)), HBM_SPEC], out_specs=pl.BlockSpec((tr, C), out_map))
    return _pcall(kern, name=name, grid_spec=grid_spec, out_shape=jax.ShapeDtypeStruct(full, ACT_DTYPE),
                  compiler_params=_params(("parallel",), 6 * tr * C * 4))(j, w, after)


def _gather_start(land, shard_shape, col_sharded, token, name):
    R, C = shard_shape

    def body(land_ref, tok_ref, send_sems, recv_sems, land_thru, tok_out):
        x, y, c = _pos()
        w = _gather_win(land_ref, R, C, col_sharded)(2 * x + y, c)
        for k, (cx, cy) in enumerate(_other_chips(x, y)):
            _remote(w, w, send_sems, recv_sems, k, (cx, cy, c)).start()
        tok_out[...] = tok_ref[...]

    return _pcall(
        body, name=name,
        out_shape=(pltpu.SemaphoreType.DMA((3,)), pltpu.SemaphoreType.DMA((3,)),
                   pltpu.HBM(land.shape, land.dtype), TOKEN),
        in_specs=(HBM_BLK, VMEM_BLK), out_specs=(SEM_BLK, SEM_BLK, HBM_BLK, VMEM_BLK),
        input_output_aliases={0: 2},
        compiler_params=pltpu.CompilerParams(has_side_effects=DATAFLOW),
    )(_in_hbm(land), token)


def _gather_wait(started, shard_shape, after, col_sharded, name):
    send_sems, recv_sems, land_thru, _ = started
    R, C = shard_shape

    def body(land_ref, send_sems, recv_sems, after_ref, got_ref):
        x, y, c = _pos()
        win = _gather_win(land_ref, R, C, col_sharded)
        for k, (cx, cy) in enumerate(_other_chips(x, y)):
            cp = _remote(win(2 * x + y, c), win(2 * cx + cy, c), send_sems, recv_sems, k, (cx, cy, c))
            cp.wait_send()
            cp.wait_recv()

    return _pcall(
        body, name=name, out_shape=pltpu.HBM(land_thru.shape, land_thru.dtype),
        in_specs=(HBM_BLK, SEM_BLK, SEM_BLK, HBM_SPEC), out_specs=HBM_BLK, input_output_aliases={0: 0},
        compiler_params=pltpu.CompilerParams(has_side_effects=DATAFLOW),
    )(land_thru, send_sems, recv_sems, after)


def _gather_pass_on(landed, shard_shape, col_sharded, name):
    R, C = shard_shape

    def body(in_ref, o_ref, send_sems, recv_sems):
        x, y, c = _pos()
        src, dst = _gather_win(in_ref, R, C, col_sharded), _gather_win(o_ref, R, C, col_sharded)
        chips = _other_chips(x, y)
        passed = [_remote(src(2 * cx + cy, c), dst(2 * cx + cy, c), send_sems, recv_sems, k, (x, y, 1 - c))
                  for k, (cx, cy) in enumerate(chips)]
        for cp in passed:
            cp.start()
        for k, (cx, cy) in enumerate(chips):
            w = dst(2 * cx + cy, 1 - c)
            _remote(w, w, send_sems, recv_sems, k, (x, y, c)).wait_recv()
        for cp in passed:
            cp.wait_send()

    return _pcall(body, name=name, in_specs=[HBM_SPEC], out_specs=HBM_SPEC,
                  out_shape=jax.ShapeDtypeStruct(landed.shape, landed.dtype), input_output_aliases={0: 0},
                  scratch_shapes=[pltpu.SemaphoreType.DMA((3,)), pltpu.SemaphoreType.DMA((3,))])(landed)


def _pair_exchange(g4, name):
    J, _, Rh, W = g4.shape

    def body(g_ref, p_ref, send_sems, recv_sems):
        x, y, c = _pos()
        cp = _remote(g_ref.at[pl.ds(0, J), 1 - c], p_ref, send_sems, recv_sems, 0, (x, y, 1 - c))
        cp.start()
        cp.wait()

    return _pcall(body, name=name, in_specs=[HBM_SPEC], out_specs=HBM_SPEC,
                  out_shape=jax.ShapeDtypeStruct((J, Rh, W), g4.dtype),
                  scratch_shapes=[pltpu.SemaphoreType.DMA((1,)), pltpu.SemaphoreType.DMA((1,))])(g4)


def _pair_add(g4, p, name):
    J, _, Rh, W = g4.shape
    tr = _tile(Rh, 256, 16)
    tw = _tile(W, 2048, V7X_LANES)
    c = lax.axis_index("c").astype(jnp.int32).reshape(1)

    def kern(c_ref, g_ref, p_ref, o_ref):
        o_ref[...] = (g_ref[...].astype(F32) + p_ref[...].astype(F32)).astype(ACT_DTYPE)

    grid_spec = pltpu.PrefetchScalarGridSpec(
        num_scalar_prefetch=1, grid=(J, Rh // tr, W // tw),
        in_specs=[pl.BlockSpec((None, None, tr, tw), lambda j, i, w, cr: (j, cr[0], i, w)),
                  pl.BlockSpec((None, tr, tw), lambda j, i, w, cr: (j, i, w))],
        out_specs=pl.BlockSpec((None, tr, tw), lambda j, i, w, cr: (j, i, w)))
    return _pcall(kern, name=name, grid_spec=grid_spec,
                  out_shape=jax.ShapeDtypeStruct((J, Rh, W), ACT_DTYPE),
                  compiler_params=_params(("parallel", "parallel", "parallel"), 12 * tr * tw * 4))(c, g4, p)


def _pair_start(g, token, name):
    def body(g_ref, land_ref, tok_ref, send_sems, recv_sems, g_thru, land_thru, tok_out):
        x, y, c = _pos()
        _remote(g_ref, land_ref, send_sems, recv_sems, 0, (x, y, 1 - c)).start()
        tok_out[...] = tok_ref[...]

    return _pcall(
        body, name=name,
        out_shape=(pltpu.SemaphoreType.DMA((1,)), pltpu.SemaphoreType.DMA((1,)),
                   pltpu.HBM(g.shape, g.dtype), pltpu.HBM(g.shape, g.dtype), TOKEN),
        in_specs=(HBM_BLK, HBM_BLK, VMEM_BLK), out_specs=(SEM_BLK, SEM_BLK, HBM_BLK, HBM_BLK, VMEM_BLK),
        input_output_aliases={0: 2, 1: 3},
        compiler_params=pltpu.CompilerParams(has_side_effects=DATAFLOW),
    )(_in_hbm(g), _in_hbm(lax.empty(g.shape, g.dtype)), token)


def _pair_wait(started, after, name):
    send_sems, recv_sems, g_thru, land_thru, _ = started

    def body(g_ref, land_ref, send_sems, recv_sems, after_ref, g_dead, got_ref):
        x, y, c = _pos()
        cp = _remote(g_ref, land_ref, send_sems, recv_sems, 0, (x, y, 1 - c))
        cp.wait_send()
        cp.wait_recv()

    return _pcall(
        body, name=name,
        out_shape=(pltpu.HBM(g_thru.shape, g_thru.dtype), pltpu.HBM(land_thru.shape, land_thru.dtype)),
        in_specs=(HBM_BLK, HBM_BLK, SEM_BLK, SEM_BLK, HBM_SPEC), out_specs=(HBM_BLK, HBM_BLK),
        input_output_aliases={0: 0, 1: 1},
        compiler_params=pltpu.CompilerParams(has_side_effects=DATAFLOW),
    )(g_thru, land_thru, send_sems, recv_sems, after)[1]


def _own_piece_into_slots(cs, name):
    _, Rh, W = cs.shape
    C = W // 4
    tr = _tile(Rh, 256, 16)
    tw = _tile(C, 8192, V7X_LANES)
    nw = C // tw
    x, y, c = _pos()
    chip = (2 * x + y).astype(jnp.int32).reshape(1)
    core = c.astype(jnp.int32).reshape(1)

    def kern(j_ref, c_ref, s_ref, o_ref):
        o_ref[...] = s_ref[...]

    grid_spec = pltpu.PrefetchScalarGridSpec(
        num_scalar_prefetch=2, grid=(Rh // tr, nw),
        in_specs=[pl.BlockSpec((None, tr, tw), lambda i, w, j, cc: (0, i, j[0] * nw + w))],
        out_specs=pl.BlockSpec((None, None, tr, tw), lambda i, w, j, cc: (j[0], cc[0], i, w)))
    return _pcall(kern, name=name, grid_spec=grid_spec,
                  out_shape=jax.ShapeDtypeStruct((4, 2, Rh, C), cs.dtype),
                  compiler_params=_params(("parallel", "parallel"), 8 * tr * tw * 4))(chip, core, cs)


def _chip_sum_piece(cs_ref, C, col_sharded):
    return lambda j: cs_ref.at[0, :, pl.ds(j * C, C)] if col_sharded else cs_ref.at[j]


def _pair_add_own(g4, p, col_sharded, name):
    J, _, Rh, W = g4.shape
    C = W // 4 if col_sharded else W
    tr = _tile(Rh, 256, 16)
    tw = _tile(C, 8192, V7X_LANES)
    nw = C // tw
    x, y, c = _pos()
    chip = (2 * x + y).astype(jnp.int32).reshape(1)
    core = c.astype(jnp.int32).reshape(1)

    def kern(j_ref, c_ref, g_ref, p_ref, o_ref):
        o_ref[...] = (g_ref[...].astype(F32) + p_ref[...].astype(F32)).astype(ACT_DTYPE)

    if col_sharded:
        g_map = lambda i, w, j, cc: (0, cc[0], i, j[0] * nw + w)
        p_map = lambda i, w, j, cc: (0, i, j[0] * nw + w)
    else:
        g_map = lambda i, w, j, cc: (j[0], cc[0], i, w)
        p_map = lambda i, w, j, cc: (j[0], i, w)
    grid_spec = pltpu.PrefetchScalarGridSpec(
        num_scalar_prefetch=2, grid=(Rh // tr, nw),
        in_specs=[pl.BlockSpec((None, None, tr, tw), g_map), pl.BlockSpec((None, tr, tw), p_map)],
        out_specs=pl.BlockSpec((None, None, tr, tw), lambda i, w, j, cc: (j[0], cc[0], i, w)))
    return _pcall(kern, name=name, grid_spec=grid_spec,
                  out_shape=jax.ShapeDtypeStruct((4, 2, Rh, C), ACT_DTYPE),
                  compiler_params=_params(("parallel", "parallel"), 12 * tr * tw * 4))(chip, core, g4, p)


def _scatter_start(cs, slots, col_sharded, token, name):
    C = slots.shape[3]

    def body(cs_ref, land_ref, tok_ref, send_sems, recv_sems, cs_thru, land_thru, tok_out):
        x, y, c = _pos()
        piece = _chip_sum_piece(cs_ref, C, col_sharded)
        for k, (cx, cy) in enumerate(_other_chips(x, y)):
            _remote(piece(2 * cx + cy), land_ref.at[2 * x + y, c], send_sems, recv_sems, k, (cx, cy, c)).start()
        tok_out[...] = tok_ref[...]

    return _pcall(
        body, name=name,
        out_shape=(pltpu.SemaphoreType.DMA((3,)), pltpu.SemaphoreType.DMA((3,)),
                   pltpu.HBM(cs.shape, cs.dtype), pltpu.HBM(slots.shape, cs.dtype), TOKEN),
        in_specs=(HBM_BLK, HBM_BLK, VMEM_BLK), out_specs=(SEM_BLK, SEM_BLK, HBM_BLK, HBM_BLK, VMEM_BLK),
        input_output_aliases={0: 2, 1: 3},
        compiler_params=pltpu.CompilerParams(has_side_effects=DATAFLOW),
    )(_in_hbm(cs), _in_hbm(slots), token)


def _scatter_wait(started, after, col_sharded, name):
    send_sems, recv_sems, cs_thru, land_thru, _ = started
    C = land_thru.shape[3]

    def body(cs_ref, land_ref, send_sems, recv_sems, after_ref, cs_dead, got_ref):
        x, y, c = _pos()
        piece = _chip_sum_piece(cs_ref, C, col_sharded)
        for k, (cx, cy) in enumerate(_other_chips(x, y)):
            cp = _remote(piece(2 * cx + cy), land_ref.at[2 * cx + cy, c], send_sems, recv_sems, k, (cx, cy, c))
            cp.wait_send()
            cp.wait_recv()

    return _pcall(
        body, name=name,
        out_shape=(pltpu.HBM(cs_thru.shape, cs_thru.dtype), pltpu.HBM(land_thru.shape, land_thru.dtype)),
        in_specs=(HBM_BLK, HBM_BLK, SEM_BLK, SEM_BLK, HBM_SPEC), out_specs=(HBM_BLK, HBM_BLK),
        input_output_aliases={0: 0, 1: 1},
        compiler_params=pltpu.CompilerParams(has_side_effects=DATAFLOW),
    )(cs_thru, land_thru, send_sems, recv_sems, after)


def _scatter_pass_on(landed, name):
    def body(in_ref, o_ref, send_sems, recv_sems):
        x, y, c = _pos()
        sends = [_remote(in_ref.at[i, c], o_ref.at[i, c], send_sems, recv_sems, i, (x, y, 1 - c)) for i in range(4)]
        for cp in sends:
            cp.start()
        for i in range(4):
            w = o_ref.at[i, 1 - c]
            _remote(w, w, send_sems, recv_sems, i, (x, y, c)).wait_recv()
        for cp in sends:
            cp.wait_send()

    return _pcall(body, name=name, in_specs=[HBM_SPEC], out_specs=HBM_SPEC,
                  out_shape=jax.ShapeDtypeStruct(landed.shape, landed.dtype), input_output_aliases={0: 0},
                  scratch_shapes=[pltpu.SemaphoreType.DMA((4,)), pltpu.SemaphoreType.DMA((4,))])(landed)


def _gather_small(buf, name):
    rows = buf.shape[0]

    def body(b_ref, o_ref, send_sems, recv_sems):
        x, y, c = _pos()
        jme = 2 * x + y
        chips = _other_chips(x, y)
        o_ref[jme] = b_ref[...]
        sends = [_remote(b_ref, o_ref.at[jme], send_sems, recv_sems, k, (cx, cy, c))
                 for k, (cx, cy) in enumerate(chips)]
        for cp in sends:
            cp.start()
        for k, (cx, cy) in enumerate(chips):
            w = o_ref.at[2 * cx + cy]
            _remote(w, w, send_sems, recv_sems, k, (x, y, c)).wait_recv()
        for cp in sends:
            cp.wait_send()

    vm = pl.BlockSpec(memory_space=pltpu.VMEM)
    return _pcall(body, name=name, in_specs=[vm], out_specs=vm,
                  out_shape=jax.ShapeDtypeStruct((4, rows, V7X_LANES), buf.dtype),
                  scratch_shapes=[pltpu.SemaphoreType.DMA((3,)), pltpu.SemaphoreType.DMA((3,))],
                  compiler_params=_params(None, 16 * rows * V7X_LANES * 4))(buf)


def _allreduce_small(buf, name):
    rows = buf.shape[0]

    def body(b_ref, o_ref, slots, send_sems, recv_sems):
        x, y, c = _pos()
        me = 4 * x + 2 * y + c
        slots[me] = b_ref[...]
        sends = []
        for k in range(1, 8):
            kx, ky, kc = (k >> 2) & 1, (k >> 1) & 1, k & 1
            to = (x ^ kx, y ^ ky, c ^ kc)
            sends.append(_remote(b_ref, slots.at[me], send_sems, recv_sems, k - 1, to))
        for cp in sends:
            cp.start()
        for k in range(1, 8):
            kx, ky, kc = (k >> 2) & 1, (k >> 1) & 1, k & 1
            w = slots.at[4 * (x ^ kx) + 2 * (y ^ ky) + (c ^ kc)]
            _remote(w, w, send_sems, recv_sems, k - 1, (x, y, c)).wait_recv()
        for cp in sends:
            cp.wait_send()
        acc = slots[0]
        for d in range(1, 8):
            acc = acc + slots[d]
        o_ref[...] = acc

    vm = pl.BlockSpec(memory_space=pltpu.VMEM)
    return _pcall(body, name=name, in_specs=[vm], out_specs=vm,
                  out_shape=jax.ShapeDtypeStruct(buf.shape, buf.dtype),
                  scratch_shapes=[pltpu.VMEM((8, rows, V7X_LANES), buf.dtype),
                                  pltpu.SemaphoreType.DMA((7,)), pltpu.SemaphoreType.DMA((7,))],
                  compiler_params=_params(None, 14 * rows * V7X_LANES * 4))(buf)


def _adamw_math(w, g, m, v):
    m = ADAM_B1 * m + (1.0 - ADAM_B1) * g
    v = ADAM_B2 * v + (1.0 - ADAM_B2) * (g * g)
    m_hat = m / (1.0 - ADAM_B1 ** ADAM_STEP)
    v_hat = v / (1.0 - ADAM_B2 ** ADAM_STEP)
    delta = -ADAM_LR * (m_hat / (jnp.sqrt(v_hat) + ADAM_EPS) + ADAM_WD * w)
    return delta, m, v


def _adamw_big(w, m, v, slots, name):
    R, C = w.shape
    tr = _tile(R, 32, 16)

    def kern(w_ref, m_ref, v_ref, s_ref, g_ref, d_ref, mo_ref, vo_ref):
        g = s_ref[0].astype(F32)
        for i in range(1, 4):
            g = g + s_ref[i].astype(F32)
        d, mn, vn = _adamw_math(w_ref[...], g, m_ref[...], v_ref[...])
        g_ref[...], d_ref[...], mo_ref[...], vo_ref[...] = g, d, mn, vn

    row = pl.BlockSpec((tr, C), lambda i: (i, 0))
    shp = jax.ShapeDtypeStruct((R, C), F32)
    return _pcall(kern, name=name, grid=(R // tr,),
                  in_specs=[row, row, row, pl.BlockSpec((4, tr, C), lambda i: (0, i, 0))],
                  out_specs=[row] * 4, out_shape=[shp] * 4,
                  compiler_params=_params(("parallel",), 36 * tr * C * 4))(w, m, v, slots)


def _adamw_small(w, g, m, v, name):
    def kern(w_ref, g_ref, m_ref, v_ref, d_ref, mo_ref, vo_ref):
        d_ref[...], mo_ref[...], vo_ref[...] = _adamw_math(w_ref[...], g_ref[...], m_ref[...], v_ref[...])

    vm = pl.BlockSpec(memory_space=pltpu.VMEM)
    shp = jax.ShapeDtypeStruct(w.shape, F32)
    return _pcall(kern, name=name, in_specs=[vm] * 4, out_specs=[vm] * 3, out_shape=[shp] * 3,
                  compiler_params=_params(None, 10 * w.size * 4))(w, g, m, v)


def _pack(arrs):
    flat = jnp.concatenate([a.reshape(-1).astype(F32) for a in arrs])
    n = flat.shape[0]
    rows = -(-n // (8 * V7X_LANES)) * 8
    return jnp.pad(flat, (0, rows * V7X_LANES - n)).reshape(rows, V7X_LANES)


def _unpack(buf, shapes):
    flat = buf.reshape(-1)
    out, off = [], 0
    for s in shapes:
        n = int(np.prod(s))
        out.append(flat[off:off + n].reshape(s))
        off += n
    return out


def _reduce_start(gfull, col_sharded, shard_shape, token, tag):
    R, C = shard_shape
    if col_sharded:
        g4 = gfull.reshape(1, 2, R // 2, 4 * C)
    else:
        g4 = gfull.reshape(4, 2, R // 2, C)
    p = _pair_exchange(g4, "pair_exchange_" + tag)
    cs = _pair_add(g4, p, "pair_add_" + tag)
    slots = _pair_add_own(g4, p, col_sharded, "pair_add_own_" + tag)
    return _scatter_start(cs, slots, col_sharded, token, "scatter_start_" + tag)


def _col_cut_gradient(a, d, token, tag, other_work):
    c = lax.axis_index("c").astype(jnp.int32)
    for_sibling = _matmul_tn_half(a, d, (1 - c).reshape(1), None, 1024, 512, "mm_g_%s_sibling" % tag)
    sent = _pair_start(for_sibling, token, "pair_start_" + tag)
    other = other_work(sent[4])
    arrived = _pair_wait(sent, other, "pair_wait_" + tag)
    cs = _matmul_tn_half(a, d, c.reshape(1), arrived, 1024, 512, "mm_g_%s_own" % tag)
    cs = cs.reshape((1,) + cs.shape)
    slots = _own_piece_into_slots(cs, "own_piece_" + tag)
    return _scatter_start(cs, slots, True, sent[4], "scatter_start_" + tag), other


def _reduce_finish(started, after, col_sharded, w, m, v, tag):
    R, C = w.shape
    _, landed = _scatter_wait(started, after, col_sharded, "scatter_wait_" + tag)
    slots = _scatter_pass_on(landed, "scatter_pass_on_" + tag)
    return _adamw_big(w, m, v, slots.reshape(4, R, C), "adamw_" + tag)


def kernel(x, ln1_w, w_in, lb_gamma, hg_norm_w, lru_conv_w, lru_conv_b, lru_wa, lru_ba, lru_wx, lru_bx, lru_lambda, lru_norm_w, w_out, ln2_w, ffn_w_up, ffn_conv_w, ffn_conv_b, ffn_w_down, final_norm_w, loss_target, m_ln1_w, m_w_in, m_lb_gamma, m_hg_norm_w, m_lru_conv_w, m_lru_conv_b, m_lru_wa, m_lru_ba, m_lru_wx, m_lru_bx, m_lru_lambda, m_lru_norm_w, m_w_out, m_ln2_w, m_ffn_w_up, m_ffn_conv_w, m_ffn_conv_b, m_ffn_w_down, m_final_norm_w, v_ln1_w, v_w_in, v_lb_gamma, v_hg_norm_w, v_lru_conv_w, v_lru_conv_b, v_lru_wa, v_lru_ba, v_lru_wx, v_lru_bx, v_lru_lambda, v_lru_norm_w, v_w_out, v_ln2_w, v_ffn_w_up, v_ffn_conv_w, v_ffn_conv_b, v_ffn_w_down, v_final_norm_w):
    B, S, D = x.shape
    T = B * S
    HW = lb_gamma.shape[1]
    LW = lru_conv_b.shape[1]
    assert S % CHUNK == 0 and HW % HEAD_DIM == 0 and lru_wa.shape[2] == HEAD_DIM
    x2 = x.reshape(T, D)
    tgt = loss_target.reshape(T, D)
    jchip = 2 * lax.axis_index("x") + lax.axis_index("y")

    conv_shapes = [lru_conv_w[0].shape, ffn_conv_w[0].shape]
    convs = _gather_small(_pack([lru_conv_w[0], ffn_conv_w[0]]), "gather_conv_w")
    per_chip = [_unpack(convs[j], conv_shapes) for j in range(4)]
    lcw = jnp.concatenate([pc[0] for pc in per_chip], axis=1)
    fcw = jnp.concatenate([pc[1] for pc in per_chip], axis=1)
    masters = dict(w_in=w_in[0], w_out=w_out[0], w_up=ffn_w_up[0], w_down=ffn_w_down[0])
    col_of = dict(w_in=True, w_out=False, w_up=True, w_down=False)
    started, token, after = {}, jnp.zeros(TOKEN.shape, F32), convs
    for n in ("w_in", "w_out", "w_up", "w_down"):
        land = _cast_into_window(masters[n], col_of[n], after, "cast_" + n)
        started[n] = _gather_start(land, masters[n].shape, col_of[n], token, "gather_start_" + n)
        token = after = started[n][3]

    def gathered(n, after):
        landed = _gather_wait(started[n], masters[n].shape, after, col_of[n], "gather_wait_" + n)
        return _gather_pass_on(landed, masters[n].shape, col_of[n], "gather_pass_on_" + n)

    W_in = gathered("w_in", token)

    hn1 = _rms_fwd(x2, ln1_w, "rms1")
    proj = _matmul(hn1, W_in, "NN", F32, 1024, 512, 4096, name="mm_proj")
    o_raw, o_hg, states = _hgrn_fwd(proj, lb_gamma, hg_norm_w, B, HW, "hgrn_fwd")
    h_lru, z = _lru_fwd(proj, lcw, lru_conv_b, lru_wa[0], lru_ba, lru_wx[0], lru_bx, lru_lambda, B, HW, LW, "lru_fwd")
    o_lru = _rms_fwd(z, lru_norm_w, "rms_lru")
    mix = jnp.concatenate([o_hg, o_lru], axis=1)
    W_out = gathered("w_out", mix)
    h1 = _matmul(mix, W_out, "NN", F32, 1024, 512, 4096, add=x2, name="mm_out")
    hn2 = _rms_fwd(h1, ln2_w, "rms2")
    W_up = gathered("w_up", hn2)
    up = _matmul(hn2, W_up, "NN", F32, 1024, 512, 4096, name="mm_up")
    act, act_dg, act_dv = _ffn_act(up, fcw, ffn_conv_b, B, "ffn_act")
    W_down = gathered("w_down", act)
    h2 = _matmul(act, W_down, "NN", F32, 1024, 512, 5504, add=h1, name="mm_down")

    dh2, dh2a, d_final_w, loss_part = _loss_bwd(h2, tgt, final_norm_w.reshape(1, D), "loss_bwd")
    g_down = _matmul(act, dh2a, "TN", ACT_DTYPE, 256, 2048, 4096, n_outer=True, name="mm_g_down")
    red_down = _reduce_start(g_down, False, ffn_w_down[0].shape, token, "w_down")
    d_act = _matmul(dh2a, W_down, "NT", ACT_DTYPE, 512, 5504, 512, after=red_down[4], name="mm_d_act")
    d_up, d_fcw, d_fcb = _ffn_act_bwd(up, fcw, act_dg, act_dv, d_act, B, "ffn_act_bwd")
    red_up, d_hn2 = _col_cut_gradient(
        hn2, d_up, red_down[4], "w_up",
        lambda tok: _matmul(d_up, W_up, "NT", F32, 2048, 1024, 512, after=tok, name="mm_d_hn2"))
    dh1, dh1a, d_ln2 = _rms_bwd(h1, ln2_w, d_hn2, 0, dh2, True, "rms2_bwd", after=red_up[4])
    g_out = _matmul(mix, dh1a, "TN", ACT_DTYPE, 1024, 512, 4096, name="mm_g_out")
    red_out = _reduce_start(g_out, False, w_out[0].shape, red_up[4], "w_out")
    d_mix = _matmul(dh1a, W_out, "NT", F32, 1024, 512, 4096, after=red_out[4], name="mm_d_mix")
    dz, d_lru_norm = _rms_bwd(z, lru_norm_w, d_mix, HW // LW, None, False, "rms_lru_bwd")
    (d_xr, d_yr, d_wa, d_wx, d_ba, d_bx, d_lam, d_lcw, d_lcb) = _lru_bwd(
        proj, lcw, lru_conv_b, lru_wa[0], lru_ba, lru_wx[0], lru_bx, lru_lambda, h_lru, dz, B, HW, LW, "lru_bwd")
    d_q, d_f, d_i, d_g, d_lbg, d_hgw = _hgrn_bwd(proj, lb_gamma, hg_norm_w, o_raw, states, d_mix, B, HW, "hgrn_bwd")
    d_proj = jnp.concatenate([d_q, d_f, d_i, d_g, d_xr, d_yr], axis=1)
    red_in, d_hn1 = _col_cut_gradient(
        hn1, d_proj, red_out[4], "w_in",
        lambda tok: _matmul(d_proj, W_in, "NT", F32, 2048, 1024, 1024, after=tok, name="mm_d_hn1"))
    dx, d_ln1 = _rms_bwd(x2, ln1_w, d_hn1, 0, dh1, False, "rms1_bwd", after=red_in[4])

    big = {}
    big["ffn_w_down"] = _reduce_finish(red_down, dx, False, ffn_w_down[0], m_ffn_w_down[0], v_ffn_w_down[0], "w_down")
    big["ffn_w_up"] = _reduce_finish(red_up, big["ffn_w_down"][1], True, ffn_w_up[0], m_ffn_w_up[0], v_ffn_w_up[0], "w_up")
    big["w_out"] = _reduce_finish(red_out, big["ffn_w_up"][1], False, w_out[0], m_w_out[0], v_w_out[0], "w_out")
    big["w_in"] = _reduce_finish(red_in, big["w_out"][1], True, w_in[0], m_w_in[0], v_w_in[0], "w_in")

    small_names = ["ln1_w", "lb_gamma", "hg_norm_w", "lru_conv_w", "lru_conv_b", "lru_wa", "lru_ba", "lru_wx",
                   "lru_bx", "lru_lambda", "lru_norm_w", "ln2_w", "ffn_conv_w", "ffn_conv_b", "final_norm_w"]
    small_grads = [d_ln1, d_lbg, d_hgw, d_lcw, d_lcb, d_wa, d_ba, d_wx, d_bx, d_lam, d_lru_norm, d_ln2,
                   d_fcw, d_fcb, d_final_w]
    red = _allreduce_small(_pack([loss_part[0:1, 0:1]] + small_grads), "allreduce_small")
    red = _unpack(red, [(1, 1)] + [g.shape for g in small_grads])
    loss = red[0].reshape(())
    gs = dict(zip(small_names, red[1:]))
    nlc, nfc = lru_conv_w.shape[2], ffn_conv_w.shape[2]
    gs["lru_conv_w"] = lax.dynamic_slice_in_dim(gs["lru_conv_w"], jchip * nlc, nlc, axis=1)
    gs["ffn_conv_w"] = lax.dynamic_slice_in_dim(gs["ffn_conv_w"], jchip * nfc, nfc, axis=1)
    args = dict(ln1_w=(ln1_w, m_ln1_w, v_ln1_w), lb_gamma=(lb_gamma, m_lb_gamma, v_lb_gamma),
                hg_norm_w=(hg_norm_w, m_hg_norm_w, v_hg_norm_w), lru_conv_w=(lru_conv_w, m_lru_conv_w, v_lru_conv_w),
                lru_conv_b=(lru_conv_b, m_lru_conv_b, v_lru_conv_b), lru_wa=(lru_wa, m_lru_wa, v_lru_wa),
                lru_ba=(lru_ba, m_lru_ba, v_lru_ba), lru_wx=(lru_wx, m_lru_wx, v_lru_wx),
                lru_bx=(lru_bx, m_lru_bx, v_lru_bx), lru_lambda=(lru_lambda, m_lru_lambda, v_lru_lambda),
                lru_norm_w=(lru_norm_w, m_lru_norm_w, v_lru_norm_w), ln2_w=(ln2_w, m_ln2_w, v_ln2_w),
                ffn_conv_w=(ffn_conv_w, m_ffn_conv_w, v_ffn_conv_w), ffn_conv_b=(ffn_conv_b, m_ffn_conv_b, v_ffn_conv_b),
                final_norm_w=(final_norm_w, m_final_norm_w, v_final_norm_w))
    shapes = [args[n][0].shape for n in small_names]
    upd = _adamw_small(_pack([args[n][0] for n in small_names]), _pack([gs[n] for n in small_names]),
                       _pack([args[n][1] for n in small_names]), _pack([args[n][2] for n in small_names]), "adamw_small")
    s_delta, s_m, s_v = (dict(zip(small_names, _unpack(u, shapes))) for u in upd)

    order = ["ln1_w", "w_in", "lb_gamma", "hg_norm_w", "lru_conv_w", "lru_conv_b", "lru_wa", "lru_ba", "lru_wx",
             "lru_bx", "lru_lambda", "lru_norm_w", "w_out", "ln2_w", "ffn_w_up", "ffn_conv_w", "ffn_conv_b",
             "ffn_w_down", "final_norm_w"]
    full_shape = dict(w_in=w_in.shape, w_out=w_out.shape, ffn_w_up=ffn_w_up.shape, ffn_w_down=ffn_w_down.shape)
    grads, deltas, new_m, new_v = [], [], [], []
    for n in order:
        if n in big:
            g, d, mn, vn = (t.reshape(full_shape[n]) for t in big[n])
        else:
            g, d, mn, vn = gs[n].reshape(args[n][0].shape), s_delta[n], s_m[n], s_v[n]
        grads.append(g), deltas.append(d), new_m.append(mn), new_v.append(vn)
    return (loss, dx.reshape(B, S, D), *grads, *deltas, *new_m, *new_v)
```

```python
import functools
import math

import numpy as np
import jax
import jax.numpy as jnp
from jax import lax
from jax.experimental import pallas as pl
from jax.experimental.pallas import tpu as pltpu

F32 = jnp.float32
MXU_DTYPE = jnp.bfloat16
ACT_DTYPE = jnp.bfloat16

EPS = 1e-6
HEAD_DIM = 128
CHUNK = 64
LEVEL_HALVES = (32, 16, 8, 4, 2, 1)
LRU_CONV = 4
FFN_CONV = 3
LRU_C = 8.0
ADAM_LR, ADAM_B1, ADAM_B2, ADAM_EPS, ADAM_WD, ADAM_STEP = 0.001, 0.9, 0.999, 1e-08, 0.01, 10

V7X_LANES = 128
V7X_VMEM_BUDGET = 56 << 20

NN = (((1,), (0,)), ((), ()))
NT = (((1,), (1,)), ((), ()))
TN = (((0,), (0,)), ((), ()))
MESH = pl.DeviceIdType.MESH
HBM_SPEC = pl.BlockSpec(memory_space=pl.ANY)


def _pcall(kern, **kw):
    return pl.pallas_call(kern, **kw)


def _params(sem=None, vmem=None):
    kw = {}
    if sem is not None:
        kw["dimension_semantics"] = sem
    if vmem is not None:
        kw["vmem_limit_bytes"] = int(min(max(vmem, 16 << 20), V7X_VMEM_BUDGET))
    return pltpu.CompilerParams(**kw)


def _dot(a, b, dims=NN):
    return lax.dot_general(a.astype(MXU_DTYPE), b.astype(MXU_DTYPE), dims, preferred_element_type=F32)


def _tile(dim, pref, align):
    t = min(pref, dim) // align * align
    while t >= align:
        if dim % t == 0:
            return t
        t -= align
    return dim


def _sigmoid(x):
    return 1.0 / (1.0 + jnp.exp(-x))


def _silu_and_grad(x):
    s = _sigmoid(x)
    return x * s, s * (1.0 + x * (1.0 - s))


def _gelu_and_grad(x):
    k0, k1 = math.sqrt(2.0 / math.pi), 0.044715
    t = jnp.tanh(k0 * (x + k1 * x * x * x))
    g = 0.5 * x * (1.0 + t)
    dg = 0.5 * (1.0 + t) + 0.5 * x * (1.0 - t * t) * k0 * (1.0 + 3.0 * k1 * x * x)
    return g, dg


def _one_minus_exp(x):
    p = x * (1.0 + x * (0.5 + x * (1.0 / 6.0 + x * (1.0 / 24.0 + x * (1.0 / 120.0)))))
    return jnp.where(x > -0.05, -p, 1.0 - jnp.exp(x))


def _rows(n):
    return lax.broadcasted_iota(jnp.int32, (n, 1), 0)


def _matmul(a, b, mode, out_dtype, tm, tn, tk, add=None, after=None, n_outer=False, name="mm"):
    if mode == "TN":
        K, M = a.shape
    else:
        M, K = a.shape
    N = b.shape[0] if mode == "NT" else b.shape[1]
    tm, tn = _tile(M, tm, V7X_LANES), _tile(N, tn, V7X_LANES)
    tk = _tile(K, tk, V7X_LANES)
    nk = K // tk
    dims = {"NN": NN, "NT": NT, "TN": TN}[mode]
    order = (lambda f: (lambda j, i, k: f(i, j, k))) if n_outer else (lambda f: f)
    a_spec = (pl.BlockSpec((tk, tm), order(lambda i, j, k: (k, i))) if mode == "TN"
              else pl.BlockSpec((tm, tk), order(lambda i, j, k: (i, k))))
    b_spec = (pl.BlockSpec((tn, tk), order(lambda i, j, k: (j, k))) if mode == "NT"
              else pl.BlockSpec((tk, tn), order(lambda i, j, k: (k, j))))
    o_spec = pl.BlockSpec((tm, tn), order(lambda i, j, k: (i, j)))
    has_add = add is not None

    def kern(*refs):
        a_ref, b_ref = refs[:2]
        add_ref = refs[2] if has_add else None

        def finish(r, o_ref):
            if has_add:
                r = r + add_ref[...]
            o_ref[...] = r.astype(out_dtype)

        if nk == 1:
            finish(_dot(a_ref[...], b_ref[...], dims), refs[-1])
            return
        o_ref, acc_ref = refs[-2:]
        k = pl.program_id(2)

        @pl.when(k == 0)
        def _():
            acc_ref[...] = jnp.zeros_like(acc_ref)

        acc_ref[...] += _dot(a_ref[...], b_ref[...], dims)

        @pl.when(k == nk - 1)
        def _():
            finish(acc_ref[...], o_ref)

    ab = jnp.dtype(a.dtype).itemsize
    ob = jnp.dtype(out_dtype).itemsize
    vmem = 2 * (tm * tk + tk * tn) * ab + tm * tn * (8 + 2 * ob + (8 if has_add else 0)) + (4 << 20)
    ins = [a, b] + ([add] if has_add else []) + ([after] if after is not None else [])
    in_specs = [a_spec, b_spec] + ([o_spec] if has_add else []) + ([HBM_SPEC] if after is not None else [])
    grid = (N // tn, M // tm, nk) if n_outer else (M // tm, N // tn, nk)
    return _pcall(
        kern, name=name, grid=grid,
        in_specs=in_specs, out_specs=o_spec,
        out_shape=jax.ShapeDtypeStruct((M, N), out_dtype),
        scratch_shapes=[pltpu.VMEM((tm, tn), F32)] if nk > 1 else [],
        compiler_params=_params(("parallel", "parallel", "arbitrary"), vmem),
    )(*ins)


def _matmul_tn_half(a, b, half, add, tm, tn, by_rows, name):
    K, M = a.shape
    N = b.shape[1]
    Mo, No = (M // 2, N) if by_rows else (M, N // 2)
    tm, tn = _tile(Mo, tm, V7X_LANES), _tile(No, tn, V7X_LANES)
    nm, nn = Mo // tm, No // tn
    has_add = add is not None

    def kern(h_ref, a_ref, b_ref, *rest):
        r = _dot(a_ref[...], b_ref[...], TN)
        if has_add:
            r = r + rest[0][...].astype(F32)
        rest[-1][...] = r.astype(ACT_DTYPE)

    if by_rows:
        a_map, b_map = (lambda i, j, h: (0, h[0] * nm + i)), (lambda i, j, h: (0, j))
        o_map, grid = (lambda i, j, h: (i, j)), (nm, nn)
    else:
        a_map, b_map = (lambda j, i, h: (0, i)), (lambda j, i, h: (0, h[0] * nn + j))
        o_map, grid = (lambda j, i, h: (i, j)), (nn, nm)
    blk = pl.BlockSpec((tm, tn), o_map)
    grid_spec = pltpu.PrefetchScalarGridSpec(
        num_scalar_prefetch=1, grid=grid,
        in_specs=[pl.BlockSpec((K, tm), a_map), pl.BlockSpec((K, tn), b_map)] + ([blk] if has_add else []),
        out_specs=blk)
    ab = jnp.dtype(a.dtype).itemsize
    vmem = 2 * K * (tm + tn) * ab + tm * tn * 16 + (4 << 20)
    return _pcall(kern, name=name, grid_spec=grid_spec, out_shape=jax.ShapeDtypeStruct((Mo, No), ACT_DTYPE),
                  compiler_params=_params(("parallel", "parallel"), vmem))(half, a, b, *([add] if has_add else []))


def _rms_fwd(x, w, name):
    T, D = x.shape
    tm = _tile(T, 256, 16)

    def kern(x_ref, w_ref, o_ref):
        xv = x_ref[...]
        r = lax.rsqrt(jnp.mean(xv * xv, axis=-1, keepdims=True) + EPS)
        o_ref[...] = (xv * r * w_ref[...]).astype(ACT_DTYPE)

    return _pcall(kern, name=name, grid=(T // tm,),
                  in_specs=[pl.BlockSpec((tm, D), lambda i: (i, 0)), pl.BlockSpec((1, D), lambda i: (0, 0))],
                  out_specs=pl.BlockSpec((tm, D), lambda i: (i, 0)),
                  out_shape=jax.ShapeDtypeStruct((T, D), ACT_DTYPE),
                  compiler_params=_params(("parallel",), 8 * tm * D * 4))(x, w)


def _rms_bwd(x, w, g, g_col, res, want_act, name, after=None):
    T, D = x.shape
    tm = _tile(T, 256, 16)
    has_res = res is not None

    def kern(*refs):
        refs = list(refs)
        x_ref, w_ref, g_ref = refs[:3]
        res_ref = refs[3] if has_res else None
        outs = refs[3 + has_res + (after is not None):]
        dx_ref = outs[0]
        dxa_ref = outs[1] if want_act else None
        dw_ref = outs[-1]
        i = pl.program_id(0)
        xv = x_ref[...]
        gv = g_ref[...].astype(F32)
        r = lax.rsqrt(jnp.mean(xv * xv, axis=-1, keepdims=True) + EPS)
        gw = gv * w_ref[...]
        dx = r * gw - xv * (r * r * r) * jnp.mean(gw * xv, axis=-1, keepdims=True)
        if has_res:
            dx = dx + res_ref[...]
        dx_ref[...] = dx
        if want_act:
            dxa_ref[...] = dx.astype(ACT_DTYPE)

        @pl.when(i == 0)
        def _():
            dw_ref[...] = jnp.zeros_like(dw_ref)

        dw_ref[...] += jnp.sum(gv * xv * r, axis=0, keepdims=True)

    row = pl.BlockSpec((tm, D), lambda i: (i, 0))
    vec = pl.BlockSpec((1, D), lambda i: (0, 0))
    in_specs = ([row, vec, pl.BlockSpec((tm, D), lambda i: (i, g_col))] + ([row] if has_res else [])
                + ([HBM_SPEC] if after is not None else []))
    out_specs = [row] + ([row] if want_act else []) + [vec]
    out_shape = ([jax.ShapeDtypeStruct((T, D), F32)]
                 + ([jax.ShapeDtypeStruct((T, D), ACT_DTYPE)] if want_act else [])
                 + [jax.ShapeDtypeStruct((1, D), F32)])
    ins = [x, w, g] + ([res] if has_res else []) + ([after] if after is not None else [])
    return _pcall(kern, name=name, grid=(T // tm,), in_specs=in_specs, out_specs=out_specs,
                  out_shape=out_shape, compiler_params=_params(("arbitrary",), 14 * tm * D * 4))(*ins)


def _loss_bwd(h, target, w, name):
    T, D = h.shape
    tm = _tile(T, 256, 16)

    def kern(h_ref, t_ref, w_ref, dh_ref, dha_ref, dw_ref, loss_ref):
        i = pl.program_id(0)
        hv = h_ref[...]
        r = lax.rsqrt(jnp.mean(hv * hv, axis=-1, keepdims=True) + EPS)
        e = hv * r * w_ref[...] - t_ref[...]
        dy = e * (1.0 / D)
        gw = dy * w_ref[...]
        dh = r * gw - hv * (r * r * r) * jnp.mean(gw * hv, axis=-1, keepdims=True)
        dh_ref[...] = dh
        dha_ref[...] = dh.astype(ACT_DTYPE)

        @pl.when(i == 0)
        def _():
            dw_ref[...] = jnp.zeros_like(dw_ref)
            loss_ref[...] = jnp.zeros_like(loss_ref)

        dw_ref[...] += jnp.sum(dy * hv * r, axis=0, keepdims=True)
        part = 0.5 * jnp.sum(jnp.mean(e * e, axis=-1, keepdims=True), axis=0, keepdims=True)
        loss_ref[...] += jnp.broadcast_to(part, loss_ref.shape)

    row = pl.BlockSpec((tm, D), lambda i: (i, 0))
    vec = pl.BlockSpec((1, D), lambda i: (0, 0))
    return _pcall(kern, name=name, grid=(T // tm,), in_specs=[row, row, vec],
                  out_specs=[row, row, vec, pl.BlockSpec((8, V7X_LANES), lambda i: (0, 0))],
                  out_shape=[jax.ShapeDtypeStruct((T, D), F32), jax.ShapeDtypeStruct((T, D), ACT_DTYPE),
                             jax.ShapeDtypeStruct((1, D), F32), jax.ShapeDtypeStruct((8, V7X_LANES), F32)],
                  compiler_params=_params(("arbitrary",), 14 * tm * D * 4))(h, target, w)


def _conv(x, w_ref, b, width):
    S = x.shape[0]
    row = _rows(S)
    y = b + x * w_ref[pl.ds(width - 1, 1), :]
    for j in range(width - 1):
        sh = width - 1 - j
        y = y + jnp.where(row >= sh, pltpu.roll(x, sh, 0), 0.0) * w_ref[pl.ds(j, 1), :]
    return y


def _conv_bwd(x, dy, w_ref, width):
    S = dy.shape[0]
    row = _rows(S)
    dx = dy * w_ref[pl.ds(width - 1, 1), :]
    dw = [None] * (width - 1) + [jnp.sum(x * dy, axis=0, keepdims=True)]
    for j in range(width - 1):
        sh = width - 1 - j
        dys = jnp.where(row < S - sh, pltpu.roll(dy, S - sh, 0), 0.0)
        dx = dx + dys * w_ref[pl.ds(j, 1), :]
        dw[j] = jnp.sum(x * dys, axis=0, keepdims=True)
    return dx, dw


def _ffn_act(up, cw, cb, B, name):
    T, F2 = up.shape
    S, F = T // B, F2 // 2
    tw = _tile(F, 256, V7X_LANES)
    nt = F // tw

    def kern(g_ref, v_ref, wg_ref, wv_ref, bg_ref, bv_ref, o_ref, mg_ref, mv_ref):
        gc = _conv(g_ref[...], wg_ref, bg_ref[...], FFN_CONV)
        vc = _conv(v_ref[...], wv_ref, bv_ref[...], FFN_CONV)
        silu, dsilu = _silu_and_grad(gc)
        o_ref[...] = (silu * vc).astype(ACT_DTYPE)
        mg_ref[...] = (vc * dsilu).astype(ACT_DTYPE)
        mv_ref[...] = silu.astype(ACT_DTYPE)

    blk = lambda off: pl.BlockSpec((S, tw), lambda b, i: (b, off + i))
    wblk = lambda off: pl.BlockSpec((FFN_CONV, tw), lambda b, i: (0, off + i))
    bblk = lambda off: pl.BlockSpec((1, tw), lambda b, i: (0, off + i))
    half = jax.ShapeDtypeStruct((T, F), ACT_DTYPE)
    return _pcall(kern, name=name, grid=(B, nt),
                  in_specs=[blk(0), blk(nt), wblk(0), wblk(nt), bblk(0), bblk(nt)],
                  out_specs=[blk(0), blk(0), blk(0)], out_shape=[half, half, half],
                  compiler_params=_params(("parallel", "parallel"), 20 * S * tw * 4))(up, up, cw, cw, cb, cb)


def _ffn_act_bwd(up, cw, mg, mv, d_act, B, name):
    T, F2 = up.shape
    S, F = T // B, F2 // 2
    tw = _tile(F, 256, V7X_LANES)
    nt = F // tw

    def kern(s_ref, ws_ref, mg_ref, mv_ref, da_ref, du_ref, dcw_ref, dcb_ref):
        t, b = pl.program_id(0), pl.program_id(1)
        mult = jnp.where(t < nt, mg_ref[...], mv_ref[...])
        d = da_ref[...].astype(F32) * mult.astype(F32)
        dx, dw = _conv_bwd(s_ref[...], d, ws_ref, FFN_CONV)
        du_ref[...] = dx.astype(ACT_DTYPE)

        @pl.when(b == 0)
        def _():
            dcw_ref[...] = jnp.zeros_like(dcw_ref)
            dcb_ref[...] = jnp.zeros_like(dcb_ref)

        for j, rj in enumerate(dw):
            dcw_ref[pl.ds(j, 1), :] += rj
        dcb_ref[...] += jnp.sum(d, axis=0, keepdims=True)

    own = lambda t, b: (b, t % nt)
    return _pcall(
        kern, name=name, grid=(2 * nt, B),
        in_specs=[pl.BlockSpec((S, tw), lambda t, b: (b, t)),
                  pl.BlockSpec((FFN_CONV, tw), lambda t, b: (0, t)),
                  pl.BlockSpec((S, tw), own), pl.BlockSpec((S, tw), own), pl.BlockSpec((S, tw), own)],
        out_specs=[pl.BlockSpec((S, tw), lambda t, b: (b, t)),
                   pl.BlockSpec((FFN_CONV, tw), lambda t, b: (0, t)),
                   pl.BlockSpec((1, tw), lambda t, b: (0, t))],
        out_shape=[jax.ShapeDtypeStruct((T, F2), ACT_DTYPE), jax.ShapeDtypeStruct((FFN_CONV, F2), F32),
                   jax.ShapeDtypeStruct((1, F2), F32)],
        compiler_params=_params(("parallel", "arbitrary"), 20 * S * tw * 4),
    )(up, cw, mg, mv, d_act)


def _hgrn_tables():
    C = CHUNK
    t = np.arange(C)
    mats = [(t[:, None] >= t[None, :]).astype(np.float32)]
    masks = []
    gsum = [(t[:, None] <= t[None, :]).astype(np.float32), (t[:, None] > t[None, :]).astype(np.float32)]
    for hs in LEVEL_HALVES:
        m = (t // (2 * hs)) * 2 * hs + hs
        later = t >= m
        d = np.zeros((C, C), np.float32)
        for i in range(C):
            if later[i]:
                d[i, m[i]:i + 1] = 1.0
            else:
                d[i, i + 1:m[i]] = -1.0
        mats.append(d)
        same = (t[:, None] // (2 * hs)) == (t[None, :] // (2 * hs))
        masks.append((same & later[:, None] & (~later)[None, :]).astype(np.float32))
        gsum.append((same & later[:, None] & (t[None, :] >= t[:, None])).astype(np.float32))
        gsum.append((same & (~later)[:, None] & (t[None, :] < t[:, None])).astype(np.float32))
    return np.concatenate(mats, 0), np.stack(masks, 0), np.concatenate(gsum, 1)


def _split_dot(mat, v):
    hi = v.astype(MXU_DTYPE)
    lo = (v - hi.astype(F32)).astype(MXU_DTYPE)
    r = _dot(mat, jnp.concatenate([hi, lo], axis=1))
    n = v.shape[1]
    return r[:, :n] + r[:, n:]


def _hgrn_gates(qr, fr, lb, mc):
    C = CHUNK
    q, dq_dqr = _silu_and_grad(qr)
    sf = _sigmoid(fr)
    f = lb + (1.0 - lb) * sf
    k = 1.0 - f
    dall = _split_dot(mc, jnp.log(f))
    b = dall[0:C]
    dl = [dall[C * (l + 1):C * (l + 2)] for l in range(len(LEVEL_HALVES))]
    eq = [jnp.exp(jnp.minimum(d, 0.0)) for d in dl]
    ek = [jnp.exp(jnp.minimum(-d, 0.0)) for d in dl]
    return q, dq_dqr, sf, f, k, b, eq, ek


def _hgrn_scores(q, k, eq, ek, masks_ref):
    p = jnp.where(_rows(CHUNK) == lax.broadcasted_iota(jnp.int32, (1, CHUNK), 1),
                  jnp.sum(q * k, axis=-1, keepdims=True), 0.0)
    for l in range(len(LEVEL_HALVES)):
        p = p + masks_ref[l] * _dot(q * eq[l], k * ek[l], NT)
    return p


def _hgrn_fwd(proj, lb_gamma, norm_w, B, HW, name):
    T = proj.shape[0]
    S, H, C = T // B, HW // HEAD_DIM, CHUNK
    NC = S // C
    mc_np, masks_np, _ = _hgrn_tables()
    mc, masks = jnp.asarray(mc_np, MXU_DTYPE), jnp.asarray(masks_np, F32)

    def kern(q_ref, f_ref, i_ref, g_ref, lbg_ref, nw_ref, mc_ref, masks_ref, oraw_ref, o_ref, st_ref):
        g0, g1 = lbg_ref[pl.ds(0, 1), :], lbg_ref[pl.ds(1, 1), :]
        mx = jnp.maximum(g0, g1)
        e0, e1 = jnp.exp(g0 - mx), jnp.exp(g1 - mx)
        lb = e0 / (e0 + e1)
        nw = nw_ref[...]
        mcv = mc_ref[...]

        def body(n, sts):
            out = []
            for s, st in enumerate(sts):
                rows = pl.ds(pl.multiple_of(s * S + n * C, C), C)
                st_ref[s, n] = st
                q, _, _, _, k, b, eq, ek = _hgrn_gates(q_ref[rows, :], f_ref[rows, :], lb, mcv)
                v = i_ref[rows, :]
                o = _dot(q * jnp.exp(b), st, NT) + _dot(_hgrn_scores(q, k, eq, ek, masks_ref), v)
                b_last = b[C - 1:C]
                out.append(st * jnp.exp(b_last) + _dot(v, k * jnp.exp(b_last - b), TN))
                oraw_ref[rows, :] = o
                r = lax.rsqrt(jnp.mean(o * o, axis=-1, keepdims=True) + EPS)
                gate, _ = _silu_and_grad(g_ref[rows, :])
                o_ref[rows, :] = (o * r * nw * gate).astype(ACT_DTYPE)
            return tuple(out)

        lax.fori_loop(0, NC, body, tuple(jnp.zeros((HEAD_DIM, HEAD_DIM), F32) for _ in range(B)))

    col = lambda off: pl.BlockSpec((T, HEAD_DIM), lambda h: (0, off + h))
    return _pcall(
        kern, name=name, grid=(H,),
        in_specs=[col(0), col(H), col(2 * H), col(3 * H),
                  pl.BlockSpec((2, HEAD_DIM), lambda h: (0, h)),
                  pl.BlockSpec((1, HEAD_DIM), lambda h: (0, h)),
                  pl.BlockSpec(mc.shape, lambda h: (0, 0)),
                  pl.BlockSpec(masks.shape, lambda h: (0, 0, 0))],
        out_specs=[col(0), col(0),
                   pl.BlockSpec((B, None, NC, HEAD_DIM, HEAD_DIM), lambda h: (0, h, 0, 0, 0))],
        out_shape=[jax.ShapeDtypeStruct((T, HW), F32), jax.ShapeDtypeStruct((T, HW), ACT_DTYPE),
                   jax.ShapeDtypeStruct((B, H, NC, HEAD_DIM, HEAD_DIM), F32)],
        compiler_params=_params(("parallel",), 20 * T * HEAD_DIM * 4 + (8 << 20)),
    )(proj, proj, proj, proj, lb_gamma, norm_w, mc, masks)


def _hgrn_bwd(proj, lb_gamma, norm_w, o_raw, states, d_mix, B, HW, name):
    T = proj.shape[0]
    S, H, C = T // B, HW // HEAD_DIM, CHUNK
    NC = S // C
    mc_np, masks_np, gsum_np = _hgrn_tables()
    mc, masks, gsum = jnp.asarray(mc_np, MXU_DTYPE), jnp.asarray(masks_np, F32), jnp.asarray(gsum_np, MXU_DTYPE)
    nl = len(LEVEL_HALVES)

    def kern(q_ref, f_ref, i_ref, g_ref, lbg_ref, nw_ref, mc_ref, masks_ref, gsum_ref, oraw_ref, st_ref, do_ref,
             dq_ref, df_ref, di_ref, dg_ref, dlbg_ref, dnw_ref):
        g0, g1 = lbg_ref[pl.ds(0, 1), :], lbg_ref[pl.ds(1, 1), :]
        mx = jnp.maximum(g0, g1)
        e0, e1 = jnp.exp(g0 - mx), jnp.exp(g1 - mx)
        lb = e0 / (e0 + e1)
        nw = nw_ref[...]
        mcv, gsumv = mc_ref[...], gsum_ref[...]

        def chunk(s, n, dst, dlb, dnw):
            rows = pl.ds(pl.multiple_of(s * S + n * C, C), C)
            qr, fr, v = q_ref[rows, :], f_ref[rows, :], i_ref[rows, :]
            q, dq_dqr, sf, f, k, b, eq, ek = _hgrn_gates(qr, fr, lb, mcv)
            o = oraw_ref[rows, :]
            dout = do_ref[rows, :].astype(F32)
            gate, dgate = _silu_and_grad(g_ref[rows, :])
            r = lax.rsqrt(jnp.mean(o * o, axis=-1, keepdims=True) + EPS)
            dg_ref[rows, :] = (dout * o * r * nw * dgate).astype(ACT_DTYPE)
            don = dout * gate
            dnw = dnw + jnp.sum(don * o * r, axis=0, keepdims=True)
            gw = don * nw
            do = r * gw - o * (r * r * r) * jnp.mean(gw * o, axis=-1, keepdims=True)
            st_prev = st_ref[s, n]
            eb = jnp.exp(b)
            b_last = b[C - 1:C]
            ebl = jnp.exp(b_last - b)
            p = _hgrn_scores(q, k, eq, ek, masks_ref)
            dp = _dot(do, v, NT)
            dpd = jnp.sum(do * v, axis=-1, keepdims=True)
            dq_state = _dot(do, st_prev) * eb
            dk_state = _dot(v, dst) * ebl
            dq = dq_state + dpd * k
            dk = dk_state + dpd * q
            pairs = [q * dq_state, k * dk_state]
            for l in range(nl):
                mdp = masks_ref[l] * dp
                dql = _dot(mdp, k * ek[l]) * eq[l]
                dkl = _dot(mdp, q * eq[l], TN) * ek[l]
                dq, dk = dq + dql, dk + dkl
                pairs += [q * dql, k * dkl]
            dv = _dot(p, do, TN) + _dot(k * ebl, dst, NT)
            through = jnp.exp(b_last) * jnp.sum(dst * st_prev, axis=0, keepdims=True)
            dlg = _split_dot(gsumv, jnp.concatenate(pairs, axis=0)) + through
            dst = dst * jnp.exp(b_last) + _dot(do, q * eb, TN)
            dq_ref[rows, :] = (dq * dq_dqr).astype(ACT_DTYPE)
            dfv = dlg / f - dk
            df_ref[rows, :] = (dfv * (1.0 - lb) * sf * (1.0 - sf)).astype(ACT_DTYPE)
            di_ref[rows, :] = dv.astype(ACT_DTYPE)
            dlb = dlb + jnp.sum(dfv * (1.0 - sf), axis=0, keepdims=True)
            return dst, dlb, dnw

        def body(it, carry):
            dsts, dlb, dnw = carry
            out = []
            for s, dst in enumerate(dsts):
                dst, dlb, dnw = chunk(s, NC - 1 - it, dst, dlb, dnw)
                out.append(dst)
            return tuple(out), dlb, dnw

        zrow = jnp.zeros((1, HEAD_DIM), F32)
        zst = tuple(jnp.zeros((HEAD_DIM, HEAD_DIM), F32) for _ in range(B))
        _, dlb, dnw = lax.fori_loop(0, NC, body, (zst, zrow, zrow))
        dg0 = dlb * lb * (1.0 - lb)
        dlbg_ref[pl.ds(0, 1), :] = dg0
        dlbg_ref[pl.ds(1, 1), :] = -dg0
        dnw_ref[...] = dnw

    col = lambda off: pl.BlockSpec((T, HEAD_DIM), lambda h: (0, off + h))
    full = lambda a: pl.BlockSpec(a.shape, lambda h: (0,) * a.ndim)
    part = jax.ShapeDtypeStruct((T, HW), ACT_DTYPE)
    return _pcall(
        kern, name=name, grid=(H,),
        in_specs=[col(0), col(H), col(2 * H), col(3 * H),
                  pl.BlockSpec((2, HEAD_DIM), lambda h: (0, h)),
                  pl.BlockSpec((1, HEAD_DIM), lambda h: (0, h)),
                  full(mc), full(masks), full(gsum), col(0),
                  pl.BlockSpec((B, None, NC, HEAD_DIM, HEAD_DIM), lambda h: (0, h, 0, 0, 0)),
                  col(0)],
        out_specs=[col(0), col(0), col(0), col(0),
                   pl.BlockSpec((2, HEAD_DIM), lambda h: (0, h)),
                   pl.BlockSpec((1, HEAD_DIM), lambda h: (0, h))],
        out_shape=[part, part, part, part, jax.ShapeDtypeStruct((2, HW), F32), jax.ShapeDtypeStruct((1, HW), F32)],
        compiler_params=_params(("parallel",), 28 * T * HEAD_DIM * 4 + (8 << 20)),
    )(proj, proj, proj, proj, lb_gamma, norm_w, mc, masks, gsum, o_raw, states, d_mix)


def _lru_gates(xr, cw_ref, cb, wa, ba, wx, bx, lam):
    S = xr.shape[0]
    xb = _conv(xr, cw_ref, cb, LRU_CONV)
    r = _sigmoid(_dot(xb, wa) + ba)
    ig = _sigmoid(_dot(xb, wx) + bx)
    sp = jnp.maximum(-lam, 0.0) + jnp.log(1.0 + jnp.exp(-jnp.abs(lam)))
    la = -LRU_C * r * sp
    a = jnp.exp(la)
    mult = jnp.where(_rows(S) == 0, 1.0, jnp.sqrt(_one_minus_exp(2.0 * la)))
    return xb, r, ig, sp, a, mult


def _scan_rows(a_ref, u_ref, h_ref, reverse):
    S, W = a_ref.shape
    nb = S // 8
    row = _rows(8)

    def body(it, carry):
        blk = nb - 1 - it if reverse else it
        rows = pl.ds(pl.multiple_of(blk * 8, 8), 8)
        a, u = a_ref[rows, :], u_ref[rows, :]
        for d in (1, 2, 4):
            sh = 8 - d if reverse else d
            keep = (row < 8 - d) if reverse else (row >= d)
            u = u + jnp.where(keep, a * pltpu.roll(u, sh, 0), 0.0)
            a = jnp.where(keep, a * pltpu.roll(a, sh, 0), a)
        h = u + a * carry
        h_ref[rows, :] = h
        return h[0:1] if reverse else h[7:8]

    lax.fori_loop(0, nb, body, jnp.zeros((1, W), F32))


def _lru_fwd(proj, cw, cb, wa, ba, wx, bx, lam, B, HW, LW, name):
    T = proj.shape[0]
    S, NB = T // B, LW // HEAD_DIM
    xoff, yoff = 4 * HW // HEAD_DIM, 4 * HW // HEAD_DIM + NB

    def kern(x_ref, y_ref, cw_ref, cb_ref, wa_ref, ba_ref, wx_ref, bx_ref, lam_ref, h_ref, z_ref, a_s, u_s):
        xb, _, ig, _, a, mult = _lru_gates(x_ref[...], cw_ref, cb_ref[...], wa_ref[...], ba_ref[...],
                                           wx_ref[...], bx_ref[...], lam_ref[...])
        a_s[...] = a
        u_s[...] = xb * ig * mult
        _scan_rows(a_s, u_s, h_ref, False)
        gy, _ = _gelu_and_grad(y_ref[...])
        z_ref[...] = h_ref[...] * gy

    blk = lambda off: pl.BlockSpec((S, HEAD_DIM), lambda b, n: (b, off + n))
    vec = pl.BlockSpec((1, HEAD_DIM), lambda b, n: (0, n))
    mat = pl.BlockSpec((None, HEAD_DIM, HEAD_DIM), lambda b, n: (n, 0, 0))
    return _pcall(
        kern, name=name, grid=(B, NB),
        in_specs=[blk(xoff), blk(yoff), pl.BlockSpec((LRU_CONV, HEAD_DIM), lambda b, n: (0, n)),
                  vec, mat, vec, mat, vec, vec],
        out_specs=[blk(0), blk(0)],
        out_shape=[jax.ShapeDtypeStruct((T, LW), F32), jax.ShapeDtypeStruct((T, LW), F32)],
        scratch_shapes=[pltpu.VMEM((S, HEAD_DIM), F32), pltpu.VMEM((S, HEAD_DIM), F32)],
        compiler_params=_params(("parallel", "parallel"), 24 * S * HEAD_DIM * 4),
    )(proj, proj, cw, cb, wa, ba, wx, bx, lam)


def _lru_bwd(proj, cw, cb, wa, ba, wx, bx, lam, h, dz, B, HW, LW, name):
    T = proj.shape[0]
    S, NB = T // B, LW // HEAD_DIM
    xoff, yoff = 4 * HW // HEAD_DIM, 4 * HW // HEAD_DIM + NB

    def kern(x_ref, y_ref, cw_ref, cb_ref, wa_ref, ba_ref, wx_ref, bx_ref, lam_ref, h_ref, dz_ref,
             dx_ref, dy_ref, dwa_ref, dwx_ref, dba_ref, dbx_ref, dlam_ref, dcw_ref, dcb_ref, a_s, u_s, dh_s):
        bi = pl.program_id(1)
        row = _rows(S)
        xr, lam = x_ref[...], lam_ref[...]
        wa, wx = wa_ref[...], wx_ref[...]
        xb, r, ig, sp, a, mult = _lru_gates(xr, cw_ref, cb_ref[...], wa, ba_ref[...], wx, bx_ref[...], lam)
        hv, dz = h_ref[...], dz_ref[...]
        gy, dgy = _gelu_and_grad(y_ref[...])
        dy_ref[...] = (dz * hv * dgy).astype(ACT_DTYPE)
        a_s[...] = jnp.where(row < S - 1, pltpu.roll(a, S - 1, 0), 0.0)
        u_s[...] = dz * gy
        _scan_rows(a_s, u_s, dh_s, True)
        dh = dh_s[...]
        h_prev = jnp.where(row >= 1, pltpu.roll(hv, 1, 0), 0.0)
        d_ig = dh * xb * mult
        d_mult = jnp.where(row == 0, 0.0, dh * xb * ig)
        dxb = dh * ig * mult
        dla = dh * h_prev * a - d_mult * (a * a) / mult
        dpre_r = dla * (-LRU_C * sp) * r * (1.0 - r)
        dpre_i = d_ig * ig * (1.0 - ig)
        dxb = dxb + _dot(dpre_r, wa, NT) + _dot(dpre_i, wx, NT)
        dxr, dcw = _conv_bwd(xr, dxb, cw_ref, LRU_CONV)
        dx_ref[...] = dxr.astype(ACT_DTYPE)

        @pl.when(bi == 0)
        def _():
            for ref in (dwa_ref, dwx_ref, dba_ref, dbx_ref, dlam_ref, dcw_ref, dcb_ref):
                ref[...] = jnp.zeros_like(ref)

        dwa_ref[...] += _dot(xb, dpre_r, TN)
        dwx_ref[...] += _dot(xb, dpre_i, TN)
        dba_ref[...] += jnp.sum(dpre_r, axis=0, keepdims=True)
        dbx_ref[...] += jnp.sum(dpre_i, axis=0, keepdims=True)
        dsp = jnp.sum(dla * (-LRU_C) * r, axis=0, keepdims=True)
        dlam_ref[...] += -dsp * _sigmoid(-lam)
        for j, rj in enumerate(dcw):
            dcw_ref[pl.ds(j, 1), :] += rj
        dcb_ref[...] += jnp.sum(dxb, axis=0, keepdims=True)

    blk = lambda off: pl.BlockSpec((S, HEAD_DIM), lambda n, b: (b, off + n))
    vec = pl.BlockSpec((1, HEAD_DIM), lambda n, b: (0, n))
    mat = pl.BlockSpec((None, HEAD_DIM, HEAD_DIM), lambda n, b: (n, 0, 0))
    cwb = pl.BlockSpec((LRU_CONV, HEAD_DIM), lambda n, b: (0, n))
    part = jax.ShapeDtypeStruct((T, LW), ACT_DTYPE)
    vshape = jax.ShapeDtypeStruct((1, LW), F32)
    mshape = jax.ShapeDtypeStruct((NB, HEAD_DIM, HEAD_DIM), F32)
    return _pcall(
        kern, name=name, grid=(NB, B),
        in_specs=[blk(xoff), blk(yoff), cwb, vec, mat, vec, mat, vec, vec, blk(0), blk(0)],
        out_specs=[blk(0), blk(0), mat, mat, vec, vec, vec, cwb, vec],
        out_shape=[part, part, mshape, mshape, vshape, vshape, vshape,
                   jax.ShapeDtypeStruct((LRU_CONV, LW), F32), vshape],
        scratch_shapes=[pltpu.VMEM((S, HEAD_DIM), F32)] * 3,
        compiler_params=_params(("parallel", "arbitrary"), 40 * S * HEAD_DIM * 4),
    )(proj, proj, cw, cb, wa, ba, wx, bx, lam, h, dz)


def _pos():
    return lax.axis_index("x"), lax.axis_index("y"), lax.axis_index("c")


def _other_chips(x, y):
    return [(1 - x, y), (x, 1 - y), (1 - x, 1 - y)]


def _remote(src, dst, send_sems, recv_sems, k, to):
    return pltpu.make_async_remote_copy(src_ref=src, dst_ref=dst, send_sem=send_sems.at[k],
                                        recv_sem=recv_sems.at[k], device_id=to, device_id_type=MESH)


HBM_BLK = pl.BlockSpec(memory_space=pltpu.HBM)
SEM_BLK = pl.BlockSpec(memory_space=pltpu.SEMAPHORE)
VMEM_BLK = pl.BlockSpec(memory_space=pltpu.VMEM)
DATAFLOW = pltpu.SideEffectType.DATAFLOW_SIDE_EFFECTING
TOKEN = jax.ShapeDtypeStruct((8, V7X_LANES), F32)


def _in_hbm(a):
    return pltpu.with_memory_space_constraint(a, pltpu.HBM)


def _gather_win(o_ref, R, C, col_sharded):
    Rh = R // 2

    def win(j, h=None):
        if col_sharded:
            rows = pl.ds(0, R) if h is None else pl.ds(h * Rh, Rh)
            return o_ref.at[rows, pl.ds(j * C, C)]
        return o_ref.at[pl.ds(j * R, R) if h is None else pl.ds(j * R + h * Rh, Rh), :]

    return win


def _cast_into_window(w, col_sharded, after, name):
    R, C = w.shape
    tr = _tile(R, 256, 16)
    nr = R // tr
    full = (R, 4 * C) if col_sharded else (4 * R, C)
    j = (2 * lax.axis_index("x") + lax.axis_index("y")).astype(jnp.int32).reshape(1)

    def kern(j_ref, w_ref, after_ref, o_ref):
        o_ref[...] = w_ref[...].astype(ACT_DTYPE)

    out_map = (lambda i, jr: (i, jr[0])) if col_sharded else (lambda i, jr: (jr[0] * nr + i, 0))
    grid_spec = pltpu.PrefetchScalarGridSpec(
        num_scalar_prefetch=1, grid=(nr,),
        in_specs=[pl.BlockSpec((tr, C), lambda i, jr: (i, 0)), HBM_SPEC], out_specs=pl.BlockSpec((tr, C), out_map))
    return _pcall(kern, name=name, grid_spec=grid_spec, out_shape=jax.ShapeDtypeStruct(full, ACT_DTYPE),
                  compiler_params=_params(("parallel",), 6 * tr * C * 4))(j, w, after)


def _gather_start(land, shard_shape, col_sharded, token, name):
    R, C = shard_shape

    def body(land_ref, tok_ref, send_sems, recv_sems, land_thru, tok_out):
        x, y, c = _pos()
        w = _gather_win(land_ref, R, C, col_sharded)(2 * x + y, c)
        for k, (cx, cy) in enumerate(_other_chips(x, y)):
            _remote(w, w, send_sems, recv_sems, k, (cx, cy, c)).start()
        tok_out[...] = tok_ref[...]

    return _pcall(
        body, name=name,
        out_shape=(pltpu.SemaphoreType.DMA((3,)), pltpu.SemaphoreType.DMA((3,)),
                   pltpu.HBM(land.shape, land.dtype), TOKEN),
        in_specs=(HBM_BLK, VMEM_BLK), out_specs=(SEM_BLK, SEM_BLK, HBM_BLK, VMEM_BLK),
        input_output_aliases={0: 2},
        compiler_params=pltpu.CompilerParams(has_side_effects=DATAFLOW),
    )(_in_hbm(land), token)


def _gather_wait(started, shard_shape, after, col_sharded, name):
    send_sems, recv_sems, land_thru, _ = started
    R, C = shard_shape

    def body(land_ref, send_sems, recv_sems, after_ref, got_ref):
        x, y, c = _pos()
        win = _gather_win(land_ref, R, C, col_sharded)
        for k, (cx, cy) in enumerate(_other_chips(x, y)):
            cp = _remote(win(2 * x + y, c), win(2 * cx + cy, c), send_sems, recv_sems, k, (cx, cy, c))
            cp.wait_send()
            cp.wait_recv()

    return _pcall(
        body, name=name, out_shape=pltpu.HBM(land_thru.shape, land_thru.dtype),
        in_specs=(HBM_BLK, SEM_BLK, SEM_BLK, HBM_SPEC), out_specs=HBM_BLK, input_output_aliases={0: 0},
        compiler_params=pltpu.CompilerParams(has_side_effects=DATAFLOW),
    )(land_thru, send_sems, recv_sems, after)


def _gather_pass_on(landed, shard_shape, col_sharded, name):
    R, C = shard_shape

    def body(in_ref, o_ref, send_sems, recv_sems):
        x, y, c = _pos()
        src, dst = _gather_win(in_ref, R, C, col_sharded), _gather_win(o_ref, R, C, col_sharded)
        chips = _other_chips(x, y)
        passed = [_remote(src(2 * cx + cy, c), dst(2 * cx + cy, c), send_sems, recv_sems, k, (x, y, 1 - c))
                  for k, (cx, cy) in enumerate(chips)]
        for cp in passed:
            cp.start()
        for k, (cx, cy) in enumerate(chips):
            w = dst(2 * cx + cy, 1 - c)
            _remote(w, w, send_sems, recv_sems, k, (x, y, c)).wait_recv()
        for cp in passed:
            cp.wait_send()

    return _pcall(body, name=name, in_specs=[HBM_SPEC], out_specs=HBM_SPEC,
                  out_shape=jax.ShapeDtypeStruct(landed.shape, landed.dtype), input_output_aliases={0: 0},
                  scratch_shapes=[pltpu.SemaphoreType.DMA((3,)), pltpu.SemaphoreType.DMA((3,))])(landed)


def _pair_start(g, token, name):
    def body(g_ref, land_ref, tok_ref, send_sems, recv_sems, g_thru, land_thru, tok_out):
        x, y, c = _pos()
        _remote(g_ref, land_ref, send_sems, recv_sems, 0, (x, y, 1 - c)).start()
        tok_out[...] = tok_ref[...]

    return _pcall(
        body, name=name,
        out_shape=(pltpu.SemaphoreType.DMA((1,)), pltpu.SemaphoreType.DMA((1,)),
                   pltpu.HBM(g.shape, g.dtype), pltpu.HBM(g.shape, g.dtype), TOKEN),
        in_specs=(HBM_BLK, HBM_BLK, VMEM_BLK), out_specs=(SEM_BLK, SEM_BLK, HBM_BLK, HBM_BLK, VMEM_BLK),
        input_output_aliases={0: 2, 1: 3},
        compiler_params=pltpu.CompilerParams(has_side_effects=DATAFLOW),
    )(_in_hbm(g), _in_hbm(lax.empty(g.shape, g.dtype)), token)


def _pair_wait(started, after, name):
    send_sems, recv_sems, g_thru, land_thru, _ = started

    def body(g_ref, land_ref, send_sems, recv_sems, after_ref, g_dead, got_ref):
        x, y, c = _pos()
        cp = _remote(g_ref, land_ref, send_sems, recv_sems, 0, (x, y, 1 - c))
        cp.wait_send()
        cp.wait_recv()

    return _pcall(
        body, name=name,
        out_shape=(pltpu.HBM(g_thru.shape, g_thru.dtype), pltpu.HBM(land_thru.shape, land_thru.dtype)),
        in_specs=(HBM_BLK, HBM_BLK, SEM_BLK, SEM_BLK, HBM_SPEC), out_specs=(HBM_BLK, HBM_BLK),
        input_output_aliases={0: 0, 1: 1},
        compiler_params=pltpu.CompilerParams(has_side_effects=DATAFLOW),
    )(g_thru, land_thru, send_sems, recv_sems, after)[1]


def _own_piece_into_slots(cs, col_sharded, name):
    J, Rp, W = cs.shape
    Cp = W // 4 if col_sharded else W
    tr = _tile(Rp, 256, 16)
    tw = _tile(Cp, 8192, V7X_LANES)
    nw = Cp // tw
    x, y, c = _pos()
    chip = (2 * x + y).astype(jnp.int32).reshape(1)
    core = c.astype(jnp.int32).reshape(1)

    def kern(j_ref, c_ref, s_ref, o_ref):
        o_ref[...] = s_ref[...]

    in_map = ((lambda i, w, j, cc: (0, i, j[0] * nw + w)) if col_sharded
              else (lambda i, w, j, cc: (j[0], i, w)))
    grid_spec = pltpu.PrefetchScalarGridSpec(
        num_scalar_prefetch=2, grid=(Rp // tr, nw),
        in_specs=[pl.BlockSpec((None, tr, tw), in_map)],
        out_specs=pl.BlockSpec((None, None, tr, tw), lambda i, w, j, cc: (j[0], cc[0], i, w)))
    return _pcall(kern, name=name, grid_spec=grid_spec,
                  out_shape=jax.ShapeDtypeStruct((4, 2, Rp, Cp), cs.dtype),
                  compiler_params=_params(("parallel", "parallel"), 8 * tr * tw * 4))(chip, core, cs)


def _chip_sum_piece(cs_ref, C, col_sharded):
    return lambda j: cs_ref.at[0, :, pl.ds(j * C, C)] if col_sharded else cs_ref.at[j]


def _scatter_start(cs, slots, col_sharded, token, name):
    C = slots.shape[3]

    def body(cs_ref, land_ref, tok_ref, send_sems, recv_sems, cs_thru, land_thru, tok_out):
        x, y, c = _pos()
        piece = _chip_sum_piece(cs_ref, C, col_sharded)
        for k, (cx, cy) in enumerate(_other_chips(x, y)):
            _remote(piece(2 * cx + cy), land_ref.at[2 * x + y, c], send_sems, recv_sems, k, (cx, cy, c)).start()
        tok_out[...] = tok_ref[...]

    return _pcall(
        body, name=name,
        out_shape=(pltpu.SemaphoreType.DMA((3,)), pltpu.SemaphoreType.DMA((3,)),
                   pltpu.HBM(cs.shape, cs.dtype), pltpu.HBM(slots.shape, cs.dtype), TOKEN),
        in_specs=(HBM_BLK, HBM_BLK, VMEM_BLK), out_specs=(SEM_BLK, SEM_BLK, HBM_BLK, HBM_BLK, VMEM_BLK),
        input_output_aliases={0: 2, 1: 3},
        compiler_params=pltpu.CompilerParams(has_side_effects=DATAFLOW),
    )(_in_hbm(cs), _in_hbm(slots), token)


def _scatter_wait(started, after, col_sharded, name):
    send_sems, recv_sems, cs_thru, land_thru, _ = started
    C = land_thru.shape[3]

    def body(cs_ref, land_ref, send_sems, recv_sems, after_ref, cs_dead, got_ref):
        x, y, c = _pos()
        piece = _chip_sum_piece(cs_ref, C, col_sharded)
        for k, (cx, cy) in enumerate(_other_chips(x, y)):
            cp = _remote(piece(2 * cx + cy), land_ref.at[2 * cx + cy, c], send_sems, recv_sems, k, (cx, cy, c))
            cp.wait_send()
            cp.wait_recv()

    return _pcall(
        body, name=name,
        out_shape=(pltpu.HBM(cs_thru.shape, cs_thru.dtype), pltpu.HBM(land_thru.shape, land_thru.dtype)),
        in_specs=(HBM_BLK, HBM_BLK, SEM_BLK, SEM_BLK, HBM_SPEC), out_specs=(HBM_BLK, HBM_BLK),
        input_output_aliases={0: 0, 1: 1},
        compiler_params=pltpu.CompilerParams(has_side_effects=DATAFLOW),
    )(cs_thru, land_thru, send_sems, recv_sems, after)


def _scatter_pass_on(landed, name):
    def body(in_ref, o_ref, send_sems, recv_sems):
        x, y, c = _pos()
        sends = [_remote(in_ref.at[i, c], o_ref.at[i, c], send_sems, recv_sems, i, (x, y, 1 - c)) for i in range(4)]
        for cp in sends:
            cp.start()
        for i in range(4):
            w = o_ref.at[i, 1 - c]
            _remote(w, w, send_sems, recv_sems, i, (x, y, c)).wait_recv()
        for cp in sends:
            cp.wait_send()

    return _pcall(body, name=name, in_specs=[HBM_SPEC], out_specs=HBM_SPEC,
                  out_shape=jax.ShapeDtypeStruct(landed.shape, landed.dtype), input_output_aliases={0: 0},
                  scratch_shapes=[pltpu.SemaphoreType.DMA((4,)), pltpu.SemaphoreType.DMA((4,))])(landed)


def _gather_small(buf, name):
    rows = buf.shape[0]

    def body(b_ref, o_ref, send_sems, recv_sems):
        x, y, c = _pos()
        jme = 2 * x + y
        chips = _other_chips(x, y)
        o_ref[jme] = b_ref[...]
        sends = [_remote(b_ref, o_ref.at[jme], send_sems, recv_sems, k, (cx, cy, c))
                 for k, (cx, cy) in enumerate(chips)]
        for cp in sends:
            cp.start()
        for k, (cx, cy) in enumerate(chips):
            w = o_ref.at[2 * cx + cy]
            _remote(w, w, send_sems, recv_sems, k, (x, y, c)).wait_recv()
        for cp in sends:
            cp.wait_send()

    vm = pl.BlockSpec(memory_space=pltpu.VMEM)
    return _pcall(body, name=name, in_specs=[vm], out_specs=vm,
                  out_shape=jax.ShapeDtypeStruct((4, rows, V7X_LANES), buf.dtype),
                  scratch_shapes=[pltpu.SemaphoreType.DMA((3,)), pltpu.SemaphoreType.DMA((3,))],
                  compiler_params=_params(None, 16 * rows * V7X_LANES * 4))(buf)


def _allreduce_small(buf, name):
    rows = buf.shape[0]
    rh = rows // 2

    def body(b_ref, o_ref, pair, mine, slots, send_sems, recv_sems):
        x, y, c = _pos()
        me, sib, jme = (x, y, c), (x, y, 1 - c), 2 * x + y
        half = pl.ds(pl.multiple_of(c * rh, 8), rh)
        other = pl.ds(pl.multiple_of((1 - c) * rh, 8), rh)
        to_sib = _remote(b_ref, pair.at[c], send_sems, recv_sems, 0, sib)
        to_sib.start()
        pair[c] = b_ref[...]
        _remote(pair.at[1 - c], pair.at[1 - c], send_sems, recv_sems, 0, me).wait_recv()
        mine[...] = pair[0, half, :] + pair[1, half, :]
        chips = _other_chips(x, y)
        sends = [_remote(mine, slots.at[jme], send_sems, recv_sems, 1 + k, (cx, cy, c))
                 for k, (cx, cy) in enumerate(chips)]
        for cp in sends:
            cp.start()
        slots[jme] = mine[...]
        for k, (cx, cy) in enumerate(chips):
            w = slots.at[2 * cx + cy]
            _remote(w, w, send_sems, recv_sems, 1 + k, me).wait_recv()
        o_ref[half, :] = (slots[0] + slots[1]) + (slots[2] + slots[3])
        back = _remote(o_ref.at[half, :], o_ref.at[half, :], send_sems, recv_sems, 4, sib)
        back.start()
        _remote(o_ref.at[other, :], o_ref.at[other, :], send_sems, recv_sems, 4, me).wait_recv()
        for cp in [to_sib, back] + sends:
            cp.wait_send()

    vm = pl.BlockSpec(memory_space=pltpu.VMEM)
    return _pcall(body, name=name, in_specs=[vm], out_specs=vm,
                  out_shape=jax.ShapeDtypeStruct(buf.shape, buf.dtype),
                  scratch_shapes=[pltpu.VMEM((2, rows, V7X_LANES), buf.dtype),
                                  pltpu.VMEM((rh, V7X_LANES), buf.dtype),
                                  pltpu.VMEM((4, rh, V7X_LANES), buf.dtype),
                                  pltpu.SemaphoreType.DMA((5,)), pltpu.SemaphoreType.DMA((5,))],
                  compiler_params=_params(None, 10 * rows * V7X_LANES * 4))(buf)


def _adamw_math(w, g, m, v):
    m = ADAM_B1 * m + (1.0 - ADAM_B1) * g
    v = ADAM_B2 * v + (1.0 - ADAM_B2) * (g * g)
    m_hat = m / (1.0 - ADAM_B1 ** ADAM_STEP)
    v_hat = v / (1.0 - ADAM_B2 ** ADAM_STEP)
    delta = -ADAM_LR * (m_hat / (jnp.sqrt(v_hat) + ADAM_EPS) + ADAM_WD * w)
    return delta, m, v


def _adamw_big(w, m, v, slots, name):
    R, C = w.shape
    by_rows = slots.shape[3] == C
    tr = _tile(R, 32, 16)

    def kern(w_ref, m_ref, v_ref, s_ref, g_ref, d_ref, mo_ref, vo_ref):
        def chip_sum(h):
            g = s_ref[0, h].astype(F32)
            for i in range(1, 4):
                g = g + s_ref[i, h].astype(F32)
            return g

        g = chip_sum(0) if by_rows else jnp.concatenate([chip_sum(0), chip_sum(1)], axis=1)
        d, mn, vn = _adamw_math(w_ref[...], g, m_ref[...], v_ref[...])
        g_ref[...], d_ref[...], mo_ref[...], vo_ref[...] = g, d, mn, vn

    row = pl.BlockSpec((tr, C), lambda i: (i, 0))
    shp = jax.ShapeDtypeStruct((R, C), F32)
    if by_rows:
        per_half = R // 2 // tr
        s_spec = pl.BlockSpec((4, 1, tr, C), lambda i: (0, i // per_half, i % per_half, 0))
    else:
        s_spec = pl.BlockSpec((4, 2, tr, C // 2), lambda i: (0, 0, i, 0))
    return _pcall(kern, name=name, grid=(R // tr,),
                  in_specs=[row, row, row, s_spec], out_specs=[row] * 4, out_shape=[shp] * 4,
                  compiler_params=_params(("parallel",), 36 * tr * C * 4))(w, m, v, slots)


def _adamw_small(w, g, m, v, name):
    def kern(w_ref, g_ref, m_ref, v_ref, d_ref, mo_ref, vo_ref):
        d_ref[...], mo_ref[...], vo_ref[...] = _adamw_math(w_ref[...], g_ref[...], m_ref[...], v_ref[...])

    vm = pl.BlockSpec(memory_space=pltpu.VMEM)
    shp = jax.ShapeDtypeStruct(w.shape, F32)
    return _pcall(kern, name=name, in_specs=[vm] * 4, out_specs=[vm] * 3, out_shape=[shp] * 3,
                  compiler_params=_params(None, 10 * w.size * 4))(w, g, m, v)


def _pack(arrs):
    flat = jnp.concatenate([a.reshape(-1).astype(F32) for a in arrs])
    n = flat.shape[0]
    rows = -(-n // (16 * V7X_LANES)) * 16
    return jnp.pad(flat, (0, rows * V7X_LANES - n)).reshape(rows, V7X_LANES)


def _unpack(buf, shapes):
    flat = buf.reshape(-1)
    out, off = [], 0
    for s in shapes:
        n = int(np.prod(s))
        out.append(flat[off:off + n].reshape(s))
        off += n
    return out


def _cut_gradient(a, d, col_sharded, tm, tn, token, tag, other_work):
    c = lax.axis_index("c").astype(jnp.int32)
    for_sibling = _matmul_tn_half(a, d, (1 - c).reshape(1), None, tm, tn, col_sharded, "mm_g_%s_sibling" % tag)
    sent = _pair_start(for_sibling, token, "pair_start_" + tag)
    other, last = other_work(sent[4])
    arrived = _pair_wait(sent, last, "pair_wait_" + tag)
    cs = _matmul_tn_half(a, d, c.reshape(1), arrived, tm, tn, col_sharded, "mm_g_%s_own" % tag)
    cs = cs.reshape((1,) + cs.shape if col_sharded else (4, cs.shape[0] // 4, cs.shape[1]))
    slots = _own_piece_into_slots(cs, col_sharded, "own_piece_" + tag)
    return _scatter_start(cs, slots, col_sharded, sent[4], "scatter_start_" + tag), other


def _reduce_finish(started, after, col_sharded, w, m, v, tag):
    _, landed = _scatter_wait(started, after, col_sharded, "scatter_wait_" + tag)
    slots = _scatter_pass_on(landed, "scatter_pass_on_" + tag)
    return _adamw_big(w, m, v, slots, "adamw_" + tag)


def kernel(x, ln1_w, w_in, lb_gamma, hg_norm_w, lru_conv_w, lru_conv_b, lru_wa, lru_ba, lru_wx, lru_bx, lru_lambda, lru_norm_w, w_out, ln2_w, ffn_w_up, ffn_conv_w, ffn_conv_b, ffn_w_down, final_norm_w, loss_target, m_ln1_w, m_w_in, m_lb_gamma, m_hg_norm_w, m_lru_conv_w, m_lru_conv_b, m_lru_wa, m_lru_ba, m_lru_wx, m_lru_bx, m_lru_lambda, m_lru_norm_w, m_w_out, m_ln2_w, m_ffn_w_up, m_ffn_conv_w, m_ffn_conv_b, m_ffn_w_down, m_final_norm_w, v_ln1_w, v_w_in, v_lb_gamma, v_hg_norm_w, v_lru_conv_w, v_lru_conv_b, v_lru_wa, v_lru_ba, v_lru_wx, v_lru_bx, v_lru_lambda, v_lru_norm_w, v_w_out, v_ln2_w, v_ffn_w_up, v_ffn_conv_w, v_ffn_conv_b, v_ffn_w_down, v_final_norm_w):
    B, S, D = x.shape
    T = B * S
    HW = lb_gamma.shape[1]
    LW = lru_conv_b.shape[1]
    assert S % CHUNK == 0 and HW % HEAD_DIM == 0 and lru_wa.shape[2] == HEAD_DIM
    x2 = x.reshape(T, D)
    tgt = loss_target.reshape(T, D)
    jchip = 2 * lax.axis_index("x") + lax.axis_index("y")

    conv_shapes = [lru_conv_w[0].shape, ffn_conv_w[0].shape]
    convs = _gather_small(_pack([lru_conv_w[0], ffn_conv_w[0]]), "gather_conv_w")
    per_chip = [_unpack(convs[j], conv_shapes) for j in range(4)]
    lcw = jnp.concatenate([pc[0] for pc in per_chip], axis=1)
    fcw = jnp.concatenate([pc[1] for pc in per_chip], axis=1)
    masters = dict(w_in=w_in[0], w_out=w_out[0], w_up=ffn_w_up[0], w_down=ffn_w_down[0])
    col_of = dict(w_in=True, w_out=False, w_up=True, w_down=False)
    started, token, after = {}, jnp.zeros(TOKEN.shape, F32), convs
    for n in ("w_in", "w_out", "w_up", "w_down"):
        land = _cast_into_window(masters[n], col_of[n], after, "cast_" + n)
        started[n] = _gather_start(land, masters[n].shape, col_of[n], token, "gather_start_" + n)
        token = after = started[n][3]

    def gathered(n, after):
        landed = _gather_wait(started[n], masters[n].shape, after, col_of[n], "gather_wait_" + n)
        return _gather_pass_on(landed, masters[n].shape, col_of[n], "gather_pass_on_" + n)

    W_in = gathered("w_in", token)

    hn1 = _rms_fwd(x2, ln1_w, "rms1")
    proj = _matmul(hn1, W_in, "NN", F32, 1024, 512, 4096, name="mm_proj")
    o_raw, o_hg, states = _hgrn_fwd(proj, lb_gamma, hg_norm_w, B, HW, "hgrn_fwd")
    h_lru, z = _lru_fwd(proj, lcw, lru_conv_b, lru_wa[0], lru_ba, lru_wx[0], lru_bx, lru_lambda, B, HW, LW, "lru_fwd")
    o_lru = _rms_fwd(z, lru_norm_w, "rms_lru")
    mix = jnp.concatenate([o_hg, o_lru], axis=1)
    W_out = gathered("w_out", mix)
    h1 = _matmul(mix, W_out, "NN", F32, 1024, 512, 4096, add=x2, name="mm_out")
    hn2 = _rms_fwd(h1, ln2_w, "rms2")
    W_up = gathered("w_up", hn2)
    up = _matmul(hn2, W_up, "NN", F32, 1024, 512, 4096, name="mm_up")
    act, act_dg, act_dv = _ffn_act(up, fcw, ffn_conv_b, B, "ffn_act")
    W_down = gathered("w_down", act)
    h2 = _matmul(act, W_down, "NN", F32, 1024, 512, 5504, add=h1, name="mm_down")

    dh2, dh2a, d_final_w, loss_part = _loss_bwd(h2, tgt, final_norm_w.reshape(1, D), "loss_bwd")
    def through_w_down(tok):
        d = _matmul(dh2a, W_down, "NT", ACT_DTYPE, 512, 5504, 512, after=tok, name="mm_d_act")
        return d, d

    red_down, d_act = _cut_gradient(act, dh2a, False, 256, 1024, token, "w_down", through_w_down)
    d_up, d_fcw, d_fcb = _ffn_act_bwd(up, fcw, act_dg, act_dv, d_act, B, "ffn_act_bwd")
    def through_w_up(tok):
        d = _matmul(d_up, W_up, "NT", F32, 2048, 1024, 512, after=tok, name="mm_d_hn2")
        return d, d

    red_up, d_hn2 = _cut_gradient(hn2, d_up, True, 1024, 512, red_down[4], "w_up", through_w_up)
    dh1, dh1a, d_ln2 = _rms_bwd(h1, ln2_w, d_hn2, 0, dh2, True, "rms2_bwd", after=red_up[4])

    def through_w_out(tok):
        d = _matmul(dh1a, W_out, "NT", F32, 1024, 512, 4096, after=tok, name="mm_d_mix")
        return d, d

    red_out, d_mix = _cut_gradient(mix, dh1a, False, 1024, 512, red_up[4], "w_out", through_w_out)
    dz, d_lru_norm = _rms_bwd(z, lru_norm_w, d_mix, HW // LW, None, False, "rms_lru_bwd")
    (d_xr, d_yr, d_wa, d_wx, d_ba, d_bx, d_lam, d_lcw, d_lcb) = _lru_bwd(
        proj, lcw, lru_conv_b, lru_wa[0], lru_ba, lru_wx[0], lru_bx, lru_lambda, h_lru, dz, B, HW, LW, "lru_bwd")
    d_q, d_f, d_i, d_g, d_lbg, d_hgw = _hgrn_bwd(proj, lb_gamma, hg_norm_w, o_raw, states, d_mix, B, HW, "hgrn_bwd")
    d_proj = jnp.concatenate([d_q, d_f, d_i, d_g, d_xr, d_yr], axis=1)
    small_names = ["ln1_w", "lb_gamma", "hg_norm_w", "lru_conv_w", "lru_conv_b", "lru_wa", "lru_ba", "lru_wx",
                   "lru_bx", "lru_lambda", "lru_norm_w", "ln2_w", "ffn_conv_w", "ffn_conv_b", "final_norm_w"]
    small_rest = [d_lbg, d_hgw, d_lcw, d_lcb, d_wa, d_ba, d_wx, d_bx, d_lam, d_lru_norm, d_ln2, d_fcw, d_fcb, d_final_w]

    def through_w_in(tok):
        d_hn1 = _matmul(d_proj, W_in, "NT", F32, 2048, 1024, 1024, after=tok, name="mm_d_hn1")
        dx, d_ln1 = _rms_bwd(x2, ln1_w, d_hn1, 0, dh1, False, "rms1_bwd")
        red = _allreduce_small(_pack([loss_part[0:1, 0:1], d_ln1] + small_rest), "allreduce_small")
        return (dx, d_ln1, red), red

    red_in, (dx, d_ln1, red) = _cut_gradient(hn1, d_proj, True, 1024, 512, red_out[4], "w_in", through_w_in)
    small_grads = [d_ln1] + small_rest

    big = {}
    big["ffn_w_down"] = _reduce_finish(red_down, red_in[4], False, ffn_w_down[0], m_ffn_w_down[0], v_ffn_w_down[0], "w_down")
    big["ffn_w_up"] = _reduce_finish(red_up, big["ffn_w_down"][1], True, ffn_w_up[0], m_ffn_w_up[0], v_ffn_w_up[0], "w_up")
    big["w_out"] = _reduce_finish(red_out, big["ffn_w_up"][1], False, w_out[0], m_w_out[0], v_w_out[0], "w_out")
    big["w_in"] = _reduce_finish(red_in, big["w_out"][1], True, w_in[0], m_w_in[0], v_w_in[0], "w_in")

    red = _unpack(red, [(1, 1)] + [g.shape for g in small_grads])
    loss = red[0].reshape(())
    gs = dict(zip(small_names, red[1:]))
    nlc, nfc = lru_conv_w.shape[2], ffn_conv_w.shape[2]
    gs["lru_conv_w"] = lax.dynamic_slice_in_dim(gs["lru_conv_w"], jchip * nlc, nlc, axis=1)
    gs["ffn_conv_w"] = lax.dynamic_slice_in_dim(gs["ffn_conv_w"], jchip * nfc, nfc, axis=1)
    args = dict(ln1_w=(ln1_w, m_ln1_w, v_ln1_w), lb_gamma=(lb_gamma, m_lb_gamma, v_lb_gamma),
                hg_norm_w=(hg_norm_w, m_hg_norm_w, v_hg_norm_w), lru_conv_w=(lru_conv_w, m_lru_conv_w, v_lru_conv_w),
                lru_conv_b=(lru_conv_b, m_lru_conv_b, v_lru_conv_b), lru_wa=(lru_wa, m_lru_wa, v_lru_wa),
                lru_ba=(lru_ba, m_lru_ba, v_lru_ba), lru_wx=(lru_wx, m_lru_wx, v_lru_wx),
                lru_bx=(lru_bx, m_lru_bx, v_lru_bx), lru_lambda=(lru_lambda, m_lru_lambda, v_lru_lambda),
                lru_norm_w=(lru_norm_w, m_lru_norm_w, v_lru_norm_w), ln2_w=(ln2_w, m_ln2_w, v_ln2_w),
                ffn_conv_w=(ffn_conv_w, m_ffn_conv_w, v_ffn_conv_w), ffn_conv_b=(ffn_conv_b, m_ffn_conv_b, v_ffn_conv_b),
                final_norm_w=(final_norm_w, m_final_norm_w, v_final_norm_w))
    shapes = [args[n][0].shape for n in small_names]
    upd = _adamw_small(_pack([args[n][0] for n in small_names]), _pack([gs[n] for n in small_names]),
                       _pack([args[n][1] for n in small_names]), _pack([args[n][2] for n in small_names]), "adamw_small")
    s_delta, s_m, s_v = (dict(zip(small_names, _unpack(u, shapes))) for u in upd)

    order = ["ln1_w", "w_in", "lb_gamma", "hg_norm_w", "lru_conv_w", "lru_conv_b", "lru_wa", "lru_ba", "lru_wx",
             "lru_bx", "lru_lambda", "lru_norm_w", "w_out", "ln2_w", "ffn_w_up", "ffn_conv_w", "ffn_conv_b",
             "ffn_w_down", "final_norm_w"]
    full_shape = dict(w_in=w_in.shape, w_out=w_out.shape, ffn_w_up=ffn_w_up.shape, ffn_w_down=ffn_w_down.shape)
    grads, deltas, new_m, new_v = [], [], [], []
    for n in order:
        if n in big:
            g, d, mn, vn = (t.reshape(full_shape[n]) for t in big[n])
        else:
            g, d, mn, vn = gs[n].reshape(args[n][0].shape), s_delta[n], s_m[n], s_v[n]
        grads.append(g), deltas.append(d), new_m.append(mn), new_v.append(vn)
    return (loss, dx.reshape(B, S, D), *grads, *deltas, *new_m, *new_v)
```

```python
import functools
import math

import numpy as np
import jax
import jax.numpy as jnp
from jax import lax
from jax.experimental import pallas as pl
from jax.experimental.pallas import tpu as pltpu

F32 = jnp.float32
MXU_DTYPE = jnp.bfloat16
ACT_DTYPE = jnp.bfloat16

EPS = 1e-6
HEAD_DIM = 128
CHUNK = 64
LEVEL_HALVES = (32, 16, 8, 4, 2, 1)
LRU_CONV = 4
FFN_CONV = 3
LRU_C = 8.0
ADAM_LR, ADAM_B1, ADAM_B2, ADAM_EPS, ADAM_WD, ADAM_STEP = 0.001, 0.9, 0.999, 1e-08, 0.01, 10

V7X_LANES = 128
V7X_VMEM_BUDGET = 56 << 20

NN = (((1,), (0,)), ((), ()))
NT = (((1,), (1,)), ((), ()))
TN = (((0,), (0,)), ((), ()))
MESH = pl.DeviceIdType.MESH
HBM_SPEC = pl.BlockSpec(memory_space=pl.ANY)


def _pcall(kern, **kw):
    return pl.pallas_call(kern, **kw)


def _params(sem=None, vmem=None):
    kw = {}
    if sem is not None:
        kw["dimension_semantics"] = sem
    if vmem is not None:
        kw["vmem_limit_bytes"] = int(min(max(vmem, 16 << 20), V7X_VMEM_BUDGET))
    return pltpu.CompilerParams(**kw)


def _dot(a, b, dims=NN):
    return lax.dot_general(a.astype(MXU_DTYPE), b.astype(MXU_DTYPE), dims, preferred_element_type=F32)


def _tile(dim, pref, align):
    t = min(pref, dim) // align * align
    while t >= align:
        if dim % t == 0:
            return t
        t -= align
    return dim


def _sigmoid(x):
    return 1.0 / (1.0 + jnp.exp(-x))


def _silu_and_grad(x):
    s = _sigmoid(x)
    return x * s, s * (1.0 + x * (1.0 - s))


def _gelu_and_grad(x):
    k0, k1 = math.sqrt(2.0 / math.pi), 0.044715
    t = jnp.tanh(k0 * (x + k1 * x * x * x))
    g = 0.5 * x * (1.0 + t)
    dg = 0.5 * (1.0 + t) + 0.5 * x * (1.0 - t * t) * k0 * (1.0 + 3.0 * k1 * x * x)
    return g, dg


def _one_minus_exp(x):
    p = x * (1.0 + x * (0.5 + x * (1.0 / 6.0 + x * (1.0 / 24.0 + x * (1.0 / 120.0)))))
    return jnp.where(x > -0.05, -p, 1.0 - jnp.exp(x))


def _rows(n):
    return lax.broadcasted_iota(jnp.int32, (n, 1), 0)


def _matmul(a, b, mode, out_dtype, tm, tn, tk, add=None, after=None, n_outer=False, name="mm"):
    if mode == "TN":
        K, M = a.shape
    else:
        M, K = a.shape
    N = b.shape[0] if mode == "NT" else b.shape[1]
    tm, tn = _tile(M, tm, V7X_LANES), _tile(N, tn, V7X_LANES)
    tk = _tile(K, tk, V7X_LANES)
    nk = K // tk
    dims = {"NN": NN, "NT": NT, "TN": TN}[mode]
    order = (lambda f: (lambda j, i, k: f(i, j, k))) if n_outer else (lambda f: f)
    a_spec = (pl.BlockSpec((tk, tm), order(lambda i, j, k: (k, i))) if mode == "TN"
              else pl.BlockSpec((tm, tk), order(lambda i, j, k: (i, k))))
    b_spec = (pl.BlockSpec((tn, tk), order(lambda i, j, k: (j, k))) if mode == "NT"
              else pl.BlockSpec((tk, tn), order(lambda i, j, k: (k, j))))
    o_spec = pl.BlockSpec((tm, tn), order(lambda i, j, k: (i, j)))
    has_add = add is not None

    def kern(*refs):
        a_ref, b_ref = refs[:2]
        add_ref = refs[2] if has_add else None

        def finish(r, o_ref):
            if has_add:
                r = r + add_ref[...]
            o_ref[...] = r.astype(out_dtype)

        if nk == 1:
            finish(_dot(a_ref[...], b_ref[...], dims), refs[-1])
            return
        o_ref, acc_ref = refs[-2:]
        k = pl.program_id(2)

        @pl.when(k == 0)
        def _():
            acc_ref[...] = jnp.zeros_like(acc_ref)

        acc_ref[...] += _dot(a_ref[...], b_ref[...], dims)

        @pl.when(k == nk - 1)
        def _():
            finish(acc_ref[...], o_ref)

    ab = jnp.dtype(a.dtype).itemsize
    ob = jnp.dtype(out_dtype).itemsize
    vmem = 2 * (tm * tk + tk * tn) * ab + tm * tn * (8 + 2 * ob + (8 if has_add else 0)) + (4 << 20)
    ins = [a, b] + ([add] if has_add else []) + ([after] if after is not None else [])
    in_specs = [a_spec, b_spec] + ([o_spec] if has_add else []) + ([HBM_SPEC] if after is not None else [])
    grid = (N // tn, M // tm, nk) if n_outer else (M // tm, N // tn, nk)
    return _pcall(
        kern, name=name, grid=grid,
        in_specs=in_specs, out_specs=o_spec,
        out_shape=jax.ShapeDtypeStruct((M, N), out_dtype),
        scratch_shapes=[pltpu.VMEM((tm, tn), F32)] if nk > 1 else [],
        compiler_params=_params(("parallel", "parallel", "arbitrary"), vmem),
    )(*ins)


def _matmul_tn_half(a, b, half, add, tm, tn, by_rows, name):
    K, M = a.shape
    N = b.shape[1]
    Mo, No = (M // 2, N) if by_rows else (M, N // 2)
    tm, tn = _tile(Mo, tm, V7X_LANES), _tile(No, tn, V7X_LANES)
    nm, nn = Mo // tm, No // tn
    has_add = add is not None

    def kern(h_ref, a_ref, b_ref, *rest):
        r = _dot(a_ref[...], b_ref[...], TN)
        if has_add:
            r = r + rest[0][...].astype(F32)
        rest[-1][...] = r.astype(ACT_DTYPE)

    if by_rows:
        a_map, b_map = (lambda i, j, h: (0, h[0] * nm + i)), (lambda i, j, h: (0, j))
        o_map, grid = (lambda i, j, h: (i, j)), (nm, nn)
    else:
        a_map, b_map = (lambda j, i, h: (0, i)), (lambda j, i, h: (0, h[0] * nn + j))
        o_map, grid = (lambda j, i, h: (i, j)), (nn, nm)
    blk = pl.BlockSpec((tm, tn), o_map)
    grid_spec = pltpu.PrefetchScalarGridSpec(
        num_scalar_prefetch=1, grid=grid,
        in_specs=[pl.BlockSpec((K, tm), a_map), pl.BlockSpec((K, tn), b_map)] + ([blk] if has_add else []),
        out_specs=blk)
    ab = jnp.dtype(a.dtype).itemsize
    vmem = 2 * K * (tm + tn) * ab + tm * tn * 16 + (4 << 20)
    return _pcall(kern, name=name, grid_spec=grid_spec, out_shape=jax.ShapeDtypeStruct((Mo, No), ACT_DTYPE),
                  compiler_params=_params(("parallel", "parallel"), vmem))(half, a, b, *([add] if has_add else []))


def _rms_fwd(x, w, name):
    T, D = x.shape
    tm = _tile(T, 256, 16)

    def kern(x_ref, w_ref, o_ref):
        xv = x_ref[...]
        r = lax.rsqrt(jnp.mean(xv * xv, axis=-1, keepdims=True) + EPS)
        o_ref[...] = (xv * r * w_ref[...]).astype(ACT_DTYPE)

    return _pcall(kern, name=name, grid=(T // tm,),
                  in_specs=[pl.BlockSpec((tm, D), lambda i: (i, 0)), pl.BlockSpec((1, D), lambda i: (0, 0))],
                  out_specs=pl.BlockSpec((tm, D), lambda i: (i, 0)),
                  out_shape=jax.ShapeDtypeStruct((T, D), ACT_DTYPE),
                  compiler_params=_params(("parallel",), 8 * tm * D * 4))(x, w)


def _rms_bwd(x, w, g, g_col, res, want_act, name, after=None):
    T, D = x.shape
    tm = _tile(T, 256, 16)
    has_res = res is not None

    def kern(*refs):
        refs = list(refs)
        x_ref, w_ref, g_ref = refs[:3]
        res_ref = refs[3] if has_res else None
        outs = refs[3 + has_res + (after is not None):]
        dx_ref = outs[0]
        dxa_ref = outs[1] if want_act else None
        dw_ref = outs[-1]
        i = pl.program_id(0)
        xv = x_ref[...]
        gv = g_ref[...].astype(F32)
        r = lax.rsqrt(jnp.mean(xv * xv, axis=-1, keepdims=True) + EPS)
        gw = gv * w_ref[...]
        dx = r * gw - xv * (r * r * r) * jnp.mean(gw * xv, axis=-1, keepdims=True)
        if has_res:
            dx = dx + res_ref[...]
        dx_ref[...] = dx
        if want_act:
            dxa_ref[...] = dx.astype(ACT_DTYPE)

        @pl.when(i == 0)
        def _():
            dw_ref[...] = jnp.zeros_like(dw_ref)

        dw_ref[...] += jnp.sum(gv * xv * r, axis=0, keepdims=True)

    row = pl.BlockSpec((tm, D), lambda i: (i, 0))
    vec = pl.BlockSpec((1, D), lambda i: (0, 0))
    in_specs = ([row, vec, pl.BlockSpec((tm, D), lambda i: (i, g_col))] + ([row] if has_res else [])
                + ([HBM_SPEC] if after is not None else []))
    out_specs = [row] + ([row] if want_act else []) + [vec]
    out_shape = ([jax.ShapeDtypeStruct((T, D), F32)]
                 + ([jax.ShapeDtypeStruct((T, D), ACT_DTYPE)] if want_act else [])
                 + [jax.ShapeDtypeStruct((1, D), F32)])
    ins = [x, w, g] + ([res] if has_res else []) + ([after] if after is not None else [])
    return _pcall(kern, name=name, grid=(T // tm,), in_specs=in_specs, out_specs=out_specs,
                  out_shape=out_shape, compiler_params=_params(("arbitrary",), 14 * tm * D * 4))(*ins)


def _loss_bwd(h, target, w, name):
    T, D = h.shape
    tm = _tile(T, 256, 16)

    def kern(h_ref, t_ref, w_ref, dh_ref, dha_ref, dw_ref, loss_ref):
        i = pl.program_id(0)
        hv = h_ref[...]
        r = lax.rsqrt(jnp.mean(hv * hv, axis=-1, keepdims=True) + EPS)
        e = hv * r * w_ref[...] - t_ref[...]
        dy = e * (1.0 / D)
        gw = dy * w_ref[...]
        dh = r * gw - hv * (r * r * r) * jnp.mean(gw * hv, axis=-1, keepdims=True)
        dh_ref[...] = dh
        dha_ref[...] = dh.astype(ACT_DTYPE)

        @pl.when(i == 0)
        def _():
            dw_ref[...] = jnp.zeros_like(dw_ref)
            loss_ref[...] = jnp.zeros_like(loss_ref)

        dw_ref[...] += jnp.sum(dy * hv * r, axis=0, keepdims=True)
        part = 0.5 * jnp.sum(jnp.mean(e * e, axis=-1, keepdims=True), axis=0, keepdims=True)
        loss_ref[...] += jnp.broadcast_to(part, loss_ref.shape)

    row = pl.BlockSpec((tm, D), lambda i: (i, 0))
    vec = pl.BlockSpec((1, D), lambda i: (0, 0))
    return _pcall(kern, name=name, grid=(T // tm,), in_specs=[row, row, vec],
                  out_specs=[row, row, vec, pl.BlockSpec((8, V7X_LANES), lambda i: (0, 0))],
                  out_shape=[jax.ShapeDtypeStruct((T, D), F32), jax.ShapeDtypeStruct((T, D), ACT_DTYPE),
                             jax.ShapeDtypeStruct((1, D), F32), jax.ShapeDtypeStruct((8, V7X_LANES), F32)],
                  compiler_params=_params(("arbitrary",), 14 * tm * D * 4))(h, target, w)


def _conv(x, w_ref, b, width):
    S = x.shape[0]
    row = _rows(S)
    y = b + x * w_ref[pl.ds(width - 1, 1), :]
    for j in range(width - 1):
        sh = width - 1 - j
        y = y + jnp.where(row >= sh, pltpu.roll(x, sh, 0), 0.0) * w_ref[pl.ds(j, 1), :]
    return y


def _conv_bwd(x, dy, w_ref, width):
    S = dy.shape[0]
    row = _rows(S)
    dx = dy * w_ref[pl.ds(width - 1, 1), :]
    dw = [None] * (width - 1) + [jnp.sum(x * dy, axis=0, keepdims=True)]
    for j in range(width - 1):
        sh = width - 1 - j
        dys = jnp.where(row < S - sh, pltpu.roll(dy, S - sh, 0), 0.0)
        dx = dx + dys * w_ref[pl.ds(j, 1), :]
        dw[j] = jnp.sum(x * dys, axis=0, keepdims=True)
    return dx, dw


def _ffn_act(up, cw, cb, B, name):
    T, F2 = up.shape
    S, F = T // B, F2 // 2
    tw = _tile(F, 256, V7X_LANES)
    nt = F // tw

    def kern(g_ref, v_ref, wg_ref, wv_ref, bg_ref, bv_ref, o_ref, mg_ref, mv_ref):
        gc = _conv(g_ref[...], wg_ref, bg_ref[...], FFN_CONV)
        vc = _conv(v_ref[...], wv_ref, bv_ref[...], FFN_CONV)
        silu, dsilu = _silu_and_grad(gc)
        o_ref[...] = (silu * vc).astype(ACT_DTYPE)
        mg_ref[...] = (vc * dsilu).astype(ACT_DTYPE)
        mv_ref[...] = silu.astype(ACT_DTYPE)

    blk = lambda off: pl.BlockSpec((S, tw), lambda b, i: (b, off + i))
    wblk = lambda off: pl.BlockSpec((FFN_CONV, tw), lambda b, i: (0, off + i))
    bblk = lambda off: pl.BlockSpec((1, tw), lambda b, i: (0, off + i))
    half = jax.ShapeDtypeStruct((T, F), ACT_DTYPE)
    return _pcall(kern, name=name, grid=(B, nt),
                  in_specs=[blk(0), blk(nt), wblk(0), wblk(nt), bblk(0), bblk(nt)],
                  out_specs=[blk(0), blk(0), blk(0)], out_shape=[half, half, half],
                  compiler_params=_params(("parallel", "parallel"), 20 * S * tw * 4))(up, up, cw, cw, cb, cb)


def _ffn_act_bwd(up, cw, mg, mv, d_act, B, name):
    T, F2 = up.shape
    S, F = T // B, F2 // 2
    tw = _tile(F, 256, V7X_LANES)
    nt = F // tw

    def kern(s_ref, ws_ref, mg_ref, mv_ref, da_ref, du_ref, dcw_ref, dcb_ref):
        t, b = pl.program_id(0), pl.program_id(1)
        mult = jnp.where(t < nt, mg_ref[...], mv_ref[...])
        d = da_ref[...].astype(F32) * mult.astype(F32)
        dx, dw = _conv_bwd(s_ref[...], d, ws_ref, FFN_CONV)
        du_ref[...] = dx.astype(ACT_DTYPE)

        @pl.when(b == 0)
        def _():
            dcw_ref[...] = jnp.zeros_like(dcw_ref)
            dcb_ref[...] = jnp.zeros_like(dcb_ref)

        for j, rj in enumerate(dw):
            dcw_ref[pl.ds(j, 1), :] += rj
        dcb_ref[...] += jnp.sum(d, axis=0, keepdims=True)

    own = lambda t, b: (b, t % nt)
    return _pcall(
        kern, name=name, grid=(2 * nt, B),
        in_specs=[pl.BlockSpec((S, tw), lambda t, b: (b, t)),
                  pl.BlockSpec((FFN_CONV, tw), lambda t, b: (0, t)),
                  pl.BlockSpec((S, tw), own), pl.BlockSpec((S, tw), own), pl.BlockSpec((S, tw), own)],
        out_specs=[pl.BlockSpec((S, tw), lambda t, b: (b, t)),
                   pl.BlockSpec((FFN_CONV, tw), lambda t, b: (0, t)),
                   pl.BlockSpec((1, tw), lambda t, b: (0, t))],
        out_shape=[jax.ShapeDtypeStruct((T, F2), ACT_DTYPE), jax.ShapeDtypeStruct((FFN_CONV, F2), F32),
                   jax.ShapeDtypeStruct((1, F2), F32)],
        compiler_params=_params(("parallel", "arbitrary"), 20 * S * tw * 4),
    )(up, cw, mg, mv, d_act)


def _hgrn_tables():
    C = CHUNK
    t = np.arange(C)
    mats = [(t[:, None] >= t[None, :]).astype(np.float32)]
    masks = []
    gsum = [(t[:, None] <= t[None, :]).astype(np.float32), (t[:, None] > t[None, :]).astype(np.float32)]
    for hs in LEVEL_HALVES:
        m = (t // (2 * hs)) * 2 * hs + hs
        later = t >= m
        d = np.zeros((C, C), np.float32)
        for i in range(C):
            if later[i]:
                d[i, m[i]:i + 1] = 1.0
            else:
                d[i, i + 1:m[i]] = -1.0
        mats.append(d)
        same = (t[:, None] // (2 * hs)) == (t[None, :] // (2 * hs))
        masks.append((same & later[:, None] & (~later)[None, :]).astype(np.float32))
        gsum.append((same & later[:, None] & (t[None, :] >= t[:, None])).astype(np.float32))
        gsum.append((same & (~later)[:, None] & (t[None, :] < t[:, None])).astype(np.float32))
    return np.concatenate(mats, 0), np.stack(masks, 0), np.concatenate(gsum, 1)


def _split_dot(mat, v):
    hi = v.astype(MXU_DTYPE)
    lo = (v - hi.astype(F32)).astype(MXU_DTYPE)
    r = _dot(mat, jnp.concatenate([hi, lo], axis=1))
    n = v.shape[1]
    return r[:, :n] + r[:, n:]


def _hgrn_gates(qr, fr, lb, mc):
    C = CHUNK
    q, dq_dqr = _silu_and_grad(qr)
    sf = _sigmoid(fr)
    f = lb + (1.0 - lb) * sf
    k = 1.0 - f
    dall = _split_dot(mc, jnp.log(f))
    b = dall[0:C]
    dl = [dall[C * (l + 1):C * (l + 2)] for l in range(len(LEVEL_HALVES))]
    eq = [jnp.exp(jnp.minimum(d, 0.0)) for d in dl]
    ek = [jnp.exp(jnp.minimum(-d, 0.0)) for d in dl]
    return q, dq_dqr, sf, f, k, b, eq, ek


def _hgrn_scores(q, k, eq, ek, masks_ref):
    p = jnp.where(_rows(CHUNK) == lax.broadcasted_iota(jnp.int32, (1, CHUNK), 1),
                  jnp.sum(q * k, axis=-1, keepdims=True), 0.0)
    for l in range(len(LEVEL_HALVES)):
        p = p + masks_ref[l] * _dot(q * eq[l], k * ek[l], NT)
    return p


def _hgrn_fwd(proj, lb_gamma, norm_w, B, HW, name):
    T = proj.shape[0]
    S, H, C = T // B, HW // HEAD_DIM, CHUNK
    NC = S // C
    mc_np, masks_np, _ = _hgrn_tables()
    mc, masks = jnp.asarray(mc_np, MXU_DTYPE), jnp.asarray(masks_np, F32)

    def kern(q_ref, f_ref, i_ref, g_ref, lbg_ref, nw_ref, mc_ref, masks_ref, oraw_ref, o_ref, st_ref):
        g0, g1 = lbg_ref[pl.ds(0, 1), :], lbg_ref[pl.ds(1, 1), :]
        mx = jnp.maximum(g0, g1)
        e0, e1 = jnp.exp(g0 - mx), jnp.exp(g1 - mx)
        lb = e0 / (e0 + e1)
        nw = nw_ref[...]
        mcv = mc_ref[...]

        def body(n, sts):
            out = []
            for s, st in enumerate(sts):
                rows = pl.ds(pl.multiple_of(s * S + n * C, C), C)
                st_ref[s, n] = st
                q, _, _, _, k, b, eq, ek = _hgrn_gates(q_ref[rows, :], f_ref[rows, :], lb, mcv)
                v = i_ref[rows, :]
                o = _dot(q * jnp.exp(b), st, NT) + _dot(_hgrn_scores(q, k, eq, ek, masks_ref), v)
                b_last = b[C - 1:C]
                out.append(st * jnp.exp(b_last) + _dot(v, k * jnp.exp(b_last - b), TN))
                oraw_ref[rows, :] = o
                r = lax.rsqrt(jnp.mean(o * o, axis=-1, keepdims=True) + EPS)
                gate, _ = _silu_and_grad(g_ref[rows, :])
                o_ref[rows, :] = (o * r * nw * gate).astype(ACT_DTYPE)
            return tuple(out)

        lax.fori_loop(0, NC, body, tuple(jnp.zeros((HEAD_DIM, HEAD_DIM), F32) for _ in range(B)))

    col = lambda off: pl.BlockSpec((T, HEAD_DIM), lambda h: (0, off + h))
    return _pcall(
        kern, name=name, grid=(H,),
        in_specs=[col(0), col(H), col(2 * H), col(3 * H),
                  pl.BlockSpec((2, HEAD_DIM), lambda h: (0, h)),
                  pl.BlockSpec((1, HEAD_DIM), lambda h: (0, h)),
                  pl.BlockSpec(mc.shape, lambda h: (0, 0)),
                  pl.BlockSpec(masks.shape, lambda h: (0, 0, 0))],
        out_specs=[col(0), col(0),
                   pl.BlockSpec((B, None, NC, HEAD_DIM, HEAD_DIM), lambda h: (0, h, 0, 0, 0))],
        out_shape=[jax.ShapeDtypeStruct((T, HW), F32), jax.ShapeDtypeStruct((T, HW), ACT_DTYPE),
                   jax.ShapeDtypeStruct((B, H, NC, HEAD_DIM, HEAD_DIM), F32)],
        compiler_params=_params(("parallel",), 20 * T * HEAD_DIM * 4 + (8 << 20)),
    )(proj, proj, proj, proj, lb_gamma, norm_w, mc, masks)


def _hgrn_bwd(proj, lb_gamma, norm_w, o_raw, states, d_mix, B, HW, name):
    T = proj.shape[0]
    S, H, C = T // B, HW // HEAD_DIM, CHUNK
    NC = S // C
    mc_np, masks_np, gsum_np = _hgrn_tables()
    mc, masks, gsum = jnp.asarray(mc_np, MXU_DTYPE), jnp.asarray(masks_np, F32), jnp.asarray(gsum_np, MXU_DTYPE)
    nl = len(LEVEL_HALVES)

    def kern(q_ref, f_ref, i_ref, g_ref, lbg_ref, nw_ref, mc_ref, masks_ref, gsum_ref, oraw_ref, st_ref, do_ref,
             dq_ref, df_ref, di_ref, dg_ref, dlbg_ref, dnw_ref):
        g0, g1 = lbg_ref[pl.ds(0, 1), :], lbg_ref[pl.ds(1, 1), :]
        mx = jnp.maximum(g0, g1)
        e0, e1 = jnp.exp(g0 - mx), jnp.exp(g1 - mx)
        lb = e0 / (e0 + e1)
        nw = nw_ref[...]
        mcv, gsumv = mc_ref[...], gsum_ref[...]

        def chunk(s, n, dst, dlb, dnw):
            rows = pl.ds(pl.multiple_of(s * S + n * C, C), C)
            qr, fr, v = q_ref[rows, :], f_ref[rows, :], i_ref[rows, :]
            q, dq_dqr, sf, f, k, b, eq, ek = _hgrn_gates(qr, fr, lb, mcv)
            o = oraw_ref[rows, :]
            dout = do_ref[rows, :].astype(F32)
            gate, dgate = _silu_and_grad(g_ref[rows, :])
            r = lax.rsqrt(jnp.mean(o * o, axis=-1, keepdims=True) + EPS)
            dg_ref[rows, :] = (dout * o * r * nw * dgate).astype(ACT_DTYPE)
            don = dout * gate
            dnw = dnw + jnp.sum(don * o * r, axis=0, keepdims=True)
            gw = don * nw
            do = r * gw - o * (r * r * r) * jnp.mean(gw * o, axis=-1, keepdims=True)
            st_prev = st_ref[s, n]
            eb = jnp.exp(b)
            b_last = b[C - 1:C]
            ebl = jnp.exp(b_last - b)
            p = _hgrn_scores(q, k, eq, ek, masks_ref)
            dp = _dot(do, v, NT)
            dpd = jnp.sum(do * v, axis=-1, keepdims=True)
            dq_state = _dot(do, st_prev) * eb
            dk_state = _dot(v, dst) * ebl
            dq = dq_state + dpd * k
            dk = dk_state + dpd * q
            pairs = [q * dq_state, k * dk_state]
            for l in range(nl):
                mdp = masks_ref[l] * dp
                dql = _dot(mdp, k * ek[l]) * eq[l]
                dkl = _dot(mdp, q * eq[l], TN) * ek[l]
                dq, dk = dq + dql, dk + dkl
                pairs += [q * dql, k * dkl]
            dv = _dot(p, do, TN) + _dot(k * ebl, dst, NT)
            through = jnp.exp(b_last) * jnp.sum(dst * st_prev, axis=0, keepdims=True)
            dlg = _split_dot(gsumv, jnp.concatenate(pairs, axis=0)) + through
            dst = dst * jnp.exp(b_last) + _dot(do, q * eb, TN)
            dq_ref[rows, :] = (dq * dq_dqr).astype(ACT_DTYPE)
            dfv = dlg / f - dk
            df_ref[rows, :] = (dfv * (1.0 - lb) * sf * (1.0 - sf)).astype(ACT_DTYPE)
            di_ref[rows, :] = dv.astype(ACT_DTYPE)
            dlb = dlb + jnp.sum(dfv * (1.0 - sf), axis=0, keepdims=True)
            return dst, dlb, dnw

        def body(it, carry):
            dsts, dlb, dnw = carry
            out = []
            for s, dst in enumerate(dsts):
                dst, dlb, dnw = chunk(s, NC - 1 - it, dst, dlb, dnw)
                out.append(dst)
            return tuple(out), dlb, dnw

        zrow = jnp.zeros((1, HEAD_DIM), F32)
        zst = tuple(jnp.zeros((HEAD_DIM, HEAD_DIM), F32) for _ in range(B))
        _, dlb, dnw = lax.fori_loop(0, NC, body, (zst, zrow, zrow))
        dg0 = dlb * lb * (1.0 - lb)
        dlbg_ref[pl.ds(0, 1), :] = dg0
        dlbg_ref[pl.ds(1, 1), :] = -dg0
        dnw_ref[...] = dnw

    col = lambda off: pl.BlockSpec((T, HEAD_DIM), lambda h: (0, off + h))
    full = lambda a: pl.BlockSpec(a.shape, lambda h: (0,) * a.ndim)
    part = jax.ShapeDtypeStruct((T, HW), ACT_DTYPE)
    return _pcall(
        kern, name=name, grid=(H,),
        in_specs=[col(0), col(H), col(2 * H), col(3 * H),
                  pl.BlockSpec((2, HEAD_DIM), lambda h: (0, h)),
                  pl.BlockSpec((1, HEAD_DIM), lambda h: (0, h)),
                  full(mc), full(masks), full(gsum), col(0),
                  pl.BlockSpec((B, None, NC, HEAD_DIM, HEAD_DIM), lambda h: (0, h, 0, 0, 0)),
                  col(0)],
        out_specs=[col(0), col(0), col(0), col(0),
                   pl.BlockSpec((2, HEAD_DIM), lambda h: (0, h)),
                   pl.BlockSpec((1, HEAD_DIM), lambda h: (0, h))],
        out_shape=[part, part, part, part, jax.ShapeDtypeStruct((2, HW), F32), jax.ShapeDtypeStruct((1, HW), F32)],
        compiler_params=_params(("parallel",), 28 * T * HEAD_DIM * 4 + (8 << 20)),
    )(proj, proj, proj, proj, lb_gamma, norm_w, mc, masks, gsum, o_raw, states, d_mix)


def _lru_gates(xr, cw_ref, cb, wa, ba, wx, bx, lam):
    S = xr.shape[0]
    xb = _conv(xr, cw_ref, cb, LRU_CONV)
    r = _sigmoid(_dot(xb, wa) + ba)
    ig = _sigmoid(_dot(xb, wx) + bx)
    sp = jnp.maximum(-lam, 0.0) + jnp.log(1.0 + jnp.exp(-jnp.abs(lam)))
    la = -LRU_C * r * sp
    a = jnp.exp(la)
    mult = jnp.where(_rows(S) == 0, 1.0, jnp.sqrt(_one_minus_exp(2.0 * la)))
    return xb, r, ig, sp, a, mult


def _scan_rows(a_ref, u_ref, h_ref, reverse):
    S, W = a_ref.shape
    nb = S // 8
    row = _rows(8)

    def body(it, carry):
        blk = nb - 1 - it if reverse else it
        rows = pl.ds(pl.multiple_of(blk * 8, 8), 8)
        a, u = a_ref[rows, :], u_ref[rows, :]
        for d in (1, 2, 4):
            sh = 8 - d if reverse else d
            keep = (row < 8 - d) if reverse else (row >= d)
            u = u + jnp.where(keep, a * pltpu.roll(u, sh, 0), 0.0)
            a = jnp.where(keep, a * pltpu.roll(a, sh, 0), a)
        h = u + a * carry
        h_ref[rows, :] = h
        return h[0:1] if reverse else h[7:8]

    lax.fori_loop(0, nb, body, jnp.zeros((1, W), F32))


def _lru_fwd(proj, cw, cb, wa, ba, wx, bx, lam, B, HW, LW, name):
    T = proj.shape[0]
    S, NB = T // B, LW // HEAD_DIM
    xoff, yoff = 4 * HW // HEAD_DIM, 4 * HW // HEAD_DIM + NB

    def kern(x_ref, y_ref, cw_ref, cb_ref, wa_ref, ba_ref, wx_ref, bx_ref, lam_ref, h_ref, z_ref, a_s, u_s):
        xb, _, ig, _, a, mult = _lru_gates(x_ref[...], cw_ref, cb_ref[...], wa_ref[...], ba_ref[...],
                                           wx_ref[...], bx_ref[...], lam_ref[...])
        a_s[...] = a
        u_s[...] = xb * ig * mult
        _scan_rows(a_s, u_s, h_ref, False)
        gy, _ = _gelu_and_grad(y_ref[...])
        z_ref[...] = h_ref[...] * gy

    blk = lambda off: pl.BlockSpec((S, HEAD_DIM), lambda b, n: (b, off + n))
    vec = pl.BlockSpec((1, HEAD_DIM), lambda b, n: (0, n))
    mat = pl.BlockSpec((None, HEAD_DIM, HEAD_DIM), lambda b, n: (n, 0, 0))
    return _pcall(
        kern, name=name, grid=(B, NB),
        in_specs=[blk(xoff), blk(yoff), pl.BlockSpec((LRU_CONV, HEAD_DIM), lambda b, n: (0, n)),
                  vec, mat, vec, mat, vec, vec],
        out_specs=[blk(0), blk(0)],
        out_shape=[jax.ShapeDtypeStruct((T, LW), F32), jax.ShapeDtypeStruct((T, LW), F32)],
        scratch_shapes=[pltpu.VMEM((S, HEAD_DIM), F32), pltpu.VMEM((S, HEAD_DIM), F32)],
        compiler_params=_params(("parallel", "parallel"), 24 * S * HEAD_DIM * 4),
    )(proj, proj, cw, cb, wa, ba, wx, bx, lam)


def _lru_bwd(proj, cw, cb, wa, ba, wx, bx, lam, h, dz, B, HW, LW, name):
    T = proj.shape[0]
    S, NB = T // B, LW // HEAD_DIM
    xoff, yoff = 4 * HW // HEAD_DIM, 4 * HW // HEAD_DIM + NB

    def kern(x_ref, y_ref, cw_ref, cb_ref, wa_ref, ba_ref, wx_ref, bx_ref, lam_ref, h_ref, dz_ref,
             dx_ref, dy_ref, dwa_ref, dwx_ref, dba_ref, dbx_ref, dlam_ref, dcw_ref, dcb_ref, a_s, u_s, dh_s):
        bi = pl.program_id(1)
        row = _rows(S)
        xr, lam = x_ref[...], lam_ref[...]
        wa, wx = wa_ref[...], wx_ref[...]
        xb, r, ig, sp, a, mult = _lru_gates(xr, cw_ref, cb_ref[...], wa, ba_ref[...], wx, bx_ref[...], lam)
        hv, dz = h_ref[...], dz_ref[...]
        gy, dgy = _gelu_and_grad(y_ref[...])
        dy_ref[...] = (dz * hv * dgy).astype(ACT_DTYPE)
        a_s[...] = jnp.where(row < S - 1, pltpu.roll(a, S - 1, 0), 0.0)
        u_s[...] = dz * gy
        _scan_rows(a_s, u_s, dh_s, True)
        dh = dh_s[...]
        h_prev = jnp.where(row >= 1, pltpu.roll(hv, 1, 0), 0.0)
        d_ig = dh * xb * mult
        d_mult = jnp.where(row == 0, 0.0, dh * xb * ig)
        dxb = dh * ig * mult
        dla = dh * h_prev * a - d_mult * (a * a) / mult
        dpre_r = dla * (-LRU_C * sp) * r * (1.0 - r)
        dpre_i = d_ig * ig * (1.0 - ig)
        dxb = dxb + _dot(dpre_r, wa, NT) + _dot(dpre_i, wx, NT)
        dxr, dcw = _conv_bwd(xr, dxb, cw_ref, LRU_CONV)
        dx_ref[...] = dxr.astype(ACT_DTYPE)

        @pl.when(bi == 0)
        def _():
            for ref in (dwa_ref, dwx_ref, dba_ref, dbx_ref, dlam_ref, dcw_ref, dcb_ref):
                ref[...] = jnp.zeros_like(ref)

        dwa_ref[...] += _dot(xb, dpre_r, TN)
        dwx_ref[...] += _dot(xb, dpre_i, TN)
        dba_ref[...] += jnp.sum(dpre_r, axis=0, keepdims=True)
        dbx_ref[...] += jnp.sum(dpre_i, axis=0, keepdims=True)
        dsp = jnp.sum(dla * (-LRU_C) * r, axis=0, keepdims=True)
        dlam_ref[...] += -dsp * _sigmoid(-lam)
        for j, rj in enumerate(dcw):
            dcw_ref[pl.ds(j, 1), :] += rj
        dcb_ref[...] += jnp.sum(dxb, axis=0, keepdims=True)

    blk = lambda off: pl.BlockSpec((S, HEAD_DIM), lambda n, b: (b, off + n))
    vec = pl.BlockSpec((1, HEAD_DIM), lambda n, b: (0, n))
    mat = pl.BlockSpec((None, HEAD_DIM, HEAD_DIM), lambda n, b: (n, 0, 0))
    cwb = pl.BlockSpec((LRU_CONV, HEAD_DIM), lambda n, b: (0, n))
    part = jax.ShapeDtypeStruct((T, LW), ACT_DTYPE)
    vshape = jax.ShapeDtypeStruct((1, LW), F32)
    mshape = jax.ShapeDtypeStruct((NB, HEAD_DIM, HEAD_DIM), F32)
    return _pcall(
        kern, name=name, grid=(NB, B),
        in_specs=[blk(xoff), blk(yoff), cwb, vec, mat, vec, mat, vec, vec, blk(0), blk(0)],
        out_specs=[blk(0), blk(0), mat, mat, vec, vec, vec, cwb, vec],
        out_shape=[part, part, mshape, mshape, vshape, vshape, vshape,
                   jax.ShapeDtypeStruct((LRU_CONV, LW), F32), vshape],
        scratch_shapes=[pltpu.VMEM((S, HEAD_DIM), F32)] * 3,
        compiler_params=_params(("parallel", "arbitrary"), 40 * S * HEAD_DIM * 4),
    )(proj, proj, cw, cb, wa, ba, wx, bx, lam, h, dz)


def _pos():
    return lax.axis_index("x"), lax.axis_index("y"), lax.axis_index("c")


def _other_chips(x, y):
    return [(1 - x, y), (x, 1 - y), (1 - x, 1 - y)]


def _remote(src, dst, send_sems, recv_sems, k, to):
    return pltpu.make_async_remote_copy(src_ref=src, dst_ref=dst, send_sem=send_sems.at[k],
                                        recv_sem=recv_sems.at[k], device_id=to, device_id_type=MESH)


HBM_BLK = pl.BlockSpec(memory_space=pltpu.HBM)
SEM_BLK = pl.BlockSpec(memory_space=pltpu.SEMAPHORE)
VMEM_BLK = pl.BlockSpec(memory_space=pltpu.VMEM)
DATAFLOW = pltpu.SideEffectType.DATAFLOW_SIDE_EFFECTING
TOKEN = jax.ShapeDtypeStruct((8, V7X_LANES), F32)


def _in_hbm(a):
    return pltpu.with_memory_space_constraint(a, pltpu.HBM)


def _gather_win(o_ref, R, C, col_sharded):
    Rh = R // 2

    def win(j, h=None):
        if col_sharded:
            rows = pl.ds(0, R) if h is None else pl.ds(h * Rh, Rh)
            return o_ref.at[rows, pl.ds(j * C, C)]
        return o_ref.at[pl.ds(j * R, R) if h is None else pl.ds(j * R + h * Rh, Rh), :]

    return win


def _cast_into_window(w, col_sharded, after, name):
    R, C = w.shape
    tr = _tile(R, 256, 16)
    nr = R // tr
    full = (R, 4 * C) if col_sharded else (4 * R, C)
    j = (2 * lax.axis_index("x") + lax.axis_index("y")).astype(jnp.int32).reshape(1)

    def kern(j_ref, w_ref, after_ref, o_ref):
        o_ref[...] = w_ref[...].astype(ACT_DTYPE)

    out_map = (lambda i, jr: (i, jr[0])) if col_sharded else (lambda i, jr: (jr[0] * nr + i, 0))
    grid_spec = pltpu.PrefetchScalarGridSpec(
        num_scalar_prefetch=1, grid=(nr,),
        in_specs=[pl.BlockSpec((tr, C), lambda i, jr: (i, 0)), HBM_SPEC], out_specs=pl.BlockSpec((tr, C), out_map))
    return _pcall(kern, name=name, grid_spec=grid_spec, out_shape=jax.ShapeDtypeStruct(full, ACT_DTYPE),
                  compiler_params=_params(("parallel",), 6 * tr * C * 4))(j, w, after)


def _gather_start(land, shard_shape, col_sharded, token, name):
    R, C = shard_shape

    def body(land_ref, tok_ref, send_sems, recv_sems, land_thru, tok_out):
        x, y, c = _pos()
        w = _gather_win(land_ref, R, C, col_sharded)(2 * x + y, c)
        for k, (cx, cy) in enumerate(_other_chips(x, y)):
            _remote(w, w, send_sems, recv_sems, k, (cx, cy, c)).start()
        tok_out[...] = tok_ref[...]

    return _pcall(
        body, name=name,
        out_shape=(pltpu.SemaphoreType.DMA((3,)), pltpu.SemaphoreType.DMA((3,)),
                   pltpu.HBM(land.shape, land.dtype), TOKEN),
        in_specs=(HBM_BLK, VMEM_BLK), out_specs=(SEM_BLK, SEM_BLK, HBM_BLK, VMEM_BLK),
        input_output_aliases={0: 2},
        compiler_params=pltpu.CompilerParams(has_side_effects=DATAFLOW),
    )(_in_hbm(land), token)


def _gather_wait(started, shard_shape, after, col_sharded, name):
    send_sems, recv_sems, land_thru, _ = started
    R, C = shard_shape

    def body(land_ref, send_sems, recv_sems, after_ref, got_ref):
        x, y, c = _pos()
        win = _gather_win(land_ref, R, C, col_sharded)
        for k, (cx, cy) in enumerate(_other_chips(x, y)):
            cp = _remote(win(2 * x + y, c), win(2 * cx + cy, c), send_sems, recv_sems, k, (cx, cy, c))
            cp.wait_send()
            cp.wait_recv()

    return _pcall(
        body, name=name, out_shape=pltpu.HBM(land_thru.shape, land_thru.dtype),
        in_specs=(HBM_BLK, SEM_BLK, SEM_BLK, HBM_SPEC), out_specs=HBM_BLK, input_output_aliases={0: 0},
        compiler_params=pltpu.CompilerParams(has_side_effects=DATAFLOW),
    )(land_thru, send_sems, recv_sems, after)


def _gather_pass_on(landed, shard_shape, col_sharded, name):
    R, C = shard_shape

    def body(in_ref, o_ref, send_sems, recv_sems):
        x, y, c = _pos()
        src, dst = _gather_win(in_ref, R, C, col_sharded), _gather_win(o_ref, R, C, col_sharded)
        chips = _other_chips(x, y)
        passed = [_remote(src(2 * cx + cy, c), dst(2 * cx + cy, c), send_sems, recv_sems, k, (x, y, 1 - c))
                  for k, (cx, cy) in enumerate(chips)]
        for cp in passed:
            cp.start()
        for k, (cx, cy) in enumerate(chips):
            w = dst(2 * cx + cy, 1 - c)
            _remote(w, w, send_sems, recv_sems, k, (x, y, c)).wait_recv()
        for cp in passed:
            cp.wait_send()

    return _pcall(body, name=name, in_specs=[HBM_SPEC], out_specs=HBM_SPEC,
                  out_shape=jax.ShapeDtypeStruct(landed.shape, landed.dtype), input_output_aliases={0: 0},
                  scratch_shapes=[pltpu.SemaphoreType.DMA((3,)), pltpu.SemaphoreType.DMA((3,))])(landed)


def _pair_start(g, token, name):
    def body(g_ref, land_ref, tok_ref, send_sems, recv_sems, g_thru, land_thru, tok_out):
        x, y, c = _pos()
        _remote(g_ref, land_ref, send_sems, recv_sems, 0, (x, y, 1 - c)).start()
        tok_out[...] = tok_ref[...]

    return _pcall(
        body, name=name,
        out_shape=(pltpu.SemaphoreType.DMA((1,)), pltpu.SemaphoreType.DMA((1,)),
                   pltpu.HBM(g.shape, g.dtype), pltpu.HBM(g.shape, g.dtype), TOKEN),
        in_specs=(HBM_BLK, HBM_BLK, VMEM_BLK), out_specs=(SEM_BLK, SEM_BLK, HBM_BLK, HBM_BLK, VMEM_BLK),
        input_output_aliases={0: 2, 1: 3},
        compiler_params=pltpu.CompilerParams(has_side_effects=DATAFLOW),
    )(_in_hbm(g), _in_hbm(lax.empty(g.shape, g.dtype)), token)


def _pair_wait(started, after, name):
    send_sems, recv_sems, g_thru, land_thru, _ = started

    def body(g_ref, land_ref, send_sems, recv_sems, after_ref, g_dead, got_ref):
        x, y, c = _pos()
        cp = _remote(g_ref, land_ref, send_sems, recv_sems, 0, (x, y, 1 - c))
        cp.wait_send()
        cp.wait_recv()

    return _pcall(
        body, name=name,
        out_shape=(pltpu.HBM(g_thru.shape, g_thru.dtype), pltpu.HBM(land_thru.shape, land_thru.dtype)),
        in_specs=(HBM_BLK, HBM_BLK, SEM_BLK, SEM_BLK, HBM_SPEC), out_specs=(HBM_BLK, HBM_BLK),
        input_output_aliases={0: 0, 1: 1},
        compiler_params=pltpu.CompilerParams(has_side_effects=DATAFLOW),
    )(g_thru, land_thru, send_sems, recv_sems, after)[1]


def _own_piece_into_slots(cs, col_sharded, name):
    J, Rp, W = cs.shape
    Cp = W // 4 if col_sharded else W
    tr = _tile(Rp, 256, 16)
    tw = _tile(Cp, 8192, V7X_LANES)
    nw = Cp // tw
    x, y, c = _pos()
    chip = (2 * x + y).astype(jnp.int32).reshape(1)
    core = c.astype(jnp.int32).reshape(1)

    def kern(j_ref, c_ref, s_ref, o_ref):
        o_ref[...] = s_ref[...]

    in_map = ((lambda i, w, j, cc: (0, i, j[0] * nw + w)) if col_sharded
              else (lambda i, w, j, cc: (j[0], i, w)))
    grid_spec = pltpu.PrefetchScalarGridSpec(
        num_scalar_prefetch=2, grid=(Rp // tr, nw),
        in_specs=[pl.BlockSpec((None, tr, tw), in_map)],
        out_specs=pl.BlockSpec((None, None, tr, tw), lambda i, w, j, cc: (j[0], cc[0], i, w)))
    return _pcall(kern, name=name, grid_spec=grid_spec,
                  out_shape=jax.ShapeDtypeStruct((4, 2, Rp, Cp), cs.dtype),
                  compiler_params=_params(("parallel", "parallel"), 8 * tr * tw * 4))(chip, core, cs)


def _chip_sum_piece(cs_ref, C, col_sharded):
    return lambda j: cs_ref.at[0, :, pl.ds(j * C, C)] if col_sharded else cs_ref.at[j]


def _scatter_start(cs, slots, col_sharded, token, name):
    C = slots.shape[3]

    def body(cs_ref, land_ref, tok_ref, send_sems, recv_sems, cs_thru, land_thru, tok_out):
        x, y, c = _pos()
        piece = _chip_sum_piece(cs_ref, C, col_sharded)
        for k, (cx, cy) in enumerate(_other_chips(x, y)):
            _remote(piece(2 * cx + cy), land_ref.at[2 * x + y, c], send_sems, recv_sems, k, (cx, cy, c)).start()
        tok_out[...] = tok_ref[...]

    return _pcall(
        body, name=name,
        out_shape=(pltpu.SemaphoreType.DMA((3,)), pltpu.SemaphoreType.DMA((3,)),
                   pltpu.HBM(cs.shape, cs.dtype), pltpu.HBM(slots.shape, cs.dtype), TOKEN),
        in_specs=(HBM_BLK, HBM_BLK, VMEM_BLK), out_specs=(SEM_BLK, SEM_BLK, HBM_BLK, HBM_BLK, VMEM_BLK),
        input_output_aliases={0: 2, 1: 3},
        compiler_params=pltpu.CompilerParams(has_side_effects=DATAFLOW),
    )(_in_hbm(cs), _in_hbm(slots), token)


def _scatter_wait(started, after, col_sharded, name):
    send_sems, recv_sems, cs_thru, land_thru, _ = started
    C = land_thru.shape[3]

    def body(cs_ref, land_ref, send_sems, recv_sems, after_ref, cs_dead, got_ref):
        x, y, c = _pos()
        piece = _chip_sum_piece(cs_ref, C, col_sharded)
        for k, (cx, cy) in enumerate(_other_chips(x, y)):
            cp = _remote(piece(2 * cx + cy), land_ref.at[2 * cx + cy, c], send_sems, recv_sems, k, (cx, cy, c))
            cp.wait_send()
            cp.wait_recv()

    return _pcall(
        body, name=name,
        out_shape=(pltpu.HBM(cs_thru.shape, cs_thru.dtype), pltpu.HBM(land_thru.shape, land_thru.dtype)),
        in_specs=(HBM_BLK, HBM_BLK, SEM_BLK, SEM_BLK, HBM_SPEC), out_specs=(HBM_BLK, HBM_BLK),
        input_output_aliases={0: 0, 1: 1},
        compiler_params=pltpu.CompilerParams(has_side_effects=DATAFLOW),
    )(cs_thru, land_thru, send_sems, recv_sems, after)


def _scatter_pass_on(landed, name):
    def body(in_ref, o_ref, send_sems, recv_sems):
        x, y, c = _pos()
        sends = [_remote(in_ref.at[i, c], o_ref.at[i, c], send_sems, recv_sems, i, (x, y, 1 - c)) for i in range(4)]
        for cp in sends:
            cp.start()
        for i in range(4):
            w = o_ref.at[i, 1 - c]
            _remote(w, w, send_sems, recv_sems, i, (x, y, c)).wait_recv()
        for cp in sends:
            cp.wait_send()

    return _pcall(body, name=name, in_specs=[HBM_SPEC], out_specs=HBM_SPEC,
                  out_shape=jax.ShapeDtypeStruct(landed.shape, landed.dtype), input_output_aliases={0: 0},
                  scratch_shapes=[pltpu.SemaphoreType.DMA((4,)), pltpu.SemaphoreType.DMA((4,))])(landed)


def _gather_windows(shard_shape, col_sharded):
    R, C = shard_shape

    def windows(ref, x, y, cc):
        win = _gather_win(ref, R, C, col_sharded)
        return [win(2 * cx + cy, cc) for cx, cy in _other_chips(x, y)]

    return windows


def _slot_windows(ref, x, y, cc):
    return [ref.at[i, cc] for i in range(4)]


def _pass_start(landed, windows, n, token, name):
    def body(land_ref, tok_ref, send_sems, recv_sems, land_thru, tok_out):
        x, y, c = _pos()
        for k, w in enumerate(windows(land_ref, x, y, c)):
            _remote(w, w, send_sems, recv_sems, k, (x, y, 1 - c)).start()
        tok_out[...] = tok_ref[...]

    return _pcall(
        body, name=name,
        out_shape=(pltpu.SemaphoreType.DMA((n,)), pltpu.SemaphoreType.DMA((n,)),
                   pltpu.HBM(landed.shape, landed.dtype), TOKEN),
        in_specs=(HBM_BLK, VMEM_BLK), out_specs=(SEM_BLK, SEM_BLK, HBM_BLK, VMEM_BLK),
        input_output_aliases={0: 2},
        compiler_params=pltpu.CompilerParams(has_side_effects=DATAFLOW),
    )(_in_hbm(landed), token)


def _pass_wait(started, windows, after, name):
    send_sems, recv_sems, land_thru, _ = started

    def body(land_ref, send_sems, recv_sems, after_ref, got_ref):
        x, y, c = _pos()
        mine, theirs = windows(land_ref, x, y, c), windows(land_ref, x, y, 1 - c)
        for k, (src, dst) in enumerate(zip(mine, theirs)):
            cp = _remote(src, dst, send_sems, recv_sems, k, (x, y, 1 - c))
            cp.wait_send()
            cp.wait_recv()

    return _pcall(
        body, name=name, out_shape=pltpu.HBM(land_thru.shape, land_thru.dtype),
        in_specs=(HBM_BLK, SEM_BLK, SEM_BLK, HBM_SPEC), out_specs=HBM_BLK, input_output_aliases={0: 0},
        compiler_params=pltpu.CompilerParams(has_side_effects=DATAFLOW),
    )(land_thru, send_sems, recv_sems, after)


def _gather_small(buf, name):
    rows = buf.shape[0]

    def body(b_ref, o_ref, send_sems, recv_sems):
        x, y, c = _pos()
        jme = 2 * x + y
        chips = _other_chips(x, y)
        o_ref[jme] = b_ref[...]
        sends = [_remote(b_ref, o_ref.at[jme], send_sems, recv_sems, k, (cx, cy, c))
                 for k, (cx, cy) in enumerate(chips)]
        for cp in sends:
            cp.start()
        for k, (cx, cy) in enumerate(chips):
            w = o_ref.at[2 * cx + cy]
            _remote(w, w, send_sems, recv_sems, k, (x, y, c)).wait_recv()
        for cp in sends:
            cp.wait_send()

    vm = pl.BlockSpec(memory_space=pltpu.VMEM)
    return _pcall(body, name=name, in_specs=[vm], out_specs=vm,
                  out_shape=jax.ShapeDtypeStruct((4, rows, V7X_LANES), buf.dtype),
                  scratch_shapes=[pltpu.SemaphoreType.DMA((3,)), pltpu.SemaphoreType.DMA((3,))],
                  compiler_params=_params(None, 16 * rows * V7X_LANES * 4))(buf)


def _allreduce_small(buf, name):
    rows = buf.shape[0]
    rh = rows // 2

    def body(b_ref, o_ref, pair, mine, slots, send_sems, recv_sems):
        x, y, c = _pos()
        me, sib, jme = (x, y, c), (x, y, 1 - c), 2 * x + y
        half = pl.ds(pl.multiple_of(c * rh, 8), rh)
        other = pl.ds(pl.multiple_of((1 - c) * rh, 8), rh)
        to_sib = _remote(b_ref, pair.at[c], send_sems, recv_sems, 0, sib)
        to_sib.start()
        pair[c] = b_ref[...]
        _remote(pair.at[1 - c], pair.at[1 - c], send_sems, recv_sems, 0, me).wait_recv()
        mine[...] = pair[0, half, :] + pair[1, half, :]
        chips = _other_chips(x, y)
        sends = [_remote(mine, slots.at[jme], send_sems, recv_sems, 1 + k, (cx, cy, c))
                 for k, (cx, cy) in enumerate(chips)]
        for cp in sends:
            cp.start()
        slots[jme] = mine[...]
        for k, (cx, cy) in enumerate(chips):
            w = slots.at[2 * cx + cy]
            _remote(w, w, send_sems, recv_sems, 1 + k, me).wait_recv()
        o_ref[half, :] = (slots[0] + slots[1]) + (slots[2] + slots[3])
        back = _remote(o_ref.at[half, :], o_ref.at[half, :], send_sems, recv_sems, 4, sib)
        back.start()
        _remote(o_ref.at[other, :], o_ref.at[other, :], send_sems, recv_sems, 4, me).wait_recv()
        for cp in [to_sib, back] + sends:
            cp.wait_send()

    vm = pl.BlockSpec(memory_space=pltpu.VMEM)
    return _pcall(body, name=name, in_specs=[vm], out_specs=vm,
                  out_shape=jax.ShapeDtypeStruct(buf.shape, buf.dtype),
                  scratch_shapes=[pltpu.VMEM((2, rows, V7X_LANES), buf.dtype),
                                  pltpu.VMEM((rh, V7X_LANES), buf.dtype),
                                  pltpu.VMEM((4, rh, V7X_LANES), buf.dtype),
                                  pltpu.SemaphoreType.DMA((5,)), pltpu.SemaphoreType.DMA((5,))],
                  compiler_params=_params(None, 10 * rows * V7X_LANES * 4))(buf)


def _adamw_math(w, g, m, v):
    m = ADAM_B1 * m + (1.0 - ADAM_B1) * g
    v = ADAM_B2 * v + (1.0 - ADAM_B2) * (g * g)
    m_hat = m / (1.0 - ADAM_B1 ** ADAM_STEP)
    v_hat = v / (1.0 - ADAM_B2 ** ADAM_STEP)
    delta = -ADAM_LR * (m_hat / (jnp.sqrt(v_hat) + ADAM_EPS) + ADAM_WD * w)
    return delta, m, v


def _adamw_big(w, m, v, slots, name):
    R, C = w.shape
    by_rows = slots.shape[3] == C
    tr = _tile(R, 32, 16)

    def kern(w_ref, m_ref, v_ref, s_ref, g_ref, d_ref, mo_ref, vo_ref):
        def chip_sum(h):
            g = s_ref[0, h].astype(F32)
            for i in range(1, 4):
                g = g + s_ref[i, h].astype(F32)
            return g

        g = chip_sum(0) if by_rows else jnp.concatenate([chip_sum(0), chip_sum(1)], axis=1)
        d, mn, vn = _adamw_math(w_ref[...], g, m_ref[...], v_ref[...])
        g_ref[...], d_ref[...], mo_ref[...], vo_ref[...] = g, d, mn, vn

    row = pl.BlockSpec((tr, C), lambda i: (i, 0))
    shp = jax.ShapeDtypeStruct((R, C), F32)
    if by_rows:
        per_half = R // 2 // tr
        s_spec = pl.BlockSpec((4, 1, tr, C), lambda i: (0, i // per_half, i % per_half, 0))
    else:
        s_spec = pl.BlockSpec((4, 2, tr, C // 2), lambda i: (0, 0, i, 0))
    return _pcall(kern, name=name, grid=(R // tr,),
                  in_specs=[row, row, row, s_spec], out_specs=[row] * 4, out_shape=[shp] * 4,
                  compiler_params=_params(("parallel",), 36 * tr * C * 4))(w, m, v, slots)


def _adamw_small(w, g, m, v, name):
    def kern(w_ref, g_ref, m_ref, v_ref, d_ref, mo_ref, vo_ref):
        d_ref[...], mo_ref[...], vo_ref[...] = _adamw_math(w_ref[...], g_ref[...], m_ref[...], v_ref[...])

    vm = pl.BlockSpec(memory_space=pltpu.VMEM)
    shp = jax.ShapeDtypeStruct(w.shape, F32)
    return _pcall(kern, name=name, in_specs=[vm] * 4, out_specs=[vm] * 3, out_shape=[shp] * 3,
                  compiler_params=_params(None, 10 * w.size * 4))(w, g, m, v)


def _pack(arrs):
    flat = jnp.concatenate([a.reshape(-1).astype(F32) for a in arrs])
    n = flat.shape[0]
    rows = -(-n // (16 * V7X_LANES)) * 16
    return jnp.pad(flat, (0, rows * V7X_LANES - n)).reshape(rows, V7X_LANES)


def _unpack(buf, shapes):
    flat = buf.reshape(-1)
    out, off = [], 0
    for s in shapes:
        n = int(np.prod(s))
        out.append(flat[off:off + n].reshape(s))
        off += n
    return out


def _cut_gradient(a, d, col_sharded, tm, tn, token, tag, other_work):
    c = lax.axis_index("c").astype(jnp.int32)
    for_sibling = _matmul_tn_half(a, d, (1 - c).reshape(1), None, tm, tn, col_sharded, "mm_g_%s_sibling" % tag)
    sent = _pair_start(for_sibling, token, "pair_start_" + tag)
    other, last = other_work(sent[4])
    arrived = _pair_wait(sent, last, "pair_wait_" + tag)
    cs = _matmul_tn_half(a, d, c.reshape(1), arrived, tm, tn, col_sharded, "mm_g_%s_own" % tag)
    cs = cs.reshape((1,) + cs.shape if col_sharded else (4, cs.shape[0] // 4, cs.shape[1]))
    slots = _own_piece_into_slots(cs, col_sharded, "own_piece_" + tag)
    return _scatter_start(cs, slots, col_sharded, sent[4], "scatter_start_" + tag), other


def _reduce_finish(started, after, col_sharded, w, m, v, tag):
    _, landed = _scatter_wait(started, after, col_sharded, "scatter_wait_" + tag)
    slots = _scatter_pass_on(landed, "scatter_pass_on_" + tag)
    return _adamw_big(w, m, v, slots, "adamw_" + tag)


def kernel(x, ln1_w, w_in, lb_gamma, hg_norm_w, lru_conv_w, lru_conv_b, lru_wa, lru_ba, lru_wx, lru_bx, lru_lambda, lru_norm_w, w_out, ln2_w, ffn_w_up, ffn_conv_w, ffn_conv_b, ffn_w_down, final_norm_w, loss_target, m_ln1_w, m_w_in, m_lb_gamma, m_hg_norm_w, m_lru_conv_w, m_lru_conv_b, m_lru_wa, m_lru_ba, m_lru_wx, m_lru_bx, m_lru_lambda, m_lru_norm_w, m_w_out, m_ln2_w, m_ffn_w_up, m_ffn_conv_w, m_ffn_conv_b, m_ffn_w_down, m_final_norm_w, v_ln1_w, v_w_in, v_lb_gamma, v_hg_norm_w, v_lru_conv_w, v_lru_conv_b, v_lru_wa, v_lru_ba, v_lru_wx, v_lru_bx, v_lru_lambda, v_lru_norm_w, v_w_out, v_ln2_w, v_ffn_w_up, v_ffn_conv_w, v_ffn_conv_b, v_ffn_w_down, v_final_norm_w):
    B, S, D = x.shape
    T = B * S
    HW = lb_gamma.shape[1]
    LW = lru_conv_b.shape[1]
    assert S % CHUNK == 0 and HW % HEAD_DIM == 0 and lru_wa.shape[2] == HEAD_DIM
    x2 = x.reshape(T, D)
    tgt = loss_target.reshape(T, D)
    jchip = 2 * lax.axis_index("x") + lax.axis_index("y")

    conv_shapes = [lru_conv_w[0].shape, ffn_conv_w[0].shape]
    convs = _gather_small(_pack([lru_conv_w[0], ffn_conv_w[0]]), "gather_conv_w")
    per_chip = [_unpack(convs[j], conv_shapes) for j in range(4)]
    lcw = jnp.concatenate([pc[0] for pc in per_chip], axis=1)
    fcw = jnp.concatenate([pc[1] for pc in per_chip], axis=1)
    masters = dict(w_in=w_in[0], w_out=w_out[0], w_up=ffn_w_up[0], w_down=ffn_w_down[0])
    col_of = dict(w_in=True, w_out=False, w_up=True, w_down=False)
    started, token, after = {}, jnp.zeros(TOKEN.shape, F32), convs
    for n in ("w_in", "w_out", "w_up", "w_down"):
        land = _cast_into_window(masters[n], col_of[n], after, "cast_" + n)
        started[n] = _gather_start(land, masters[n].shape, col_of[n], token, "gather_start_" + n)
        token = after = started[n][3]

    def landed(n, after):
        return _gather_wait(started[n], masters[n].shape, after, col_of[n], "gather_wait_" + n)

    def pass_on_start(n, after, tok):
        wins = _gather_windows(masters[n].shape, col_of[n])
        return _pass_start(landed(n, after), wins, 3, tok, "gather_pass_start_" + n)

    def pass_on_wait(n, sent, after):
        return _pass_wait(sent, _gather_windows(masters[n].shape, col_of[n]), after, "gather_pass_wait_" + n)

    W_in = _gather_pass_on(landed("w_in", token), masters["w_in"].shape, True, "gather_pass_on_w_in")

    hn1 = _rms_fwd(x2, ln1_w, "rms1")
    proj = _matmul(hn1, W_in, "NN", F32, 1024, 512, 4096, name="mm_proj")
    sent_out = pass_on_start("w_out", proj, token)
    o_raw, o_hg, states = _hgrn_fwd(proj, lb_gamma, hg_norm_w, B, HW, "hgrn_fwd")
    h_lru, z = _lru_fwd(proj, lcw, lru_conv_b, lru_wa[0], lru_ba, lru_wx[0], lru_bx, lru_lambda, B, HW, LW, "lru_fwd")
    o_lru = _rms_fwd(z, lru_norm_w, "rms_lru")
    mix = jnp.concatenate([o_hg, o_lru], axis=1)
    sent_up = pass_on_start("w_up", mix, sent_out[3])
    W_out = pass_on_wait("w_out", sent_out, sent_up[3])
    h1 = _matmul(mix, W_out, "NN", F32, 1024, 512, 4096, add=x2, name="mm_out")
    hn2 = _rms_fwd(h1, ln2_w, "rms2")
    W_up = pass_on_wait("w_up", sent_up, hn2)
    up = _matmul(hn2, W_up, "NN", F32, 1024, 512, 4096, name="mm_up")
    act, act_dg, act_dv = _ffn_act(up, fcw, ffn_conv_b, B, "ffn_act")
    W_down = _gather_pass_on(landed("w_down", act), masters["w_down"].shape, False, "gather_pass_on_w_down")
    h2 = _matmul(act, W_down, "NN", F32, 1024, 512, 5504, add=h1, name="mm_down")
    token = sent_up[3]

    dh2, dh2a, d_final_w, loss_part = _loss_bwd(h2, tgt, final_norm_w.reshape(1, D), "loss_bwd")
    def through_w_down(tok):
        d = _matmul(dh2a, W_down, "NT", ACT_DTYPE, 512, 5504, 512, after=tok, name="mm_d_act")
        return d, d

    red_down, d_act = _cut_gradient(act, dh2a, False, 256, 1024, token, "w_down", through_w_down)
    d_up, d_fcw, d_fcb = _ffn_act_bwd(up, fcw, act_dg, act_dv, d_act, B, "ffn_act_bwd")
    def through_w_up(tok):
        d = _matmul(d_up, W_up, "NT", F32, 2048, 1024, 512, after=tok, name="mm_d_hn2")
        return d, d

    red_up, d_hn2 = _cut_gradient(hn2, d_up, True, 1024, 512, red_down[4], "w_up", through_w_up)
    dh1, dh1a, d_ln2 = _rms_bwd(h1, ln2_w, d_hn2, 0, dh2, True, "rms2_bwd", after=red_up[4])

    def through_w_out(tok):
        d = _matmul(dh1a, W_out, "NT", F32, 1024, 512, 4096, after=tok, name="mm_d_mix")
        return d, d

    red_out, d_mix = _cut_gradient(mix, dh1a, False, 1024, 512, red_up[4], "w_out", through_w_out)
    dz, d_lru_norm = _rms_bwd(z, lru_norm_w, d_mix, HW // LW, None, False, "rms_lru_bwd")
    (d_xr, d_yr, d_wa, d_wx, d_ba, d_bx, d_lam, d_lcw, d_lcb) = _lru_bwd(
        proj, lcw, lru_conv_b, lru_wa[0], lru_ba, lru_wx[0], lru_bx, lru_lambda, h_lru, dz, B, HW, LW, "lru_bwd")
    d_q, d_f, d_i, d_g, d_lbg, d_hgw = _hgrn_bwd(proj, lb_gamma, hg_norm_w, o_raw, states, d_mix, B, HW, "hgrn_bwd")
    d_proj = jnp.concatenate([d_q, d_f, d_i, d_g, d_xr, d_yr], axis=1)
    small_names = ["ln1_w", "lb_gamma", "hg_norm_w", "lru_conv_w", "lru_conv_b", "lru_wa", "lru_ba", "lru_wx",
                   "lru_bx", "lru_lambda", "lru_norm_w", "ln2_w", "ffn_conv_w", "ffn_conv_b", "final_norm_w"]
    small_rest = [d_lbg, d_hgw, d_lcw, d_lcb, d_wa, d_ba, d_wx, d_bx, d_lam, d_lru_norm, d_ln2, d_fcw, d_fcb, d_final_w]

    def through_w_in(tok):
        d_hn1 = _matmul(d_proj, W_in, "NT", F32, 2048, 1024, 1024, after=tok, name="mm_d_hn1")
        dx, d_ln1 = _rms_bwd(x2, ln1_w, d_hn1, 0, dh1, False, "rms1_bwd")
        red = _allreduce_small(_pack([loss_part[0:1, 0:1], d_ln1] + small_rest), "allreduce_small")
        return (dx, d_ln1, red), red

    red_in, (dx, d_ln1, red) = _cut_gradient(hn1, d_proj, True, 1024, 512, red_out[4], "w_in", through_w_in)
    small_grads = [d_ln1] + small_rest

    def slots_on_their_way(red, col_sharded, after, tok, tag):
        _, got = _scatter_wait(red, after, col_sharded, "scatter_wait_" + tag)
        return _pass_start(got, _slot_windows, 4, tok, "scatter_pass_start_" + tag)

    def update(sent, after, w, m, v, tag):
        slots = _pass_wait(sent, _slot_windows, after, "scatter_pass_wait_" + tag)
        return _adamw_big(w, m, v, slots, "adamw_" + tag)

    sent_down = slots_on_their_way(red_down, False, red_in[4], red_in[4], "w_down")
    sent_up = slots_on_their_way(red_up, True, sent_down[3], sent_down[3], "w_up")
    sent_out = slots_on_their_way(red_out, False, sent_up[3], sent_up[3], "w_out")
    big = {}
    big["ffn_w_down"] = update(sent_down, sent_out[3], ffn_w_down[0], m_ffn_w_down[0], v_ffn_w_down[0], "w_down")
    big["ffn_w_up"] = update(sent_up, big["ffn_w_down"][1], ffn_w_up[0], m_ffn_w_up[0], v_ffn_w_up[0], "w_up")
    big["w_out"] = update(sent_out, big["ffn_w_up"][1], w_out[0], m_w_out[0], v_w_out[0], "w_out")
    big["w_in"] = _reduce_finish(red_in, big["w_out"][1], True, w_in[0], m_w_in[0], v_w_in[0], "w_in")

    red = _unpack(red, [(1, 1)] + [g.shape for g in small_grads])
    loss = red[0].reshape(())
    gs = dict(zip(small_names, red[1:]))
    nlc, nfc = lru_conv_w.shape[2], ffn_conv_w.shape[2]
    gs["lru_conv_w"] = lax.dynamic_slice_in_dim(gs["lru_conv_w"], jchip * nlc, nlc, axis=1)
    gs["ffn_conv_w"] = lax.dynamic_slice_in_dim(gs["ffn_conv_w"], jchip * nfc, nfc, axis=1)
    args = dict(ln1_w=(ln1_w, m_ln1_w, v_ln1_w), lb_gamma=(lb_gamma, m_lb_gamma, v_lb_gamma),
                hg_norm_w=(hg_norm_w, m_hg_norm_w, v_hg_norm_w), lru_conv_w=(lru_conv_w, m_lru_conv_w, v_lru_conv_w),
                lru_conv_b=(lru_conv_b, m_lru_conv_b, v_lru_conv_b), lru_wa=(lru_wa, m_lru_wa, v_lru_wa),
                lru_ba=(lru_ba, m_lru_ba, v_lru_ba), lru_wx=(lru_wx, m_lru_wx, v_lru_wx),
                lru_bx=(lru_bx, m_lru_bx, v_lru_bx), lru_lambda=(lru_lambda, m_lru_lambda, v_lru_lambda),
                lru_norm_w=(lru_norm_w, m_lru_norm_w, v_lru_norm_w), ln2_w=(ln2_w, m_ln2_w, v_ln2_w),
                ffn_conv_w=(ffn_conv_w, m_ffn_conv_w, v_ffn_conv_w), ffn_conv_b=(ffn_conv_b, m_ffn_conv_b, v_ffn_conv_b),
                final_norm_w=(final_norm_w, m_final_norm_w, v_final_norm_w))
    shapes = [args[n][0].shape for n in small_names]
    upd = _adamw_small(_pack([args[n][0] for n in small_names]), _pack([gs[n] for n in small_names]),
                       _pack([args[n][1] for n in small_names]), _pack([args[n][2] for n in small_names]), "adamw_small")
    s_delta, s_m, s_v = (dict(zip(small_names, _unpack(u, shapes))) for u in upd)

    order = ["ln1_w", "w_in", "lb_gamma", "hg_norm_w", "lru_conv_w", "lru_conv_b", "lru_wa", "lru_ba", "lru_wx",
             "lru_bx", "lru_lambda", "lru_norm_w", "w_out", "ln2_w", "ffn_w_up", "ffn_conv_w", "ffn_conv_b",
             "ffn_w_down", "final_norm_w"]
    full_shape = dict(w_in=w_in.shape, w_out=w_out.shape, ffn_w_up=ffn_w_up.shape, ffn_w_down=ffn_w_down.shape)
    grads, deltas, new_m, new_v = [], [], [], []
    for n in order:
        if n in big:
            g, d, mn, vn = (t.reshape(full_shape[n]) for t in big[n])
        else:
            g, d, mn, vn = gs[n].reshape(args[n][0].shape), s_delta[n], s_m[n], s_v[n]
        grads.append(g), deltas.append(d), new_m.append(mn), new_v.append(vn)
    return (loss, dx.reshape(B, S, D), *grads, *deltas, *new_m, *new_v)
```

```python
import functools
import math

import numpy as np
import jax
import jax.numpy as jnp
from jax import lax
from jax.experimental import pallas as pl
from jax.experimental.pallas import tpu as pltpu

F32 = jnp.float32
MXU_DTYPE = jnp.bfloat16
ACT_DTYPE = jnp.bfloat16

EPS = 1e-6
HEAD_DIM = 128
CHUNK = 64
LEVEL_HALVES = (32, 16, 8, 4, 2, 1)
LRU_CONV = 4
FFN_CONV = 3
LRU_C = 8.0
ADAM_LR, ADAM_B1, ADAM_B2, ADAM_EPS, ADAM_WD, ADAM_STEP = 0.001, 0.9, 0.999, 1e-08, 0.01, 10

V7X_LANES = 128
V7X_VMEM_BUDGET = 56 << 20

NN = (((1,), (0,)), ((), ()))
NT = (((1,), (1,)), ((), ()))
TN = (((0,), (0,)), ((), ()))
MESH = pl.DeviceIdType.MESH
HBM_SPEC = pl.BlockSpec(memory_space=pl.ANY)


def _pcall(kern, **kw):
    return pl.pallas_call(kern, **kw)


def _params(sem=None, vmem=None):
    kw = {}
    if sem is not None:
        kw["dimension_semantics"] = sem
    if vmem is not None:
        kw["vmem_limit_bytes"] = int(min(max(vmem, 16 << 20), V7X_VMEM_BUDGET))
    return pltpu.CompilerParams(**kw)


def _dot(a, b, dims=NN):
    return lax.dot_general(a.astype(MXU_DTYPE), b.astype(MXU_DTYPE), dims, preferred_element_type=F32)


def _tile(dim, pref, align):
    t = min(pref, dim) // align * align
    while t >= align:
        if dim % t == 0:
            return t
        t -= align
    return dim


def _sigmoid(x):
    return 1.0 / (1.0 + jnp.exp(-x))


def _silu_and_grad(x):
    s = _sigmoid(x)
    return x * s, s * (1.0 + x * (1.0 - s))


def _gelu_and_grad(x):
    k0, k1 = math.sqrt(2.0 / math.pi), 0.044715
    t = jnp.tanh(k0 * (x + k1 * x * x * x))
    g = 0.5 * x * (1.0 + t)
    dg = 0.5 * (1.0 + t) + 0.5 * x * (1.0 - t * t) * k0 * (1.0 + 3.0 * k1 * x * x)
    return g, dg


def _one_minus_exp(x):
    p = x * (1.0 + x * (0.5 + x * (1.0 / 6.0 + x * (1.0 / 24.0 + x * (1.0 / 120.0)))))
    return jnp.where(x > -0.05, -p, 1.0 - jnp.exp(x))


def _rows(n):
    return lax.broadcasted_iota(jnp.int32, (n, 1), 0)


def _matmul(a, b, mode, out_dtype, tm, tn, tk, add=None, after=None, n_outer=False, name="mm"):
    if mode == "TN":
        K, M = a.shape
    else:
        M, K = a.shape
    N = b.shape[0] if mode == "NT" else b.shape[1]
    tm, tn = _tile(M, tm, V7X_LANES), _tile(N, tn, V7X_LANES)
    tk = _tile(K, tk, V7X_LANES)
    nk = K // tk
    dims = {"NN": NN, "NT": NT, "TN": TN}[mode]
    order = (lambda f: (lambda j, i, k: f(i, j, k))) if n_outer else (lambda f: f)
    a_spec = (pl.BlockSpec((tk, tm), order(lambda i, j, k: (k, i))) if mode == "TN"
              else pl.BlockSpec((tm, tk), order(lambda i, j, k: (i, k))))
    b_spec = (pl.BlockSpec((tn, tk), order(lambda i, j, k: (j, k))) if mode == "NT"
              else pl.BlockSpec((tk, tn), order(lambda i, j, k: (k, j))))
    o_spec = pl.BlockSpec((tm, tn), order(lambda i, j, k: (i, j)))
    has_add = add is not None

    def kern(*refs):
        a_ref, b_ref = refs[:2]
        add_ref = refs[2] if has_add else None

        def finish(r, o_ref):
            if has_add:
                r = r + add_ref[...]
            o_ref[...] = r.astype(out_dtype)

        if nk == 1:
            finish(_dot(a_ref[...], b_ref[...], dims), refs[-1])
            return
        o_ref, acc_ref = refs[-2:]
        k = pl.program_id(2)

        @pl.when(k == 0)
        def _():
            acc_ref[...] = jnp.zeros_like(acc_ref)

        acc_ref[...] += _dot(a_ref[...], b_ref[...], dims)

        @pl.when(k == nk - 1)
        def _():
            finish(acc_ref[...], o_ref)

    ab = jnp.dtype(a.dtype).itemsize
    ob = jnp.dtype(out_dtype).itemsize
    vmem = 2 * (tm * tk + tk * tn) * ab + tm * tn * (8 + 2 * ob + (8 if has_add else 0)) + (4 << 20)
    ins = [a, b] + ([add] if has_add else []) + ([after] if after is not None else [])
    in_specs = [a_spec, b_spec] + ([o_spec] if has_add else []) + ([HBM_SPEC] if after is not None else [])
    grid = (N // tn, M // tm, nk) if n_outer else (M // tm, N // tn, nk)
    return _pcall(
        kern, name=name, grid=grid,
        in_specs=in_specs, out_specs=o_spec,
        out_shape=jax.ShapeDtypeStruct((M, N), out_dtype),
        scratch_shapes=[pltpu.VMEM((tm, tn), F32)] if nk > 1 else [],
        compiler_params=_params(("parallel", "parallel", "arbitrary"), vmem),
    )(*ins)


def _matmul_tn_half(a, b, half, add, tm, tn, by_rows, name):
    K, M = a.shape
    N = b.shape[1]
    Mo, No = (M // 2, N) if by_rows else (M, N // 2)
    tm, tn = _tile(Mo, tm, V7X_LANES), _tile(No, tn, V7X_LANES)
    nm, nn = Mo // tm, No // tn
    has_add = add is not None

    def kern(h_ref, a_ref, b_ref, *rest):
        r = _dot(a_ref[...], b_ref[...], TN)
        if has_add:
            r = r + rest[0][...].astype(F32)
        rest[-1][...] = r.astype(ACT_DTYPE)

    if by_rows:
        a_map, b_map = (lambda i, j, h: (0, h[0] * nm + i)), (lambda i, j, h: (0, j))
        o_map, grid = (lambda i, j, h: (i, j)), (nm, nn)
    else:
        a_map, b_map = (lambda j, i, h: (0, i)), (lambda j, i, h: (0, h[0] * nn + j))
        o_map, grid = (lambda j, i, h: (i, j)), (nn, nm)
    blk = pl.BlockSpec((tm, tn), o_map)
    grid_spec = pltpu.PrefetchScalarGridSpec(
        num_scalar_prefetch=1, grid=grid,
        in_specs=[pl.BlockSpec((K, tm), a_map), pl.BlockSpec((K, tn), b_map)] + ([blk] if has_add else []),
        out_specs=blk)
    ab = jnp.dtype(a.dtype).itemsize
    vmem = 2 * K * (tm + tn) * ab + tm * tn * 16 + (4 << 20)
    return _pcall(kern, name=name, grid_spec=grid_spec, out_shape=jax.ShapeDtypeStruct((Mo, No), ACT_DTYPE),
                  compiler_params=_params(("parallel", "parallel"), vmem))(half, a, b, *([add] if has_add else []))


def _matmul_col_slice(a, b, j, width, prev, tm, tn, name):
    M, K = a.shape
    N = b.shape[1]
    tm, tn = _tile(M, tm, V7X_LANES), _tile(width, tn, V7X_LANES)
    nn = width // tn
    has_prev = prev is not None

    def kern(j_ref, a_ref, b_ref, *rest):
        rest[-1][...] = _dot(a_ref[...], b_ref[...])

    grid_spec = pltpu.PrefetchScalarGridSpec(
        num_scalar_prefetch=1, grid=(M // tm, nn),
        in_specs=[pl.BlockSpec((tm, K), lambda i, n, j: (i, 0)),
                  pl.BlockSpec((K, tn), lambda i, n, j: (0, j[0] * nn + n))] + ([HBM_SPEC] if has_prev else []),
        out_specs=pl.BlockSpec((tm, tn), lambda i, n, j: (i, j[0] * nn + n)))
    ab = jnp.dtype(a.dtype).itemsize
    vmem = 2 * K * (tm + tn) * ab + tm * tn * 16 + (4 << 20)
    return _pcall(kern, name=name, grid_spec=grid_spec, out_shape=jax.ShapeDtypeStruct((M, N), F32),
                  input_output_aliases={3: 0} if has_prev else {},
                  compiler_params=_params(("parallel", "parallel"), vmem))(j, a, b, *([prev] if has_prev else []))


def _rms_fwd(x, w, name):
    T, D = x.shape
    tm = _tile(T, 256, 16)

    def kern(x_ref, w_ref, o_ref):
        xv = x_ref[...]
        r = lax.rsqrt(jnp.mean(xv * xv, axis=-1, keepdims=True) + EPS)
        o_ref[...] = (xv * r * w_ref[...]).astype(ACT_DTYPE)

    return _pcall(kern, name=name, grid=(T // tm,),
                  in_specs=[pl.BlockSpec((tm, D), lambda i: (i, 0)), pl.BlockSpec((1, D), lambda i: (0, 0))],
                  out_specs=pl.BlockSpec((tm, D), lambda i: (i, 0)),
                  out_shape=jax.ShapeDtypeStruct((T, D), ACT_DTYPE),
                  compiler_params=_params(("parallel",), 8 * tm * D * 4))(x, w)


def _rms_bwd(x, w, g, g_col, res, want_act, name, after=None):
    T, D = x.shape
    tm = _tile(T, 256, 16)
    has_res = res is not None

    def kern(*refs):
        refs = list(refs)
        x_ref, w_ref, g_ref = refs[:3]
        res_ref = refs[3] if has_res else None
        outs = refs[3 + has_res + (after is not None):]
        dx_ref = outs[0]
        dxa_ref = outs[1] if want_act else None
        dw_ref = outs[-1]
        i = pl.program_id(0)
        xv = x_ref[...]
        gv = g_ref[...].astype(F32)
        r = lax.rsqrt(jnp.mean(xv * xv, axis=-1, keepdims=True) + EPS)
        gw = gv * w_ref[...]
        dx = r * gw - xv * (r * r * r) * jnp.mean(gw * xv, axis=-1, keepdims=True)
        if has_res:
            dx = dx + res_ref[...]
        dx_ref[...] = dx
        if want_act:
            dxa_ref[...] = dx.astype(ACT_DTYPE)

        @pl.when(i == 0)
        def _():
            dw_ref[...] = jnp.zeros_like(dw_ref)

        dw_ref[...] += jnp.sum(gv * xv * r, axis=0, keepdims=True)

    row = pl.BlockSpec((tm, D), lambda i: (i, 0))
    vec = pl.BlockSpec((1, D), lambda i: (0, 0))
    in_specs = ([row, vec, pl.BlockSpec((tm, D), lambda i: (i, g_col))] + ([row] if has_res else [])
                + ([HBM_SPEC] if after is not None else []))
    out_specs = [row] + ([row] if want_act else []) + [vec]
    out_shape = ([jax.ShapeDtypeStruct((T, D), F32)]
                 + ([jax.ShapeDtypeStruct((T, D), ACT_DTYPE)] if want_act else [])
                 + [jax.ShapeDtypeStruct((1, D), F32)])
    ins = [x, w, g] + ([res] if has_res else []) + ([after] if after is not None else [])
    return _pcall(kern, name=name, grid=(T // tm,), in_specs=in_specs, out_specs=out_specs,
                  out_shape=out_shape, compiler_params=_params(("arbitrary",), 14 * tm * D * 4))(*ins)


def _loss_bwd(h, target, w, name):
    T, D = h.shape
    tm = _tile(T, 256, 16)

    def kern(h_ref, t_ref, w_ref, dh_ref, dha_ref, dw_ref, loss_ref):
        i = pl.program_id(0)
        hv = h_ref[...]
        r = lax.rsqrt(jnp.mean(hv * hv, axis=-1, keepdims=True) + EPS)
        e = hv * r * w_ref[...] - t_ref[...]
        dy = e * (1.0 / D)
        gw = dy * w_ref[...]
        dh = r * gw - hv * (r * r * r) * jnp.mean(gw * hv, axis=-1, keepdims=True)
        dh_ref[...] = dh
        dha_ref[...] = dh.astype(ACT_DTYPE)

        @pl.when(i == 0)
        def _():
            dw_ref[...] = jnp.zeros_like(dw_ref)
            loss_ref[...] = jnp.zeros_like(loss_ref)

        dw_ref[...] += jnp.sum(dy * hv * r, axis=0, keepdims=True)
        part = 0.5 * jnp.sum(jnp.mean(e * e, axis=-1, keepdims=True), axis=0, keepdims=True)
        loss_ref[...] += jnp.broadcast_to(part, loss_ref.shape)

    row = pl.BlockSpec((tm, D), lambda i: (i, 0))
    vec = pl.BlockSpec((1, D), lambda i: (0, 0))
    return _pcall(kern, name=name, grid=(T // tm,), in_specs=[row, row, vec],
                  out_specs=[row, row, vec, pl.BlockSpec((8, V7X_LANES), lambda i: (0, 0))],
                  out_shape=[jax.ShapeDtypeStruct((T, D), F32), jax.ShapeDtypeStruct((T, D), ACT_DTYPE),
                             jax.ShapeDtypeStruct((1, D), F32), jax.ShapeDtypeStruct((8, V7X_LANES), F32)],
                  compiler_params=_params(("arbitrary",), 14 * tm * D * 4))(h, target, w)


def _conv(x, w_ref, b, width):
    S = x.shape[0]
    row = _rows(S)
    y = b + x * w_ref[pl.ds(width - 1, 1), :]
    for j in range(width - 1):
        sh = width - 1 - j
        y = y + jnp.where(row >= sh, pltpu.roll(x, sh, 0), 0.0) * w_ref[pl.ds(j, 1), :]
    return y


def _conv_bwd(x, dy, w_ref, width):
    S = dy.shape[0]
    row = _rows(S)
    dx = dy * w_ref[pl.ds(width - 1, 1), :]
    dw = [None] * (width - 1) + [jnp.sum(x * dy, axis=0, keepdims=True)]
    for j in range(width - 1):
        sh = width - 1 - j
        dys = jnp.where(row < S - sh, pltpu.roll(dy, S - sh, 0), 0.0)
        dx = dx + dys * w_ref[pl.ds(j, 1), :]
        dw[j] = jnp.sum(x * dys, axis=0, keepdims=True)
    return dx, dw


def _ffn_act(up, cw, cb, B, name):
    T, F2 = up.shape
    S, F = T // B, F2 // 2
    tw = _tile(F, 256, V7X_LANES)
    nt = F // tw

    def kern(g_ref, v_ref, wg_ref, wv_ref, bg_ref, bv_ref, o_ref, mg_ref, mv_ref):
        gc = _conv(g_ref[...], wg_ref, bg_ref[...], FFN_CONV)
        vc = _conv(v_ref[...], wv_ref, bv_ref[...], FFN_CONV)
        silu, dsilu = _silu_and_grad(gc)
        o_ref[...] = (silu * vc).astype(ACT_DTYPE)
        mg_ref[...] = (vc * dsilu).astype(ACT_DTYPE)
        mv_ref[...] = silu.astype(ACT_DTYPE)

    blk = lambda off: pl.BlockSpec((S, tw), lambda b, i: (b, off + i))
    wblk = lambda off: pl.BlockSpec((FFN_CONV, tw), lambda b, i: (0, off + i))
    bblk = lambda off: pl.BlockSpec((1, tw), lambda b, i: (0, off + i))
    half = jax.ShapeDtypeStruct((T, F), ACT_DTYPE)
    return _pcall(kern, name=name, grid=(B, nt),
                  in_specs=[blk(0), blk(nt), wblk(0), wblk(nt), bblk(0), bblk(nt)],
                  out_specs=[blk(0), blk(0), blk(0)], out_shape=[half, half, half],
                  compiler_params=_params(("parallel", "parallel"), 20 * S * tw * 4))(up, up, cw, cw, cb, cb)


def _ffn_act_bwd(up, cw, mg, mv, d_act, B, name):
    T, F2 = up.shape
    S, F = T // B, F2 // 2
    tw = _tile(F, 256, V7X_LANES)
    nt = F // tw

    def kern(s_ref, ws_ref, mg_ref, mv_ref, da_ref, du_ref, dcw_ref, dcb_ref):
        t, b = pl.program_id(0), pl.program_id(1)
        mult = jnp.where(t < nt, mg_ref[...], mv_ref[...])
        d = da_ref[...].astype(F32) * mult.astype(F32)
        dx, dw = _conv_bwd(s_ref[...], d, ws_ref, FFN_CONV)
        du_ref[...] = dx.astype(ACT_DTYPE)

        @pl.when(b == 0)
        def _():
            dcw_ref[...] = jnp.zeros_like(dcw_ref)
            dcb_ref[...] = jnp.zeros_like(dcb_ref)

        for j, rj in enumerate(dw):
            dcw_ref[pl.ds(j, 1), :] += rj
        dcb_ref[...] += jnp.sum(d, axis=0, keepdims=True)

    own = lambda t, b: (b, t % nt)
    return _pcall(
        kern, name=name, grid=(2 * nt, B),
        in_specs=[pl.BlockSpec((S, tw), lambda t, b: (b, t)),
                  pl.BlockSpec((FFN_CONV, tw), lambda t, b: (0, t)),
                  pl.BlockSpec((S, tw), own), pl.BlockSpec((S, tw), own), pl.BlockSpec((S, tw), own)],
        out_specs=[pl.BlockSpec((S, tw), lambda t, b: (b, t)),
                   pl.BlockSpec((FFN_CONV, tw), lambda t, b: (0, t)),
                   pl.BlockSpec((1, tw), lambda t, b: (0, t))],
        out_shape=[jax.ShapeDtypeStruct((T, F2), ACT_DTYPE), jax.ShapeDtypeStruct((FFN_CONV, F2), F32),
                   jax.ShapeDtypeStruct((1, F2), F32)],
        compiler_params=_params(("parallel", "arbitrary"), 20 * S * tw * 4),
    )(up, cw, mg, mv, d_act)


def _hgrn_tables():
    C = CHUNK
    t = np.arange(C)
    mats = [(t[:, None] >= t[None, :]).astype(np.float32)]
    masks = []
    gsum = [(t[:, None] <= t[None, :]).astype(np.float32), (t[:, None] > t[None, :]).astype(np.float32)]
    for hs in LEVEL_HALVES:
        m = (t // (2 * hs)) * 2 * hs + hs
        later = t >= m
        d = np.zeros((C, C), np.float32)
        for i in range(C):
            if later[i]:
                d[i, m[i]:i + 1] = 1.0
            else:
                d[i, i + 1:m[i]] = -1.0
        mats.append(d)
        same = (t[:, None] // (2 * hs)) == (t[None, :] // (2 * hs))
        masks.append((same & later[:, None] & (~later)[None, :]).astype(np.float32))
        gsum.append((same & later[:, None] & (t[None, :] >= t[:, None])).astype(np.float32))
        gsum.append((same & (~later)[:, None] & (t[None, :] < t[:, None])).astype(np.float32))
    return np.concatenate(mats, 0), np.stack(masks, 0), np.concatenate(gsum, 1)


def _split_dot(mat, v):
    hi = v.astype(MXU_DTYPE)
    lo = (v - hi.astype(F32)).astype(MXU_DTYPE)
    r = _dot(mat, jnp.concatenate([hi, lo], axis=1))
    n = v.shape[1]
    return r[:, :n] + r[:, n:]


def _hgrn_gates(qr, fr, lb, mc):
    C = CHUNK
    q, dq_dqr = _silu_and_grad(qr)
    sf = _sigmoid(fr)
    f = lb + (1.0 - lb) * sf
    k = 1.0 - f
    dall = _split_dot(mc, jnp.log(f))
    b = dall[0:C]
    dl = [dall[C * (l + 1):C * (l + 2)] for l in range(len(LEVEL_HALVES))]
    eq = [jnp.exp(jnp.minimum(d, 0.0)) for d in dl]
    ek = [jnp.exp(jnp.minimum(-d, 0.0)) for d in dl]
    return q, dq_dqr, sf, f, k, b, eq, ek


def _hgrn_scores(q, k, eq, ek, masks_ref):
    p = jnp.where(_rows(CHUNK) == lax.broadcasted_iota(jnp.int32, (1, CHUNK), 1),
                  jnp.sum(q * k, axis=-1, keepdims=True), 0.0)
    for l in range(len(LEVEL_HALVES)):
        p = p + masks_ref[l] * _dot(q * eq[l], k * ek[l], NT)
    return p


def _hgrn_fwd(proj, lb_gamma, norm_w, B, HW, name):
    T = proj.shape[0]
    S, H, C = T // B, HW // HEAD_DIM, CHUNK
    NC = S // C
    mc_np, masks_np, _ = _hgrn_tables()
    mc, masks = jnp.asarray(mc_np, MXU_DTYPE), jnp.asarray(masks_np, F32)

    def kern(q_ref, f_ref, i_ref, g_ref, lbg_ref, nw_ref, mc_ref, masks_ref, oraw_ref, o_ref, st_ref):
        g0, g1 = lbg_ref[pl.ds(0, 1), :], lbg_ref[pl.ds(1, 1), :]
        mx = jnp.maximum(g0, g1)
        e0, e1 = jnp.exp(g0 - mx), jnp.exp(g1 - mx)
        lb = e0 / (e0 + e1)
        nw = nw_ref[...]
        mcv = mc_ref[...]

        def body(n, sts):
            out = []
            for s, st in enumerate(sts):
                rows = pl.ds(pl.multiple_of(s * S + n * C, C), C)
                st_ref[s, n] = st
                q, _, _, _, k, b, eq, ek = _hgrn_gates(q_ref[rows, :], f_ref[rows, :], lb, mcv)
                v = i_ref[rows, :]
                o = _dot(q * jnp.exp(b), st, NT) + _dot(_hgrn_scores(q, k, eq, ek, masks_ref), v)
                b_last = b[C - 1:C]
                out.append(st * jnp.exp(b_last) + _dot(v, k * jnp.exp(b_last - b), TN))
                oraw_ref[rows, :] = o
                r = lax.rsqrt(jnp.mean(o * o, axis=-1, keepdims=True) + EPS)
                gate, _ = _silu_and_grad(g_ref[rows, :])
                o_ref[rows, :] = (o * r * nw * gate).astype(ACT_DTYPE)
            return tuple(out)

        lax.fori_loop(0, NC, body, tuple(jnp.zeros((HEAD_DIM, HEAD_DIM), F32) for _ in range(B)))

    col = lambda off: pl.BlockSpec((T, HEAD_DIM), lambda h: (0, off + h))
    return _pcall(
        kern, name=name, grid=(H,),
        in_specs=[col(0), col(H), col(2 * H), col(3 * H),
                  pl.BlockSpec((2, HEAD_DIM), lambda h: (0, h)),
                  pl.BlockSpec((1, HEAD_DIM), lambda h: (0, h)),
                  pl.BlockSpec(mc.shape, lambda h: (0, 0)),
                  pl.BlockSpec(masks.shape, lambda h: (0, 0, 0))],
        out_specs=[col(0), col(0),
                   pl.BlockSpec((B, None, NC, HEAD_DIM, HEAD_DIM), lambda h: (0, h, 0, 0, 0))],
        out_shape=[jax.ShapeDtypeStruct((T, HW), F32), jax.ShapeDtypeStruct((T, HW), ACT_DTYPE),
                   jax.ShapeDtypeStruct((B, H, NC, HEAD_DIM, HEAD_DIM), F32)],
        compiler_params=_params(("parallel",), 20 * T * HEAD_DIM * 4 + (8 << 20)),
    )(proj, proj, proj, proj, lb_gamma, norm_w, mc, masks)


def _hgrn_bwd(proj, lb_gamma, norm_w, o_raw, states, d_mix, B, HW, name):
    T = proj.shape[0]
    S, H, C = T // B, HW // HEAD_DIM, CHUNK
    NC = S // C
    mc_np, masks_np, gsum_np = _hgrn_tables()
    mc, masks, gsum = jnp.asarray(mc_np, MXU_DTYPE), jnp.asarray(masks_np, F32), jnp.asarray(gsum_np, MXU_DTYPE)
    nl = len(LEVEL_HALVES)

    def kern(q_ref, f_ref, i_ref, g_ref, lbg_ref, nw_ref, mc_ref, masks_ref, gsum_ref, oraw_ref, st_ref, do_ref,
             dq_ref, df_ref, di_ref, dg_ref, dlbg_ref, dnw_ref):
        g0, g1 = lbg_ref[pl.ds(0, 1), :], lbg_ref[pl.ds(1, 1), :]
        mx = jnp.maximum(g0, g1)
        e0, e1 = jnp.exp(g0 - mx), jnp.exp(g1 - mx)
        lb = e0 / (e0 + e1)
        nw = nw_ref[...]
        mcv, gsumv = mc_ref[...], gsum_ref[...]

        def chunk(s, n, dst, dlb, dnw):
            rows = pl.ds(pl.multiple_of(s * S + n * C, C), C)
            qr, fr, v = q_ref[rows, :], f_ref[rows, :], i_ref[rows, :]
            q, dq_dqr, sf, f, k, b, eq, ek = _hgrn_gates(qr, fr, lb, mcv)
            o = oraw_ref[rows, :]
            dout = do_ref[rows, :].astype(F32)
            gate, dgate = _silu_and_grad(g_ref[rows, :])
            r = lax.rsqrt(jnp.mean(o * o, axis=-1, keepdims=True) + EPS)
            dg_ref[rows, :] = (dout * o * r * nw * dgate).astype(ACT_DTYPE)
            don = dout * gate
            dnw = dnw + jnp.sum(don * o * r, axis=0, keepdims=True)
            gw = don * nw
            do = r * gw - o * (r * r * r) * jnp.mean(gw * o, axis=-1, keepdims=True)
            st_prev = st_ref[s, n]
            eb = jnp.exp(b)
            b_last = b[C - 1:C]
            ebl = jnp.exp(b_last - b)
            p = _hgrn_scores(q, k, eq, ek, masks_ref)
            dp = _dot(do, v, NT)
            dpd = jnp.sum(do * v, axis=-1, keepdims=True)
            dq_state = _dot(do, st_prev) * eb
            dk_state = _dot(v, dst) * ebl
            dq = dq_state + dpd * k
            dk = dk_state + dpd * q
            pairs = [q * dq_state, k * dk_state]
            for l in range(nl):
                mdp = masks_ref[l] * dp
                dql = _dot(mdp, k * ek[l]) * eq[l]
                dkl = _dot(mdp, q * eq[l], TN) * ek[l]
                dq, dk = dq + dql, dk + dkl
                pairs += [q * dql, k * dkl]
            dv = _dot(p, do, TN) + _dot(k * ebl, dst, NT)
            through = jnp.exp(b_last) * jnp.sum(dst * st_prev, axis=0, keepdims=True)
            dlg = _split_dot(gsumv, jnp.concatenate(pairs, axis=0)) + through
            dst = dst * jnp.exp(b_last) + _dot(do, q * eb, TN)
            dq_ref[rows, :] = (dq * dq_dqr).astype(ACT_DTYPE)
            dfv = dlg / f - dk
            df_ref[rows, :] = (dfv * (1.0 - lb) * sf * (1.0 - sf)).astype(ACT_DTYPE)
            di_ref[rows, :] = dv.astype(ACT_DTYPE)
            dlb = dlb + jnp.sum(dfv * (1.0 - sf), axis=0, keepdims=True)
            return dst, dlb, dnw

        def body(it, carry):
            dsts, dlb, dnw = carry
            out = []
            for s, dst in enumerate(dsts):
                dst, dlb, dnw = chunk(s, NC - 1 - it, dst, dlb, dnw)
                out.append(dst)
            return tuple(out), dlb, dnw

        zrow = jnp.zeros((1, HEAD_DIM), F32)
        zst = tuple(jnp.zeros((HEAD_DIM, HEAD_DIM), F32) for _ in range(B))
        _, dlb, dnw = lax.fori_loop(0, NC, body, (zst, zrow, zrow))
        dg0 = dlb * lb * (1.0 - lb)
        dlbg_ref[pl.ds(0, 1), :] = dg0
        dlbg_ref[pl.ds(1, 1), :] = -dg0
        dnw_ref[...] = dnw

    col = lambda off: pl.BlockSpec((T, HEAD_DIM), lambda h: (0, off + h))
    full = lambda a: pl.BlockSpec(a.shape, lambda h: (0,) * a.ndim)
    part = jax.ShapeDtypeStruct((T, HW), ACT_DTYPE)
    return _pcall(
        kern, name=name, grid=(H,),
        in_specs=[col(0), col(H), col(2 * H), col(3 * H),
                  pl.BlockSpec((2, HEAD_DIM), lambda h: (0, h)),
                  pl.BlockSpec((1, HEAD_DIM), lambda h: (0, h)),
                  full(mc), full(masks), full(gsum), col(0),
                  pl.BlockSpec((B, None, NC, HEAD_DIM, HEAD_DIM), lambda h: (0, h, 0, 0, 0)),
                  col(0)],
        out_specs=[col(0), col(0), col(0), col(0),
                   pl.BlockSpec((2, HEAD_DIM), lambda h: (0, h)),
                   pl.BlockSpec((1, HEAD_DIM), lambda h: (0, h))],
        out_shape=[part, part, part, part, jax.ShapeDtypeStruct((2, HW), F32), jax.ShapeDtypeStruct((1, HW), F32)],
        compiler_params=_params(("parallel",), 28 * T * HEAD_DIM * 4 + (8 << 20)),
    )(proj, proj, proj, proj, lb_gamma, norm_w, mc, masks, gsum, o_raw, states, d_mix)


def _lru_gates(xr, cw_ref, cb, wa, ba, wx, bx, lam):
    S = xr.shape[0]
    xb = _conv(xr, cw_ref, cb, LRU_CONV)
    r = _sigmoid(_dot(xb, wa) + ba)
    ig = _sigmoid(_dot(xb, wx) + bx)
    sp = jnp.maximum(-lam, 0.0) + jnp.log(1.0 + jnp.exp(-jnp.abs(lam)))
    la = -LRU_C * r * sp
    a = jnp.exp(la)
    mult = jnp.where(_rows(S) == 0, 1.0, jnp.sqrt(_one_minus_exp(2.0 * la)))
    return xb, r, ig, sp, a, mult


def _scan_rows(a_ref, u_ref, h_ref, reverse):
    S, W = a_ref.shape
    nb = S // 8
    row = _rows(8)

    def body(it, carry):
        blk = nb - 1 - it if reverse else it
        rows = pl.ds(pl.multiple_of(blk * 8, 8), 8)
        a, u = a_ref[rows, :], u_ref[rows, :]
        for d in (1, 2, 4):
            sh = 8 - d if reverse else d
            keep = (row < 8 - d) if reverse else (row >= d)
            u = u + jnp.where(keep, a * pltpu.roll(u, sh, 0), 0.0)
            a = jnp.where(keep, a * pltpu.roll(a, sh, 0), a)
        h = u + a * carry
        h_ref[rows, :] = h
        return h[0:1] if reverse else h[7:8]

    lax.fori_loop(0, nb, body, jnp.zeros((1, W), F32))


def _lru_fwd(proj, cw, cb, wa, ba, wx, bx, lam, B, HW, LW, name):
    T = proj.shape[0]
    S, NB = T // B, LW // HEAD_DIM
    xoff, yoff = 4 * HW // HEAD_DIM, 4 * HW // HEAD_DIM + NB

    def kern(x_ref, y_ref, cw_ref, cb_ref, wa_ref, ba_ref, wx_ref, bx_ref, lam_ref, h_ref, z_ref, a_s, u_s):
        xb, _, ig, _, a, mult = _lru_gates(x_ref[...], cw_ref, cb_ref[...], wa_ref[...], ba_ref[...],
                                           wx_ref[...], bx_ref[...], lam_ref[...])
        a_s[...] = a
        u_s[...] = xb * ig * mult
        _scan_rows(a_s, u_s, h_ref, False)
        gy, _ = _gelu_and_grad(y_ref[...])
        z_ref[...] = h_ref[...] * gy

    blk = lambda off: pl.BlockSpec((S, HEAD_DIM), lambda b, n: (b, off + n))
    vec = pl.BlockSpec((1, HEAD_DIM), lambda b, n: (0, n))
    mat = pl.BlockSpec((None, HEAD_DIM, HEAD_DIM), lambda b, n: (n, 0, 0))
    return _pcall(
        kern, name=name, grid=(B, NB),
        in_specs=[blk(xoff), blk(yoff), pl.BlockSpec((LRU_CONV, HEAD_DIM), lambda b, n: (0, n)),
                  vec, mat, vec, mat, vec, vec],
        out_specs=[blk(0), blk(0)],
        out_shape=[jax.ShapeDtypeStruct((T, LW), F32), jax.ShapeDtypeStruct((T, LW), F32)],
        scratch_shapes=[pltpu.VMEM((S, HEAD_DIM), F32), pltpu.VMEM((S, HEAD_DIM), F32)],
        compiler_params=_params(("parallel", "parallel"), 24 * S * HEAD_DIM * 4),
    )(proj, proj, cw, cb, wa, ba, wx, bx, lam)


def _lru_bwd(proj, cw, cb, wa, ba, wx, bx, lam, h, dz, B, HW, LW, name):
    T = proj.shape[0]
    S, NB = T // B, LW // HEAD_DIM
    xoff, yoff = 4 * HW // HEAD_DIM, 4 * HW // HEAD_DIM + NB

    def kern(x_ref, y_ref, cw_ref, cb_ref, wa_ref, ba_ref, wx_ref, bx_ref, lam_ref, h_ref, dz_ref,
             dx_ref, dy_ref, dwa_ref, dwx_ref, dba_ref, dbx_ref, dlam_ref, dcw_ref, dcb_ref, a_s, u_s, dh_s):
        bi = pl.program_id(1)
        row = _rows(S)
        xr, lam = x_ref[...], lam_ref[...]
        wa, wx = wa_ref[...], wx_ref[...]
        xb, r, ig, sp, a, mult = _lru_gates(xr, cw_ref, cb_ref[...], wa, ba_ref[...], wx, bx_ref[...], lam)
        hv, dz = h_ref[...], dz_ref[...]
        gy, dgy = _gelu_and_grad(y_ref[...])
        dy_ref[...] = (dz * hv * dgy).astype(ACT_DTYPE)
        a_s[...] = jnp.where(row < S - 1, pltpu.roll(a, S - 1, 0), 0.0)
        u_s[...] = dz * gy
        _scan_rows(a_s, u_s, dh_s, True)
        dh = dh_s[...]
        h_prev = jnp.where(row >= 1, pltpu.roll(hv, 1, 0), 0.0)
        d_ig = dh * xb * mult
        d_mult = jnp.where(row == 0, 0.0, dh * xb * ig)
        dxb = dh * ig * mult
        dla = dh * h_prev * a - d_mult * (a * a) / mult
        dpre_r = dla * (-LRU_C * sp) * r * (1.0 - r)
        dpre_i = d_ig * ig * (1.0 - ig)
        dxb = dxb + _dot(dpre_r, wa, NT) + _dot(dpre_i, wx, NT)
        dxr, dcw = _conv_bwd(xr, dxb, cw_ref, LRU_CONV)
        dx_ref[...] = dxr.astype(ACT_DTYPE)

        @pl.when(bi == 0)
        def _():
            for ref in (dwa_ref, dwx_ref, dba_ref, dbx_ref, dlam_ref, dcw_ref, dcb_ref):
                ref[...] = jnp.zeros_like(ref)

        dwa_ref[...] += _dot(xb, dpre_r, TN)
        dwx_ref[...] += _dot(xb, dpre_i, TN)
        dba_ref[...] += jnp.sum(dpre_r, axis=0, keepdims=True)
        dbx_ref[...] += jnp.sum(dpre_i, axis=0, keepdims=True)
        dsp = jnp.sum(dla * (-LRU_C) * r, axis=0, keepdims=True)
        dlam_ref[...] += -dsp * _sigmoid(-lam)
        for j, rj in enumerate(dcw):
            dcw_ref[pl.ds(j, 1), :] += rj
        dcb_ref[...] += jnp.sum(dxb, axis=0, keepdims=True)

    blk = lambda off: pl.BlockSpec((S, HEAD_DIM), lambda n, b: (b, off + n))
    vec = pl.BlockSpec((1, HEAD_DIM), lambda n, b: (0, n))
    mat = pl.BlockSpec((None, HEAD_DIM, HEAD_DIM), lambda n, b: (n, 0, 0))
    cwb = pl.BlockSpec((LRU_CONV, HEAD_DIM), lambda n, b: (0, n))
    part = jax.ShapeDtypeStruct((T, LW), ACT_DTYPE)
    vshape = jax.ShapeDtypeStruct((1, LW), F32)
    mshape = jax.ShapeDtypeStruct((NB, HEAD_DIM, HEAD_DIM), F32)
    return _pcall(
        kern, name=name, grid=(NB, B),
        in_specs=[blk(xoff), blk(yoff), cwb, vec, mat, vec, mat, vec, vec, blk(0), blk(0)],
        out_specs=[blk(0), blk(0), mat, mat, vec, vec, vec, cwb, vec],
        out_shape=[part, part, mshape, mshape, vshape, vshape, vshape,
                   jax.ShapeDtypeStruct((LRU_CONV, LW), F32), vshape],
        scratch_shapes=[pltpu.VMEM((S, HEAD_DIM), F32)] * 3,
        compiler_params=_params(("parallel", "arbitrary"), 40 * S * HEAD_DIM * 4),
    )(proj, proj, cw, cb, wa, ba, wx, bx, lam, h, dz)


def _pos():
    return lax.axis_index("x"), lax.axis_index("y"), lax.axis_index("c")


def _other_chips(x, y):
    return [(1 - x, y), (x, 1 - y), (1 - x, 1 - y)]


def _remote(src, dst, send_sems, recv_sems, k, to):
    return pltpu.make_async_remote_copy(src_ref=src, dst_ref=dst, send_sem=send_sems.at[k],
                                        recv_sem=recv_sems.at[k], device_id=to, device_id_type=MESH)


HBM_BLK = pl.BlockSpec(memory_space=pltpu.HBM)
SEM_BLK = pl.BlockSpec(memory_space=pltpu.SEMAPHORE)
VMEM_BLK = pl.BlockSpec(memory_space=pltpu.VMEM)
DATAFLOW = pltpu.SideEffectType.DATAFLOW_SIDE_EFFECTING
TOKEN = jax.ShapeDtypeStruct((8, V7X_LANES), F32)


def _in_hbm(a):
    return pltpu.with_memory_space_constraint(a, pltpu.HBM)


def _gather_win(o_ref, R, C, col_sharded):
    Rh = R // 2

    def win(j, h=None):
        if col_sharded:
            rows = pl.ds(0, R) if h is None else pl.ds(h * Rh, Rh)
            return o_ref.at[rows, pl.ds(j * C, C)]
        return o_ref.at[pl.ds(j * R, R) if h is None else pl.ds(j * R + h * Rh, Rh), :]

    return win


def _cast_into_window(w, col_sharded, after, name):
    R, C = w.shape
    tr = _tile(R, 256, 16)
    nr = R // tr
    full = (R, 4 * C) if col_sharded else (4 * R, C)
    j = (2 * lax.axis_index("x") + lax.axis_index("y")).astype(jnp.int32).reshape(1)

    def kern(j_ref, w_ref, after_ref, o_ref):
        o_ref[...] = w_ref[...].astype(ACT_DTYPE)

    out_map = (lambda i, jr: (i, jr[0])) if col_sharded else (lambda i, jr: (jr[0] * nr + i, 0))
    grid_spec = pltpu.PrefetchScalarGridSpec(
        num_scalar_prefetch=1, grid=(nr,),
        in_specs=[pl.BlockSpec((tr, C), lambda i, jr: (i, 0)), HBM_SPEC], out_specs=pl.BlockSpec((tr, C), out_map))
    return _pcall(kern, name=name, grid_spec=grid_spec, out_shape=jax.ShapeDtypeStruct(full, ACT_DTYPE),
                  compiler_params=_params(("parallel",), 6 * tr * C * 4))(j, w, after)


def _gather_start(land, shard_shape, col_sharded, token, name):
    R, C = shard_shape

    def body(land_ref, tok_ref, send_sems, recv_sems, land_thru, tok_out):
        x, y, c = _pos()
        w = _gather_win(land_ref, R, C, col_sharded)(2 * x + y, c)
        for k, (cx, cy) in enumerate(_other_chips(x, y)):
            _remote(w, w, send_sems, recv_sems, k, (cx, cy, c)).start()
        tok_out[...] = tok_ref[...]

    return _pcall(
        body, name=name,
        out_shape=(pltpu.SemaphoreType.DMA((3,)), pltpu.SemaphoreType.DMA((3,)),
                   pltpu.HBM(land.shape, land.dtype), TOKEN),
        in_specs=(HBM_BLK, VMEM_BLK), out_specs=(SEM_BLK, SEM_BLK, HBM_BLK, VMEM_BLK),
        input_output_aliases={0: 2},
        compiler_params=pltpu.CompilerParams(has_side_effects=DATAFLOW),
    )(_in_hbm(land), token)


def _gather_wait(started, shard_shape, after, col_sharded, name, which=(0, 1, 2)):
    send_sems, recv_sems, land_thru = started[:3]
    R, C = shard_shape

    def body(land_ref, send_sems, recv_sems, after_ref, got_ref):
        x, y, c = _pos()
        win = _gather_win(land_ref, R, C, col_sharded)
        for k, (cx, cy) in enumerate(_other_chips(x, y)):
            if k in which:
                cp = _remote(win(2 * x + y, c), win(2 * cx + cy, c), send_sems, recv_sems, k, (cx, cy, c))
                cp.wait_send()
                cp.wait_recv()

    return _pcall(
        body, name=name, out_shape=pltpu.HBM(land_thru.shape, land_thru.dtype),
        in_specs=(HBM_BLK, SEM_BLK, SEM_BLK, HBM_SPEC), out_specs=HBM_BLK, input_output_aliases={0: 0},
        compiler_params=pltpu.CompilerParams(has_side_effects=DATAFLOW),
    )(land_thru, send_sems, recv_sems, after)


def _gather_pass_on(landed, shard_shape, col_sharded, name, which=(0, 1, 2)):
    R, C = shard_shape

    def body(in_ref, o_ref, send_sems, recv_sems):
        x, y, c = _pos()
        src, dst = _gather_win(in_ref, R, C, col_sharded), _gather_win(o_ref, R, C, col_sharded)
        chips = [(k, 2 * cx + cy) for k, (cx, cy) in enumerate(_other_chips(x, y)) if k in which]
        passed = [_remote(src(j, c), dst(j, c), send_sems, recv_sems, k, (x, y, 1 - c)) for k, j in chips]
        for cp in passed:
            cp.start()
        for k, j in chips:
            w = dst(j, 1 - c)
            _remote(w, w, send_sems, recv_sems, k, (x, y, c)).wait_recv()
        for cp in passed:
            cp.wait_send()

    return _pcall(body, name=name, in_specs=[HBM_SPEC], out_specs=HBM_SPEC,
                  out_shape=jax.ShapeDtypeStruct(landed.shape, landed.dtype), input_output_aliases={0: 0},
                  scratch_shapes=[pltpu.SemaphoreType.DMA((3,)), pltpu.SemaphoreType.DMA((3,))])(landed)


def _pair_start(g, token, name):
    def body(g_ref, land_ref, tok_ref, send_sems, recv_sems, g_thru, land_thru, tok_out):
        x, y, c = _pos()
        _remote(g_ref, land_ref, send_sems, recv_sems, 0, (x, y, 1 - c)).start()
        tok_out[...] = tok_ref[...]

    return _pcall(
        body, name=name,
        out_shape=(pltpu.SemaphoreType.DMA((1,)), pltpu.SemaphoreType.DMA((1,)),
                   pltpu.HBM(g.shape, g.dtype), pltpu.HBM(g.shape, g.dtype), TOKEN),
        in_specs=(HBM_BLK, HBM_BLK, VMEM_BLK), out_specs=(SEM_BLK, SEM_BLK, HBM_BLK, HBM_BLK, VMEM_BLK),
        input_output_aliases={0: 2, 1: 3},
        compiler_params=pltpu.CompilerParams(has_side_effects=DATAFLOW),
    )(_in_hbm(g), _in_hbm(lax.empty(g.shape, g.dtype)), token)


def _pair_wait(started, after, name):
    send_sems, recv_sems, g_thru, land_thru, _ = started

    def body(g_ref, land_ref, send_sems, recv_sems, after_ref, g_dead, got_ref):
        x, y, c = _pos()
        cp = _remote(g_ref, land_ref, send_sems, recv_sems, 0, (x, y, 1 - c))
        cp.wait_send()
        cp.wait_recv()

    return _pcall(
        body, name=name,
        out_shape=(pltpu.HBM(g_thru.shape, g_thru.dtype), pltpu.HBM(land_thru.shape, land_thru.dtype)),
        in_specs=(HBM_BLK, HBM_BLK, SEM_BLK, SEM_BLK, HBM_SPEC), out_specs=(HBM_BLK, HBM_BLK),
        input_output_aliases={0: 0, 1: 1},
        compiler_params=pltpu.CompilerParams(has_side_effects=DATAFLOW),
    )(g_thru, land_thru, send_sems, recv_sems, after)[1]


def _own_piece_into_slots(cs, col_sharded, name):
    J, Rp, W = cs.shape
    Cp = W // 4 if col_sharded else W
    tr = _tile(Rp, 256, 16)
    tw = _tile(Cp, 8192, V7X_LANES)
    nw = Cp // tw
    x, y, c = _pos()
    chip = (2 * x + y).astype(jnp.int32).reshape(1)
    core = c.astype(jnp.int32).reshape(1)

    def kern(j_ref, c_ref, s_ref, o_ref):
        o_ref[...] = s_ref[...]

    in_map = ((lambda i, w, j, cc: (0, i, j[0] * nw + w)) if col_sharded
              else (lambda i, w, j, cc: (j[0], i, w)))
    grid_spec = pltpu.PrefetchScalarGridSpec(
        num_scalar_prefetch=2, grid=(Rp // tr, nw),
        in_specs=[pl.BlockSpec((None, tr, tw), in_map)],
        out_specs=pl.BlockSpec((None, None, tr, tw), lambda i, w, j, cc: (j[0], cc[0], i, w)))
    return _pcall(kern, name=name, grid_spec=grid_spec,
                  out_shape=jax.ShapeDtypeStruct((4, 2, Rp, Cp), cs.dtype),
                  compiler_params=_params(("parallel", "parallel"), 8 * tr * tw * 4))(chip, core, cs)


def _chip_sum_piece(cs_ref, C, col_sharded):
    return lambda j: cs_ref.at[0, :, pl.ds(j * C, C)] if col_sharded else cs_ref.at[j]


def _scatter_start(cs, slots, col_sharded, token, name):
    C = slots.shape[3]

    def body(cs_ref, land_ref, tok_ref, send_sems, recv_sems, cs_thru, land_thru, tok_out):
        x, y, c = _pos()
        piece = _chip_sum_piece(cs_ref, C, col_sharded)
        for k, (cx, cy) in enumerate(_other_chips(x, y)):
            _remote(piece(2 * cx + cy), land_ref.at[2 * x + y, c], send_sems, recv_sems, k, (cx, cy, c)).start()
        tok_out[...] = tok_ref[...]

    return _pcall(
        body, name=name,
        out_shape=(pltpu.SemaphoreType.DMA((3,)), pltpu.SemaphoreType.DMA((3,)),
                   pltpu.HBM(cs.shape, cs.dtype), pltpu.HBM(slots.shape, cs.dtype), TOKEN),
        in_specs=(HBM_BLK, HBM_BLK, VMEM_BLK), out_specs=(SEM_BLK, SEM_BLK, HBM_BLK, HBM_BLK, VMEM_BLK),
        input_output_aliases={0: 2, 1: 3},
        compiler_params=pltpu.CompilerParams(has_side_effects=DATAFLOW),
    )(_in_hbm(cs), _in_hbm(slots), token)


def _scatter_wait(started, after, col_sharded, name):
    send_sems, recv_sems, cs_thru, land_thru, _ = started
    C = land_thru.shape[3]

    def body(cs_ref, land_ref, send_sems, recv_sems, after_ref, cs_dead, got_ref):
        x, y, c = _pos()
        piece = _chip_sum_piece(cs_ref, C, col_sharded)
        for k, (cx, cy) in enumerate(_other_chips(x, y)):
            cp = _remote(piece(2 * cx + cy), land_ref.at[2 * cx + cy, c], send_sems, recv_sems, k, (cx, cy, c))
            cp.wait_send()
            cp.wait_recv()

    return _pcall(
        body, name=name,
        out_shape=(pltpu.HBM(cs_thru.shape, cs_thru.dtype), pltpu.HBM(land_thru.shape, land_thru.dtype)),
        in_specs=(HBM_BLK, HBM_BLK, SEM_BLK, SEM_BLK, HBM_SPEC), out_specs=(HBM_BLK, HBM_BLK),
        input_output_aliases={0: 0, 1: 1},
        compiler_params=pltpu.CompilerParams(has_side_effects=DATAFLOW),
    )(cs_thru, land_thru, send_sems, recv_sems, after)


def _scatter_pass_on(landed, name):
    def body(in_ref, o_ref, send_sems, recv_sems):
        x, y, c = _pos()
        sends = [_remote(in_ref.at[i, c], o_ref.at[i, c], send_sems, recv_sems, i, (x, y, 1 - c)) for i in range(4)]
        for cp in sends:
            cp.start()
        for i in range(4):
            w = o_ref.at[i, 1 - c]
            _remote(w, w, send_sems, recv_sems, i, (x, y, c)).wait_recv()
        for cp in sends:
            cp.wait_send()

    return _pcall(body, name=name, in_specs=[HBM_SPEC], out_specs=HBM_SPEC,
                  out_shape=jax.ShapeDtypeStruct(landed.shape, landed.dtype), input_output_aliases={0: 0},
                  scratch_shapes=[pltpu.SemaphoreType.DMA((4,)), pltpu.SemaphoreType.DMA((4,))])(landed)


def _gather_windows(shard_shape, col_sharded):
    R, C = shard_shape

    def windows(ref, x, y, cc):
        win = _gather_win(ref, R, C, col_sharded)
        return [win(2 * cx + cy, cc) for cx, cy in _other_chips(x, y)]

    return windows


def _slot_windows(ref, x, y, cc):
    return [ref.at[i, cc] for i in range(4)]


def _pass_start(landed, windows, n, token, name):
    def body(land_ref, tok_ref, send_sems, recv_sems, land_thru, tok_out):
        x, y, c = _pos()
        for k, w in enumerate(windows(land_ref, x, y, c)):
            _remote(w, w, send_sems, recv_sems, k, (x, y, 1 - c)).start()
        tok_out[...] = tok_ref[...]

    return _pcall(
        body, name=name,
        out_shape=(pltpu.SemaphoreType.DMA((n,)), pltpu.SemaphoreType.DMA((n,)),
                   pltpu.HBM(landed.shape, landed.dtype), TOKEN),
        in_specs=(HBM_BLK, VMEM_BLK), out_specs=(SEM_BLK, SEM_BLK, HBM_BLK, VMEM_BLK),
        input_output_aliases={0: 2},
        compiler_params=pltpu.CompilerParams(has_side_effects=DATAFLOW),
    )(_in_hbm(landed), token)


def _pass_wait(started, windows, after, name):
    send_sems, recv_sems, land_thru, _ = started

    def body(land_ref, send_sems, recv_sems, after_ref, got_ref):
        x, y, c = _pos()
        mine, theirs = windows(land_ref, x, y, c), windows(land_ref, x, y, 1 - c)
        for k, (src, dst) in enumerate(zip(mine, theirs)):
            cp = _remote(src, dst, send_sems, recv_sems, k, (x, y, 1 - c))
            cp.wait_send()
            cp.wait_recv()

    return _pcall(
        body, name=name, out_shape=pltpu.HBM(land_thru.shape, land_thru.dtype),
        in_specs=(HBM_BLK, SEM_BLK, SEM_BLK, HBM_SPEC), out_specs=HBM_BLK, input_output_aliases={0: 0},
        compiler_params=pltpu.CompilerParams(has_side_effects=DATAFLOW),
    )(land_thru, send_sems, recv_sems, after)


def _gather_small(buf, name):
    rows = buf.shape[0]

    def body(b_ref, o_ref, send_sems, recv_sems):
        x, y, c = _pos()
        jme = 2 * x + y
        chips = _other_chips(x, y)
        o_ref[jme] = b_ref[...]
        sends = [_remote(b_ref, o_ref.at[jme], send_sems, recv_sems, k, (cx, cy, c))
                 for k, (cx, cy) in enumerate(chips)]
        for cp in sends:
            cp.start()
        for k, (cx, cy) in enumerate(chips):
            w = o_ref.at[2 * cx + cy]
            _remote(w, w, send_sems, recv_sems, k, (x, y, c)).wait_recv()
        for cp in sends:
            cp.wait_send()

    vm = pl.BlockSpec(memory_space=pltpu.VMEM)
    return _pcall(body, name=name, in_specs=[vm], out_specs=vm,
                  out_shape=jax.ShapeDtypeStruct((4, rows, V7X_LANES), buf.dtype),
                  scratch_shapes=[pltpu.SemaphoreType.DMA((3,)), pltpu.SemaphoreType.DMA((3,))],
                  compiler_params=_params(None, 16 * rows * V7X_LANES * 4))(buf)


def _allreduce_small(buf, name):
    rows = buf.shape[0]
    rh = rows // 2

    def body(b_ref, o_ref, pair, mine, slots, send_sems, recv_sems):
        x, y, c = _pos()
        me, sib, jme = (x, y, c), (x, y, 1 - c), 2 * x + y
        half = pl.ds(pl.multiple_of(c * rh, 8), rh)
        other = pl.ds(pl.multiple_of((1 - c) * rh, 8), rh)
        to_sib = _remote(b_ref, pair.at[c], send_sems, recv_sems, 0, sib)
        to_sib.start()
        pair[c] = b_ref[...]
        _remote(pair.at[1 - c], pair.at[1 - c], send_sems, recv_sems, 0, me).wait_recv()
        mine[...] = pair[0, half, :] + pair[1, half, :]
        chips = _other_chips(x, y)
        sends = [_remote(mine, slots.at[jme], send_sems, recv_sems, 1 + k, (cx, cy, c))
                 for k, (cx, cy) in enumerate(chips)]
        for cp in sends:
            cp.start()
        slots[jme] = mine[...]
        for k, (cx, cy) in enumerate(chips):
            w = slots.at[2 * cx + cy]
            _remote(w, w, send_sems, recv_sems, 1 + k, me).wait_recv()
        o_ref[half, :] = (slots[0] + slots[1]) + (slots[2] + slots[3])
        back = _remote(o_ref.at[half, :], o_ref.at[half, :], send_sems, recv_sems, 4, sib)
        back.start()
        _remote(o_ref.at[other, :], o_ref.at[other, :], send_sems, recv_sems, 4, me).wait_recv()
        for cp in [to_sib, back] + sends:
            cp.wait_send()

    vm = pl.BlockSpec(memory_space=pltpu.VMEM)
    return _pcall(body, name=name, in_specs=[vm], out_specs=vm,
                  out_shape=jax.ShapeDtypeStruct(buf.shape, buf.dtype),
                  scratch_shapes=[pltpu.VMEM((2, rows, V7X_LANES), buf.dtype),
                                  pltpu.VMEM((rh, V7X_LANES), buf.dtype),
                                  pltpu.VMEM((4, rh, V7X_LANES), buf.dtype),
                                  pltpu.SemaphoreType.DMA((5,)), pltpu.SemaphoreType.DMA((5,))],
                  compiler_params=_params(None, 10 * rows * V7X_LANES * 4))(buf)


def _adamw_math(w, g, m, v):
    m = ADAM_B1 * m + (1.0 - ADAM_B1) * g
    v = ADAM_B2 * v + (1.0 - ADAM_B2) * (g * g)
    m_hat = m / (1.0 - ADAM_B1 ** ADAM_STEP)
    v_hat = v / (1.0 - ADAM_B2 ** ADAM_STEP)
    delta = -ADAM_LR * (m_hat / (jnp.sqrt(v_hat) + ADAM_EPS) + ADAM_WD * w)
    return delta, m, v


def _adamw_big(w, m, v, slots, name):
    R, C = w.shape
    by_rows = slots.shape[3] == C
    tr = _tile(R, 32, 16)

    def kern(w_ref, m_ref, v_ref, s_ref, g_ref, d_ref, mo_ref, vo_ref):
        def chip_sum(h):
            g = s_ref[0, h].astype(F32)
            for i in range(1, 4):
                g = g + s_ref[i, h].astype(F32)
            return g

        g = chip_sum(0) if by_rows else jnp.concatenate([chip_sum(0), chip_sum(1)], axis=1)
        d, mn, vn = _adamw_math(w_ref[...], g, m_ref[...], v_ref[...])
        g_ref[...], d_ref[...], mo_ref[...], vo_ref[...] = g, d, mn, vn

    row = pl.BlockSpec((tr, C), lambda i: (i, 0))
    shp = jax.ShapeDtypeStruct((R, C), F32)
    if by_rows:
        per_half = R // 2 // tr
        s_spec = pl.BlockSpec((4, 1, tr, C), lambda i: (0, i // per_half, i % per_half, 0))
    else:
        s_spec = pl.BlockSpec((4, 2, tr, C // 2), lambda i: (0, 0, i, 0))
    return _pcall(kern, name=name, grid=(R // tr,),
                  in_specs=[row, row, row, s_spec], out_specs=[row] * 4, out_shape=[shp] * 4,
                  compiler_params=_params(("parallel",), 36 * tr * C * 4))(w, m, v, slots)


def _adamw_small(w, g, m, v, name):
    def kern(w_ref, g_ref, m_ref, v_ref, d_ref, mo_ref, vo_ref):
        d_ref[...], mo_ref[...], vo_ref[...] = _adamw_math(w_ref[...], g_ref[...], m_ref[...], v_ref[...])

    vm = pl.BlockSpec(memory_space=pltpu.VMEM)
    shp = jax.ShapeDtypeStruct(w.shape, F32)
    return _pcall(kern, name=name, in_specs=[vm] * 4, out_specs=[vm] * 3, out_shape=[shp] * 3,
                  compiler_params=_params(None, 10 * w.size * 4))(w, g, m, v)


def _pack(arrs):
    flat = jnp.concatenate([a.reshape(-1).astype(F32) for a in arrs])
    n = flat.shape[0]
    rows = -(-n // (16 * V7X_LANES)) * 16
    return jnp.pad(flat, (0, rows * V7X_LANES - n)).reshape(rows, V7X_LANES)


def _unpack(buf, shapes):
    flat = buf.reshape(-1)
    out, off = [], 0
    for s in shapes:
        n = int(np.prod(s))
        out.append(flat[off:off + n].reshape(s))
        off += n
    return out


def _cut_gradient(a, d, col_sharded, tm, tn, token, tag, other_work):
    c = lax.axis_index("c").astype(jnp.int32)
    for_sibling = _matmul_tn_half(a, d, (1 - c).reshape(1), None, tm, tn, col_sharded, "mm_g_%s_sibling" % tag)
    sent = _pair_start(for_sibling, token, "pair_start_" + tag)
    other, last = other_work(sent[4])
    arrived = _pair_wait(sent, last, "pair_wait_" + tag)
    cs = _matmul_tn_half(a, d, c.reshape(1), arrived, tm, tn, col_sharded, "mm_g_%s_own" % tag)
    cs = cs.reshape((1,) + cs.shape if col_sharded else (4, cs.shape[0] // 4, cs.shape[1]))
    slots = _own_piece_into_slots(cs, col_sharded, "own_piece_" + tag)
    return _scatter_start(cs, slots, col_sharded, sent[4], "scatter_start_" + tag), other


def _reduce_finish(started, after, col_sharded, w, m, v, tag):
    _, landed = _scatter_wait(started, after, col_sharded, "scatter_wait_" + tag)
    slots = _scatter_pass_on(landed, "scatter_pass_on_" + tag)
    return _adamw_big(w, m, v, slots, "adamw_" + tag)


def kernel(x, ln1_w, w_in, lb_gamma, hg_norm_w, lru_conv_w, lru_conv_b, lru_wa, lru_ba, lru_wx, lru_bx, lru_lambda, lru_norm_w, w_out, ln2_w, ffn_w_up, ffn_conv_w, ffn_conv_b, ffn_w_down, final_norm_w, loss_target, m_ln1_w, m_w_in, m_lb_gamma, m_hg_norm_w, m_lru_conv_w, m_lru_conv_b, m_lru_wa, m_lru_ba, m_lru_wx, m_lru_bx, m_lru_lambda, m_lru_norm_w, m_w_out, m_ln2_w, m_ffn_w_up, m_ffn_conv_w, m_ffn_conv_b, m_ffn_w_down, m_final_norm_w, v_ln1_w, v_w_in, v_lb_gamma, v_hg_norm_w, v_lru_conv_w, v_lru_conv_b, v_lru_wa, v_lru_ba, v_lru_wx, v_lru_bx, v_lru_lambda, v_lru_norm_w, v_w_out, v_ln2_w, v_ffn_w_up, v_ffn_conv_w, v_ffn_conv_b, v_ffn_w_down, v_final_norm_w):
    B, S, D = x.shape
    T = B * S
    HW = lb_gamma.shape[1]
    LW = lru_conv_b.shape[1]
    assert S % CHUNK == 0 and HW % HEAD_DIM == 0 and lru_wa.shape[2] == HEAD_DIM
    x2 = x.reshape(T, D)
    tgt = loss_target.reshape(T, D)
    jchip = 2 * lax.axis_index("x") + lax.axis_index("y")

    conv_shapes = [lru_conv_w[0].shape, ffn_conv_w[0].shape]
    convs = _gather_small(_pack([lru_conv_w[0], ffn_conv_w[0]]), "gather_conv_w")
    per_chip = [_unpack(convs[j], conv_shapes) for j in range(4)]
    lcw = jnp.concatenate([pc[0] for pc in per_chip], axis=1)
    fcw = jnp.concatenate([pc[1] for pc in per_chip], axis=1)
    masters = dict(w_in=w_in[0], w_out=w_out[0], w_up=ffn_w_up[0], w_down=ffn_w_down[0])
    col_of = dict(w_in=True, w_out=False, w_up=True, w_down=False)
    started, token, after = {}, jnp.zeros(TOKEN.shape, F32), convs
    for n in ("w_in", "w_out", "w_up", "w_down"):
        land = _cast_into_window(masters[n], col_of[n], after, "cast_" + n)
        started[n] = _gather_start(land, masters[n].shape, col_of[n], token, "gather_start_" + n)
        token = after = started[n][3]

    def landed(n, after):
        return _gather_wait(started[n], masters[n].shape, after, col_of[n], "gather_wait_" + n)

    def pass_on_start(n, after, tok):
        wins = _gather_windows(masters[n].shape, col_of[n])
        return _pass_start(landed(n, after), wins, 3, tok, "gather_pass_start_" + n)

    def pass_on_wait(n, sent, after):
        return _pass_wait(sent, _gather_windows(masters[n].shape, col_of[n]), after, "gather_pass_wait_" + n)

    hn1 = _rms_fwd(x2, ln1_w, "rms1")
    mx, my = lax.axis_index("x"), lax.axis_index("y")
    shape_in = masters["w_in"].shape
    as_j = lambda v: v.astype(jnp.int32).reshape(1)
    near = _gather_wait(started["w_in"], shape_in, token, True, "gather_wait_w_in_near", which=(0, 1))
    near = _gather_pass_on(near, shape_in, True, "gather_pass_on_w_in_near", which=(0, 1))
    proj = None
    for tag, j in (("own", 2 * mx + my), ("x", 2 * (1 - mx) + my), ("y", 2 * mx + 1 - my)):
        proj = _matmul_col_slice(hn1, near, as_j(j), shape_in[1], proj, 1024, 512, "mm_proj_" + tag)
    far = _gather_wait(tuple(started["w_in"][:2]) + (near,), shape_in, proj, True, "gather_wait_w_in_far", which=(2,))
    W_in = _gather_pass_on(far, shape_in, True, "gather_pass_on_w_in_far", which=(2,))
    proj = _matmul_col_slice(hn1, W_in, as_j(2 * (1 - mx) + 1 - my), shape_in[1], proj, 1024, 512, "mm_proj_far")
    sent_out = pass_on_start("w_out", proj, token)
    o_raw, o_hg, states = _hgrn_fwd(proj, lb_gamma, hg_norm_w, B, HW, "hgrn_fwd")
    h_lru, z = _lru_fwd(proj, lcw, lru_conv_b, lru_wa[0], lru_ba, lru_wx[0], lru_bx, lru_lambda, B, HW, LW, "lru_fwd")
    o_lru = _rms_fwd(z, lru_norm_w, "rms_lru")
    mix = jnp.concatenate([o_hg, o_lru], axis=1)
    sent_up = pass_on_start("w_up", mix, sent_out[3])
    W_out = pass_on_wait("w_out", sent_out, sent_up[3])
    h1 = _matmul(mix, W_out, "NN", F32, 1024, 512, 4096, add=x2, name="mm_out")
    hn2 = _rms_fwd(h1, ln2_w, "rms2")
    W_up = pass_on_wait("w_up", sent_up, hn2)
    up = _matmul(hn2, W_up, "NN", F32, 1024, 512, 4096, name="mm_up")
    act, act_dg, act_dv = _ffn_act(up, fcw, ffn_conv_b, B, "ffn_act")
    W_down = _gather_pass_on(landed("w_down", act), masters["w_down"].shape, False, "gather_pass_on_w_down")
    h2 = _matmul(act, W_down, "NN", F32, 1024, 512, 5504, add=h1, name="mm_down")
    token = sent_up[3]

    dh2, dh2a, d_final_w, loss_part = _loss_bwd(h2, tgt, final_norm_w.reshape(1, D), "loss_bwd")
    def through_w_down(tok):
        d = _matmul(dh2a, W_down, "NT", ACT_DTYPE, 512, 5504, 512, after=tok, name="mm_d_act")
        return d, d

    red_down, d_act = _cut_gradient(act, dh2a, False, 256, 1024, token, "w_down", through_w_down)
    d_up, d_fcw, d_fcb = _ffn_act_bwd(up, fcw, act_dg, act_dv, d_act, B, "ffn_act_bwd")
    def through_w_up(tok):
        d = _matmul(d_up, W_up, "NT", F32, 2048, 1024, 512, after=tok, name="mm_d_hn2")
        return d, d

    red_up, d_hn2 = _cut_gradient(hn2, d_up, True, 1024, 512, red_down[4], "w_up", through_w_up)
    dh1, dh1a, d_ln2 = _rms_bwd(h1, ln2_w, d_hn2, 0, dh2, True, "rms2_bwd", after=red_up[4])

    def through_w_out(tok):
        d = _matmul(dh1a, W_out, "NT", F32, 1024, 512, 4096, after=tok, name="mm_d_mix")
        return d, d

    red_out, d_mix = _cut_gradient(mix, dh1a, False, 1024, 512, red_up[4], "w_out", through_w_out)
    dz, d_lru_norm = _rms_bwd(z, lru_norm_w, d_mix, HW // LW, None, False, "rms_lru_bwd")
    (d_xr, d_yr, d_wa, d_wx, d_ba, d_bx, d_lam, d_lcw, d_lcb) = _lru_bwd(
        proj, lcw, lru_conv_b, lru_wa[0], lru_ba, lru_wx[0], lru_bx, lru_lambda, h_lru, dz, B, HW, LW, "lru_bwd")
    d_q, d_f, d_i, d_g, d_lbg, d_hgw = _hgrn_bwd(proj, lb_gamma, hg_norm_w, o_raw, states, d_mix, B, HW, "hgrn_bwd")
    d_proj = jnp.concatenate([d_q, d_f, d_i, d_g, d_xr, d_yr], axis=1)
    small_names = ["ln1_w", "lb_gamma", "hg_norm_w", "lru_conv_w", "lru_conv_b", "lru_wa", "lru_ba", "lru_wx",
                   "lru_bx", "lru_lambda", "lru_norm_w", "ln2_w", "ffn_conv_w", "ffn_conv_b", "final_norm_w"]
    small_rest = [d_lbg, d_hgw, d_lcw, d_lcb, d_wa, d_ba, d_wx, d_bx, d_lam, d_lru_norm, d_ln2, d_fcw, d_fcb, d_final_w]

    def through_w_in(tok):
        d_hn1 = _matmul(d_proj, W_in, "NT", F32, 2048, 1024, 1024, after=tok, name="mm_d_hn1")
        dx, d_ln1 = _rms_bwd(x2, ln1_w, d_hn1, 0, dh1, False, "rms1_bwd")
        red = _allreduce_small(_pack([loss_part[0:1, 0:1], d_ln1] + small_rest), "allreduce_small")
        return (dx, d_ln1, red), red

    red_in, (dx, d_ln1, red) = _cut_gradient(hn1, d_proj, True, 1024, 512, red_out[4], "w_in", through_w_in)
    small_grads = [d_ln1] + small_rest

    def slots_on_their_way(red, col_sharded, after, tok, tag):
        _, got = _scatter_wait(red, after, col_sharded, "scatter_wait_" + tag)
        return _pass_start(got, _slot_windows, 4, tok, "scatter_pass_start_" + tag)

    def update(sent, after, w, m, v, tag):
        slots = _pass_wait(sent, _slot_windows, after, "scatter_pass_wait_" + tag)
        return _adamw_big(w, m, v, slots, "adamw_" + tag)

    sent_down = slots_on_their_way(red_down, False, red_in[4], red_in[4], "w_down")
    sent_up = slots_on_their_way(red_up, True, sent_down[3], sent_down[3], "w_up")
    sent_out = slots_on_their_way(red_out, False, sent_up[3], sent_up[3], "w_out")
    big = {}
    big["ffn_w_down"] = update(sent_down, sent_out[3], ffn_w_down[0], m_ffn_w_down[0], v_ffn_w_down[0], "w_down")
    big["ffn_w_up"] = update(sent_up, big["ffn_w_down"][1], ffn_w_up[0], m_ffn_w_up[0], v_ffn_w_up[0], "w_up")
    big["w_out"] = update(sent_out, big["ffn_w_up"][1], w_out[0], m_w_out[0], v_w_out[0], "w_out")
    big["w_in"] = _reduce_finish(red_in, big["w_out"][1], True, w_in[0], m_w_in[0], v_w_in[0], "w_in")

    red = _unpack(red, [(1, 1)] + [g.shape for g in small_grads])
    loss = red[0].reshape(())
    gs = dict(zip(small_names, red[1:]))
    nlc, nfc = lru_conv_w.shape[2], ffn_conv_w.shape[2]
    gs["lru_conv_w"] = lax.dynamic_slice_in_dim(gs["lru_conv_w"], jchip * nlc, nlc, axis=1)
    gs["ffn_conv_w"] = lax.dynamic_slice_in_dim(gs["ffn_conv_w"], jchip * nfc, nfc, axis=1)
    args = dict(ln1_w=(ln1_w, m_ln1_w, v_ln1_w), lb_gamma=(lb_gamma, m_lb_gamma, v_lb_gamma),
                hg_norm_w=(hg_norm_w, m_hg_norm_w, v_hg_norm_w), lru_conv_w=(lru_conv_w, m_lru_conv_w, v_lru_conv_w),
                lru_conv_b=(lru_conv_b, m_lru_conv_b, v_lru_conv_b), lru_wa=(lru_wa, m_lru_wa, v_lru_wa),
                lru_ba=(lru_ba, m_lru_ba, v_lru_ba), lru_wx=(lru_wx, m_lru_wx, v_lru_wx),
                lru_bx=(lru_bx, m_lru_bx, v_lru_bx), lru_lambda=(lru_lambda, m_lru_lambda, v_lru_lambda),
                lru_norm_w=(lru_norm_w, m_lru_norm_w, v_lru_norm_w), ln2_w=(ln2_w, m_ln2_w, v_ln2_w),
                ffn_conv_w=(ffn_conv_w, m_ffn_conv_w, v_ffn_conv_w), ffn_conv_b=(ffn_conv_b, m_ffn_conv_b, v_ffn_conv_b),
                final_norm_w=(final_norm_w, m_final_norm_w, v_final_norm_w))
    shapes = [args[n][0].shape for n in small_names]
    upd = _adamw_small(_pack([args[n][0] for n in small_names]), _pack([gs[n] for n in small_names]),
                       _pack([args[n][1] for n in small_names]), _pack([args[n][2] for n in small_names]), "adamw_small")
    s_delta, s_m, s_v = (dict(zip(small_names, _unpack(u, shapes))) for u in upd)

    order = ["ln1_w", "w_in", "lb_gamma", "hg_norm_w", "lru_conv_w", "lru_conv_b", "lru_wa", "lru_ba", "lru_wx",
             "lru_bx", "lru_lambda", "lru_norm_w", "w_out", "ln2_w", "ffn_w_up", "ffn_conv_w", "ffn_conv_b",
             "ffn_w_down", "final_norm_w"]
    full_shape = dict(w_in=w_in.shape, w_out=w_out.shape, ffn_w_up=ffn_w_up.shape, ffn_w_down=ffn_w_down.shape)
    grads, deltas, new_m, new_v = [], [], [], []
    for n in order:
        if n in big:
            g, d, mn, vn = (t.reshape(full_shape[n]) for t in big[n])
        else:
            g, d, mn, vn = gs[n].reshape(args[n][0].shape), s_delta[n], s_m[n], s_v[n]
        grads.append(g), deltas.append(d), new_m.append(mn), new_v.append(vn)
    return (loss, dx.reshape(B, S, D), *grads, *deltas, *new_m, *new_v)
```

```python
import functools
import math

import numpy as np
import jax
import jax.numpy as jnp
from jax import lax
from jax.experimental import pallas as pl
from jax.experimental.pallas import tpu as pltpu

F32 = jnp.float32
MXU_DTYPE = jnp.bfloat16
ACT_DTYPE = jnp.bfloat16

EPS = 1e-6
HEAD_DIM = 128
CHUNK = 64
LEVEL_HALVES = (32, 16, 8, 4, 2, 1)
LRU_CONV = 4
FFN_CONV = 3
LRU_C = 8.0
ADAM_LR, ADAM_B1, ADAM_B2, ADAM_EPS, ADAM_WD, ADAM_STEP = 0.001, 0.9, 0.999, 1e-08, 0.01, 10

V7X_LANES = 128
V7X_VMEM_BUDGET = 56 << 20

NN = (((1,), (0,)), ((), ()))
NT = (((1,), (1,)), ((), ()))
TN = (((0,), (0,)), ((), ()))
MESH = pl.DeviceIdType.MESH
HBM_SPEC = pl.BlockSpec(memory_space=pl.ANY)


def _pcall(kern, **kw):
    return pl.pallas_call(kern, **kw)


def _params(sem=None, vmem=None):
    kw = {}
    if sem is not None:
        kw["dimension_semantics"] = sem
    if vmem is not None:
        kw["vmem_limit_bytes"] = int(min(max(vmem, 16 << 20), V7X_VMEM_BUDGET))
    return pltpu.CompilerParams(**kw)


def _dot(a, b, dims=NN):
    return lax.dot_general(a.astype(MXU_DTYPE), b.astype(MXU_DTYPE), dims, preferred_element_type=F32)


def _tile(dim, pref, align):
    t = min(pref, dim) // align * align
    while t >= align:
        if dim % t == 0:
            return t
        t -= align
    return dim


def _sigmoid(x):
    return 1.0 / (1.0 + jnp.exp(-x))


def _silu_and_grad(x):
    s = _sigmoid(x)
    return x * s, s * (1.0 + x * (1.0 - s))


def _gelu_and_grad(x):
    k0, k1 = math.sqrt(2.0 / math.pi), 0.044715
    t = jnp.tanh(k0 * (x + k1 * x * x * x))
    g = 0.5 * x * (1.0 + t)
    dg = 0.5 * (1.0 + t) + 0.5 * x * (1.0 - t * t) * k0 * (1.0 + 3.0 * k1 * x * x)
    return g, dg


def _one_minus_exp(x):
    p = x * (1.0 + x * (0.5 + x * (1.0 / 6.0 + x * (1.0 / 24.0 + x * (1.0 / 120.0)))))
    return jnp.where(x > -0.05, -p, 1.0 - jnp.exp(x))


def _rows(n):
    return lax.broadcasted_iota(jnp.int32, (n, 1), 0)


def _matmul(a, b, mode, out_dtype, tm, tn, tk, add=None, after=None, n_outer=False, name="mm"):
    if mode == "TN":
        K, M = a.shape
    else:
        M, K = a.shape
    N = b.shape[0] if mode == "NT" else b.shape[1]
    tm, tn = _tile(M, tm, V7X_LANES), _tile(N, tn, V7X_LANES)
    tk = _tile(K, tk, V7X_LANES)
    nk = K // tk
    dims = {"NN": NN, "NT": NT, "TN": TN}[mode]
    order = (lambda f: (lambda j, i, k: f(i, j, k))) if n_outer else (lambda f: f)
    a_spec = (pl.BlockSpec((tk, tm), order(lambda i, j, k: (k, i))) if mode == "TN"
              else pl.BlockSpec((tm, tk), order(lambda i, j, k: (i, k))))
    b_spec = (pl.BlockSpec((tn, tk), order(lambda i, j, k: (j, k))) if mode == "NT"
              else pl.BlockSpec((tk, tn), order(lambda i, j, k: (k, j))))
    o_spec = pl.BlockSpec((tm, tn), order(lambda i, j, k: (i, j)))
    has_add = add is not None

    def kern(*refs):
        a_ref, b_ref = refs[:2]
        add_ref = refs[2] if has_add else None

        def finish(r, o_ref):
            if has_add:
                r = r + add_ref[...]
            o_ref[...] = r.astype(out_dtype)

        if nk == 1:
            finish(_dot(a_ref[...], b_ref[...], dims), refs[-1])
            return
        o_ref, acc_ref = refs[-2:]
        k = pl.program_id(2)

        @pl.when(k == 0)
        def _():
            acc_ref[...] = jnp.zeros_like(acc_ref)

        acc_ref[...] += _dot(a_ref[...], b_ref[...], dims)

        @pl.when(k == nk - 1)
        def _():
            finish(acc_ref[...], o_ref)

    ab = jnp.dtype(a.dtype).itemsize
    ob = jnp.dtype(out_dtype).itemsize
    vmem = 2 * (tm * tk + tk * tn) * ab + tm * tn * (8 + 2 * ob + (8 if has_add else 0)) + (4 << 20)
    ins = [a, b] + ([add] if has_add else []) + ([after] if after is not None else [])
    in_specs = [a_spec, b_spec] + ([o_spec] if has_add else []) + ([HBM_SPEC] if after is not None else [])
    grid = (N // tn, M // tm, nk) if n_outer else (M // tm, N // tn, nk)
    return _pcall(
        kern, name=name, grid=grid,
        in_specs=in_specs, out_specs=o_spec,
        out_shape=jax.ShapeDtypeStruct((M, N), out_dtype),
        scratch_shapes=[pltpu.VMEM((tm, tn), F32)] if nk > 1 else [],
        compiler_params=_params(("parallel", "parallel", "arbitrary"), vmem),
    )(*ins)


def _matmul_tn_half(a, b, half, add, tm, tn, by_rows, name):
    K, M = a.shape
    N = b.shape[1]
    Mo, No = (M // 2, N) if by_rows else (M, N // 2)
    tm, tn = _tile(Mo, tm, V7X_LANES), _tile(No, tn, V7X_LANES)
    nm, nn = Mo // tm, No // tn
    has_add = add is not None

    def kern(h_ref, a_ref, b_ref, *rest):
        r = _dot(a_ref[...], b_ref[...], TN)
        if has_add:
            r = r + rest[0][...].astype(F32)
        rest[-1][...] = r.astype(ACT_DTYPE)

    if by_rows:
        a_map, b_map = (lambda i, j, h: (0, h[0] * nm + i)), (lambda i, j, h: (0, j))
        o_map, grid = (lambda i, j, h: (i, j)), (nm, nn)
    else:
        a_map, b_map = (lambda j, i, h: (0, i)), (lambda j, i, h: (0, h[0] * nn + j))
        o_map, grid = (lambda j, i, h: (i, j)), (nn, nm)
    blk = pl.BlockSpec((tm, tn), o_map)
    grid_spec = pltpu.PrefetchScalarGridSpec(
        num_scalar_prefetch=1, grid=grid,
        in_specs=[pl.BlockSpec((K, tm), a_map), pl.BlockSpec((K, tn), b_map)] + ([blk] if has_add else []),
        out_specs=blk)
    ab = jnp.dtype(a.dtype).itemsize
    vmem = 2 * K * (tm + tn) * ab + tm * tn * 16 + (4 << 20)
    return _pcall(kern, name=name, grid_spec=grid_spec, out_shape=jax.ShapeDtypeStruct((Mo, No), ACT_DTYPE),
                  compiler_params=_params(("parallel", "parallel"), vmem))(half, a, b, *([add] if has_add else []))


def _matmul_col_slice(a, b, j, n_total, prev, tm, tn, name):
    M, K = a.shape
    N, width = n_total, n_total // 4
    alone = b.shape[1] == width
    tm, tn = _tile(M, tm, V7X_LANES), _tile(width, tn, V7X_LANES)
    nn = width // tn
    has_prev = prev is not None

    def kern(j_ref, a_ref, b_ref, *rest):
        rest[-1][...] = _dot(a_ref[...], b_ref[...])

    b_map = (lambda i, n, j: (0, n)) if alone else (lambda i, n, j: (0, j[0] * nn + n))
    grid_spec = pltpu.PrefetchScalarGridSpec(
        num_scalar_prefetch=1, grid=(M // tm, nn),
        in_specs=[pl.BlockSpec((tm, K), lambda i, n, j: (i, 0)),
                  pl.BlockSpec((K, tn), b_map)] + ([HBM_SPEC] if has_prev else []),
        out_specs=pl.BlockSpec((tm, tn), lambda i, n, j: (i, j[0] * nn + n)))
    ab = jnp.dtype(a.dtype).itemsize
    vmem = 2 * K * (tm + tn) * ab + tm * tn * 16 + (4 << 20)
    return _pcall(kern, name=name, grid_spec=grid_spec, out_shape=jax.ShapeDtypeStruct((M, N), F32),
                  input_output_aliases={3: 0} if has_prev else {},
                  compiler_params=_params(("parallel", "parallel"), vmem))(j, a, b, *([prev] if has_prev else []))


def _rms_fwd(x, w, name):
    T, D = x.shape
    tm = _tile(T, 256, 16)

    def kern(x_ref, w_ref, o_ref):
        xv = x_ref[...]
        r = lax.rsqrt(jnp.mean(xv * xv, axis=-1, keepdims=True) + EPS)
        o_ref[...] = (xv * r * w_ref[...]).astype(ACT_DTYPE)

    return _pcall(kern, name=name, grid=(T // tm,),
                  in_specs=[pl.BlockSpec((tm, D), lambda i: (i, 0)), pl.BlockSpec((1, D), lambda i: (0, 0))],
                  out_specs=pl.BlockSpec((tm, D), lambda i: (i, 0)),
                  out_shape=jax.ShapeDtypeStruct((T, D), ACT_DTYPE),
                  compiler_params=_params(("parallel",), 8 * tm * D * 4))(x, w)


def _rms_bwd(x, w, g, g_col, res, want_act, name, after=None):
    T, D = x.shape
    tm = _tile(T, 256, 16)
    has_res = res is not None

    def kern(*refs):
        refs = list(refs)
        x_ref, w_ref, g_ref = refs[:3]
        res_ref = refs[3] if has_res else None
        outs = refs[3 + has_res + (after is not None):]
        dx_ref = outs[0]
        dxa_ref = outs[1] if want_act else None
        dw_ref = outs[-1]
        i = pl.program_id(0)
        xv = x_ref[...]
        gv = g_ref[...].astype(F32)
        r = lax.rsqrt(jnp.mean(xv * xv, axis=-1, keepdims=True) + EPS)
        gw = gv * w_ref[...]
        dx = r * gw - xv * (r * r * r) * jnp.mean(gw * xv, axis=-1, keepdims=True)
        if has_res:
            dx = dx + res_ref[...]
        dx_ref[...] = dx
        if want_act:
            dxa_ref[...] = dx.astype(ACT_DTYPE)

        @pl.when(i == 0)
        def _():
            dw_ref[...] = jnp.zeros_like(dw_ref)

        dw_ref[...] += jnp.sum(gv * xv * r, axis=0, keepdims=True)

    row = pl.BlockSpec((tm, D), lambda i: (i, 0))
    vec = pl.BlockSpec((1, D), lambda i: (0, 0))
    in_specs = ([row, vec, pl.BlockSpec((tm, D), lambda i: (i, g_col))] + ([row] if has_res else [])
                + ([HBM_SPEC] if after is not None else []))
    out_specs = [row] + ([row] if want_act else []) + [vec]
    out_shape = ([jax.ShapeDtypeStruct((T, D), F32)]
                 + ([jax.ShapeDtypeStruct((T, D), ACT_DTYPE)] if want_act else [])
                 + [jax.ShapeDtypeStruct((1, D), F32)])
    ins = [x, w, g] + ([res] if has_res else []) + ([after] if after is not None else [])
    return _pcall(kern, name=name, grid=(T // tm,), in_specs=in_specs, out_specs=out_specs,
                  out_shape=out_shape, compiler_params=_params(("arbitrary",), 14 * tm * D * 4))(*ins)


def _loss_bwd(h, target, w, name):
    T, D = h.shape
    tm = _tile(T, 256, 16)

    def kern(h_ref, t_ref, w_ref, dh_ref, dha_ref, dw_ref, loss_ref):
        i = pl.program_id(0)
        hv = h_ref[...]
        r = lax.rsqrt(jnp.mean(hv * hv, axis=-1, keepdims=True) + EPS)
        e = hv * r * w_ref[...] - t_ref[...]
        dy = e * (1.0 / D)
        gw = dy * w_ref[...]
        dh = r * gw - hv * (r * r * r) * jnp.mean(gw * hv, axis=-1, keepdims=True)
        dh_ref[...] = dh
        dha_ref[...] = dh.astype(ACT_DTYPE)

        @pl.when(i == 0)
        def _():
            dw_ref[...] = jnp.zeros_like(dw_ref)
            loss_ref[...] = jnp.zeros_like(loss_ref)

        dw_ref[...] += jnp.sum(dy * hv * r, axis=0, keepdims=True)
        part = 0.5 * jnp.sum(jnp.mean(e * e, axis=-1, keepdims=True), axis=0, keepdims=True)
        loss_ref[...] += jnp.broadcast_to(part, loss_ref.shape)

    row = pl.BlockSpec((tm, D), lambda i: (i, 0))
    vec = pl.BlockSpec((1, D), lambda i: (0, 0))
    return _pcall(kern, name=name, grid=(T // tm,), in_specs=[row, row, vec],
                  out_specs=[row, row, vec, pl.BlockSpec((8, V7X_LANES), lambda i: (0, 0))],
                  out_shape=[jax.ShapeDtypeStruct((T, D), F32), jax.ShapeDtypeStruct((T, D), ACT_DTYPE),
                             jax.ShapeDtypeStruct((1, D), F32), jax.ShapeDtypeStruct((8, V7X_LANES), F32)],
                  compiler_params=_params(("arbitrary",), 14 * tm * D * 4))(h, target, w)


def _conv(x, w_ref, b, width):
    S = x.shape[0]
    row = _rows(S)
    y = b + x * w_ref[pl.ds(width - 1, 1), :]
    for j in range(width - 1):
        sh = width - 1 - j
        y = y + jnp.where(row >= sh, pltpu.roll(x, sh, 0), 0.0) * w_ref[pl.ds(j, 1), :]
    return y


def _conv_bwd(x, dy, w_ref, width):
    S = dy.shape[0]
    row = _rows(S)
    dx = dy * w_ref[pl.ds(width - 1, 1), :]
    dw = [None] * (width - 1) + [jnp.sum(x * dy, axis=0, keepdims=True)]
    for j in range(width - 1):
        sh = width - 1 - j
        dys = jnp.where(row < S - sh, pltpu.roll(dy, S - sh, 0), 0.0)
        dx = dx + dys * w_ref[pl.ds(j, 1), :]
        dw[j] = jnp.sum(x * dys, axis=0, keepdims=True)
    return dx, dw


def _ffn_act(up, cw, cb, B, name):
    T, F2 = up.shape
    S, F = T // B, F2 // 2
    tw = _tile(F, 256, V7X_LANES)
    nt = F // tw

    def kern(g_ref, v_ref, wg_ref, wv_ref, bg_ref, bv_ref, o_ref, mg_ref, mv_ref):
        gc = _conv(g_ref[...], wg_ref, bg_ref[...], FFN_CONV)
        vc = _conv(v_ref[...], wv_ref, bv_ref[...], FFN_CONV)
        silu, dsilu = _silu_and_grad(gc)
        o_ref[...] = (silu * vc).astype(ACT_DTYPE)
        mg_ref[...] = (vc * dsilu).astype(ACT_DTYPE)
        mv_ref[...] = silu.astype(ACT_DTYPE)

    blk = lambda off: pl.BlockSpec((S, tw), lambda b, i: (b, off + i))
    wblk = lambda off: pl.BlockSpec((FFN_CONV, tw), lambda b, i: (0, off + i))
    bblk = lambda off: pl.BlockSpec((1, tw), lambda b, i: (0, off + i))
    half = jax.ShapeDtypeStruct((T, F), ACT_DTYPE)
    return _pcall(kern, name=name, grid=(B, nt),
                  in_specs=[blk(0), blk(nt), wblk(0), wblk(nt), bblk(0), bblk(nt)],
                  out_specs=[blk(0), blk(0), blk(0)], out_shape=[half, half, half],
                  compiler_params=_params(("parallel", "parallel"), 20 * S * tw * 4))(up, up, cw, cw, cb, cb)


def _ffn_act_bwd(up, cw, mg, mv, d_act, B, name):
    T, F2 = up.shape
    S, F = T // B, F2 // 2
    tw = _tile(F, 256, V7X_LANES)
    nt = F // tw

    def kern(s_ref, ws_ref, mg_ref, mv_ref, da_ref, du_ref, dcw_ref, dcb_ref):
        t, b = pl.program_id(0), pl.program_id(1)
        mult = jnp.where(t < nt, mg_ref[...], mv_ref[...])
        d = da_ref[...].astype(F32) * mult.astype(F32)
        dx, dw = _conv_bwd(s_ref[...], d, ws_ref, FFN_CONV)
        du_ref[...] = dx.astype(ACT_DTYPE)

        @pl.when(b == 0)
        def _():
            dcw_ref[...] = jnp.zeros_like(dcw_ref)
            dcb_ref[...] = jnp.zeros_like(dcb_ref)

        for j, rj in enumerate(dw):
            dcw_ref[pl.ds(j, 1), :] += rj
        dcb_ref[...] += jnp.sum(d, axis=0, keepdims=True)

    own = lambda t, b: (b, t % nt)
    return _pcall(
        kern, name=name, grid=(2 * nt, B),
        in_specs=[pl.BlockSpec((S, tw), lambda t, b: (b, t)),
                  pl.BlockSpec((FFN_CONV, tw), lambda t, b: (0, t)),
                  pl.BlockSpec((S, tw), own), pl.BlockSpec((S, tw), own), pl.BlockSpec((S, tw), own)],
        out_specs=[pl.BlockSpec((S, tw), lambda t, b: (b, t)),
                   pl.BlockSpec((FFN_CONV, tw), lambda t, b: (0, t)),
                   pl.BlockSpec((1, tw), lambda t, b: (0, t))],
        out_shape=[jax.ShapeDtypeStruct((T, F2), ACT_DTYPE), jax.ShapeDtypeStruct((FFN_CONV, F2), F32),
                   jax.ShapeDtypeStruct((1, F2), F32)],
        compiler_params=_params(("parallel", "arbitrary"), 20 * S * tw * 4),
    )(up, cw, mg, mv, d_act)


def _hgrn_tables():
    C = CHUNK
    t = np.arange(C)
    mats = [(t[:, None] >= t[None, :]).astype(np.float32)]
    masks = []
    gsum = [(t[:, None] <= t[None, :]).astype(np.float32), (t[:, None] > t[None, :]).astype(np.float32)]
    for hs in LEVEL_HALVES:
        m = (t // (2 * hs)) * 2 * hs + hs
        later = t >= m
        d = np.zeros((C, C), np.float32)
        for i in range(C):
            if later[i]:
                d[i, m[i]:i + 1] = 1.0
            else:
                d[i, i + 1:m[i]] = -1.0
        mats.append(d)
        same = (t[:, None] // (2 * hs)) == (t[None, :] // (2 * hs))
        masks.append((same & later[:, None] & (~later)[None, :]).astype(np.float32))
        gsum.append((same & later[:, None] & (t[None, :] >= t[:, None])).astype(np.float32))
        gsum.append((same & (~later)[:, None] & (t[None, :] < t[:, None])).astype(np.float32))
    return np.concatenate(mats, 0), np.stack(masks, 0), np.concatenate(gsum, 1)


def _split_dot(mat, v):
    hi = v.astype(MXU_DTYPE)
    lo = (v - hi.astype(F32)).astype(MXU_DTYPE)
    r = _dot(mat, jnp.concatenate([hi, lo], axis=1))
    n = v.shape[1]
    return r[:, :n] + r[:, n:]


def _hgrn_gates(qr, fr, lb, mc):
    C = CHUNK
    q, dq_dqr = _silu_and_grad(qr)
    sf = _sigmoid(fr)
    f = lb + (1.0 - lb) * sf
    k = 1.0 - f
    dall = _split_dot(mc, jnp.log(f))
    b = dall[0:C]
    dl = [dall[C * (l + 1):C * (l + 2)] for l in range(len(LEVEL_HALVES))]
    eq = [jnp.exp(jnp.minimum(d, 0.0)) for d in dl]
    ek = [jnp.exp(jnp.minimum(-d, 0.0)) for d in dl]
    return q, dq_dqr, sf, f, k, b, eq, ek


def _hgrn_scores(q, k, eq, ek, masks_ref):
    p = jnp.where(_rows(CHUNK) == lax.broadcasted_iota(jnp.int32, (1, CHUNK), 1),
                  jnp.sum(q * k, axis=-1, keepdims=True), 0.0)
    for l in range(len(LEVEL_HALVES)):
        p = p + masks_ref[l] * _dot(q * eq[l], k * ek[l], NT)
    return p


def _hgrn_fwd(proj, lb_gamma, norm_w, B, HW, name):
    T = proj.shape[0]
    S, H, C = T // B, HW // HEAD_DIM, CHUNK
    NC = S // C
    mc_np, masks_np, _ = _hgrn_tables()
    mc, masks = jnp.asarray(mc_np, MXU_DTYPE), jnp.asarray(masks_np, F32)

    def kern(q_ref, f_ref, i_ref, g_ref, lbg_ref, nw_ref, mc_ref, masks_ref, oraw_ref, o_ref, st_ref, p_ref):
        g0, g1 = lbg_ref[pl.ds(0, 1), :], lbg_ref[pl.ds(1, 1), :]
        mx = jnp.maximum(g0, g1)
        e0, e1 = jnp.exp(g0 - mx), jnp.exp(g1 - mx)
        lb = e0 / (e0 + e1)
        nw = nw_ref[...]
        mcv = mc_ref[...]

        def body(n, sts):
            out = []
            for s, st in enumerate(sts):
                rows = pl.ds(pl.multiple_of(s * S + n * C, C), C)
                st_ref[s, n] = st
                q, _, _, _, k, b, eq, ek = _hgrn_gates(q_ref[rows, :], f_ref[rows, :], lb, mcv)
                v = i_ref[rows, :]
                p = _hgrn_scores(q, k, eq, ek, masks_ref)
                p_ref[s, n] = p
                o = _dot(q * jnp.exp(b), st, NT) + _dot(p, v)
                b_last = b[C - 1:C]
                out.append(st * jnp.exp(b_last) + _dot(v, k * jnp.exp(b_last - b), TN))
                oraw_ref[rows, :] = o
                r = lax.rsqrt(jnp.mean(o * o, axis=-1, keepdims=True) + EPS)
                gate, _ = _silu_and_grad(g_ref[rows, :])
                o_ref[rows, :] = (o * r * nw * gate).astype(ACT_DTYPE)
            return tuple(out)

        lax.fori_loop(0, NC, body, tuple(jnp.zeros((HEAD_DIM, HEAD_DIM), F32) for _ in range(B)))

    col = lambda off: pl.BlockSpec((T, HEAD_DIM), lambda h: (0, off + h))
    return _pcall(
        kern, name=name, grid=(H,),
        in_specs=[col(0), col(H), col(2 * H), col(3 * H),
                  pl.BlockSpec((2, HEAD_DIM), lambda h: (0, h)),
                  pl.BlockSpec((1, HEAD_DIM), lambda h: (0, h)),
                  pl.BlockSpec(mc.shape, lambda h: (0, 0)),
                  pl.BlockSpec(masks.shape, lambda h: (0, 0, 0))],
        out_specs=[col(0), col(0),
                   pl.BlockSpec((B, None, NC, HEAD_DIM, HEAD_DIM), lambda h: (0, h, 0, 0, 0)),
                   pl.BlockSpec((B, None, NC, C, C), lambda h: (0, h, 0, 0, 0))],
        out_shape=[jax.ShapeDtypeStruct((T, HW), F32), jax.ShapeDtypeStruct((T, HW), ACT_DTYPE),
                   jax.ShapeDtypeStruct((B, H, NC, HEAD_DIM, HEAD_DIM), F32),
                   jax.ShapeDtypeStruct((B, H, NC, C, C), F32)],
        compiler_params=_params(("parallel",), 20 * T * HEAD_DIM * 4 + (8 << 20)),
    )(proj, proj, proj, proj, lb_gamma, norm_w, mc, masks)


def _hgrn_bwd(proj, lb_gamma, norm_w, o_raw, states, scores, d_mix, B, HW, name):
    T = proj.shape[0]
    S, H, C = T // B, HW // HEAD_DIM, CHUNK
    NC = S // C
    mc_np, masks_np, gsum_np = _hgrn_tables()
    mc, masks, gsum = jnp.asarray(mc_np, MXU_DTYPE), jnp.asarray(masks_np, F32), jnp.asarray(gsum_np, MXU_DTYPE)
    nl = len(LEVEL_HALVES)

    def kern(q_ref, f_ref, i_ref, g_ref, lbg_ref, nw_ref, mc_ref, masks_ref, gsum_ref, oraw_ref, st_ref, p_ref, do_ref,
             dq_ref, df_ref, di_ref, dg_ref, dlbg_ref, dnw_ref):
        g0, g1 = lbg_ref[pl.ds(0, 1), :], lbg_ref[pl.ds(1, 1), :]
        mx = jnp.maximum(g0, g1)
        e0, e1 = jnp.exp(g0 - mx), jnp.exp(g1 - mx)
        lb = e0 / (e0 + e1)
        nw = nw_ref[...]
        mcv, gsumv = mc_ref[...], gsum_ref[...]

        def chunk(s, n, dst, dlb, dnw):
            rows = pl.ds(pl.multiple_of(s * S + n * C, C), C)
            qr, fr, v = q_ref[rows, :], f_ref[rows, :], i_ref[rows, :]
            q, dq_dqr, sf, f, k, b, eq, ek = _hgrn_gates(qr, fr, lb, mcv)
            o = oraw_ref[rows, :]
            dout = do_ref[rows, :].astype(F32)
            gate, dgate = _silu_and_grad(g_ref[rows, :])
            r = lax.rsqrt(jnp.mean(o * o, axis=-1, keepdims=True) + EPS)
            dg_ref[rows, :] = (dout * o * r * nw * dgate).astype(ACT_DTYPE)
            don = dout * gate
            dnw = dnw + jnp.sum(don * o * r, axis=0, keepdims=True)
            gw = don * nw
            do = r * gw - o * (r * r * r) * jnp.mean(gw * o, axis=-1, keepdims=True)
            st_prev = st_ref[s, n]
            eb = jnp.exp(b)
            b_last = b[C - 1:C]
            ebl = jnp.exp(b_last - b)
            p = p_ref[s, n]
            dp = _dot(do, v, NT)
            dpd = jnp.sum(do * v, axis=-1, keepdims=True)
            dq_state = _dot(do, st_prev) * eb
            dk_state = _dot(v, dst) * ebl
            dq = dq_state + dpd * k
            dk = dk_state + dpd * q
            pairs = [q * dq_state, k * dk_state]
            for l in range(nl):
                mdp = masks_ref[l] * dp
                dql = _dot(mdp, k * ek[l]) * eq[l]
                dkl = _dot(mdp, q * eq[l], TN) * ek[l]
                dq, dk = dq + dql, dk + dkl
                pairs += [q * dql, k * dkl]
            dv = _dot(p, do, TN) + _dot(k * ebl, dst, NT)
            through = jnp.exp(b_last) * jnp.sum(dst * st_prev, axis=0, keepdims=True)
            dlg = _split_dot(gsumv, jnp.concatenate(pairs, axis=0)) + through
            dst = dst * jnp.exp(b_last) + _dot(do, q * eb, TN)
            dq_ref[rows, :] = (dq * dq_dqr).astype(ACT_DTYPE)
            dfv = dlg / f - dk
            df_ref[rows, :] = (dfv * (1.0 - lb) * sf * (1.0 - sf)).astype(ACT_DTYPE)
            di_ref[rows, :] = dv.astype(ACT_DTYPE)
            dlb = dlb + jnp.sum(dfv * (1.0 - sf), axis=0, keepdims=True)
            return dst, dlb, dnw

        def body(it, carry):
            dsts, dlb, dnw = carry
            out = []
            for s, dst in enumerate(dsts):
                dst, dlb, dnw = chunk(s, NC - 1 - it, dst, dlb, dnw)
                out.append(dst)
            return tuple(out), dlb, dnw

        zrow = jnp.zeros((1, HEAD_DIM), F32)
        zst = tuple(jnp.zeros((HEAD_DIM, HEAD_DIM), F32) for _ in range(B))
        _, dlb, dnw = lax.fori_loop(0, NC, body, (zst, zrow, zrow))
        dg0 = dlb * lb * (1.0 - lb)
        dlbg_ref[pl.ds(0, 1), :] = dg0
        dlbg_ref[pl.ds(1, 1), :] = -dg0
        dnw_ref[...] = dnw

    col = lambda off: pl.BlockSpec((T, HEAD_DIM), lambda h: (0, off + h))
    full = lambda a: pl.BlockSpec(a.shape, lambda h: (0,) * a.ndim)
    part = jax.ShapeDtypeStruct((T, HW), ACT_DTYPE)
    return _pcall(
        kern, name=name, grid=(H,),
        in_specs=[col(0), col(H), col(2 * H), col(3 * H),
                  pl.BlockSpec((2, HEAD_DIM), lambda h: (0, h)),
                  pl.BlockSpec((1, HEAD_DIM), lambda h: (0, h)),
                  full(mc), full(masks), full(gsum), col(0),
                  pl.BlockSpec((B, None, NC, HEAD_DIM, HEAD_DIM), lambda h: (0, h, 0, 0, 0)),
                  pl.BlockSpec((B, None, NC, C, C), lambda h: (0, h, 0, 0, 0)),
                  col(0)],
        out_specs=[col(0), col(0), col(0), col(0),
                   pl.BlockSpec((2, HEAD_DIM), lambda h: (0, h)),
                   pl.BlockSpec((1, HEAD_DIM), lambda h: (0, h))],
        out_shape=[part, part, part, part, jax.ShapeDtypeStruct((2, HW), F32), jax.ShapeDtypeStruct((1, HW), F32)],
        compiler_params=_params(("parallel",), 28 * T * HEAD_DIM * 4 + (8 << 20)),
    )(proj, proj, proj, proj, lb_gamma, norm_w, mc, masks, gsum, o_raw, states, scores, d_mix)


def _lru_gates(xr, cw_ref, cb, wa, ba, wx, bx, lam):
    S = xr.shape[0]
    xb = _conv(xr, cw_ref, cb, LRU_CONV)
    r = _sigmoid(_dot(xb, wa) + ba)
    ig = _sigmoid(_dot(xb, wx) + bx)
    sp = jnp.maximum(-lam, 0.0) + jnp.log(1.0 + jnp.exp(-jnp.abs(lam)))
    la = -LRU_C * r * sp
    a = jnp.exp(la)
    mult = jnp.where(_rows(S) == 0, 1.0, jnp.sqrt(_one_minus_exp(2.0 * la)))
    return xb, r, ig, sp, a, mult


def _scan_rows(a_ref, u_ref, h_ref, reverse):
    S, W = a_ref.shape
    nb = S // 8
    row = _rows(8)

    def body(it, carry):
        blk = nb - 1 - it if reverse else it
        rows = pl.ds(pl.multiple_of(blk * 8, 8), 8)
        a, u = a_ref[rows, :], u_ref[rows, :]
        for d in (1, 2, 4):
            sh = 8 - d if reverse else d
            keep = (row < 8 - d) if reverse else (row >= d)
            u = u + jnp.where(keep, a * pltpu.roll(u, sh, 0), 0.0)
            a = jnp.where(keep, a * pltpu.roll(a, sh, 0), a)
        h = u + a * carry
        h_ref[rows, :] = h
        return h[0:1] if reverse else h[7:8]

    lax.fori_loop(0, nb, body, jnp.zeros((1, W), F32))


def _lru_fwd(proj, cw, cb, wa, ba, wx, bx, lam, B, HW, LW, name):
    T = proj.shape[0]
    S, NB = T // B, LW // HEAD_DIM
    xoff, yoff = 4 * HW // HEAD_DIM, 4 * HW // HEAD_DIM + NB

    def kern(x_ref, y_ref, cw_ref, cb_ref, wa_ref, ba_ref, wx_ref, bx_ref, lam_ref, h_ref, z_ref, a_s, u_s):
        xb, _, ig, _, a, mult = _lru_gates(x_ref[...], cw_ref, cb_ref[...], wa_ref[...], ba_ref[...],
                                           wx_ref[...], bx_ref[...], lam_ref[...])
        a_s[...] = a
        u_s[...] = xb * ig * mult
        _scan_rows(a_s, u_s, h_ref, False)
        gy, _ = _gelu_and_grad(y_ref[...])
        z_ref[...] = h_ref[...] * gy

    blk = lambda off: pl.BlockSpec((S, HEAD_DIM), lambda b, n: (b, off + n))
    vec = pl.BlockSpec((1, HEAD_DIM), lambda b, n: (0, n))
    mat = pl.BlockSpec((None, HEAD_DIM, HEAD_DIM), lambda b, n: (n, 0, 0))
    return _pcall(
        kern, name=name, grid=(B, NB),
        in_specs=[blk(xoff), blk(yoff), pl.BlockSpec((LRU_CONV, HEAD_DIM), lambda b, n: (0, n)),
                  vec, mat, vec, mat, vec, vec],
        out_specs=[blk(0), blk(0)],
        out_shape=[jax.ShapeDtypeStruct((T, LW), F32), jax.ShapeDtypeStruct((T, LW), F32)],
        scratch_shapes=[pltpu.VMEM((S, HEAD_DIM), F32), pltpu.VMEM((S, HEAD_DIM), F32)],
        compiler_params=_params(("parallel", "parallel"), 24 * S * HEAD_DIM * 4),
    )(proj, proj, cw, cb, wa, ba, wx, bx, lam)


def _lru_bwd(proj, cw, cb, wa, ba, wx, bx, lam, h, dz, B, HW, LW, name):
    T = proj.shape[0]
    S, NB = T // B, LW // HEAD_DIM
    xoff, yoff = 4 * HW // HEAD_DIM, 4 * HW // HEAD_DIM + NB

    def kern(x_ref, y_ref, cw_ref, cb_ref, wa_ref, ba_ref, wx_ref, bx_ref, lam_ref, h_ref, dz_ref,
             dx_ref, dy_ref, dwa_ref, dwx_ref, dba_ref, dbx_ref, dlam_ref, dcw_ref, dcb_ref, a_s, u_s, dh_s):
        bi = pl.program_id(1)
        row = _rows(S)
        xr, lam = x_ref[...], lam_ref[...]
        wa, wx = wa_ref[...], wx_ref[...]
        xb, r, ig, sp, a, mult = _lru_gates(xr, cw_ref, cb_ref[...], wa, ba_ref[...], wx, bx_ref[...], lam)
        hv, dz = h_ref[...], dz_ref[...]
        gy, dgy = _gelu_and_grad(y_ref[...])
        dy_ref[...] = (dz * hv * dgy).astype(ACT_DTYPE)
        a_s[...] = jnp.where(row < S - 1, pltpu.roll(a, S - 1, 0), 0.0)
        u_s[...] = dz * gy
        _scan_rows(a_s, u_s, dh_s, True)
        dh = dh_s[...]
        h_prev = jnp.where(row >= 1, pltpu.roll(hv, 1, 0), 0.0)
        d_ig = dh * xb * mult
        d_mult = jnp.where(row == 0, 0.0, dh * xb * ig)
        dxb = dh * ig * mult
        dla = dh * h_prev * a - d_mult * (a * a) / mult
        dpre_r = dla * (-LRU_C * sp) * r * (1.0 - r)
        dpre_i = d_ig * ig * (1.0 - ig)
        dxb = dxb + _dot(dpre_r, wa, NT) + _dot(dpre_i, wx, NT)
        dxr, dcw = _conv_bwd(xr, dxb, cw_ref, LRU_CONV)
        dx_ref[...] = dxr.astype(ACT_DTYPE)

        @pl.when(bi == 0)
        def _():
            for ref in (dwa_ref, dwx_ref, dba_ref, dbx_ref, dlam_ref, dcw_ref, dcb_ref):
                ref[...] = jnp.zeros_like(ref)

        dwa_ref[...] += _dot(xb, dpre_r, TN)
        dwx_ref[...] += _dot(xb, dpre_i, TN)
        dba_ref[...] += jnp.sum(dpre_r, axis=0, keepdims=True)
        dbx_ref[...] += jnp.sum(dpre_i, axis=0, keepdims=True)
        dsp = jnp.sum(dla * (-LRU_C) * r, axis=0, keepdims=True)
        dlam_ref[...] += -dsp * _sigmoid(-lam)
        for j, rj in enumerate(dcw):
            dcw_ref[pl.ds(j, 1), :] += rj
        dcb_ref[...] += jnp.sum(dxb, axis=0, keepdims=True)

    blk = lambda off: pl.BlockSpec((S, HEAD_DIM), lambda n, b: (b, off + n))
    vec = pl.BlockSpec((1, HEAD_DIM), lambda n, b: (0, n))
    mat = pl.BlockSpec((None, HEAD_DIM, HEAD_DIM), lambda n, b: (n, 0, 0))
    cwb = pl.BlockSpec((LRU_CONV, HEAD_DIM), lambda n, b: (0, n))
    part = jax.ShapeDtypeStruct((T, LW), ACT_DTYPE)
    vshape = jax.ShapeDtypeStruct((1, LW), F32)
    mshape = jax.ShapeDtypeStruct((NB, HEAD_DIM, HEAD_DIM), F32)
    return _pcall(
        kern, name=name, grid=(NB, B),
        in_specs=[blk(xoff), blk(yoff), cwb, vec, mat, vec, mat, vec, vec, blk(0), blk(0)],
        out_specs=[blk(0), blk(0), mat, mat, vec, vec, vec, cwb, vec],
        out_shape=[part, part, mshape, mshape, vshape, vshape, vshape,
                   jax.ShapeDtypeStruct((LRU_CONV, LW), F32), vshape],
        scratch_shapes=[pltpu.VMEM((S, HEAD_DIM), F32)] * 3,
        compiler_params=_params(("parallel", "arbitrary"), 40 * S * HEAD_DIM * 4),
    )(proj, proj, cw, cb, wa, ba, wx, bx, lam, h, dz)


def _pos():
    return lax.axis_index("x"), lax.axis_index("y"), lax.axis_index("c")


def _other_chips(x, y):
    return [(1 - x, y), (x, 1 - y), (1 - x, 1 - y)]


def _remote(src, dst, send_sems, recv_sems, k, to):
    return pltpu.make_async_remote_copy(src_ref=src, dst_ref=dst, send_sem=send_sems.at[k],
                                        recv_sem=recv_sems.at[k], device_id=to, device_id_type=MESH)


HBM_BLK = pl.BlockSpec(memory_space=pltpu.HBM)
SEM_BLK = pl.BlockSpec(memory_space=pltpu.SEMAPHORE)
VMEM_BLK = pl.BlockSpec(memory_space=pltpu.VMEM)
DATAFLOW = pltpu.SideEffectType.DATAFLOW_SIDE_EFFECTING
TOKEN = jax.ShapeDtypeStruct((8, V7X_LANES), F32)


def _in_hbm(a):
    return pltpu.with_memory_space_constraint(a, pltpu.HBM)


def _gather_win(o_ref, R, C, col_sharded):
    Rh = R // 2

    def win(j, h=None):
        if col_sharded:
            rows = pl.ds(0, R) if h is None else pl.ds(h * Rh, Rh)
            return o_ref.at[rows, pl.ds(j * C, C)]
        return o_ref.at[pl.ds(j * R, R) if h is None else pl.ds(j * R + h * Rh, Rh), :]

    return win


def _cast_into_window(w, col_sharded, after, also_alone, name):
    R, C = w.shape
    tr = _tile(R, 256, 16)
    nr = R // tr
    full = (R, 4 * C) if col_sharded else (4 * R, C)
    j = (2 * lax.axis_index("x") + lax.axis_index("y")).astype(jnp.int32).reshape(1)

    def kern(j_ref, w_ref, after_ref, *o_refs):
        for o_ref in o_refs:
            o_ref[...] = w_ref[...].astype(ACT_DTYPE)

    out_map = (lambda i, jr: (i, jr[0])) if col_sharded else (lambda i, jr: (jr[0] * nr + i, 0))
    row = pl.BlockSpec((tr, C), lambda i, jr: (i, 0))
    grid_spec = pltpu.PrefetchScalarGridSpec(
        num_scalar_prefetch=1, grid=(nr,),
        in_specs=[row, HBM_SPEC], out_specs=[pl.BlockSpec((tr, C), out_map)] + ([row] if also_alone else []))
    out_shape = [jax.ShapeDtypeStruct(full, ACT_DTYPE)] + ([jax.ShapeDtypeStruct((R, C), ACT_DTYPE)] if also_alone else [])
    return _pcall(kern, name=name, grid_spec=grid_spec, out_shape=out_shape,
                  compiler_params=_params(("parallel",), 8 * tr * C * 4))(j, w, after)


def _gather_start(land, shard_shape, col_sharded, token, name):
    R, C = shard_shape

    def body(land_ref, tok_ref, send_sems, recv_sems, land_thru, tok_out):
        x, y, c = _pos()
        w = _gather_win(land_ref, R, C, col_sharded)(2 * x + y, c)
        for k, (cx, cy) in enumerate(_other_chips(x, y)):
            _remote(w, w, send_sems, recv_sems, k, (cx, cy, c)).start()
        tok_out[...] = tok_ref[...]

    return _pcall(
        body, name=name,
        out_shape=(pltpu.SemaphoreType.DMA((3,)), pltpu.SemaphoreType.DMA((3,)),
                   pltpu.HBM(land.shape, land.dtype), TOKEN),
        in_specs=(HBM_BLK, VMEM_BLK), out_specs=(SEM_BLK, SEM_BLK, HBM_BLK, VMEM_BLK),
        input_output_aliases={0: 2},
        compiler_params=pltpu.CompilerParams(has_side_effects=DATAFLOW),
    )(_in_hbm(land), token)


def _gather_wait(started, shard_shape, after, col_sharded, name):
    send_sems, recv_sems, land_thru, _ = started
    R, C = shard_shape

    def body(land_ref, send_sems, recv_sems, after_ref, got_ref):
        x, y, c = _pos()
        win = _gather_win(land_ref, R, C, col_sharded)
        for k, (cx, cy) in enumerate(_other_chips(x, y)):
            cp = _remote(win(2 * x + y, c), win(2 * cx + cy, c), send_sems, recv_sems, k, (cx, cy, c))
            cp.wait_send()
            cp.wait_recv()

    return _pcall(
        body, name=name, out_shape=pltpu.HBM(land_thru.shape, land_thru.dtype),
        in_specs=(HBM_BLK, SEM_BLK, SEM_BLK, HBM_SPEC), out_specs=HBM_BLK, input_output_aliases={0: 0},
        compiler_params=pltpu.CompilerParams(has_side_effects=DATAFLOW),
    )(land_thru, send_sems, recv_sems, after)


def _gather_pass_on(landed, shard_shape, col_sharded, name):
    R, C = shard_shape

    def body(in_ref, o_ref, send_sems, recv_sems):
        x, y, c = _pos()
        src, dst = _gather_win(in_ref, R, C, col_sharded), _gather_win(o_ref, R, C, col_sharded)
        chips = [(k, 2 * cx + cy) for k, (cx, cy) in enumerate(_other_chips(x, y))]
        passed = [_remote(src(j, c), dst(j, c), send_sems, recv_sems, k, (x, y, 1 - c)) for k, j in chips]
        for cp in passed:
            cp.start()
        for k, j in chips:
            w = dst(j, 1 - c)
            _remote(w, w, send_sems, recv_sems, k, (x, y, c)).wait_recv()
        for cp in passed:
            cp.wait_send()

    return _pcall(body, name=name, in_specs=[HBM_SPEC], out_specs=HBM_SPEC,
                  out_shape=jax.ShapeDtypeStruct(landed.shape, landed.dtype), input_output_aliases={0: 0},
                  scratch_shapes=[pltpu.SemaphoreType.DMA((3,)), pltpu.SemaphoreType.DMA((3,))])(landed)


def _pair_start(g, token, name):
    def body(g_ref, land_ref, tok_ref, send_sems, recv_sems, g_thru, land_thru, tok_out):
        x, y, c = _pos()
        _remote(g_ref, land_ref, send_sems, recv_sems, 0, (x, y, 1 - c)).start()
        tok_out[...] = tok_ref[...]

    return _pcall(
        body, name=name,
        out_shape=(pltpu.SemaphoreType.DMA((1,)), pltpu.SemaphoreType.DMA((1,)),
                   pltpu.HBM(g.shape, g.dtype), pltpu.HBM(g.shape, g.dtype), TOKEN),
        in_specs=(HBM_BLK, HBM_BLK, VMEM_BLK), out_specs=(SEM_BLK, SEM_BLK, HBM_BLK, HBM_BLK, VMEM_BLK),
        input_output_aliases={0: 2, 1: 3},
        compiler_params=pltpu.CompilerParams(has_side_effects=DATAFLOW),
    )(_in_hbm(g), _in_hbm(lax.empty(g.shape, g.dtype)), token)


def _pair_wait(started, after, name):
    send_sems, recv_sems, g_thru, land_thru, _ = started

    def body(g_ref, land_ref, send_sems, recv_sems, after_ref, g_dead, got_ref):
        x, y, c = _pos()
        cp = _remote(g_ref, land_ref, send_sems, recv_sems, 0, (x, y, 1 - c))
        cp.wait_send()
        cp.wait_recv()

    return _pcall(
        body, name=name,
        out_shape=(pltpu.HBM(g_thru.shape, g_thru.dtype), pltpu.HBM(land_thru.shape, land_thru.dtype)),
        in_specs=(HBM_BLK, HBM_BLK, SEM_BLK, SEM_BLK, HBM_SPEC), out_specs=(HBM_BLK, HBM_BLK),
        input_output_aliases={0: 0, 1: 1},
        compiler_params=pltpu.CompilerParams(has_side_effects=DATAFLOW),
    )(g_thru, land_thru, send_sems, recv_sems, after)[1]


def _own_piece_into_slots(cs, col_sharded, name):
    J, Rp, W = cs.shape
    Cp = W // 4 if col_sharded else W
    tr = _tile(Rp, 256, 16)
    tw = _tile(Cp, 8192, V7X_LANES)
    nw = Cp // tw
    x, y, c = _pos()
    chip = (2 * x + y).astype(jnp.int32).reshape(1)
    core = c.astype(jnp.int32).reshape(1)

    def kern(j_ref, c_ref, s_ref, o_ref):
        o_ref[...] = s_ref[...]

    in_map = ((lambda i, w, j, cc: (0, i, j[0] * nw + w)) if col_sharded
              else (lambda i, w, j, cc: (j[0], i, w)))
    grid_spec = pltpu.PrefetchScalarGridSpec(
        num_scalar_prefetch=2, grid=(Rp // tr, nw),
        in_specs=[pl.BlockSpec((None, tr, tw), in_map)],
        out_specs=pl.BlockSpec((None, None, tr, tw), lambda i, w, j, cc: (j[0], cc[0], i, w)))
    return _pcall(kern, name=name, grid_spec=grid_spec,
                  out_shape=jax.ShapeDtypeStruct((4, 2, Rp, Cp), cs.dtype),
                  compiler_params=_params(("parallel", "parallel"), 8 * tr * tw * 4))(chip, core, cs)


def _chip_sum_piece(cs_ref, C, col_sharded):
    return lambda j: cs_ref.at[0, :, pl.ds(j * C, C)] if col_sharded else cs_ref.at[j]


def _scatter_start(cs, slots, col_sharded, token, name):
    C = slots.shape[3]

    def body(cs_ref, land_ref, tok_ref, send_sems, recv_sems, cs_thru, land_thru, tok_out):
        x, y, c = _pos()
        piece = _chip_sum_piece(cs_ref, C, col_sharded)
        for k, (cx, cy) in enumerate(_other_chips(x, y)):
            _remote(piece(2 * cx + cy), land_ref.at[2 * x + y, c], send_sems, recv_sems, k, (cx, cy, c)).start()
        tok_out[...] = tok_ref[...]

    return _pcall(
        body, name=name,
        out_shape=(pltpu.SemaphoreType.DMA((3,)), pltpu.SemaphoreType.DMA((3,)),
                   pltpu.HBM(cs.shape, cs.dtype), pltpu.HBM(slots.shape, cs.dtype), TOKEN),
        in_specs=(HBM_BLK, HBM_BLK, VMEM_BLK), out_specs=(SEM_BLK, SEM_BLK, HBM_BLK, HBM_BLK, VMEM_BLK),
        input_output_aliases={0: 2, 1: 3},
        compiler_params=pltpu.CompilerParams(has_side_effects=DATAFLOW),
    )(_in_hbm(cs), _in_hbm(slots), token)


def _scatter_wait(started, after, col_sharded, name):
    send_sems, recv_sems, cs_thru, land_thru, _ = started
    C = land_thru.shape[3]

    def body(cs_ref, land_ref, send_sems, recv_sems, after_ref, cs_dead, got_ref):
        x, y, c = _pos()
        piece = _chip_sum_piece(cs_ref, C, col_sharded)
        for k, (cx, cy) in enumerate(_other_chips(x, y)):
            cp = _remote(piece(2 * cx + cy), land_ref.at[2 * cx + cy, c], send_sems, recv_sems, k, (cx, cy, c))
            cp.wait_send()
            cp.wait_recv()

    return _pcall(
        body, name=name,
        out_shape=(pltpu.HBM(cs_thru.shape, cs_thru.dtype), pltpu.HBM(land_thru.shape, land_thru.dtype)),
        in_specs=(HBM_BLK, HBM_BLK, SEM_BLK, SEM_BLK, HBM_SPEC), out_specs=(HBM_BLK, HBM_BLK),
        input_output_aliases={0: 0, 1: 1},
        compiler_params=pltpu.CompilerParams(has_side_effects=DATAFLOW),
    )(cs_thru, land_thru, send_sems, recv_sems, after)


def _scatter_pass_on(landed, name):
    def body(in_ref, o_ref, send_sems, recv_sems):
        x, y, c = _pos()
        sends = [_remote(in_ref.at[i, c], o_ref.at[i, c], send_sems, recv_sems, i, (x, y, 1 - c)) for i in range(4)]
        for cp in sends:
            cp.start()
        for i in range(4):
            w = o_ref.at[i, 1 - c]
            _remote(w, w, send_sems, recv_sems, i, (x, y, c)).wait_recv()
        for cp in sends:
            cp.wait_send()

    return _pcall(body, name=name, in_specs=[HBM_SPEC], out_specs=HBM_SPEC,
                  out_shape=jax.ShapeDtypeStruct(landed.shape, landed.dtype), input_output_aliases={0: 0},
                  scratch_shapes=[pltpu.SemaphoreType.DMA((4,)), pltpu.SemaphoreType.DMA((4,))])(landed)


def _gather_windows(shard_shape, col_sharded):
    R, C = shard_shape

    def windows(ref, x, y, cc):
        win = _gather_win(ref, R, C, col_sharded)
        return [win(2 * cx + cy, cc) for cx, cy in _other_chips(x, y)]

    return windows


def _slot_windows(ref, x, y, cc):
    return [ref.at[i, cc] for i in range(4)]


def _pass_start(landed, windows, n, token, name):
    def body(land_ref, tok_ref, send_sems, recv_sems, land_thru, tok_out):
        x, y, c = _pos()
        for k, w in enumerate(windows(land_ref, x, y, c)):
            _remote(w, w, send_sems, recv_sems, k, (x, y, 1 - c)).start()
        tok_out[...] = tok_ref[...]

    return _pcall(
        body, name=name,
        out_shape=(pltpu.SemaphoreType.DMA((n,)), pltpu.SemaphoreType.DMA((n,)),
                   pltpu.HBM(landed.shape, landed.dtype), TOKEN),
        in_specs=(HBM_BLK, VMEM_BLK), out_specs=(SEM_BLK, SEM_BLK, HBM_BLK, VMEM_BLK),
        input_output_aliases={0: 2},
        compiler_params=pltpu.CompilerParams(has_side_effects=DATAFLOW),
    )(_in_hbm(landed), token)


def _pass_wait(started, windows, after, name):
    send_sems, recv_sems, land_thru, _ = started

    def body(land_ref, send_sems, recv_sems, after_ref, got_ref):
        x, y, c = _pos()
        mine, theirs = windows(land_ref, x, y, c), windows(land_ref, x, y, 1 - c)
        for k, (src, dst) in enumerate(zip(mine, theirs)):
            cp = _remote(src, dst, send_sems, recv_sems, k, (x, y, 1 - c))
            cp.wait_send()
            cp.wait_recv()

    return _pcall(
        body, name=name, out_shape=pltpu.HBM(land_thru.shape, land_thru.dtype),
        in_specs=(HBM_BLK, SEM_BLK, SEM_BLK, HBM_SPEC), out_specs=HBM_BLK, input_output_aliases={0: 0},
        compiler_params=pltpu.CompilerParams(has_side_effects=DATAFLOW),
    )(land_thru, send_sems, recv_sems, after)


def _gather_small(buf, name):
    rows = buf.shape[0]

    def body(b_ref, o_ref, send_sems, recv_sems):
        x, y, c = _pos()
        jme = 2 * x + y
        chips = _other_chips(x, y)
        o_ref[jme] = b_ref[...]
        sends = [_remote(b_ref, o_ref.at[jme], send_sems, recv_sems, k, (cx, cy, c))
                 for k, (cx, cy) in enumerate(chips)]
        for cp in sends:
            cp.start()
        for k, (cx, cy) in enumerate(chips):
            w = o_ref.at[2 * cx + cy]
            _remote(w, w, send_sems, recv_sems, k, (x, y, c)).wait_recv()
        for cp in sends:
            cp.wait_send()

    vm = pl.BlockSpec(memory_space=pltpu.VMEM)
    return _pcall(body, name=name, in_specs=[vm], out_specs=vm,
                  out_shape=jax.ShapeDtypeStruct((4, rows, V7X_LANES), buf.dtype),
                  scratch_shapes=[pltpu.SemaphoreType.DMA((3,)), pltpu.SemaphoreType.DMA((3,))],
                  compiler_params=_params(None, 16 * rows * V7X_LANES * 4))(buf)


def _allreduce_small(buf, name):
    rows = buf.shape[0]
    rh = rows // 2

    def body(b_ref, o_ref, pair, mine, slots, send_sems, recv_sems):
        x, y, c = _pos()
        me, sib, jme = (x, y, c), (x, y, 1 - c), 2 * x + y
        half = pl.ds(pl.multiple_of(c * rh, 8), rh)
        other = pl.ds(pl.multiple_of((1 - c) * rh, 8), rh)
        to_sib = _remote(b_ref, pair.at[c], send_sems, recv_sems, 0, sib)
        to_sib.start()
        pair[c] = b_ref[...]
        _remote(pair.at[1 - c], pair.at[1 - c], send_sems, recv_sems, 0, me).wait_recv()
        mine[...] = pair[0, half, :] + pair[1, half, :]
        chips = _other_chips(x, y)
        sends = [_remote(mine, slots.at[jme], send_sems, recv_sems, 1 + k, (cx, cy, c))
                 for k, (cx, cy) in enumerate(chips)]
        for cp in sends:
            cp.start()
        slots[jme] = mine[...]
        for k, (cx, cy) in enumerate(chips):
            w = slots.at[2 * cx + cy]
            _remote(w, w, send_sems, recv_sems, 1 + k, me).wait_recv()
        o_ref[half, :] = (slots[0] + slots[1]) + (slots[2] + slots[3])
        back = _remote(o_ref.at[half, :], o_ref.at[half, :], send_sems, recv_sems, 4, sib)
        back.start()
        _remote(o_ref.at[other, :], o_ref.at[other, :], send_sems, recv_sems, 4, me).wait_recv()
        for cp in [to_sib, back] + sends:
            cp.wait_send()

    vm = pl.BlockSpec(memory_space=pltpu.VMEM)
    return _pcall(body, name=name, in_specs=[vm], out_specs=vm,
                  out_shape=jax.ShapeDtypeStruct(buf.shape, buf.dtype),
                  scratch_shapes=[pltpu.VMEM((2, rows, V7X_LANES), buf.dtype),
                                  pltpu.VMEM((rh, V7X_LANES), buf.dtype),
                                  pltpu.VMEM((4, rh, V7X_LANES), buf.dtype),
                                  pltpu.SemaphoreType.DMA((5,)), pltpu.SemaphoreType.DMA((5,))],
                  compiler_params=_params(None, 10 * rows * V7X_LANES * 4))(buf)


def _adamw_math(w, g, m, v):
    m = ADAM_B1 * m + (1.0 - ADAM_B1) * g
    v = ADAM_B2 * v + (1.0 - ADAM_B2) * (g * g)
    m_hat = m / (1.0 - ADAM_B1 ** ADAM_STEP)
    v_hat = v / (1.0 - ADAM_B2 ** ADAM_STEP)
    delta = -ADAM_LR * (m_hat / (jnp.sqrt(v_hat) + ADAM_EPS) + ADAM_WD * w)
    return delta, m, v


def _adamw_big(w, m, v, slots, name):
    R, C = w.shape
    by_rows = slots.shape[3] == C
    tr = _tile(R, 64, 16)

    def kern(w_ref, m_ref, v_ref, s_ref, g_ref, d_ref, mo_ref, vo_ref):
        def chip_sum(h):
            g = s_ref[0, h].astype(F32)
            for i in range(1, 4):
                g = g + s_ref[i, h].astype(F32)
            return g

        g = chip_sum(0) if by_rows else jnp.concatenate([chip_sum(0), chip_sum(1)], axis=1)
        d, mn, vn = _adamw_math(w_ref[...], g, m_ref[...], v_ref[...])
        g_ref[...], d_ref[...], mo_ref[...], vo_ref[...] = g, d, mn, vn

    row = pl.BlockSpec((tr, C), lambda i: (i, 0))
    shp = jax.ShapeDtypeStruct((R, C), F32)
    if by_rows:
        per_half = R // 2 // tr
        s_spec = pl.BlockSpec((4, 1, tr, C), lambda i: (0, i // per_half, i % per_half, 0))
    else:
        s_spec = pl.BlockSpec((4, 2, tr, C // 2), lambda i: (0, 0, i, 0))
    return _pcall(kern, name=name, grid=(R // tr,),
                  in_specs=[row, row, row, s_spec], out_specs=[row] * 4, out_shape=[shp] * 4,
                  compiler_params=_params(("parallel",), 30 * tr * C * 4))(w, m, v, slots)


def _adamw_small(w, g, m, v, name):
    def kern(w_ref, g_ref, m_ref, v_ref, d_ref, mo_ref, vo_ref):
        d_ref[...], mo_ref[...], vo_ref[...] = _adamw_math(w_ref[...], g_ref[...], m_ref[...], v_ref[...])

    vm = pl.BlockSpec(memory_space=pltpu.VMEM)
    shp = jax.ShapeDtypeStruct(w.shape, F32)
    return _pcall(kern, name=name, in_specs=[vm] * 4, out_specs=[vm] * 3, out_shape=[shp] * 3,
                  compiler_params=_params(None, 10 * w.size * 4))(w, g, m, v)


def _pack(arrs):
    flat = jnp.concatenate([a.reshape(-1).astype(F32) for a in arrs])
    n = flat.shape[0]
    rows = -(-n // (16 * V7X_LANES)) * 16
    return jnp.pad(flat, (0, rows * V7X_LANES - n)).reshape(rows, V7X_LANES)


def _unpack(buf, shapes):
    flat = buf.reshape(-1)
    out, off = [], 0
    for s in shapes:
        n = int(np.prod(s))
        out.append(flat[off:off + n].reshape(s))
        off += n
    return out


def _cut_gradient(a, d, col_sharded, tm, tn, token, tag, other_work):
    c = lax.axis_index("c").astype(jnp.int32)
    for_sibling = _matmul_tn_half(a, d, (1 - c).reshape(1), None, tm, tn, col_sharded, "mm_g_%s_sibling" % tag)
    sent = _pair_start(for_sibling, token, "pair_start_" + tag)
    other, last = other_work(sent[4])
    arrived = _pair_wait(sent, last, "pair_wait_" + tag)
    cs = _matmul_tn_half(a, d, c.reshape(1), arrived, tm, tn, col_sharded, "mm_g_%s_own" % tag)
    cs = cs.reshape((1,) + cs.shape if col_sharded else (4, cs.shape[0] // 4, cs.shape[1]))
    slots = _own_piece_into_slots(cs, col_sharded, "own_piece_" + tag)
    return _scatter_start(cs, slots, col_sharded, sent[4], "scatter_start_" + tag), other


def _reduce_finish(started, after, col_sharded, w, m, v, tag):
    _, landed = _scatter_wait(started, after, col_sharded, "scatter_wait_" + tag)
    slots = _scatter_pass_on(landed, "scatter_pass_on_" + tag)
    return _adamw_big(w, m, v, slots, "adamw_" + tag)


def kernel(x, ln1_w, w_in, lb_gamma, hg_norm_w, lru_conv_w, lru_conv_b, lru_wa, lru_ba, lru_wx, lru_bx, lru_lambda, lru_norm_w, w_out, ln2_w, ffn_w_up, ffn_conv_w, ffn_conv_b, ffn_w_down, final_norm_w, loss_target, m_ln1_w, m_w_in, m_lb_gamma, m_hg_norm_w, m_lru_conv_w, m_lru_conv_b, m_lru_wa, m_lru_ba, m_lru_wx, m_lru_bx, m_lru_lambda, m_lru_norm_w, m_w_out, m_ln2_w, m_ffn_w_up, m_ffn_conv_w, m_ffn_conv_b, m_ffn_w_down, m_final_norm_w, v_ln1_w, v_w_in, v_lb_gamma, v_hg_norm_w, v_lru_conv_w, v_lru_conv_b, v_lru_wa, v_lru_ba, v_lru_wx, v_lru_bx, v_lru_lambda, v_lru_norm_w, v_w_out, v_ln2_w, v_ffn_w_up, v_ffn_conv_w, v_ffn_conv_b, v_ffn_w_down, v_final_norm_w):
    B, S, D = x.shape
    T = B * S
    HW = lb_gamma.shape[1]
    LW = lru_conv_b.shape[1]
    assert S % CHUNK == 0 and HW % HEAD_DIM == 0 and lru_wa.shape[2] == HEAD_DIM
    x2 = x.reshape(T, D)
    tgt = loss_target.reshape(T, D)
    jchip = 2 * lax.axis_index("x") + lax.axis_index("y")

    conv_shapes = [lru_conv_w[0].shape, ffn_conv_w[0].shape]
    convs = _gather_small(_pack([lru_conv_w[0], ffn_conv_w[0]]), "gather_conv_w")
    per_chip = [_unpack(convs[j], conv_shapes) for j in range(4)]
    lcw = jnp.concatenate([pc[0] for pc in per_chip], axis=1)
    fcw = jnp.concatenate([pc[1] for pc in per_chip], axis=1)
    masters = dict(w_in=w_in[0], w_out=w_out[0], w_up=ffn_w_up[0], w_down=ffn_w_down[0])
    col_of = dict(w_in=True, w_out=False, w_up=True, w_down=False)
    started, token, after = {}, jnp.zeros(TOKEN.shape, F32), convs
    for n in ("w_in", "w_out", "w_up", "w_down"):
        land, *alone = _cast_into_window(masters[n], col_of[n], after, n == "w_in", "cast_" + n)
        if n == "w_in":
            own_w_in = alone[0]
        started[n] = _gather_start(land, masters[n].shape, col_of[n], token, "gather_start_" + n)
        token = after = started[n][3]

    def landed(n, after):
        return _gather_wait(started[n], masters[n].shape, after, col_of[n], "gather_wait_" + n)

    def pass_on_start(n, after, tok):
        wins = _gather_windows(masters[n].shape, col_of[n])
        return _pass_start(landed(n, after), wins, 3, tok, "gather_pass_start_" + n)

    def pass_on_wait(n, sent, after):
        return _pass_wait(sent, _gather_windows(masters[n].shape, col_of[n]), after, "gather_pass_wait_" + n)

    hn1 = _rms_fwd(x2, ln1_w, "rms1")
    mx, my = lax.axis_index("x"), lax.axis_index("y")
    n_in = 4 * masters["w_in"].shape[1]
    as_j = lambda v: v.astype(jnp.int32).reshape(1)
    proj = _matmul_col_slice(hn1, own_w_in, as_j(2 * mx + my), n_in, None, 1024, 512, "mm_proj_own")
    W_in = _gather_pass_on(landed("w_in", proj), masters["w_in"].shape, True, "gather_pass_on_w_in")
    for tag, j in (("x", 2 * (1 - mx) + my), ("y", 2 * mx + 1 - my), ("xy", 2 * (1 - mx) + 1 - my)):
        proj = _matmul_col_slice(hn1, W_in, as_j(j), n_in, proj, 1024, 512, "mm_proj_" + tag)
    sent_out = pass_on_start("w_out", proj, token)
    o_raw, o_hg, states, scores = _hgrn_fwd(proj, lb_gamma, hg_norm_w, B, HW, "hgrn_fwd")
    h_lru, z = _lru_fwd(proj, lcw, lru_conv_b, lru_wa[0], lru_ba, lru_wx[0], lru_bx, lru_lambda, B, HW, LW, "lru_fwd")
    o_lru = _rms_fwd(z, lru_norm_w, "rms_lru")
    mix = jnp.concatenate([o_hg, o_lru], axis=1)
    sent_up = pass_on_start("w_up", mix, sent_out[3])
    W_out = pass_on_wait("w_out", sent_out, sent_up[3])
    h1 = _matmul(mix, W_out, "NN", F32, 1024, 512, 4096, add=x2, name="mm_out")
    hn2 = _rms_fwd(h1, ln2_w, "rms2")
    W_up = pass_on_wait("w_up", sent_up, hn2)
    up = _matmul(hn2, W_up, "NN", F32, 1024, 512, 4096, name="mm_up")
    act, act_dg, act_dv = _ffn_act(up, fcw, ffn_conv_b, B, "ffn_act")
    W_down = _gather_pass_on(landed("w_down", act), masters["w_down"].shape, False, "gather_pass_on_w_down")
    h2 = _matmul(act, W_down, "NN", F32, 1024, 512, 5504, add=h1, name="mm_down")
    token = sent_up[3]

    dh2, dh2a, d_final_w, loss_part = _loss_bwd(h2, tgt, final_norm_w.reshape(1, D), "loss_bwd")
    def through_w_down(tok):
        d = _matmul(dh2a, W_down, "NT", ACT_DTYPE, 512, 5504, 512, after=tok, name="mm_d_act")
        return d, d

    red_down, d_act = _cut_gradient(act, dh2a, False, 256, 1024, token, "w_down", through_w_down)
    d_up, d_fcw, d_fcb = _ffn_act_bwd(up, fcw, act_dg, act_dv, d_act, B, "ffn_act_bwd")
    def through_w_up(tok):
        d = _matmul(d_up, W_up, "NT", F32, 2048, 1024, 512, after=tok, name="mm_d_hn2")
        return d, d

    red_up, d_hn2 = _cut_gradient(hn2, d_up, True, 1024, 512, red_down[4], "w_up", through_w_up)
    dh1, dh1a, d_ln2 = _rms_bwd(h1, ln2_w, d_hn2, 0, dh2, True, "rms2_bwd", after=red_up[4])

    def through_w_out(tok):
        d = _matmul(dh1a, W_out, "NT", F32, 1024, 512, 4096, after=tok, name="mm_d_mix")
        return d, d

    red_out, d_mix = _cut_gradient(mix, dh1a, False, 1024, 512, red_up[4], "w_out", through_w_out)
    dz, d_lru_norm = _rms_bwd(z, lru_norm_w, d_mix, HW // LW, None, False, "rms_lru_bwd")
    (d_xr, d_yr, d_wa, d_wx, d_ba, d_bx, d_lam, d_lcw, d_lcb) = _lru_bwd(
        proj, lcw, lru_conv_b, lru_wa[0], lru_ba, lru_wx[0], lru_bx, lru_lambda, h_lru, dz, B, HW, LW, "lru_bwd")
    d_q, d_f, d_i, d_g, d_lbg, d_hgw = _hgrn_bwd(
        proj, lb_gamma, hg_norm_w, o_raw, states, scores, d_mix, B, HW, "hgrn_bwd")
    d_proj = jnp.concatenate([d_q, d_f, d_i, d_g, d_xr, d_yr], axis=1)
    small_names = ["ln1_w", "lb_gamma", "hg_norm_w", "lru_conv_w", "lru_conv_b", "lru_wa", "lru_ba", "lru_wx",
                   "lru_bx", "lru_lambda", "lru_norm_w", "ln2_w", "ffn_conv_w", "ffn_conv_b", "final_norm_w"]
    small_rest = [d_lbg, d_hgw, d_lcw, d_lcb, d_wa, d_ba, d_wx, d_bx, d_lam, d_lru_norm, d_ln2, d_fcw, d_fcb, d_final_w]

    def through_w_in(tok):
        d_hn1 = _matmul(d_proj, W_in, "NT", F32, 2048, 1024, 1024, after=tok, name="mm_d_hn1")
        dx, d_ln1 = _rms_bwd(x2, ln1_w, d_hn1, 0, dh1, False, "rms1_bwd")
        red = _allreduce_small(_pack([loss_part[0:1, 0:1], d_ln1] + small_rest), "allreduce_small")
        return (dx, d_ln1, red), red

    red_in, (dx, d_ln1, red) = _cut_gradient(hn1, d_proj, True, 1024, 512, red_out[4], "w_in", through_w_in)
    small_grads = [d_ln1] + small_rest

    def slots_on_their_way(red, col_sharded, after, tok, tag):
        _, got = _scatter_wait(red, after, col_sharded, "scatter_wait_" + tag)
        return _pass_start(got, _slot_windows, 4, tok, "scatter_pass_start_" + tag)

    def update(sent, after, w, m, v, tag):
        slots = _pass_wait(sent, _slot_windows, after, "scatter_pass_wait_" + tag)
        return _adamw_big(w, m, v, slots, "adamw_" + tag)

    sent_down = slots_on_their_way(red_down, False, red_in[4], red_in[4], "w_down")
    sent_up = slots_on_their_way(red_up, True, sent_down[3], sent_down[3], "w_up")
    sent_out = slots_on_their_way(red_out, False, sent_up[3], sent_up[3], "w_out")
    big = {}
    big["ffn_w_down"] = update(sent_down, sent_out[3], ffn_w_down[0], m_ffn_w_down[0], v_ffn_w_down[0], "w_down")
    big["ffn_w_up"] = update(sent_up, big["ffn_w_down"][1], ffn_w_up[0], m_ffn_w_up[0], v_ffn_w_up[0], "w_up")
    big["w_out"] = update(sent_out, big["ffn_w_up"][1], w_out[0], m_w_out[0], v_w_out[0], "w_out")
    big["w_in"] = _reduce_finish(red_in, big["w_out"][1], True, w_in[0], m_w_in[0], v_w_in[0], "w_in")

    red = _unpack(red, [(1, 1)] + [g.shape for g in small_grads])
    loss = red[0].reshape(())
    gs = dict(zip(small_names, red[1:]))
    nlc, nfc = lru_conv_w.shape[2], ffn_conv_w.shape[2]
    gs["lru_conv_w"] = lax.dynamic_slice_in_dim(gs["lru_conv_w"], jchip * nlc, nlc, axis=1)
    gs["ffn_conv_w"] = lax.dynamic_slice_in_dim(gs["ffn_conv_w"], jchip * nfc, nfc, axis=1)
    args = dict(ln1_w=(ln1_w, m_ln1_w, v_ln1_w), lb_gamma=(lb_gamma, m_lb_gamma, v_lb_gamma),
                hg_norm_w=(hg_norm_w, m_hg_norm_w, v_hg_norm_w), lru_conv_w=(lru_conv_w, m_lru_conv_w, v_lru_conv_w),
                lru_conv_b=(lru_conv_b, m_lru_conv_b, v_lru_conv_b), lru_wa=(lru_wa, m_lru_wa, v_lru_wa),
                lru_ba=(lru_ba, m_lru_ba, v_lru_ba), lru_wx=(lru_wx, m_lru_wx, v_lru_wx),
                lru_bx=(lru_bx, m_lru_bx, v_lru_bx), lru_lambda=(lru_lambda, m_lru_lambda, v_lru_lambda),
                lru_norm_w=(lru_norm_w, m_lru_norm_w, v_lru_norm_w), ln2_w=(ln2_w, m_ln2_w, v_ln2_w),
                ffn_conv_w=(ffn_conv_w, m_ffn_conv_w, v_ffn_conv_w), ffn_conv_b=(ffn_conv_b, m_ffn_conv_b, v_ffn_conv_b),
                final_norm_w=(final_norm_w, m_final_norm_w, v_final_norm_w))
    shapes = [args[n][0].shape for n in small_names]
    upd = _adamw_small(_pack([args[n][0] for n in small_names]), _pack([gs[n] for n in small_names]),
                       _pack([args[n][1] for n in small_names]), _pack([args[n][2] for n in small_names]), "adamw_small")
    s_delta, s_m, s_v = (dict(zip(small_names, _unpack(u, shapes))) for u in upd)

    order = ["ln1_w", "w_in", "lb_gamma", "hg_norm_w", "lru_conv_w", "lru_conv_b", "lru_wa", "lru_ba", "lru_wx",
             "lru_bx", "lru_lambda", "lru_norm_w", "w_out", "ln2_w", "ffn_w_up", "ffn_conv_w", "ffn_conv_b",
             "ffn_w_down", "final_norm_w"]
    full_shape = dict(w_in=w_in.shape, w_out=w_out.shape, ffn_w_up=ffn_w_up.shape, ffn_w_down=ffn_w_down.shape)
    grads, deltas, new_m, new_v = [], [], [], []
    for n in order:
        if n in big:
            g, d, mn, vn = (t.reshape(full_shape[n]) for t in big[n])
        else:
            g, d, mn, vn = gs[n].reshape(args[n][0].shape), s_delta[n], s_m[n], s_v[n]
        grads.append(g), deltas.append(d), new_m.append(mn), new_v.append(vn)
    return (loss, dx.reshape(B, S, D), *grads, *deltas, *new_m, *new_v)
```

```python
import functools
import math

import numpy as np
import jax
import jax.numpy as jnp
from jax import lax
from jax.experimental import pallas as pl
from jax.experimental.pallas import tpu as pltpu

F32 = jnp.float32
MXU_DTYPE = jnp.bfloat16
ACT_DTYPE = jnp.bfloat16

EPS = 1e-6
HEAD_DIM = 128
CHUNK = 64
LEVEL_HALVES = (32, 16, 8, 4, 2, 1)
LRU_CONV = 4
FFN_CONV = 3
LRU_C = 8.0
ADAM_LR, ADAM_B1, ADAM_B2, ADAM_EPS, ADAM_WD, ADAM_STEP = 0.001, 0.9, 0.999, 1e-08, 0.01, 10

V7X_LANES = 128
V7X_VMEM_BUDGET = 56 << 20

NN = (((1,), (0,)), ((), ()))
NT = (((1,), (1,)), ((), ()))
TN = (((0,), (0,)), ((), ()))
MESH = pl.DeviceIdType.MESH
HBM_SPEC = pl.BlockSpec(memory_space=pl.ANY)


def _pcall(kern, **kw):
    return pl.pallas_call(kern, **kw)


def _params(sem=None, vmem=None):
    kw = {}
    if sem is not None:
        kw["dimension_semantics"] = sem
    if vmem is not None:
        kw["vmem_limit_bytes"] = int(min(max(vmem, 16 << 20), V7X_VMEM_BUDGET))
    return pltpu.CompilerParams(**kw)


def _dot(a, b, dims=NN):
    return lax.dot_general(a.astype(MXU_DTYPE), b.astype(MXU_DTYPE), dims, preferred_element_type=F32)


def _tile(dim, pref, align):
    t = min(pref, dim) // align * align
    while t >= align:
        if dim % t == 0:
            return t
        t -= align
    return dim


def _sigmoid(x):
    return 1.0 / (1.0 + jnp.exp(-x))


def _silu_and_grad(x):
    s = _sigmoid(x)
    return x * s, s * (1.0 + x * (1.0 - s))


def _gelu_and_grad(x):
    k0, k1 = math.sqrt(2.0 / math.pi), 0.044715
    t = jnp.tanh(k0 * (x + k1 * x * x * x))
    g = 0.5 * x * (1.0 + t)
    dg = 0.5 * (1.0 + t) + 0.5 * x * (1.0 - t * t) * k0 * (1.0 + 3.0 * k1 * x * x)
    return g, dg


def _one_minus_exp(x):
    p = x * (1.0 + x * (0.5 + x * (1.0 / 6.0 + x * (1.0 / 24.0 + x * (1.0 / 120.0)))))
    return jnp.where(x > -0.05, -p, 1.0 - jnp.exp(x))


def _rows(n):
    return lax.broadcasted_iota(jnp.int32, (n, 1), 0)


def _matmul(a, b, mode, out_dtype, tm, tn, tk, add=None, after=None, n_outer=False, name="mm"):
    if mode == "TN":
        K, M = a.shape
    else:
        M, K = a.shape
    N = b.shape[0] if mode == "NT" else b.shape[1]
    tm, tn = _tile(M, tm, V7X_LANES), _tile(N, tn, V7X_LANES)
    tk = _tile(K, tk, V7X_LANES)
    nk = K // tk
    dims = {"NN": NN, "NT": NT, "TN": TN}[mode]
    order = (lambda f: (lambda j, i, k: f(i, j, k))) if n_outer else (lambda f: f)
    a_spec = (pl.BlockSpec((tk, tm), order(lambda i, j, k: (k, i))) if mode == "TN"
              else pl.BlockSpec((tm, tk), order(lambda i, j, k: (i, k))))
    b_spec = (pl.BlockSpec((tn, tk), order(lambda i, j, k: (j, k))) if mode == "NT"
              else pl.BlockSpec((tk, tn), order(lambda i, j, k: (k, j))))
    o_spec = pl.BlockSpec((tm, tn), order(lambda i, j, k: (i, j)))
    has_add = add is not None

    def kern(*refs):
        a_ref, b_ref = refs[:2]
        add_ref = refs[2] if has_add else None

        def finish(r, o_ref):
            if has_add:
                r = r + add_ref[...]
            o_ref[...] = r.astype(out_dtype)

        if nk == 1:
            finish(_dot(a_ref[...], b_ref[...], dims), refs[-1])
            return
        o_ref, acc_ref = refs[-2:]
        k = pl.program_id(2)

        @pl.when(k == 0)
        def _():
            acc_ref[...] = jnp.zeros_like(acc_ref)

        acc_ref[...] += _dot(a_ref[...], b_ref[...], dims)

        @pl.when(k == nk - 1)
        def _():
            finish(acc_ref[...], o_ref)

    ab = jnp.dtype(a.dtype).itemsize
    ob = jnp.dtype(out_dtype).itemsize
    vmem = 2 * (tm * tk + tk * tn) * ab + tm * tn * (8 + 2 * ob + (8 if has_add else 0)) + (4 << 20)
    ins = [a, b] + ([add] if has_add else []) + ([after] if after is not None else [])
    in_specs = [a_spec, b_spec] + ([o_spec] if has_add else []) + ([HBM_SPEC] if after is not None else [])
    grid = (N // tn, M // tm, nk) if n_outer else (M // tm, N // tn, nk)
    return _pcall(
        kern, name=name, grid=grid,
        in_specs=in_specs, out_specs=o_spec,
        out_shape=jax.ShapeDtypeStruct((M, N), out_dtype),
        scratch_shapes=[pltpu.VMEM((tm, tn), F32)] if nk > 1 else [],
        compiler_params=_params(("parallel", "parallel", "arbitrary"), vmem),
    )(*ins)


def _matmul_tn_half(a, b, half, add, tm, tn, by_rows, name):
    K, M = a.shape
    N = b.shape[1]
    Mo, No = (M // 2, N) if by_rows else (M, N // 2)
    tm, tn = _tile(Mo, tm, V7X_LANES), _tile(No, tn, V7X_LANES)
    nm, nn = Mo // tm, No // tn
    has_add = add is not None

    def kern(h_ref, a_ref, b_ref, *rest):
        r = _dot(a_ref[...], b_ref[...], TN)
        if has_add:
            r = r + rest[0][...].astype(F32)
        rest[-1][...] = r.astype(ACT_DTYPE)

    if by_rows:
        a_map, b_map = (lambda i, j, h: (0, h[0] * nm + i)), (lambda i, j, h: (0, j))
        o_map, grid = (lambda i, j, h: (i, j)), (nm, nn)
    else:
        a_map, b_map = (lambda j, i, h: (0, i)), (lambda j, i, h: (0, h[0] * nn + j))
        o_map, grid = (lambda j, i, h: (i, j)), (nn, nm)
    blk = pl.BlockSpec((tm, tn), o_map)
    grid_spec = pltpu.PrefetchScalarGridSpec(
        num_scalar_prefetch=1, grid=grid,
        in_specs=[pl.BlockSpec((K, tm), a_map), pl.BlockSpec((K, tn), b_map)] + ([blk] if has_add else []),
        out_specs=blk)
    ab = jnp.dtype(a.dtype).itemsize
    vmem = 2 * K * (tm + tn) * ab + tm * tn * 16 + (4 << 20)
    return _pcall(kern, name=name, grid_spec=grid_spec, out_shape=jax.ShapeDtypeStruct((Mo, No), ACT_DTYPE),
                  compiler_params=_params(("parallel", "parallel"), vmem))(half, a, b, *([add] if has_add else []))


def _matmul_col_slice(a, b, j, n_total, prev, tm, tn, name):
    M, K = a.shape
    N, width = n_total, n_total // 4
    alone = b.shape[1] == width
    tm, tn = _tile(M, tm, V7X_LANES), _tile(width, tn, V7X_LANES)
    nn = width // tn
    in_place = prev.shape == (M, N)

    def kern(j_ref, a_ref, b_ref, prev_ref, o_ref):
        o_ref[...] = _dot(a_ref[...], b_ref[...])

    b_map = (lambda i, n, j: (0, n)) if alone else (lambda i, n, j: (0, j[0] * nn + n))
    grid_spec = pltpu.PrefetchScalarGridSpec(
        num_scalar_prefetch=1, grid=(M // tm, nn),
        in_specs=[pl.BlockSpec((tm, K), lambda i, n, j: (i, 0)), pl.BlockSpec((K, tn), b_map), HBM_SPEC],
        out_specs=pl.BlockSpec((tm, tn), lambda i, n, j: (i, j[0] * nn + n)))
    ab = jnp.dtype(a.dtype).itemsize
    vmem = 2 * K * (tm + tn) * ab + tm * tn * 16 + (4 << 20)
    return _pcall(kern, name=name, grid_spec=grid_spec, out_shape=jax.ShapeDtypeStruct((M, N), F32),
                  input_output_aliases={3: 0} if in_place else {},
                  compiler_params=_params(("parallel", "parallel"), vmem))(j, a, b, prev)


def _rms_fwd(x, w, name):
    T, D = x.shape
    tm = _tile(T, 256, 16)

    def kern(x_ref, w_ref, o_ref):
        xv = x_ref[...]
        r = lax.rsqrt(jnp.mean(xv * xv, axis=-1, keepdims=True) + EPS)
        o_ref[...] = (xv * r * w_ref[...]).astype(ACT_DTYPE)

    return _pcall(kern, name=name, grid=(T // tm,),
                  in_specs=[pl.BlockSpec((tm, D), lambda i: (i, 0)), pl.BlockSpec((1, D), lambda i: (0, 0))],
                  out_specs=pl.BlockSpec((tm, D), lambda i: (i, 0)),
                  out_shape=jax.ShapeDtypeStruct((T, D), ACT_DTYPE),
                  compiler_params=_params(("parallel",), 8 * tm * D * 4))(x, w)


def _rms_bwd(x, w, g, g_col, res, want_act, name, after=None):
    T, D = x.shape
    tm = _tile(T, 256, 16)
    has_res = res is not None

    def kern(*refs):
        refs = list(refs)
        x_ref, w_ref, g_ref = refs[:3]
        res_ref = refs[3] if has_res else None
        outs = refs[3 + has_res + (after is not None):]
        dx_ref = outs[0]
        dxa_ref = outs[1] if want_act else None
        dw_ref = outs[-1]
        i = pl.program_id(0)
        xv = x_ref[...]
        gv = g_ref[...].astype(F32)
        r = lax.rsqrt(jnp.mean(xv * xv, axis=-1, keepdims=True) + EPS)
        gw = gv * w_ref[...]
        dx = r * gw - xv * (r * r * r) * jnp.mean(gw * xv, axis=-1, keepdims=True)
        if has_res:
            dx = dx + res_ref[...]
        dx_ref[...] = dx
        if want_act:
            dxa_ref[...] = dx.astype(ACT_DTYPE)

        @pl.when(i == 0)
        def _():
            dw_ref[...] = jnp.zeros_like(dw_ref)

        dw_ref[...] += jnp.sum(gv * xv * r, axis=0, keepdims=True)

    row = pl.BlockSpec((tm, D), lambda i: (i, 0))
    vec = pl.BlockSpec((1, D), lambda i: (0, 0))
    in_specs = ([row, vec, pl.BlockSpec((tm, D), lambda i: (i, g_col))] + ([row] if has_res else [])
                + ([HBM_SPEC] if after is not None else []))
    out_specs = [row] + ([row] if want_act else []) + [vec]
    out_shape = ([jax.ShapeDtypeStruct((T, D), F32)]
                 + ([jax.ShapeDtypeStruct((T, D), ACT_DTYPE)] if want_act else [])
                 + [jax.ShapeDtypeStruct((1, D), F32)])
    ins = [x, w, g] + ([res] if has_res else []) + ([after] if after is not None else [])
    return _pcall(kern, name=name, grid=(T // tm,), in_specs=in_specs, out_specs=out_specs,
                  out_shape=out_shape, compiler_params=_params(("arbitrary",), 14 * tm * D * 4))(*ins)


def _loss_bwd(h, target, w, name):
    T, D = h.shape
    tm = _tile(T, 256, 16)

    def kern(h_ref, t_ref, w_ref, dh_ref, dha_ref, dw_ref, loss_ref):
        i = pl.program_id(0)
        hv = h_ref[...]
        r = lax.rsqrt(jnp.mean(hv * hv, axis=-1, keepdims=True) + EPS)
        e = hv * r * w_ref[...] - t_ref[...]
        dy = e * (1.0 / D)
        gw = dy * w_ref[...]
        dh = r * gw - hv * (r * r * r) * jnp.mean(gw * hv, axis=-1, keepdims=True)
        dh_ref[...] = dh
        dha_ref[...] = dh.astype(ACT_DTYPE)

        @pl.when(i == 0)
        def _():
            dw_ref[...] = jnp.zeros_like(dw_ref)
            loss_ref[...] = jnp.zeros_like(loss_ref)

        dw_ref[...] += jnp.sum(dy * hv * r, axis=0, keepdims=True)
        part = 0.5 * jnp.sum(jnp.mean(e * e, axis=-1, keepdims=True), axis=0, keepdims=True)
        loss_ref[...] += jnp.broadcast_to(part, loss_ref.shape)

    row = pl.BlockSpec((tm, D), lambda i: (i, 0))
    vec = pl.BlockSpec((1, D), lambda i: (0, 0))
    return _pcall(kern, name=name, grid=(T // tm,), in_specs=[row, row, vec],
                  out_specs=[row, row, vec, pl.BlockSpec((8, V7X_LANES), lambda i: (0, 0))],
                  out_shape=[jax.ShapeDtypeStruct((T, D), F32), jax.ShapeDtypeStruct((T, D), ACT_DTYPE),
                             jax.ShapeDtypeStruct((1, D), F32), jax.ShapeDtypeStruct((8, V7X_LANES), F32)],
                  compiler_params=_params(("arbitrary",), 14 * tm * D * 4))(h, target, w)


def _conv(x, w_ref, b, width):
    S = x.shape[0]
    row = _rows(S)
    y = b + x * w_ref[pl.ds(width - 1, 1), :]
    for j in range(width - 1):
        sh = width - 1 - j
        y = y + jnp.where(row >= sh, pltpu.roll(x, sh, 0), 0.0) * w_ref[pl.ds(j, 1), :]
    return y


def _conv_bwd(x, dy, w_ref, width):
    S = dy.shape[0]
    row = _rows(S)
    dx = dy * w_ref[pl.ds(width - 1, 1), :]
    dw = [None] * (width - 1) + [jnp.sum(x * dy, axis=0, keepdims=True)]
    for j in range(width - 1):
        sh = width - 1 - j
        dys = jnp.where(row < S - sh, pltpu.roll(dy, S - sh, 0), 0.0)
        dx = dx + dys * w_ref[pl.ds(j, 1), :]
        dw[j] = jnp.sum(x * dys, axis=0, keepdims=True)
    return dx, dw


def _ffn_act(up, cw, cb, B, name):
    T, F2 = up.shape
    S, F = T // B, F2 // 2
    tw = _tile(F, 256, V7X_LANES)
    nt = F // tw

    def kern(g_ref, v_ref, wg_ref, wv_ref, bg_ref, bv_ref, o_ref, mg_ref, mv_ref):
        gc = _conv(g_ref[...], wg_ref, bg_ref[...], FFN_CONV)
        vc = _conv(v_ref[...], wv_ref, bv_ref[...], FFN_CONV)
        silu, dsilu = _silu_and_grad(gc)
        o_ref[...] = (silu * vc).astype(ACT_DTYPE)
        mg_ref[...] = (vc * dsilu).astype(ACT_DTYPE)
        mv_ref[...] = silu.astype(ACT_DTYPE)

    blk = lambda off: pl.BlockSpec((S, tw), lambda b, i: (b, off + i))
    wblk = lambda off: pl.BlockSpec((FFN_CONV, tw), lambda b, i: (0, off + i))
    bblk = lambda off: pl.BlockSpec((1, tw), lambda b, i: (0, off + i))
    half = jax.ShapeDtypeStruct((T, F), ACT_DTYPE)
    return _pcall(kern, name=name, grid=(B, nt),
                  in_specs=[blk(0), blk(nt), wblk(0), wblk(nt), bblk(0), bblk(nt)],
                  out_specs=[blk(0), blk(0), blk(0)], out_shape=[half, half, half],
                  compiler_params=_params(("parallel", "parallel"), 20 * S * tw * 4))(up, up, cw, cw, cb, cb)


def _ffn_act_bwd(up, cw, mg, mv, d_act, B, name):
    T, F2 = up.shape
    S, F = T // B, F2 // 2
    tw = _tile(F, 256, V7X_LANES)
    nt = F // tw

    def kern(s_ref, ws_ref, mg_ref, mv_ref, da_ref, du_ref, dcw_ref, dcb_ref):
        t, b = pl.program_id(0), pl.program_id(1)
        mult = jnp.where(t < nt, mg_ref[...], mv_ref[...])
        d = da_ref[...].astype(F32) * mult.astype(F32)
        dx, dw = _conv_bwd(s_ref[...], d, ws_ref, FFN_CONV)
        du_ref[...] = dx.astype(ACT_DTYPE)

        @pl.when(b == 0)
        def _():
            dcw_ref[...] = jnp.zeros_like(dcw_ref)
            dcb_ref[...] = jnp.zeros_like(dcb_ref)

        for j, rj in enumerate(dw):
            dcw_ref[pl.ds(j, 1), :] += rj
        dcb_ref[...] += jnp.sum(d, axis=0, keepdims=True)

    own = lambda t, b: (b, t % nt)
    return _pcall(
        kern, name=name, grid=(2 * nt, B),
        in_specs=[pl.BlockSpec((S, tw), lambda t, b: (b, t)),
                  pl.BlockSpec((FFN_CONV, tw), lambda t, b: (0, t)),
                  pl.BlockSpec((S, tw), own), pl.BlockSpec((S, tw), own), pl.BlockSpec((S, tw), own)],
        out_specs=[pl.BlockSpec((S, tw), lambda t, b: (b, t)),
                   pl.BlockSpec((FFN_CONV, tw), lambda t, b: (0, t)),
                   pl.BlockSpec((1, tw), lambda t, b: (0, t))],
        out_shape=[jax.ShapeDtypeStruct((T, F2), ACT_DTYPE), jax.ShapeDtypeStruct((FFN_CONV, F2), F32),
                   jax.ShapeDtypeStruct((1, F2), F32)],
        compiler_params=_params(("parallel", "arbitrary"), 20 * S * tw * 4),
    )(up, cw, mg, mv, d_act)


def _hgrn_tables():
    C = CHUNK
    t = np.arange(C)
    mats = [(t[:, None] >= t[None, :]).astype(np.float32)]
    masks = []
    gsum = [(t[:, None] <= t[None, :]).astype(np.float32), (t[:, None] > t[None, :]).astype(np.float32)]
    for hs in LEVEL_HALVES:
        m = (t // (2 * hs)) * 2 * hs + hs
        later = t >= m
        d = np.zeros((C, C), np.float32)
        for i in range(C):
            if later[i]:
                d[i, m[i]:i + 1] = 1.0
            else:
                d[i, i + 1:m[i]] = -1.0
        mats.append(d)
        same = (t[:, None] // (2 * hs)) == (t[None, :] // (2 * hs))
        masks.append((same & later[:, None] & (~later)[None, :]).astype(np.float32))
        gsum.append((same & later[:, None] & (t[None, :] >= t[:, None])).astype(np.float32))
        gsum.append((same & (~later)[:, None] & (t[None, :] < t[:, None])).astype(np.float32))
    return np.concatenate(mats, 0), np.stack(masks, 0), np.concatenate(gsum, 1)


def _split_dot(mat, v):
    hi = v.astype(MXU_DTYPE)
    lo = (v - hi.astype(F32)).astype(MXU_DTYPE)
    r = _dot(mat, jnp.concatenate([hi, lo], axis=1))
    n = v.shape[1]
    return r[:, :n] + r[:, n:]


def _hgrn_gates(qr, fr, lb, mc):
    C = CHUNK
    q, dq_dqr = _silu_and_grad(qr)
    sf = _sigmoid(fr)
    f = lb + (1.0 - lb) * sf
    k = 1.0 - f
    dall = _split_dot(mc, jnp.log(f))
    b = dall[0:C]
    dl = [dall[C * (l + 1):C * (l + 2)] for l in range(len(LEVEL_HALVES))]
    eq = [jnp.exp(jnp.minimum(d, 0.0)) for d in dl]
    ek = [jnp.exp(jnp.minimum(-d, 0.0)) for d in dl]
    return q, dq_dqr, sf, f, k, b, eq, ek


def _hgrn_scores(q, k, eq, ek, masks_ref):
    p = jnp.where(_rows(CHUNK) == lax.broadcasted_iota(jnp.int32, (1, CHUNK), 1),
                  jnp.sum(q * k, axis=-1, keepdims=True), 0.0)
    for l in range(len(LEVEL_HALVES)):
        p = p + masks_ref[l] * _dot(q * eq[l], k * ek[l], NT)
    return p


def _hgrn_fwd(proj, lb_gamma, norm_w, B, HW, name):
    T = proj.shape[0]
    S, H, C = T // B, HW // HEAD_DIM, CHUNK
    NC = S // C
    mc_np, masks_np, _ = _hgrn_tables()
    mc, masks = jnp.asarray(mc_np, MXU_DTYPE), jnp.asarray(masks_np, F32)

    def kern(q_ref, f_ref, i_ref, g_ref, lbg_ref, nw_ref, mc_ref, masks_ref, oraw_ref, o_ref, st_ref, p_ref):
        g0, g1 = lbg_ref[pl.ds(0, 1), :], lbg_ref[pl.ds(1, 1), :]
        mx = jnp.maximum(g0, g1)
        e0, e1 = jnp.exp(g0 - mx), jnp.exp(g1 - mx)
        lb = e0 / (e0 + e1)
        nw = nw_ref[...]
        mcv = mc_ref[...]

        def body(n, sts):
            out = []
            for s, st in enumerate(sts):
                rows = pl.ds(pl.multiple_of(s * S + n * C, C), C)
                st_ref[s, n] = st
                q, _, _, _, k, b, eq, ek = _hgrn_gates(q_ref[rows, :], f_ref[rows, :], lb, mcv)
                v = i_ref[rows, :]
                p = _hgrn_scores(q, k, eq, ek, masks_ref)
                p_ref[s, n] = p
                o = _dot(q * jnp.exp(b), st, NT) + _dot(p, v)
                b_last = b[C - 1:C]
                out.append(st * jnp.exp(b_last) + _dot(v, k * jnp.exp(b_last - b), TN))
                oraw_ref[rows, :] = o
                r = lax.rsqrt(jnp.mean(o * o, axis=-1, keepdims=True) + EPS)
                gate, _ = _silu_and_grad(g_ref[rows, :])
                o_ref[rows, :] = (o * r * nw * gate).astype(ACT_DTYPE)
            return tuple(out)

        lax.fori_loop(0, NC, body, tuple(jnp.zeros((HEAD_DIM, HEAD_DIM), F32) for _ in range(B)))

    col = lambda off: pl.BlockSpec((T, HEAD_DIM), lambda h: (0, off + h))
    return _pcall(
        kern, name=name, grid=(H,),
        in_specs=[col(0), col(H), col(2 * H), col(3 * H),
                  pl.BlockSpec((2, HEAD_DIM), lambda h: (0, h)),
                  pl.BlockSpec((1, HEAD_DIM), lambda h: (0, h)),
                  pl.BlockSpec(mc.shape, lambda h: (0, 0)),
                  pl.BlockSpec(masks.shape, lambda h: (0, 0, 0))],
        out_specs=[col(0), col(0),
                   pl.BlockSpec((B, None, NC, HEAD_DIM, HEAD_DIM), lambda h: (0, h, 0, 0, 0)),
                   pl.BlockSpec((B, None, NC, C, C), lambda h: (0, h, 0, 0, 0))],
        out_shape=[jax.ShapeDtypeStruct((T, HW), F32), jax.ShapeDtypeStruct((T, HW), ACT_DTYPE),
                   jax.ShapeDtypeStruct((B, H, NC, HEAD_DIM, HEAD_DIM), F32),
                   jax.ShapeDtypeStruct((B, H, NC, C, C), F32)],
        compiler_params=_params(("parallel",), 20 * T * HEAD_DIM * 4 + (8 << 20)),
    )(proj, proj, proj, proj, lb_gamma, norm_w, mc, masks)


def _hgrn_bwd(proj, lb_gamma, norm_w, o_raw, states, scores, d_mix, B, HW, name):
    T = proj.shape[0]
    S, H, C = T // B, HW // HEAD_DIM, CHUNK
    NC = S // C
    mc_np, masks_np, gsum_np = _hgrn_tables()
    mc, masks, gsum = jnp.asarray(mc_np, MXU_DTYPE), jnp.asarray(masks_np, F32), jnp.asarray(gsum_np, MXU_DTYPE)
    nl = len(LEVEL_HALVES)

    def kern(q_ref, f_ref, i_ref, g_ref, lbg_ref, nw_ref, mc_ref, masks_ref, gsum_ref, oraw_ref, st_ref, p_ref, do_ref,
             dq_ref, df_ref, di_ref, dg_ref, dlbg_ref, dnw_ref):
        g0, g1 = lbg_ref[pl.ds(0, 1), :], lbg_ref[pl.ds(1, 1), :]
        mx = jnp.maximum(g0, g1)
        e0, e1 = jnp.exp(g0 - mx), jnp.exp(g1 - mx)
        lb = e0 / (e0 + e1)
        nw = nw_ref[...]
        mcv, gsumv = mc_ref[...], gsum_ref[...]

        def chunk(s, n, dst, dlb, dnw):
            rows = pl.ds(pl.multiple_of(s * S + n * C, C), C)
            qr, fr, v = q_ref[rows, :], f_ref[rows, :], i_ref[rows, :]
            q, dq_dqr, sf, f, k, b, eq, ek = _hgrn_gates(qr, fr, lb, mcv)
            o = oraw_ref[rows, :]
            dout = do_ref[rows, :].astype(F32)
            gate, dgate = _silu_and_grad(g_ref[rows, :])
            r = lax.rsqrt(jnp.mean(o * o, axis=-1, keepdims=True) + EPS)
            dg_ref[rows, :] = (dout * o * r * nw * dgate).astype(ACT_DTYPE)
            don = dout * gate
            dnw = dnw + jnp.sum(don * o * r, axis=0, keepdims=True)
            gw = don * nw
            do = r * gw - o * (r * r * r) * jnp.mean(gw * o, axis=-1, keepdims=True)
            st_prev = st_ref[s, n]
            eb = jnp.exp(b)
            b_last = b[C - 1:C]
            ebl = jnp.exp(b_last - b)
            p = p_ref[s, n]
            dp = _dot(do, v, NT)
            dpd = jnp.sum(do * v, axis=-1, keepdims=True)
            dq_state = _dot(do, st_prev) * eb
            dk_state = _dot(v, dst) * ebl
            dq = dq_state + dpd * k
            dk = dk_state + dpd * q
            pairs = [q * dq_state, k * dk_state]
            for l in range(nl):
                mdp = masks_ref[l] * dp
                dql = _dot(mdp, k * ek[l]) * eq[l]
                dkl = _dot(mdp, q * eq[l], TN) * ek[l]
                dq, dk = dq + dql, dk + dkl
                pairs += [q * dql, k * dkl]
            dv = _dot(p, do, TN) + _dot(k * ebl, dst, NT)
            through = jnp.exp(b_last) * jnp.sum(dst * st_prev, axis=0, keepdims=True)
            dlg = _split_dot(gsumv, jnp.concatenate(pairs, axis=0)) + through
            dst = dst * jnp.exp(b_last) + _dot(do, q * eb, TN)
            dq_ref[rows, :] = (dq * dq_dqr).astype(ACT_DTYPE)
            dfv = dlg / f - dk
            df_ref[rows, :] = (dfv * (1.0 - lb) * sf * (1.0 - sf)).astype(ACT_DTYPE)
            di_ref[rows, :] = dv.astype(ACT_DTYPE)
            dlb = dlb + jnp.sum(dfv * (1.0 - sf), axis=0, keepdims=True)
            return dst, dlb, dnw

        def body(it, carry):
            dsts, dlb, dnw = carry
            out = []
            for s, dst in enumerate(dsts):
                dst, dlb, dnw = chunk(s, NC - 1 - it, dst, dlb, dnw)
                out.append(dst)
            return tuple(out), dlb, dnw

        zrow = jnp.zeros((1, HEAD_DIM), F32)
        zst = tuple(jnp.zeros((HEAD_DIM, HEAD_DIM), F32) for _ in range(B))
        _, dlb, dnw = lax.fori_loop(0, NC, body, (zst, zrow, zrow))
        dg0 = dlb * lb * (1.0 - lb)
        dlbg_ref[pl.ds(0, 1), :] = dg0
        dlbg_ref[pl.ds(1, 1), :] = -dg0
        dnw_ref[...] = dnw

    col = lambda off: pl.BlockSpec((T, HEAD_DIM), lambda h: (0, off + h))
    full = lambda a: pl.BlockSpec(a.shape, lambda h: (0,) * a.ndim)
    part = jax.ShapeDtypeStruct((T, HW), ACT_DTYPE)
    return _pcall(
        kern, name=name, grid=(H,),
        in_specs=[col(0), col(H), col(2 * H), col(3 * H),
                  pl.BlockSpec((2, HEAD_DIM), lambda h: (0, h)),
                  pl.BlockSpec((1, HEAD_DIM), lambda h: (0, h)),
                  full(mc), full(masks), full(gsum), col(0),
                  pl.BlockSpec((B, None, NC, HEAD_DIM, HEAD_DIM), lambda h: (0, h, 0, 0, 0)),
                  pl.BlockSpec((B, None, NC, C, C), lambda h: (0, h, 0, 0, 0)),
                  col(0)],
        out_specs=[col(0), col(0), col(0), col(0),
                   pl.BlockSpec((2, HEAD_DIM), lambda h: (0, h)),
                   pl.BlockSpec((1, HEAD_DIM), lambda h: (0, h))],
        out_shape=[part, part, part, part, jax.ShapeDtypeStruct((2, HW), F32), jax.ShapeDtypeStruct((1, HW), F32)],
        compiler_params=_params(("parallel",), 28 * T * HEAD_DIM * 4 + (8 << 20)),
    )(proj, proj, proj, proj, lb_gamma, norm_w, mc, masks, gsum, o_raw, states, scores, d_mix)


def _lru_gates(xr, cw_ref, cb, wa, ba, wx, bx, lam):
    S = xr.shape[0]
    xb = _conv(xr, cw_ref, cb, LRU_CONV)
    r = _sigmoid(_dot(xb, wa) + ba)
    ig = _sigmoid(_dot(xb, wx) + bx)
    sp = jnp.maximum(-lam, 0.0) + jnp.log(1.0 + jnp.exp(-jnp.abs(lam)))
    la = -LRU_C * r * sp
    a = jnp.exp(la)
    mult = jnp.where(_rows(S) == 0, 1.0, jnp.sqrt(_one_minus_exp(2.0 * la)))
    return xb, r, ig, sp, a, mult


def _scan_rows(a_ref, u_ref, h_ref, reverse):
    S, W = a_ref.shape
    nb = S // 8
    row = _rows(8)

    def body(it, carry):
        blk = nb - 1 - it if reverse else it
        rows = pl.ds(pl.multiple_of(blk * 8, 8), 8)
        a, u = a_ref[rows, :], u_ref[rows, :]
        for d in (1, 2, 4):
            sh = 8 - d if reverse else d
            keep = (row < 8 - d) if reverse else (row >= d)
            u = u + jnp.where(keep, a * pltpu.roll(u, sh, 0), 0.0)
            a = jnp.where(keep, a * pltpu.roll(a, sh, 0), a)
        h = u + a * carry
        h_ref[rows, :] = h
        return h[0:1] if reverse else h[7:8]

    lax.fori_loop(0, nb, body, jnp.zeros((1, W), F32))


def _lru_fwd(proj, cw, cb, wa, ba, wx, bx, lam, B, HW, LW, name):
    T = proj.shape[0]
    S, NB = T // B, LW // HEAD_DIM
    xoff, yoff = 4 * HW // HEAD_DIM, 4 * HW // HEAD_DIM + NB

    def kern(x_ref, y_ref, cw_ref, cb_ref, wa_ref, ba_ref, wx_ref, bx_ref, lam_ref, h_ref, z_ref, a_s, u_s):
        xb, _, ig, _, a, mult = _lru_gates(x_ref[...], cw_ref, cb_ref[...], wa_ref[...], ba_ref[...],
                                           wx_ref[...], bx_ref[...], lam_ref[...])
        a_s[...] = a
        u_s[...] = xb * ig * mult
        _scan_rows(a_s, u_s, h_ref, False)
        gy, _ = _gelu_and_grad(y_ref[...])
        z_ref[...] = h_ref[...] * gy

    blk = lambda off: pl.BlockSpec((S, HEAD_DIM), lambda b, n: (b, off + n))
    vec = pl.BlockSpec((1, HEAD_DIM), lambda b, n: (0, n))
    mat = pl.BlockSpec((None, HEAD_DIM, HEAD_DIM), lambda b, n: (n, 0, 0))
    return _pcall(
        kern, name=name, grid=(B, NB),
        in_specs=[blk(xoff), blk(yoff), pl.BlockSpec((LRU_CONV, HEAD_DIM), lambda b, n: (0, n)),
                  vec, mat, vec, mat, vec, vec],
        out_specs=[blk(0), blk(0)],
        out_shape=[jax.ShapeDtypeStruct((T, LW), F32), jax.ShapeDtypeStruct((T, LW), F32)],
        scratch_shapes=[pltpu.VMEM((S, HEAD_DIM), F32), pltpu.VMEM((S, HEAD_DIM), F32)],
        compiler_params=_params(("parallel", "parallel"), 24 * S * HEAD_DIM * 4),
    )(proj, proj, cw, cb, wa, ba, wx, bx, lam)


def _lru_bwd(proj, cw, cb, wa, ba, wx, bx, lam, h, dz, B, HW, LW, name):
    T = proj.shape[0]
    S, NB = T // B, LW // HEAD_DIM
    xoff, yoff = 4 * HW // HEAD_DIM, 4 * HW // HEAD_DIM + NB

    def kern(x_ref, y_ref, cw_ref, cb_ref, wa_ref, ba_ref, wx_ref, bx_ref, lam_ref, h_ref, dz_ref,
             dx_ref, dy_ref, dwa_ref, dwx_ref, dba_ref, dbx_ref, dlam_ref, dcw_ref, dcb_ref, a_s, u_s, dh_s):
        bi = pl.program_id(1)
        row = _rows(S)
        xr, lam = x_ref[...], lam_ref[...]
        wa, wx = wa_ref[...], wx_ref[...]
        xb, r, ig, sp, a, mult = _lru_gates(xr, cw_ref, cb_ref[...], wa, ba_ref[...], wx, bx_ref[...], lam)
        hv, dz = h_ref[...], dz_ref[...]
        gy, dgy = _gelu_and_grad(y_ref[...])
        dy_ref[...] = (dz * hv * dgy).astype(ACT_DTYPE)
        a_s[...] = jnp.where(row < S - 1, pltpu.roll(a, S - 1, 0), 0.0)
        u_s[...] = dz * gy
        _scan_rows(a_s, u_s, dh_s, True)
        dh = dh_s[...]
        h_prev = jnp.where(row >= 1, pltpu.roll(hv, 1, 0), 0.0)
        d_ig = dh * xb * mult
        d_mult = jnp.where(row == 0, 0.0, dh * xb * ig)
        dxb = dh * ig * mult
        dla = dh * h_prev * a - d_mult * (a * a) / mult
        dpre_r = dla * (-LRU_C * sp) * r * (1.0 - r)
        dpre_i = d_ig * ig * (1.0 - ig)
        dxb = dxb + _dot(dpre_r, wa, NT) + _dot(dpre_i, wx, NT)
        dxr, dcw = _conv_bwd(xr, dxb, cw_ref, LRU_CONV)
        dx_ref[...] = dxr.astype(ACT_DTYPE)

        @pl.when(bi == 0)
        def _():
            for ref in (dwa_ref, dwx_ref, dba_ref, dbx_ref, dlam_ref, dcw_ref, dcb_ref):
                ref[...] = jnp.zeros_like(ref)

        dwa_ref[...] += _dot(xb, dpre_r, TN)
        dwx_ref[...] += _dot(xb, dpre_i, TN)
        dba_ref[...] += jnp.sum(dpre_r, axis=0, keepdims=True)
        dbx_ref[...] += jnp.sum(dpre_i, axis=0, keepdims=True)
        dsp = jnp.sum(dla * (-LRU_C) * r, axis=0, keepdims=True)
        dlam_ref[...] += -dsp * _sigmoid(-lam)
        for j, rj in enumerate(dcw):
            dcw_ref[pl.ds(j, 1), :] += rj
        dcb_ref[...] += jnp.sum(dxb, axis=0, keepdims=True)

    blk = lambda off: pl.BlockSpec((S, HEAD_DIM), lambda n, b: (b, off + n))
    vec = pl.BlockSpec((1, HEAD_DIM), lambda n, b: (0, n))
    mat = pl.BlockSpec((None, HEAD_DIM, HEAD_DIM), lambda n, b: (n, 0, 0))
    cwb = pl.BlockSpec((LRU_CONV, HEAD_DIM), lambda n, b: (0, n))
    part = jax.ShapeDtypeStruct((T, LW), ACT_DTYPE)
    vshape = jax.ShapeDtypeStruct((1, LW), F32)
    mshape = jax.ShapeDtypeStruct((NB, HEAD_DIM, HEAD_DIM), F32)
    return _pcall(
        kern, name=name, grid=(NB, B),
        in_specs=[blk(xoff), blk(yoff), cwb, vec, mat, vec, mat, vec, vec, blk(0), blk(0)],
        out_specs=[blk(0), blk(0), mat, mat, vec, vec, vec, cwb, vec],
        out_shape=[part, part, mshape, mshape, vshape, vshape, vshape,
                   jax.ShapeDtypeStruct((LRU_CONV, LW), F32), vshape],
        scratch_shapes=[pltpu.VMEM((S, HEAD_DIM), F32)] * 3,
        compiler_params=_params(("parallel", "arbitrary"), 40 * S * HEAD_DIM * 4),
    )(proj, proj, cw, cb, wa, ba, wx, bx, lam, h, dz)


def _pos():
    return lax.axis_index("x"), lax.axis_index("y"), lax.axis_index("c")


def _other_chips(x, y):
    return [(1 - x, y), (x, 1 - y), (1 - x, 1 - y)]


def _remote(src, dst, send_sems, recv_sems, k, to):
    return pltpu.make_async_remote_copy(src_ref=src, dst_ref=dst, send_sem=send_sems.at[k],
                                        recv_sem=recv_sems.at[k], device_id=to, device_id_type=MESH)


HBM_BLK = pl.BlockSpec(memory_space=pltpu.HBM)
SEM_BLK = pl.BlockSpec(memory_space=pltpu.SEMAPHORE)
VMEM_BLK = pl.BlockSpec(memory_space=pltpu.VMEM)
DATAFLOW = pltpu.SideEffectType.DATAFLOW_SIDE_EFFECTING
TOKEN = jax.ShapeDtypeStruct((8, V7X_LANES), F32)


def _in_hbm(a):
    return pltpu.with_memory_space_constraint(a, pltpu.HBM)


def _gather_win(o_ref, R, C, col_sharded):
    Rh = R // 2

    def win(j, h=None):
        if col_sharded:
            rows = pl.ds(0, R) if h is None else pl.ds(h * Rh, Rh)
            return o_ref.at[rows, pl.ds(j * C, C)]
        return o_ref.at[pl.ds(j * R, R) if h is None else pl.ds(j * R + h * Rh, Rh), :]

    return win


def _cast_into_window(w, col_sharded, after, also_alone, name):
    R, C = w.shape
    tr = _tile(R, 256, 16)
    nr = R // tr
    full = (R, 4 * C) if col_sharded else (4 * R, C)
    j = (2 * lax.axis_index("x") + lax.axis_index("y")).astype(jnp.int32).reshape(1)

    def kern(j_ref, w_ref, after_ref, *o_refs):
        for o_ref in o_refs:
            o_ref[...] = w_ref[...].astype(ACT_DTYPE)

    out_map = (lambda i, jr: (i, jr[0])) if col_sharded else (lambda i, jr: (jr[0] * nr + i, 0))
    row = pl.BlockSpec((tr, C), lambda i, jr: (i, 0))
    grid_spec = pltpu.PrefetchScalarGridSpec(
        num_scalar_prefetch=1, grid=(nr,),
        in_specs=[row, HBM_SPEC], out_specs=[pl.BlockSpec((tr, C), out_map)] + ([row] if also_alone else []))
    out_shape = [jax.ShapeDtypeStruct(full, ACT_DTYPE)] + ([jax.ShapeDtypeStruct((R, C), ACT_DTYPE)] if also_alone else [])
    return _pcall(kern, name=name, grid_spec=grid_spec, out_shape=out_shape,
                  compiler_params=_params(("parallel",), 8 * tr * C * 4))(j, w, after)


def _gather_start(land, shard_shape, col_sharded, token, name):
    R, C = shard_shape

    def body(land_ref, tok_ref, send_sems, recv_sems, land_thru, tok_out):
        x, y, c = _pos()
        w = _gather_win(land_ref, R, C, col_sharded)(2 * x + y, c)
        for k, (cx, cy) in enumerate(_other_chips(x, y)):
            _remote(w, w, send_sems, recv_sems, k, (cx, cy, c)).start()
        tok_out[...] = tok_ref[...]

    return _pcall(
        body, name=name,
        out_shape=(pltpu.SemaphoreType.DMA((3,)), pltpu.SemaphoreType.DMA((3,)),
                   pltpu.HBM(land.shape, land.dtype), TOKEN),
        in_specs=(HBM_BLK, VMEM_BLK), out_specs=(SEM_BLK, SEM_BLK, HBM_BLK, VMEM_BLK),
        input_output_aliases={0: 2},
        compiler_params=pltpu.CompilerParams(has_side_effects=DATAFLOW),
    )(_in_hbm(land), token)


def _gather_wait(started, shard_shape, after, col_sharded, name):
    send_sems, recv_sems, land_thru, _ = started
    R, C = shard_shape

    def body(land_ref, send_sems, recv_sems, after_ref, got_ref):
        x, y, c = _pos()
        win = _gather_win(land_ref, R, C, col_sharded)
        for k, (cx, cy) in enumerate(_other_chips(x, y)):
            cp = _remote(win(2 * x + y, c), win(2 * cx + cy, c), send_sems, recv_sems, k, (cx, cy, c))
            cp.wait_send()
            cp.wait_recv()

    return _pcall(
        body, name=name, out_shape=pltpu.HBM(land_thru.shape, land_thru.dtype),
        in_specs=(HBM_BLK, SEM_BLK, SEM_BLK, HBM_SPEC), out_specs=HBM_BLK, input_output_aliases={0: 0},
        compiler_params=pltpu.CompilerParams(has_side_effects=DATAFLOW),
    )(land_thru, send_sems, recv_sems, after)


def _gather_pass_on(landed, shard_shape, col_sharded, name):
    R, C = shard_shape

    def body(in_ref, o_ref, send_sems, recv_sems):
        x, y, c = _pos()
        src, dst = _gather_win(in_ref, R, C, col_sharded), _gather_win(o_ref, R, C, col_sharded)
        chips = [(k, 2 * cx + cy) for k, (cx, cy) in enumerate(_other_chips(x, y))]
        passed = [_remote(src(j, c), dst(j, c), send_sems, recv_sems, k, (x, y, 1 - c)) for k, j in chips]
        for cp in passed:
            cp.start()
        for k, j in chips:
            w = dst(j, 1 - c)
            _remote(w, w, send_sems, recv_sems, k, (x, y, c)).wait_recv()
        for cp in passed:
            cp.wait_send()

    return _pcall(body, name=name, in_specs=[HBM_SPEC], out_specs=HBM_SPEC,
                  out_shape=jax.ShapeDtypeStruct(landed.shape, landed.dtype), input_output_aliases={0: 0},
                  scratch_shapes=[pltpu.SemaphoreType.DMA((3,)), pltpu.SemaphoreType.DMA((3,))])(landed)


def _pair_start(g, token, name):
    def body(g_ref, land_ref, tok_ref, send_sems, recv_sems, g_thru, land_thru, tok_out):
        x, y, c = _pos()
        _remote(g_ref, land_ref, send_sems, recv_sems, 0, (x, y, 1 - c)).start()
        tok_out[...] = tok_ref[...]

    return _pcall(
        body, name=name,
        out_shape=(pltpu.SemaphoreType.DMA((1,)), pltpu.SemaphoreType.DMA((1,)),
                   pltpu.HBM(g.shape, g.dtype), pltpu.HBM(g.shape, g.dtype), TOKEN),
        in_specs=(HBM_BLK, HBM_BLK, VMEM_BLK), out_specs=(SEM_BLK, SEM_BLK, HBM_BLK, HBM_BLK, VMEM_BLK),
        input_output_aliases={0: 2, 1: 3},
        compiler_params=pltpu.CompilerParams(has_side_effects=DATAFLOW),
    )(_in_hbm(g), _in_hbm(lax.empty(g.shape, g.dtype)), token)


def _pair_wait(started, after, name):
    send_sems, recv_sems, g_thru, land_thru, _ = started

    def body(g_ref, land_ref, send_sems, recv_sems, after_ref, g_dead, got_ref):
        x, y, c = _pos()
        cp = _remote(g_ref, land_ref, send_sems, recv_sems, 0, (x, y, 1 - c))
        cp.wait_send()
        cp.wait_recv()

    return _pcall(
        body, name=name,
        out_shape=(pltpu.HBM(g_thru.shape, g_thru.dtype), pltpu.HBM(land_thru.shape, land_thru.dtype)),
        in_specs=(HBM_BLK, HBM_BLK, SEM_BLK, SEM_BLK, HBM_SPEC), out_specs=(HBM_BLK, HBM_BLK),
        input_output_aliases={0: 0, 1: 1},
        compiler_params=pltpu.CompilerParams(has_side_effects=DATAFLOW),
    )(g_thru, land_thru, send_sems, recv_sems, after)[1]


def _own_piece_into_slots(cs, col_sharded, name):
    J, Rp, W = cs.shape
    Cp = W // 4 if col_sharded else W
    tr = _tile(Rp, 256, 16)
    tw = _tile(Cp, 8192, V7X_LANES)
    nw = Cp // tw
    x, y, c = _pos()
    chip = (2 * x + y).astype(jnp.int32).reshape(1)
    core = c.astype(jnp.int32).reshape(1)

    def kern(j_ref, c_ref, s_ref, o_ref):
        o_ref[...] = s_ref[...]

    in_map = ((lambda i, w, j, cc: (0, i, j[0] * nw + w)) if col_sharded
              else (lambda i, w, j, cc: (j[0], i, w)))
    grid_spec = pltpu.PrefetchScalarGridSpec(
        num_scalar_prefetch=2, grid=(Rp // tr, nw),
        in_specs=[pl.BlockSpec((None, tr, tw), in_map)],
        out_specs=pl.BlockSpec((None, None, tr, tw), lambda i, w, j, cc: (j[0], cc[0], i, w)))
    return _pcall(kern, name=name, grid_spec=grid_spec,
                  out_shape=jax.ShapeDtypeStruct((4, 2, Rp, Cp), cs.dtype),
                  compiler_params=_params(("parallel", "parallel"), 8 * tr * tw * 4))(chip, core, cs)


def _chip_sum_piece(cs_ref, C, col_sharded):
    return lambda j: cs_ref.at[0, :, pl.ds(j * C, C)] if col_sharded else cs_ref.at[j]


def _scatter_start(cs, slots, col_sharded, token, name):
    C = slots.shape[3]

    def body(cs_ref, land_ref, tok_ref, send_sems, recv_sems, cs_thru, land_thru, tok_out):
        x, y, c = _pos()
        piece = _chip_sum_piece(cs_ref, C, col_sharded)
        for k, (cx, cy) in enumerate(_other_chips(x, y)):
            _remote(piece(2 * cx + cy), land_ref.at[2 * x + y, c], send_sems, recv_sems, k, (cx, cy, c)).start()
        tok_out[...] = tok_ref[...]

    return _pcall(
        body, name=name,
        out_shape=(pltpu.SemaphoreType.DMA((3,)), pltpu.SemaphoreType.DMA((3,)),
                   pltpu.HBM(cs.shape, cs.dtype), pltpu.HBM(slots.shape, cs.dtype), TOKEN),
        in_specs=(HBM_BLK, HBM_BLK, VMEM_BLK), out_specs=(SEM_BLK, SEM_BLK, HBM_BLK, HBM_BLK, VMEM_BLK),
        input_output_aliases={0: 2, 1: 3},
        compiler_params=pltpu.CompilerParams(has_side_effects=DATAFLOW),
    )(_in_hbm(cs), _in_hbm(slots), token)


def _scatter_wait(started, after, col_sharded, name):
    send_sems, recv_sems, cs_thru, land_thru, _ = started
    C = land_thru.shape[3]

    def body(cs_ref, land_ref, send_sems, recv_sems, after_ref, cs_dead, got_ref):
        x, y, c = _pos()
        piece = _chip_sum_piece(cs_ref, C, col_sharded)
        for k, (cx, cy) in enumerate(_other_chips(x, y)):
            cp = _remote(piece(2 * cx + cy), land_ref.at[2 * cx + cy, c], send_sems, recv_sems, k, (cx, cy, c))
            cp.wait_send()
            cp.wait_recv()

    return _pcall(
        body, name=name,
        out_shape=(pltpu.HBM(cs_thru.shape, cs_thru.dtype), pltpu.HBM(land_thru.shape, land_thru.dtype)),
        in_specs=(HBM_BLK, HBM_BLK, SEM_BLK, SEM_BLK, HBM_SPEC), out_specs=(HBM_BLK, HBM_BLK),
        input_output_aliases={0: 0, 1: 1},
        compiler_params=pltpu.CompilerParams(has_side_effects=DATAFLOW),
    )(cs_thru, land_thru, send_sems, recv_sems, after)


def _scatter_pass_on(landed, name):
    def body(in_ref, o_ref, send_sems, recv_sems):
        x, y, c = _pos()
        sends = [_remote(in_ref.at[i, c], o_ref.at[i, c], send_sems, recv_sems, i, (x, y, 1 - c)) for i in range(4)]
        for cp in sends:
            cp.start()
        for i in range(4):
            w = o_ref.at[i, 1 - c]
            _remote(w, w, send_sems, recv_sems, i, (x, y, c)).wait_recv()
        for cp in sends:
            cp.wait_send()

    return _pcall(body, name=name, in_specs=[HBM_SPEC], out_specs=HBM_SPEC,
                  out_shape=jax.ShapeDtypeStruct(landed.shape, landed.dtype), input_output_aliases={0: 0},
                  scratch_shapes=[pltpu.SemaphoreType.DMA((4,)), pltpu.SemaphoreType.DMA((4,))])(landed)


def _gather_windows(shard_shape, col_sharded):
    R, C = shard_shape

    def windows(ref, x, y, cc):
        win = _gather_win(ref, R, C, col_sharded)
        return [win(2 * cx + cy, cc) for cx, cy in _other_chips(x, y)]

    return windows


def _slot_windows(ref, x, y, cc):
    return [ref.at[i, cc] for i in range(4)]


def _pass_start(landed, windows, n, token, name):
    def body(land_ref, tok_ref, send_sems, recv_sems, land_thru, tok_out):
        x, y, c = _pos()
        for k, w in enumerate(windows(land_ref, x, y, c)):
            _remote(w, w, send_sems, recv_sems, k, (x, y, 1 - c)).start()
        tok_out[...] = tok_ref[...]

    return _pcall(
        body, name=name,
        out_shape=(pltpu.SemaphoreType.DMA((n,)), pltpu.SemaphoreType.DMA((n,)),
                   pltpu.HBM(landed.shape, landed.dtype), TOKEN),
        in_specs=(HBM_BLK, VMEM_BLK), out_specs=(SEM_BLK, SEM_BLK, HBM_BLK, VMEM_BLK),
        input_output_aliases={0: 2},
        compiler_params=pltpu.CompilerParams(has_side_effects=DATAFLOW),
    )(_in_hbm(landed), token)


def _pass_wait(started, windows, after, name):
    send_sems, recv_sems, land_thru, _ = started

    def body(land_ref, send_sems, recv_sems, after_ref, got_ref):
        x, y, c = _pos()
        mine, theirs = windows(land_ref, x, y, c), windows(land_ref, x, y, 1 - c)
        for k, (src, dst) in enumerate(zip(mine, theirs)):
            cp = _remote(src, dst, send_sems, recv_sems, k, (x, y, 1 - c))
            cp.wait_send()
            cp.wait_recv()

    return _pcall(
        body, name=name, out_shape=pltpu.HBM(land_thru.shape, land_thru.dtype),
        in_specs=(HBM_BLK, SEM_BLK, SEM_BLK, HBM_SPEC), out_specs=HBM_BLK, input_output_aliases={0: 0},
        compiler_params=pltpu.CompilerParams(has_side_effects=DATAFLOW),
    )(land_thru, send_sems, recv_sems, after)


def _gather_small(buf, name):
    rows = buf.shape[0]

    def body(b_ref, o_ref, send_sems, recv_sems):
        x, y, c = _pos()
        jme = 2 * x + y
        chips = _other_chips(x, y)
        o_ref[jme] = b_ref[...]
        sends = [_remote(b_ref, o_ref.at[jme], send_sems, recv_sems, k, (cx, cy, c))
                 for k, (cx, cy) in enumerate(chips)]
        for cp in sends:
            cp.start()
        for k, (cx, cy) in enumerate(chips):
            w = o_ref.at[2 * cx + cy]
            _remote(w, w, send_sems, recv_sems, k, (x, y, c)).wait_recv()
        for cp in sends:
            cp.wait_send()

    vm = pl.BlockSpec(memory_space=pltpu.VMEM)
    return _pcall(body, name=name, in_specs=[vm], out_specs=vm,
                  out_shape=jax.ShapeDtypeStruct((4, rows, V7X_LANES), buf.dtype),
                  scratch_shapes=[pltpu.SemaphoreType.DMA((3,)), pltpu.SemaphoreType.DMA((3,))],
                  compiler_params=_params(None, 16 * rows * V7X_LANES * 4))(buf)


def _allreduce_small(buf, name):
    rows = buf.shape[0]
    rh = rows // 2

    def body(b_ref, o_ref, pair, mine, slots, send_sems, recv_sems):
        x, y, c = _pos()
        me, sib, jme = (x, y, c), (x, y, 1 - c), 2 * x + y
        half = pl.ds(pl.multiple_of(c * rh, 8), rh)
        other = pl.ds(pl.multiple_of((1 - c) * rh, 8), rh)
        to_sib = _remote(b_ref, pair.at[c], send_sems, recv_sems, 0, sib)
        to_sib.start()
        pair[c] = b_ref[...]
        _remote(pair.at[1 - c], pair.at[1 - c], send_sems, recv_sems, 0, me).wait_recv()
        mine[...] = pair[0, half, :] + pair[1, half, :]
        chips = _other_chips(x, y)
        sends = [_remote(mine, slots.at[jme], send_sems, recv_sems, 1 + k, (cx, cy, c))
                 for k, (cx, cy) in enumerate(chips)]
        for cp in sends:
            cp.start()
        slots[jme] = mine[...]
        for k, (cx, cy) in enumerate(chips):
            w = slots.at[2 * cx + cy]
            _remote(w, w, send_sems, recv_sems, 1 + k, me).wait_recv()
        o_ref[half, :] = (slots[0] + slots[1]) + (slots[2] + slots[3])
        back = _remote(o_ref.at[half, :], o_ref.at[half, :], send_sems, recv_sems, 4, sib)
        back.start()
        _remote(o_ref.at[other, :], o_ref.at[other, :], send_sems, recv_sems, 4, me).wait_recv()
        for cp in [to_sib, back] + sends:
            cp.wait_send()

    vm = pl.BlockSpec(memory_space=pltpu.VMEM)
    return _pcall(body, name=name, in_specs=[vm], out_specs=vm,
                  out_shape=jax.ShapeDtypeStruct(buf.shape, buf.dtype),
                  scratch_shapes=[pltpu.VMEM((2, rows, V7X_LANES), buf.dtype),
                                  pltpu.VMEM((rh, V7X_LANES), buf.dtype),
                                  pltpu.VMEM((4, rh, V7X_LANES), buf.dtype),
                                  pltpu.SemaphoreType.DMA((5,)), pltpu.SemaphoreType.DMA((5,))],
                  compiler_params=_params(None, 10 * rows * V7X_LANES * 4))(buf)


def _adamw_math(w, g, m, v):
    m = ADAM_B1 * m + (1.0 - ADAM_B1) * g
    v = ADAM_B2 * v + (1.0 - ADAM_B2) * (g * g)
    m_hat = m / (1.0 - ADAM_B1 ** ADAM_STEP)
    v_hat = v / (1.0 - ADAM_B2 ** ADAM_STEP)
    delta = -ADAM_LR * (m_hat / (jnp.sqrt(v_hat) + ADAM_EPS) + ADAM_WD * w)
    return delta, m, v


def _adamw_big(w, m, v, slots, name):
    R, C = w.shape
    by_rows = slots.shape[3] == C
    tr = _tile(R, 64, 16)

    def kern(w_ref, m_ref, v_ref, s_ref, g_ref, d_ref, mo_ref, vo_ref):
        def chip_sum(h):
            g = s_ref[0, h].astype(F32)
            for i in range(1, 4):
                g = g + s_ref[i, h].astype(F32)
            return g

        g = chip_sum(0) if by_rows else jnp.concatenate([chip_sum(0), chip_sum(1)], axis=1)
        d, mn, vn = _adamw_math(w_ref[...], g, m_ref[...], v_ref[...])
        g_ref[...], d_ref[...], mo_ref[...], vo_ref[...] = g, d, mn, vn

    row = pl.BlockSpec((tr, C), lambda i: (i, 0))
    shp = jax.ShapeDtypeStruct((R, C), F32)
    if by_rows:
        per_half = R // 2 // tr
        s_spec = pl.BlockSpec((4, 1, tr, C), lambda i: (0, i // per_half, i % per_half, 0))
    else:
        s_spec = pl.BlockSpec((4, 2, tr, C // 2), lambda i: (0, 0, i, 0))
    return _pcall(kern, name=name, grid=(R // tr,),
                  in_specs=[row, row, row, s_spec], out_specs=[row] * 4, out_shape=[shp] * 4,
                  compiler_params=_params(("parallel",), 30 * tr * C * 4))(w, m, v, slots)


def _adamw_small(w, g, m, v, name):
    def kern(w_ref, g_ref, m_ref, v_ref, d_ref, mo_ref, vo_ref):
        d_ref[...], mo_ref[...], vo_ref[...] = _adamw_math(w_ref[...], g_ref[...], m_ref[...], v_ref[...])

    vm = pl.BlockSpec(memory_space=pltpu.VMEM)
    shp = jax.ShapeDtypeStruct(w.shape, F32)
    return _pcall(kern, name=name, in_specs=[vm] * 4, out_specs=[vm] * 3, out_shape=[shp] * 3,
                  compiler_params=_params(None, 10 * w.size * 4))(w, g, m, v)


def _pack(arrs):
    flat = jnp.concatenate([a.reshape(-1).astype(F32) for a in arrs])
    n = flat.shape[0]
    rows = -(-n // (16 * V7X_LANES)) * 16
    return jnp.pad(flat, (0, rows * V7X_LANES - n)).reshape(rows, V7X_LANES)


def _unpack(buf, shapes):
    flat = buf.reshape(-1)
    out, off = [], 0
    for s in shapes:
        n = int(np.prod(s))
        out.append(flat[off:off + n].reshape(s))
        off += n
    return out


def _cut_gradient(a, d, col_sharded, tm, tn, token, tag, other_work):
    c = lax.axis_index("c").astype(jnp.int32)
    for_sibling = _matmul_tn_half(a, d, (1 - c).reshape(1), None, tm, tn, col_sharded, "mm_g_%s_sibling" % tag)
    sent = _pair_start(for_sibling, token, "pair_start_" + tag)
    other, last = other_work(sent[4])
    arrived = _pair_wait(sent, last, "pair_wait_" + tag)
    cs = _matmul_tn_half(a, d, c.reshape(1), arrived, tm, tn, col_sharded, "mm_g_%s_own" % tag)
    cs = cs.reshape((1,) + cs.shape if col_sharded else (4, cs.shape[0] // 4, cs.shape[1]))
    slots = _own_piece_into_slots(cs, col_sharded, "own_piece_" + tag)
    return _scatter_start(cs, slots, col_sharded, sent[4], "scatter_start_" + tag), other


def _reduce_finish(started, after, col_sharded, w, m, v, tag):
    _, landed = _scatter_wait(started, after, col_sharded, "scatter_wait_" + tag)
    slots = _scatter_pass_on(landed, "scatter_pass_on_" + tag)
    return _adamw_big(w, m, v, slots, "adamw_" + tag)


def kernel(x, ln1_w, w_in, lb_gamma, hg_norm_w, lru_conv_w, lru_conv_b, lru_wa, lru_ba, lru_wx, lru_bx, lru_lambda, lru_norm_w, w_out, ln2_w, ffn_w_up, ffn_conv_w, ffn_conv_b, ffn_w_down, final_norm_w, loss_target, m_ln1_w, m_w_in, m_lb_gamma, m_hg_norm_w, m_lru_conv_w, m_lru_conv_b, m_lru_wa, m_lru_ba, m_lru_wx, m_lru_bx, m_lru_lambda, m_lru_norm_w, m_w_out, m_ln2_w, m_ffn_w_up, m_ffn_conv_w, m_ffn_conv_b, m_ffn_w_down, m_final_norm_w, v_ln1_w, v_w_in, v_lb_gamma, v_hg_norm_w, v_lru_conv_w, v_lru_conv_b, v_lru_wa, v_lru_ba, v_lru_wx, v_lru_bx, v_lru_lambda, v_lru_norm_w, v_w_out, v_ln2_w, v_ffn_w_up, v_ffn_conv_w, v_ffn_conv_b, v_ffn_w_down, v_final_norm_w):
    B, S, D = x.shape
    T = B * S
    HW = lb_gamma.shape[1]
    LW = lru_conv_b.shape[1]
    assert S % CHUNK == 0 and HW % HEAD_DIM == 0 and lru_wa.shape[2] == HEAD_DIM
    x2 = x.reshape(T, D)
    tgt = loss_target.reshape(T, D)
    jchip = 2 * lax.axis_index("x") + lax.axis_index("y")

    conv_shapes = [lru_conv_w[0].shape, ffn_conv_w[0].shape]
    convs = _gather_small(_pack([lru_conv_w[0], ffn_conv_w[0]]), "gather_conv_w")
    per_chip = [_unpack(convs[j], conv_shapes) for j in range(4)]
    lcw = jnp.concatenate([pc[0] for pc in per_chip], axis=1)
    fcw = jnp.concatenate([pc[1] for pc in per_chip], axis=1)
    masters = dict(w_in=w_in[0], w_out=w_out[0], w_up=ffn_w_up[0], w_down=ffn_w_down[0])
    col_of = dict(w_in=True, w_out=False, w_up=True, w_down=False)
    started, token, after = {}, jnp.zeros(TOKEN.shape, F32), convs
    for n in ("w_in", "w_out", "w_up", "w_down"):
        land, *alone = _cast_into_window(masters[n], col_of[n], after, n == "w_in", "cast_" + n)
        if n == "w_in":
            own_w_in = alone[0]
        started[n] = _gather_start(land, masters[n].shape, col_of[n], token, "gather_start_" + n)
        token = after = started[n][3]

    def landed(n, after):
        return _gather_wait(started[n], masters[n].shape, after, col_of[n], "gather_wait_" + n)

    def pass_on_start(n, after, tok):
        wins = _gather_windows(masters[n].shape, col_of[n])
        return _pass_start(landed(n, after), wins, 3, tok, "gather_pass_start_" + n)

    def pass_on_wait(n, sent, after):
        return _pass_wait(sent, _gather_windows(masters[n].shape, col_of[n]), after, "gather_pass_wait_" + n)

    hn1 = _rms_fwd(x2, ln1_w, "rms1")
    mx, my = lax.axis_index("x"), lax.axis_index("y")
    n_in = 4 * masters["w_in"].shape[1]
    as_j = lambda v: v.astype(jnp.int32).reshape(1)
    proj = _matmul_col_slice(hn1, own_w_in, as_j(2 * mx + my), n_in, token, 1024, 512, "mm_proj_own")
    W_in = _gather_pass_on(landed("w_in", proj), masters["w_in"].shape, True, "gather_pass_on_w_in")
    for tag, j in (("x", 2 * (1 - mx) + my), ("y", 2 * mx + 1 - my), ("xy", 2 * (1 - mx) + 1 - my)):
        proj = _matmul_col_slice(hn1, W_in, as_j(j), n_in, proj, 1024, 512, "mm_proj_" + tag)
    sent_out = pass_on_start("w_out", proj, token)
    o_raw, o_hg, states, scores = _hgrn_fwd(proj, lb_gamma, hg_norm_w, B, HW, "hgrn_fwd")
    h_lru, z = _lru_fwd(proj, lcw, lru_conv_b, lru_wa[0], lru_ba, lru_wx[0], lru_bx, lru_lambda, B, HW, LW, "lru_fwd")
    o_lru = _rms_fwd(z, lru_norm_w, "rms_lru")
    mix = jnp.concatenate([o_hg, o_lru], axis=1)
    sent_up = pass_on_start("w_up", mix, sent_out[3])
    W_out = pass_on_wait("w_out", sent_out, sent_up[3])
    h1 = _matmul(mix, W_out, "NN", F32, 1024, 512, 4096, add=x2, name="mm_out")
    hn2 = _rms_fwd(h1, ln2_w, "rms2")
    W_up = pass_on_wait("w_up", sent_up, hn2)
    up = _matmul(hn2, W_up, "NN", F32, 1024, 512, 4096, name="mm_up")
    act, act_dg, act_dv = _ffn_act(up, fcw, ffn_conv_b, B, "ffn_act")
    W_down = _gather_pass_on(landed("w_down", act), masters["w_down"].shape, False, "gather_pass_on_w_down")
    h2 = _matmul(act, W_down, "NN", F32, 1024, 512, 5504, add=h1, name="mm_down")
    token = sent_up[3]

    dh2, dh2a, d_final_w, loss_part = _loss_bwd(h2, tgt, final_norm_w.reshape(1, D), "loss_bwd")
    def through_w_down(tok):
        d = _matmul(dh2a, W_down, "NT", ACT_DTYPE, 512, 5504, 512, after=tok, name="mm_d_act")
        return d, d

    red_down, d_act = _cut_gradient(act, dh2a, False, 256, 1024, token, "w_down", through_w_down)
    d_up, d_fcw, d_fcb = _ffn_act_bwd(up, fcw, act_dg, act_dv, d_act, B, "ffn_act_bwd")
    def through_w_up(tok):
        d = _matmul(d_up, W_up, "NT", F32, 2048, 1024, 512, after=tok, name="mm_d_hn2")
        return d, d

    red_up, d_hn2 = _cut_gradient(hn2, d_up, True, 1024, 512, red_down[4], "w_up", through_w_up)
    dh1, dh1a, d_ln2 = _rms_bwd(h1, ln2_w, d_hn2, 0, dh2, True, "rms2_bwd", after=red_up[4])

    def through_w_out(tok):
        d = _matmul(dh1a, W_out, "NT", F32, 1024, 512, 4096, after=tok, name="mm_d_mix")
        return d, d

    red_out, d_mix = _cut_gradient(mix, dh1a, False, 1024, 512, red_up[4], "w_out", through_w_out)
    dz, d_lru_norm = _rms_bwd(z, lru_norm_w, d_mix, HW // LW, None, False, "rms_lru_bwd")
    (d_xr, d_yr, d_wa, d_wx, d_ba, d_bx, d_lam, d_lcw, d_lcb) = _lru_bwd(
        proj, lcw, lru_conv_b, lru_wa[0], lru_ba, lru_wx[0], lru_bx, lru_lambda, h_lru, dz, B, HW, LW, "lru_bwd")
    d_q, d_f, d_i, d_g, d_lbg, d_hgw = _hgrn_bwd(
        proj, lb_gamma, hg_norm_w, o_raw, states, scores, d_mix, B, HW, "hgrn_bwd")
    d_proj = jnp.concatenate([d_q, d_f, d_i, d_g, d_xr, d_yr], axis=1)
    small_names = ["ln1_w", "lb_gamma", "hg_norm_w", "lru_conv_w", "lru_conv_b", "lru_wa", "lru_ba", "lru_wx",
                   "lru_bx", "lru_lambda", "lru_norm_w", "ln2_w", "ffn_conv_w", "ffn_conv_b", "final_norm_w"]
    small_rest = [d_lbg, d_hgw, d_lcw, d_lcb, d_wa, d_ba, d_wx, d_bx, d_lam, d_lru_norm, d_ln2, d_fcw, d_fcb, d_final_w]

    def through_w_in(tok):
        d_hn1 = _matmul(d_proj, W_in, "NT", F32, 2048, 1024, 1024, after=tok, name="mm_d_hn1")
        dx, d_ln1 = _rms_bwd(x2, ln1_w, d_hn1, 0, dh1, False, "rms1_bwd")
        red = _allreduce_small(_pack([loss_part[0:1, 0:1], d_ln1] + small_rest), "allreduce_small")
        return (dx, d_ln1, red), red

    red_in, (dx, d_ln1, red) = _cut_gradient(hn1, d_proj, True, 1024, 512, red_out[4], "w_in", through_w_in)
    small_grads = [d_ln1] + small_rest

    def slots_on_their_way(red, col_sharded, after, tok, tag):
        _, got = _scatter_wait(red, after, col_sharded, "scatter_wait_" + tag)
        return _pass_start(got, _slot_windows, 4, tok, "scatter_pass_start_" + tag)

    def update(sent, after, w, m, v, tag):
        slots = _pass_wait(sent, _slot_windows, after, "scatter_pass_wait_" + tag)
        return _adamw_big(w, m, v, slots, "adamw_" + tag)

    sent_down = slots_on_their_way(red_down, False, red_in[4], red_in[4], "w_down")
    sent_up = slots_on_their_way(red_up, True, sent_down[3], sent_down[3], "w_up")
    sent_out = slots_on_their_way(red_out, False, sent_up[3], sent_up[3], "w_out")
    big = {}
    big["ffn_w_down"] = update(sent_down, sent_out[3], ffn_w_down[0], m_ffn_w_down[0], v_ffn_w_down[0], "w_down")
    big["ffn_w_up"] = update(sent_up, big["ffn_w_down"][1], ffn_w_up[0], m_ffn_w_up[0], v_ffn_w_up[0], "w_up")
    big["w_out"] = update(sent_out, big["ffn_w_up"][1], w_out[0], m_w_out[0], v_w_out[0], "w_out")
    big["w_in"] = _reduce_finish(red_in, big["w_out"][1], True, w_in[0], m_w_in[0], v_w_in[0], "w_in")

    red = _unpack(red, [(1, 1)] + [g.shape for g in small_grads])
    loss = red[0].reshape(())
    gs = dict(zip(small_names, red[1:]))
    nlc, nfc = lru_conv_w.shape[2], ffn_conv_w.shape[2]
    gs["lru_conv_w"] = lax.dynamic_slice_in_dim(gs["lru_conv_w"], jchip * nlc, nlc, axis=1)
    gs["ffn_conv_w"] = lax.dynamic_slice_in_dim(gs["ffn_conv_w"], jchip * nfc, nfc, axis=1)
    args = dict(ln1_w=(ln1_w, m_ln1_w, v_ln1_w), lb_gamma=(lb_gamma, m_lb_gamma, v_lb_gamma),
                hg_norm_w=(hg_norm_w, m_hg_norm_w, v_hg_norm_w), lru_conv_w=(lru_conv_w, m_lru_conv_w, v_lru_conv_w),
                lru_conv_b=(lru_conv_b, m_lru_conv_b, v_lru_conv_b), lru_wa=(lru_wa, m_lru_wa, v_lru_wa),
                lru_ba=(lru_ba, m_lru_ba, v_lru_ba), lru_wx=(lru_wx, m_lru_wx, v_lru_wx),
                lru_bx=(lru_bx, m_lru_bx, v_lru_bx), lru_lambda=(lru_lambda, m_lru_lambda, v_lru_lambda),
                lru_norm_w=(lru_norm_w, m_lru_norm_w, v_lru_norm_w), ln2_w=(ln2_w, m_ln2_w, v_ln2_w),
                ffn_conv_w=(ffn_conv_w, m_ffn_conv_w, v_ffn_conv_w), ffn_conv_b=(ffn_conv_b, m_ffn_conv_b, v_ffn_conv_b),
                final_norm_w=(final_norm_w, m_final_norm_w, v_final_norm_w))
    shapes = [args[n][0].shape for n in small_names]
    upd = _adamw_small(_pack([args[n][0] for n in small_names]), _pack([gs[n] for n in small_names]),
                       _pack([args[n][1] for n in small_names]), _pack([args[n][2] for n in small_names]), "adamw_small")
    s_delta, s_m, s_v = (dict(zip(small_names, _unpack(u, shapes))) for u in upd)

    order = ["ln1_w", "w_in", "lb_gamma", "hg_norm_w", "lru_conv_w", "lru_conv_b", "lru_wa", "lru_ba", "lru_wx",
             "lru_bx", "lru_lambda", "lru_norm_w", "w_out", "ln2_w", "ffn_w_up", "ffn_conv_w", "ffn_conv_b",
             "ffn_w_down", "final_norm_w"]
    full_shape = dict(w_in=w_in.shape, w_out=w_out.shape, ffn_w_up=ffn_w_up.shape, ffn_w_down=ffn_w_down.shape)
    grads, deltas, new_m, new_v = [], [], [], []
    for n in order:
        if n in big:
            g, d, mn, vn = (t.reshape(full_shape[n]) for t in big[n])
        else:
            g, d, mn, vn = gs[n].reshape(args[n][0].shape), s_delta[n], s_m[n], s_v[n]
        grads.append(g), deltas.append(d), new_m.append(mn), new_v.append(vn)
    return (loss, dx.reshape(B, S, D), *grads, *deltas, *new_m, *new_v)
```

```python
import functools
import math

import numpy as np
import jax
import jax.numpy as jnp
from jax import lax
from jax.experimental import pallas as pl
from jax.experimental.pallas import tpu as pltpu

F32 = jnp.float32
MXU_DTYPE = jnp.bfloat16
ACT_DTYPE = jnp.bfloat16

EPS = 1e-6
HEAD_DIM = 128
CHUNK = 64
LEVEL_HALVES = (32, 16, 8, 4, 2, 1)
LRU_CONV = 4
FFN_CONV = 3
LRU_C = 8.0
ADAM_LR, ADAM_B1, ADAM_B2, ADAM_EPS, ADAM_WD, ADAM_STEP = 0.001, 0.9, 0.999, 1e-08, 0.01, 10

V7X_LANES = 128
V7X_VMEM_BUDGET = 56 << 20

NN = (((1,), (0,)), ((), ()))
NT = (((1,), (1,)), ((), ()))
TN = (((0,), (0,)), ((), ()))
MESH = pl.DeviceIdType.MESH
HBM_SPEC = pl.BlockSpec(memory_space=pl.ANY)


def _pcall(kern, **kw):
    return pl.pallas_call(kern, **kw)


def _params(sem=None, vmem=None):
    kw = {}
    if sem is not None:
        kw["dimension_semantics"] = sem
    if vmem is not None:
        kw["vmem_limit_bytes"] = int(min(max(vmem, 16 << 20), V7X_VMEM_BUDGET))
    return pltpu.CompilerParams(**kw)


def _dot(a, b, dims=NN):
    return lax.dot_general(a.astype(MXU_DTYPE), b.astype(MXU_DTYPE), dims, preferred_element_type=F32)


def _tile(dim, pref, align):
    t = min(pref, dim) // align * align
    while t >= align:
        if dim % t == 0:
            return t
        t -= align
    return dim


def _sigmoid(x):
    return 1.0 / (1.0 + jnp.exp(-x))


def _silu_and_grad(x):
    s = _sigmoid(x)
    return x * s, s * (1.0 + x * (1.0 - s))


def _gelu_and_grad(x):
    k0, k1 = math.sqrt(2.0 / math.pi), 0.044715
    t = jnp.tanh(k0 * (x + k1 * x * x * x))
    g = 0.5 * x * (1.0 + t)
    dg = 0.5 * (1.0 + t) + 0.5 * x * (1.0 - t * t) * k0 * (1.0 + 3.0 * k1 * x * x)
    return g, dg


def _one_minus_exp(x):
    p = x * (1.0 + x * (0.5 + x * (1.0 / 6.0 + x * (1.0 / 24.0 + x * (1.0 / 120.0)))))
    return jnp.where(x > -0.05, -p, 1.0 - jnp.exp(x))


def _rows(n):
    return lax.broadcasted_iota(jnp.int32, (n, 1), 0)


def _matmul(a, b, mode, out_dtype, tm, tn, tk, add=None, after=None, n_outer=False, name="mm"):
    if mode == "TN":
        K, M = a.shape
    else:
        M, K = a.shape
    N = b.shape[0] if mode == "NT" else b.shape[1]
    tm, tn = _tile(M, tm, V7X_LANES), _tile(N, tn, V7X_LANES)
    tk = _tile(K, tk, V7X_LANES)
    nk = K // tk
    dims = {"NN": NN, "NT": NT, "TN": TN}[mode]
    order = (lambda f: (lambda j, i, k: f(i, j, k))) if n_outer else (lambda f: f)
    a_spec = (pl.BlockSpec((tk, tm), order(lambda i, j, k: (k, i))) if mode == "TN"
              else pl.BlockSpec((tm, tk), order(lambda i, j, k: (i, k))))
    b_spec = (pl.BlockSpec((tn, tk), order(lambda i, j, k: (j, k))) if mode == "NT"
              else pl.BlockSpec((tk, tn), order(lambda i, j, k: (k, j))))
    o_spec = pl.BlockSpec((tm, tn), order(lambda i, j, k: (i, j)))
    has_add = add is not None

    def kern(*refs):
        a_ref, b_ref = refs[:2]
        add_ref = refs[2] if has_add else None

        def finish(r, o_ref):
            if has_add:
                r = r + add_ref[...]
            o_ref[...] = r.astype(out_dtype)

        if nk == 1:
            finish(_dot(a_ref[...], b_ref[...], dims), refs[-1])
            return
        o_ref, acc_ref = refs[-2:]
        k = pl.program_id(2)

        @pl.when(k == 0)
        def _():
            acc_ref[...] = jnp.zeros_like(acc_ref)

        acc_ref[...] += _dot(a_ref[...], b_ref[...], dims)

        @pl.when(k == nk - 1)
        def _():
            finish(acc_ref[...], o_ref)

    ab = jnp.dtype(a.dtype).itemsize
    ob = jnp.dtype(out_dtype).itemsize
    vmem = 2 * (tm * tk + tk * tn) * ab + tm * tn * (8 + 2 * ob + (8 if has_add else 0)) + (4 << 20)
    ins = [a, b] + ([add] if has_add else []) + ([after] if after is not None else [])
    in_specs = [a_spec, b_spec] + ([o_spec] if has_add else []) + ([HBM_SPEC] if after is not None else [])
    grid = (N // tn, M // tm, nk) if n_outer else (M // tm, N // tn, nk)
    return _pcall(
        kern, name=name, grid=grid,
        in_specs=in_specs, out_specs=o_spec,
        out_shape=jax.ShapeDtypeStruct((M, N), out_dtype),
        scratch_shapes=[pltpu.VMEM((tm, tn), F32)] if nk > 1 else [],
        compiler_params=_params(("parallel", "parallel", "arbitrary"), vmem),
    )(*ins)


def _matmul_tn_half(a, b, half, add, tm, tn, by_rows, name):
    K, M = a.shape
    N = b.shape[1]
    Mo, No = (M // 2, N) if by_rows else (M, N // 2)
    tm, tn = _tile(Mo, tm, V7X_LANES), _tile(No, tn, V7X_LANES)
    nm, nn = Mo // tm, No // tn
    has_add = add is not None

    def kern(h_ref, a_ref, b_ref, *rest):
        r = _dot(a_ref[...], b_ref[...], TN)
        if has_add:
            r = r + rest[0][...].astype(F32)
        rest[-1][...] = r.astype(ACT_DTYPE)

    if by_rows:
        a_map, b_map = (lambda i, j, h: (0, h[0] * nm + i)), (lambda i, j, h: (0, j))
        o_map, grid = (lambda i, j, h: (i, j)), (nm, nn)
    else:
        a_map, b_map = (lambda j, i, h: (0, i)), (lambda j, i, h: (0, h[0] * nn + j))
        o_map, grid = (lambda j, i, h: (i, j)), (nn, nm)
    blk = pl.BlockSpec((tm, tn), o_map)
    grid_spec = pltpu.PrefetchScalarGridSpec(
        num_scalar_prefetch=1, grid=grid,
        in_specs=[pl.BlockSpec((K, tm), a_map), pl.BlockSpec((K, tn), b_map)] + ([blk] if has_add else []),
        out_specs=blk)
    ab = jnp.dtype(a.dtype).itemsize
    vmem = 2 * K * (tm + tn) * ab + tm * tn * 16 + (4 << 20)
    return _pcall(kern, name=name, grid_spec=grid_spec, out_shape=jax.ShapeDtypeStruct((Mo, No), ACT_DTYPE),
                  compiler_params=_params(("parallel", "parallel"), vmem))(half, a, b, *([add] if has_add else []))


def _matmul_col_slice(a, b, j, n_total, prev, tm, tn, name):
    M, K = a.shape
    N, width = n_total, n_total // 4
    alone = b.shape[1] == width
    tm, tn = _tile(M, tm, V7X_LANES), _tile(width, tn, V7X_LANES)
    nn = width // tn
    in_place = prev.shape == (M, N)

    def kern(j_ref, a_ref, b_ref, prev_ref, o_ref):
        o_ref[...] = _dot(a_ref[...], b_ref[...])

    b_map = (lambda i, n, j: (0, n)) if alone else (lambda i, n, j: (0, j[0] * nn + n))
    grid_spec = pltpu.PrefetchScalarGridSpec(
        num_scalar_prefetch=1, grid=(M // tm, nn),
        in_specs=[pl.BlockSpec((tm, K), lambda i, n, j: (i, 0)), pl.BlockSpec((K, tn), b_map), HBM_SPEC],
        out_specs=pl.BlockSpec((tm, tn), lambda i, n, j: (i, j[0] * nn + n)))
    ab = jnp.dtype(a.dtype).itemsize
    vmem = 2 * K * (tm + tn) * ab + tm * tn * 16 + (4 << 20)
    return _pcall(kern, name=name, grid_spec=grid_spec, out_shape=jax.ShapeDtypeStruct((M, N), F32),
                  input_output_aliases={3: 0} if in_place else {},
                  compiler_params=_params(("parallel", "parallel"), vmem))(j, a, b, prev)


def _rms_fwd(x, w, name):
    T, D = x.shape
    tm = _tile(T, 256, 16)

    def kern(x_ref, w_ref, o_ref):
        xv = x_ref[...]
        r = lax.rsqrt(jnp.mean(xv * xv, axis=-1, keepdims=True) + EPS)
        o_ref[...] = (xv * r * w_ref[...]).astype(ACT_DTYPE)

    return _pcall(kern, name=name, grid=(T // tm,),
                  in_specs=[pl.BlockSpec((tm, D), lambda i: (i, 0)), pl.BlockSpec((1, D), lambda i: (0, 0))],
                  out_specs=pl.BlockSpec((tm, D), lambda i: (i, 0)),
                  out_shape=jax.ShapeDtypeStruct((T, D), ACT_DTYPE),
                  compiler_params=_params(("parallel",), 8 * tm * D * 4))(x, w)


def _rms_bwd(x, w, g, g_col, res, want_act, name, after=None):
    T, D = x.shape
    tm = _tile(T, 256, 16)
    has_res = res is not None

    def kern(*refs):
        refs = list(refs)
        x_ref, w_ref, g_ref = refs[:3]
        res_ref = refs[3] if has_res else None
        outs = refs[3 + has_res + (after is not None):]
        dx_ref = outs[0]
        dxa_ref = outs[1] if want_act else None
        dw_ref = outs[-1]
        i = pl.program_id(0)
        xv = x_ref[...]
        gv = g_ref[...].astype(F32)
        r = lax.rsqrt(jnp.mean(xv * xv, axis=-1, keepdims=True) + EPS)
        gw = gv * w_ref[...]
        dx = r * gw - xv * (r * r * r) * jnp.mean(gw * xv, axis=-1, keepdims=True)
        if has_res:
            dx = dx + res_ref[...]
        dx_ref[...] = dx
        if want_act:
            dxa_ref[...] = dx.astype(ACT_DTYPE)

        @pl.when(i == 0)
        def _():
            dw_ref[...] = jnp.zeros_like(dw_ref)

        dw_ref[...] += jnp.sum(gv * xv * r, axis=0, keepdims=True)

    row = pl.BlockSpec((tm, D), lambda i: (i, 0))
    vec = pl.BlockSpec((1, D), lambda i: (0, 0))
    in_specs = ([row, vec, pl.BlockSpec((tm, D), lambda i: (i, g_col))] + ([row] if has_res else [])
                + ([HBM_SPEC] if after is not None else []))
    out_specs = [row] + ([row] if want_act else []) + [vec]
    out_shape = ([jax.ShapeDtypeStruct((T, D), F32)]
                 + ([jax.ShapeDtypeStruct((T, D), ACT_DTYPE)] if want_act else [])
                 + [jax.ShapeDtypeStruct((1, D), F32)])
    ins = [x, w, g] + ([res] if has_res else []) + ([after] if after is not None else [])
    return _pcall(kern, name=name, grid=(T // tm,), in_specs=in_specs, out_specs=out_specs,
                  out_shape=out_shape, compiler_params=_params(("arbitrary",), 14 * tm * D * 4))(*ins)


def _loss_bwd(h, target, w, name):
    T, D = h.shape
    tm = _tile(T, 256, 16)

    def kern(h_ref, t_ref, w_ref, dh_ref, dha_ref, dw_ref, loss_ref):
        i = pl.program_id(0)
        hv = h_ref[...]
        r = lax.rsqrt(jnp.mean(hv * hv, axis=-1, keepdims=True) + EPS)
        e = hv * r * w_ref[...] - t_ref[...]
        dy = e * (1.0 / D)
        gw = dy * w_ref[...]
        dh = r * gw - hv * (r * r * r) * jnp.mean(gw * hv, axis=-1, keepdims=True)
        dh_ref[...] = dh
        dha_ref[...] = dh.astype(ACT_DTYPE)

        @pl.when(i == 0)
        def _():
            dw_ref[...] = jnp.zeros_like(dw_ref)
            loss_ref[...] = jnp.zeros_like(loss_ref)

        dw_ref[...] += jnp.sum(dy * hv * r, axis=0, keepdims=True)
        part = 0.5 * jnp.sum(jnp.mean(e * e, axis=-1, keepdims=True), axis=0, keepdims=True)
        loss_ref[...] += jnp.broadcast_to(part, loss_ref.shape)

    row = pl.BlockSpec((tm, D), lambda i: (i, 0))
    vec = pl.BlockSpec((1, D), lambda i: (0, 0))
    return _pcall(kern, name=name, grid=(T // tm,), in_specs=[row, row, vec],
                  out_specs=[row, row, vec, pl.BlockSpec((8, V7X_LANES), lambda i: (0, 0))],
                  out_shape=[jax.ShapeDtypeStruct((T, D), F32), jax.ShapeDtypeStruct((T, D), ACT_DTYPE),
                             jax.ShapeDtypeStruct((1, D), F32), jax.ShapeDtypeStruct((8, V7X_LANES), F32)],
                  compiler_params=_params(("arbitrary",), 14 * tm * D * 4))(h, target, w)


def _conv(x, w_ref, b, width):
    S = x.shape[0]
    row = _rows(S)
    y = b + x * w_ref[pl.ds(width - 1, 1), :]
    for j in range(width - 1):
        sh = width - 1 - j
        y = y + jnp.where(row >= sh, pltpu.roll(x, sh, 0), 0.0) * w_ref[pl.ds(j, 1), :]
    return y


def _conv_bwd(x, dy, w_ref, width):
    S = dy.shape[0]
    row = _rows(S)
    dx = dy * w_ref[pl.ds(width - 1, 1), :]
    dw = [None] * (width - 1) + [jnp.sum(x * dy, axis=0, keepdims=True)]
    for j in range(width - 1):
        sh = width - 1 - j
        dys = jnp.where(row < S - sh, pltpu.roll(dy, S - sh, 0), 0.0)
        dx = dx + dys * w_ref[pl.ds(j, 1), :]
        dw[j] = jnp.sum(x * dys, axis=0, keepdims=True)
    return dx, dw


def _ffn_act(up, cw, cb, B, name):
    T, F2 = up.shape
    S, F = T // B, F2 // 2
    tw = _tile(F, 256, V7X_LANES)
    nt = F // tw

    def kern(g_ref, v_ref, wg_ref, wv_ref, bg_ref, bv_ref, o_ref, mg_ref, mv_ref):
        gc = _conv(g_ref[...], wg_ref, bg_ref[...], FFN_CONV)
        vc = _conv(v_ref[...], wv_ref, bv_ref[...], FFN_CONV)
        silu, dsilu = _silu_and_grad(gc)
        o_ref[...] = (silu * vc).astype(ACT_DTYPE)
        mg_ref[...] = (vc * dsilu).astype(ACT_DTYPE)
        mv_ref[...] = silu.astype(ACT_DTYPE)

    blk = lambda off: pl.BlockSpec((S, tw), lambda b, i: (b, off + i))
    wblk = lambda off: pl.BlockSpec((FFN_CONV, tw), lambda b, i: (0, off + i))
    bblk = lambda off: pl.BlockSpec((1, tw), lambda b, i: (0, off + i))
    half = jax.ShapeDtypeStruct((T, F), ACT_DTYPE)
    return _pcall(kern, name=name, grid=(B, nt),
                  in_specs=[blk(0), blk(nt), wblk(0), wblk(nt), bblk(0), bblk(nt)],
                  out_specs=[blk(0), blk(0), blk(0)], out_shape=[half, half, half],
                  compiler_params=_params(("parallel", "parallel"), 20 * S * tw * 4))(up, up, cw, cw, cb, cb)


def _ffn_act_bwd(up, cw, mg, mv, d_act, B, name):
    T, F2 = up.shape
    S, F = T // B, F2 // 2
    tw = _tile(F, 256, V7X_LANES)
    nt = F // tw

    def kern(s_ref, ws_ref, mg_ref, mv_ref, da_ref, du_ref, dcw_ref, dcb_ref):
        t, b = pl.program_id(0), pl.program_id(1)
        mult = jnp.where(t < nt, mg_ref[...], mv_ref[...])
        d = da_ref[...].astype(F32) * mult.astype(F32)
        dx, dw = _conv_bwd(s_ref[...], d, ws_ref, FFN_CONV)
        du_ref[...] = dx.astype(ACT_DTYPE)

        @pl.when(b == 0)
        def _():
            dcw_ref[...] = jnp.zeros_like(dcw_ref)
            dcb_ref[...] = jnp.zeros_like(dcb_ref)

        for j, rj in enumerate(dw):
            dcw_ref[pl.ds(j, 1), :] += rj
        dcb_ref[...] += jnp.sum(d, axis=0, keepdims=True)

    own = lambda t, b: (b, t % nt)
    return _pcall(
        kern, name=name, grid=(2 * nt, B),
        in_specs=[pl.BlockSpec((S, tw), lambda t, b: (b, t)),
                  pl.BlockSpec((FFN_CONV, tw), lambda t, b: (0, t)),
                  pl.BlockSpec((S, tw), own), pl.BlockSpec((S, tw), own), pl.BlockSpec((S, tw), own)],
        out_specs=[pl.BlockSpec((S, tw), lambda t, b: (b, t)),
                   pl.BlockSpec((FFN_CONV, tw), lambda t, b: (0, t)),
                   pl.BlockSpec((1, tw), lambda t, b: (0, t))],
        out_shape=[jax.ShapeDtypeStruct((T, F2), ACT_DTYPE), jax.ShapeDtypeStruct((FFN_CONV, F2), F32),
                   jax.ShapeDtypeStruct((1, F2), F32)],
        compiler_params=_params(("parallel", "arbitrary"), 20 * S * tw * 4),
    )(up, cw, mg, mv, d_act)


def _hgrn_tables():
    C = CHUNK
    t = np.arange(C)
    mats = [(t[:, None] >= t[None, :]).astype(np.float32)]
    masks = []
    gsum = [(t[:, None] <= t[None, :]).astype(np.float32), (t[:, None] > t[None, :]).astype(np.float32)]
    for hs in LEVEL_HALVES:
        m = (t // (2 * hs)) * 2 * hs + hs
        later = t >= m
        d = np.zeros((C, C), np.float32)
        for i in range(C):
            if later[i]:
                d[i, m[i]:i + 1] = 1.0
            else:
                d[i, i + 1:m[i]] = -1.0
        mats.append(d)
        same = (t[:, None] // (2 * hs)) == (t[None, :] // (2 * hs))
        masks.append((same & later[:, None] & (~later)[None, :]).astype(np.float32))
        gsum.append((same & later[:, None] & (t[None, :] >= t[:, None])).astype(np.float32))
        gsum.append((same & (~later)[:, None] & (t[None, :] < t[:, None])).astype(np.float32))
    return np.concatenate(mats, 0), np.stack(masks, 0), np.concatenate(gsum, 1)


def _split_dot(mat, v):
    hi = v.astype(MXU_DTYPE)
    lo = (v - hi.astype(F32)).astype(MXU_DTYPE)
    r = _dot(mat, jnp.concatenate([hi, lo], axis=1))
    n = v.shape[1]
    return r[:, :n] + r[:, n:]


def _hgrn_gates(qr, fr, lb, mc):
    C = CHUNK
    q, dq_dqr = _silu_and_grad(qr)
    sf = _sigmoid(fr)
    f = lb + (1.0 - lb) * sf
    k = 1.0 - f
    dall = _split_dot(mc, jnp.log(f))
    b = dall[0:C]
    dl = [dall[C * (l + 1):C * (l + 2)] for l in range(len(LEVEL_HALVES))]
    e = [jnp.exp(-jnp.abs(d)) for d in dl]
    return q, dq_dqr, sf, f, k, b, e, e


def _hgrn_scores(q, k, eq, ek, masks_ref):
    p = jnp.where(_rows(CHUNK) == lax.broadcasted_iota(jnp.int32, (1, CHUNK), 1),
                  jnp.sum(q * k, axis=-1, keepdims=True), 0.0)
    for l in range(len(LEVEL_HALVES)):
        p = p + masks_ref[l] * _dot(q * eq[l], k * ek[l], NT)
    return p


def _hgrn_fwd(proj, lb_gamma, norm_w, B, HW, name):
    T = proj.shape[0]
    S, H, C = T // B, HW // HEAD_DIM, CHUNK
    NC = S // C
    mc_np, masks_np, _ = _hgrn_tables()
    mc, masks = jnp.asarray(mc_np, MXU_DTYPE), jnp.asarray(masks_np, F32)

    def kern(q_ref, f_ref, i_ref, g_ref, lbg_ref, nw_ref, mc_ref, masks_ref, oraw_ref, o_ref, st_ref, p_ref):
        g0, g1 = lbg_ref[pl.ds(0, 1), :], lbg_ref[pl.ds(1, 1), :]
        mx = jnp.maximum(g0, g1)
        e0, e1 = jnp.exp(g0 - mx), jnp.exp(g1 - mx)
        lb = e0 / (e0 + e1)
        nw = nw_ref[...]
        mcv = mc_ref[...]

        def body(n, sts):
            out = []
            for s, st in enumerate(sts):
                rows = pl.ds(pl.multiple_of(s * S + n * C, C), C)
                st_ref[s, n] = st
                q, _, _, _, k, b, eq, ek = _hgrn_gates(q_ref[rows, :], f_ref[rows, :], lb, mcv)
                v = i_ref[rows, :]
                p = _hgrn_scores(q, k, eq, ek, masks_ref)
                p_ref[s, n] = p
                o = _dot(q * jnp.exp(b), st, NT) + _dot(p, v)
                b_last = b[C - 1:C]
                out.append(st * jnp.exp(b_last) + _dot(v, k * jnp.exp(b_last - b), TN))
                oraw_ref[rows, :] = o
                r = lax.rsqrt(jnp.mean(o * o, axis=-1, keepdims=True) + EPS)
                gate, _ = _silu_and_grad(g_ref[rows, :])
                o_ref[rows, :] = (o * r * nw * gate).astype(ACT_DTYPE)
            return tuple(out)

        lax.fori_loop(0, NC, body, tuple(jnp.zeros((HEAD_DIM, HEAD_DIM), F32) for _ in range(B)))

    col = lambda off: pl.BlockSpec((T, HEAD_DIM), lambda h: (0, off + h))
    return _pcall(
        kern, name=name, grid=(H,),
        in_specs=[col(0), col(H), col(2 * H), col(3 * H),
                  pl.BlockSpec((2, HEAD_DIM), lambda h: (0, h)),
                  pl.BlockSpec((1, HEAD_DIM), lambda h: (0, h)),
                  pl.BlockSpec(mc.shape, lambda h: (0, 0)),
                  pl.BlockSpec(masks.shape, lambda h: (0, 0, 0))],
        out_specs=[col(0), col(0),
                   pl.BlockSpec((B, None, NC, HEAD_DIM, HEAD_DIM), lambda h: (0, h, 0, 0, 0)),
                   pl.BlockSpec((B, None, NC, C, C), lambda h: (0, h, 0, 0, 0))],
        out_shape=[jax.ShapeDtypeStruct((T, HW), F32), jax.ShapeDtypeStruct((T, HW), ACT_DTYPE),
                   jax.ShapeDtypeStruct((B, H, NC, HEAD_DIM, HEAD_DIM), F32),
                   jax.ShapeDtypeStruct((B, H, NC, C, C), F32)],
        compiler_params=_params(("parallel",), 20 * T * HEAD_DIM * 4 + (8 << 20)),
    )(proj, proj, proj, proj, lb_gamma, norm_w, mc, masks)


def _hgrn_bwd(proj, lb_gamma, norm_w, o_raw, states, scores, d_mix, B, HW, name):
    T = proj.shape[0]
    S, H, C = T // B, HW // HEAD_DIM, CHUNK
    NC = S // C
    mc_np, masks_np, gsum_np = _hgrn_tables()
    mc, masks, gsum = jnp.asarray(mc_np, MXU_DTYPE), jnp.asarray(masks_np, F32), jnp.asarray(gsum_np, MXU_DTYPE)
    nl = len(LEVEL_HALVES)

    def kern(q_ref, f_ref, i_ref, g_ref, lbg_ref, nw_ref, mc_ref, masks_ref, gsum_ref, oraw_ref, st_ref, p_ref, do_ref,
             dq_ref, df_ref, di_ref, dg_ref, dlbg_ref, dnw_ref):
        g0, g1 = lbg_ref[pl.ds(0, 1), :], lbg_ref[pl.ds(1, 1), :]
        mx = jnp.maximum(g0, g1)
        e0, e1 = jnp.exp(g0 - mx), jnp.exp(g1 - mx)
        lb = e0 / (e0 + e1)
        nw = nw_ref[...]
        mcv, gsumv = mc_ref[...], gsum_ref[...]

        def chunk(s, n, dst, dlb, dnw):
            rows = pl.ds(pl.multiple_of(s * S + n * C, C), C)
            qr, fr, v = q_ref[rows, :], f_ref[rows, :], i_ref[rows, :]
            q, dq_dqr, sf, f, k, b, eq, ek = _hgrn_gates(qr, fr, lb, mcv)
            o = oraw_ref[rows, :]
            dout = do_ref[rows, :].astype(F32)
            gate, dgate = _silu_and_grad(g_ref[rows, :])
            r = lax.rsqrt(jnp.mean(o * o, axis=-1, keepdims=True) + EPS)
            dg_ref[rows, :] = (dout * o * r * nw * dgate).astype(ACT_DTYPE)
            don = dout * gate
            dnw = dnw + jnp.sum(don * o * r, axis=0, keepdims=True)
            gw = don * nw
            do = r * gw - o * (r * r * r) * jnp.mean(gw * o, axis=-1, keepdims=True)
            st_prev = st_ref[s, n]
            eb = jnp.exp(b)
            b_last = b[C - 1:C]
            ebl = jnp.exp(b_last - b)
            p = p_ref[s, n]
            dp = _dot(do, v, NT)
            dpd = jnp.sum(do * v, axis=-1, keepdims=True)
            dq_state = _dot(do, st_prev) * eb
            dk_state = _dot(v, dst) * ebl
            dq = dq_state + dpd * k
            dk = dk_state + dpd * q
            pairs = [q * dq_state, k * dk_state]
            for l in range(nl):
                mdp = masks_ref[l] * dp
                dql = _dot(mdp, k * ek[l]) * eq[l]
                dkl = _dot(mdp, q * eq[l], TN) * ek[l]
                dq, dk = dq + dql, dk + dkl
                pairs += [q * dql, k * dkl]
            dv = _dot(p, do, TN) + _dot(k * ebl, dst, NT)
            through = jnp.exp(b_last) * jnp.sum(dst * st_prev, axis=0, keepdims=True)
            dlg = _split_dot(gsumv, jnp.concatenate(pairs, axis=0)) + through
            dst = dst * jnp.exp(b_last) + _dot(do, q * eb, TN)
            dq_ref[rows, :] = (dq * dq_dqr).astype(ACT_DTYPE)
            dfv = dlg / f - dk
            df_ref[rows, :] = (dfv * (1.0 - lb) * sf * (1.0 - sf)).astype(ACT_DTYPE)
            di_ref[rows, :] = dv.astype(ACT_DTYPE)
            dlb = dlb + jnp.sum(dfv * (1.0 - sf), axis=0, keepdims=True)
            return dst, dlb, dnw

        def body(it, carry):
            dsts, dlb, dnw = carry
            out = []
            for s, dst in enumerate(dsts):
                dst, dlb, dnw = chunk(s, NC - 1 - it, dst, dlb, dnw)
                out.append(dst)
            return tuple(out), dlb, dnw

        zrow = jnp.zeros((1, HEAD_DIM), F32)
        zst = tuple(jnp.zeros((HEAD_DIM, HEAD_DIM), F32) for _ in range(B))
        _, dlb, dnw = lax.fori_loop(0, NC, body, (zst, zrow, zrow))
        dg0 = dlb * lb * (1.0 - lb)
        dlbg_ref[pl.ds(0, 1), :] = dg0
        dlbg_ref[pl.ds(1, 1), :] = -dg0
        dnw_ref[...] = dnw

    col = lambda off: pl.BlockSpec((T, HEAD_DIM), lambda h: (0, off + h))
    full = lambda a: pl.BlockSpec(a.shape, lambda h: (0,) * a.ndim)
    part = jax.ShapeDtypeStruct((T, HW), ACT_DTYPE)
    return _pcall(
        kern, name=name, grid=(H,),
        in_specs=[col(0), col(H), col(2 * H), col(3 * H),
                  pl.BlockSpec((2, HEAD_DIM), lambda h: (0, h)),
                  pl.BlockSpec((1, HEAD_DIM), lambda h: (0, h)),
                  full(mc), full(masks), full(gsum), col(0),
                  pl.BlockSpec((B, None, NC, HEAD_DIM, HEAD_DIM), lambda h: (0, h, 0, 0, 0)),
                  pl.BlockSpec((B, None, NC, C, C), lambda h: (0, h, 0, 0, 0)),
                  col(0)],
        out_specs=[col(0), col(0), col(0), col(0),
                   pl.BlockSpec((2, HEAD_DIM), lambda h: (0, h)),
                   pl.BlockSpec((1, HEAD_DIM), lambda h: (0, h))],
        out_shape=[part, part, part, part, jax.ShapeDtypeStruct((2, HW), F32), jax.ShapeDtypeStruct((1, HW), F32)],
        compiler_params=_params(("parallel",), 28 * T * HEAD_DIM * 4 + (8 << 20)),
    )(proj, proj, proj, proj, lb_gamma, norm_w, mc, masks, gsum, o_raw, states, scores, d_mix)


def _lru_gates(xr, cw_ref, cb, wa, ba, wx, bx, lam):
    S = xr.shape[0]
    xb = _conv(xr, cw_ref, cb, LRU_CONV)
    r = _sigmoid(_dot(xb, wa) + ba)
    ig = _sigmoid(_dot(xb, wx) + bx)
    sp = jnp.maximum(-lam, 0.0) + jnp.log(1.0 + jnp.exp(-jnp.abs(lam)))
    la = -LRU_C * r * sp
    a = jnp.exp(la)
    mult = jnp.where(_rows(S) == 0, 1.0, jnp.sqrt(_one_minus_exp(2.0 * la)))
    return xb, r, ig, sp, a, mult


def _scan_rows(a_ref, u_ref, h_ref, reverse):
    S, W = a_ref.shape
    nb = S // 8
    row = _rows(8)

    def body(it, carry):
        blk = nb - 1 - it if reverse else it
        rows = pl.ds(pl.multiple_of(blk * 8, 8), 8)
        a, u = a_ref[rows, :], u_ref[rows, :]
        for d in (1, 2, 4):
            sh = 8 - d if reverse else d
            keep = (row < 8 - d) if reverse else (row >= d)
            u = u + jnp.where(keep, a * pltpu.roll(u, sh, 0), 0.0)
            a = jnp.where(keep, a * pltpu.roll(a, sh, 0), a)
        h = u + a * carry
        h_ref[rows, :] = h
        return h[0:1] if reverse else h[7:8]

    lax.fori_loop(0, nb, body, jnp.zeros((1, W), F32))


def _lru_fwd(proj, cw, cb, wa, ba, wx, bx, lam, B, HW, LW, name):
    T = proj.shape[0]
    S, NB = T // B, LW // HEAD_DIM
    xoff, yoff = 4 * HW // HEAD_DIM, 4 * HW // HEAD_DIM + NB

    def kern(x_ref, y_ref, cw_ref, cb_ref, wa_ref, ba_ref, wx_ref, bx_ref, lam_ref, h_ref, z_ref, a_s, u_s):
        xb, _, ig, _, a, mult = _lru_gates(x_ref[...], cw_ref, cb_ref[...], wa_ref[...], ba_ref[...],
                                           wx_ref[...], bx_ref[...], lam_ref[...])
        a_s[...] = a
        u_s[...] = xb * ig * mult
        _scan_rows(a_s, u_s, h_ref, False)
        gy, _ = _gelu_and_grad(y_ref[...])
        z_ref[...] = h_ref[...] * gy

    blk = lambda off: pl.BlockSpec((S, HEAD_DIM), lambda b, n: (b, off + n))
    vec = pl.BlockSpec((1, HEAD_DIM), lambda b, n: (0, n))
    mat = pl.BlockSpec((None, HEAD_DIM, HEAD_DIM), lambda b, n: (n, 0, 0))
    return _pcall(
        kern, name=name, grid=(B, NB),
        in_specs=[blk(xoff), blk(yoff), pl.BlockSpec((LRU_CONV, HEAD_DIM), lambda b, n: (0, n)),
                  vec, mat, vec, mat, vec, vec],
        out_specs=[blk(0), blk(0)],
        out_shape=[jax.ShapeDtypeStruct((T, LW), F32), jax.ShapeDtypeStruct((T, LW), F32)],
        scratch_shapes=[pltpu.VMEM((S, HEAD_DIM), F32), pltpu.VMEM((S, HEAD_DIM), F32)],
        compiler_params=_params(("parallel", "parallel"), 24 * S * HEAD_DIM * 4),
    )(proj, proj, cw, cb, wa, ba, wx, bx, lam)


def _lru_bwd(proj, cw, cb, wa, ba, wx, bx, lam, h, dz, B, HW, LW, name):
    T = proj.shape[0]
    S, NB = T // B, LW // HEAD_DIM
    xoff, yoff = 4 * HW // HEAD_DIM, 4 * HW // HEAD_DIM + NB

    def kern(x_ref, y_ref, cw_ref, cb_ref, wa_ref, ba_ref, wx_ref, bx_ref, lam_ref, h_ref, dz_ref,
             dx_ref, dy_ref, dwa_ref, dwx_ref, dba_ref, dbx_ref, dlam_ref, dcw_ref, dcb_ref, a_s, u_s, dh_s):
        bi = pl.program_id(1)
        row = _rows(S)
        xr, lam = x_ref[...], lam_ref[...]
        wa, wx = wa_ref[...], wx_ref[...]
        xb, r, ig, sp, a, mult = _lru_gates(xr, cw_ref, cb_ref[...], wa, ba_ref[...], wx, bx_ref[...], lam)
        hv, dz = h_ref[...], dz_ref[...]
        gy, dgy = _gelu_and_grad(y_ref[...])
        dy_ref[...] = (dz * hv * dgy).astype(ACT_DTYPE)
        a_s[...] = jnp.where(row < S - 1, pltpu.roll(a, S - 1, 0), 0.0)
        u_s[...] = dz * gy
        _scan_rows(a_s, u_s, dh_s, True)
        dh = dh_s[...]
        h_prev = jnp.where(row >= 1, pltpu.roll(hv, 1, 0), 0.0)
        d_ig = dh * xb * mult
        d_mult = jnp.where(row == 0, 0.0, dh * xb * ig)
        dxb = dh * ig * mult
        dla = dh * h_prev * a - d_mult * (a * a) / mult
        dpre_r = dla * (-LRU_C * sp) * r * (1.0 - r)
        dpre_i = d_ig * ig * (1.0 - ig)
        dxb = dxb + _dot(dpre_r, wa, NT) + _dot(dpre_i, wx, NT)
        dxr, dcw = _conv_bwd(xr, dxb, cw_ref, LRU_CONV)
        dx_ref[...] = dxr.astype(ACT_DTYPE)

        @pl.when(bi == 0)
        def _():
            for ref in (dwa_ref, dwx_ref, dba_ref, dbx_ref, dlam_ref, dcw_ref, dcb_ref):
                ref[...] = jnp.zeros_like(ref)

        dwa_ref[...] += _dot(xb, dpre_r, TN)
        dwx_ref[...] += _dot(xb, dpre_i, TN)
        dba_ref[...] += jnp.sum(dpre_r, axis=0, keepdims=True)
        dbx_ref[...] += jnp.sum(dpre_i, axis=0, keepdims=True)
        dsp = jnp.sum(dla * (-LRU_C) * r, axis=0, keepdims=True)
        dlam_ref[...] += -dsp * _sigmoid(-lam)
        for j, rj in enumerate(dcw):
            dcw_ref[pl.ds(j, 1), :] += rj
        dcb_ref[...] += jnp.sum(dxb, axis=0, keepdims=True)

    blk = lambda off: pl.BlockSpec((S, HEAD_DIM), lambda n, b: (b, off + n))
    vec = pl.BlockSpec((1, HEAD_DIM), lambda n, b: (0, n))
    mat = pl.BlockSpec((None, HEAD_DIM, HEAD_DIM), lambda n, b: (n, 0, 0))
    cwb = pl.BlockSpec((LRU_CONV, HEAD_DIM), lambda n, b: (0, n))
    part = jax.ShapeDtypeStruct((T, LW), ACT_DTYPE)
    vshape = jax.ShapeDtypeStruct((1, LW), F32)
    mshape = jax.ShapeDtypeStruct((NB, HEAD_DIM, HEAD_DIM), F32)
    return _pcall(
        kern, name=name, grid=(NB, B),
        in_specs=[blk(xoff), blk(yoff), cwb, vec, mat, vec, mat, vec, vec, blk(0), blk(0)],
        out_specs=[blk(0), blk(0), mat, mat, vec, vec, vec, cwb, vec],
        out_shape=[part, part, mshape, mshape, vshape, vshape, vshape,
                   jax.ShapeDtypeStruct((LRU_CONV, LW), F32), vshape],
        scratch_shapes=[pltpu.VMEM((S, HEAD_DIM), F32)] * 3,
        compiler_params=_params(("parallel", "arbitrary"), 40 * S * HEAD_DIM * 4),
    )(proj, proj, cw, cb, wa, ba, wx, bx, lam, h, dz)


def _pos():
    return lax.axis_index("x"), lax.axis_index("y"), lax.axis_index("c")


def _other_chips(x, y):
    return [(1 - x, y), (x, 1 - y), (1 - x, 1 - y)]


def _remote(src, dst, send_sems, recv_sems, k, to):
    return pltpu.make_async_remote_copy(src_ref=src, dst_ref=dst, send_sem=send_sems.at[k],
                                        recv_sem=recv_sems.at[k], device_id=to, device_id_type=MESH)


HBM_BLK = pl.BlockSpec(memory_space=pltpu.HBM)
SEM_BLK = pl.BlockSpec(memory_space=pltpu.SEMAPHORE)
VMEM_BLK = pl.BlockSpec(memory_space=pltpu.VMEM)
DATAFLOW = pltpu.SideEffectType.DATAFLOW_SIDE_EFFECTING
TOKEN = jax.ShapeDtypeStruct((8, V7X_LANES), F32)


def _in_hbm(a):
    return pltpu.with_memory_space_constraint(a, pltpu.HBM)


def _gather_win(o_ref, R, C, col_sharded):
    Rh = R // 2

    def win(j, h=None):
        if col_sharded:
            rows = pl.ds(0, R) if h is None else pl.ds(h * Rh, Rh)
            return o_ref.at[rows, pl.ds(j * C, C)]
        return o_ref.at[pl.ds(j * R, R) if h is None else pl.ds(j * R + h * Rh, Rh), :]

    return win


def _cast_into_window(w, col_sharded, after, also_alone, name):
    R, C = w.shape
    tr = _tile(R, 256, 16)
    nr = R // tr
    full = (R, 4 * C) if col_sharded else (4 * R, C)
    j = (2 * lax.axis_index("x") + lax.axis_index("y")).astype(jnp.int32).reshape(1)

    def kern(j_ref, w_ref, after_ref, *o_refs):
        for o_ref in o_refs:
            o_ref[...] = w_ref[...].astype(ACT_DTYPE)

    out_map = (lambda i, jr: (i, jr[0])) if col_sharded else (lambda i, jr: (jr[0] * nr + i, 0))
    row = pl.BlockSpec((tr, C), lambda i, jr: (i, 0))
    grid_spec = pltpu.PrefetchScalarGridSpec(
        num_scalar_prefetch=1, grid=(nr,),
        in_specs=[row, HBM_SPEC], out_specs=[pl.BlockSpec((tr, C), out_map)] + ([row] if also_alone else []))
    out_shape = [jax.ShapeDtypeStruct(full, ACT_DTYPE)] + ([jax.ShapeDtypeStruct((R, C), ACT_DTYPE)] if also_alone else [])
    return _pcall(kern, name=name, grid_spec=grid_spec, out_shape=out_shape,
                  compiler_params=_params(("parallel",), 8 * tr * C * 4))(j, w, after)


def _gather_start(land, shard_shape, col_sharded, token, name):
    R, C = shard_shape

    def body(land_ref, tok_ref, send_sems, recv_sems, land_thru, tok_out):
        x, y, c = _pos()
        w = _gather_win(land_ref, R, C, col_sharded)(2 * x + y, c)
        for k, (cx, cy) in enumerate(_other_chips(x, y)):
            _remote(w, w, send_sems, recv_sems, k, (cx, cy, c)).start()
        tok_out[...] = tok_ref[...]

    return _pcall(
        body, name=name,
        out_shape=(pltpu.SemaphoreType.DMA((3,)), pltpu.SemaphoreType.DMA((3,)),
                   pltpu.HBM(land.shape, land.dtype), TOKEN),
        in_specs=(HBM_BLK, VMEM_BLK), out_specs=(SEM_BLK, SEM_BLK, HBM_BLK, VMEM_BLK),
        input_output_aliases={0: 2},
        compiler_params=pltpu.CompilerParams(has_side_effects=DATAFLOW),
    )(_in_hbm(land), token)


def _gather_wait(started, shard_shape, after, col_sharded, name):
    send_sems, recv_sems, land_thru, _ = started
    R, C = shard_shape

    def body(land_ref, send_sems, recv_sems, after_ref, got_ref):
        x, y, c = _pos()
        win = _gather_win(land_ref, R, C, col_sharded)
        for k, (cx, cy) in enumerate(_other_chips(x, y)):
            cp = _remote(win(2 * x + y, c), win(2 * cx + cy, c), send_sems, recv_sems, k, (cx, cy, c))
            cp.wait_send()
            cp.wait_recv()

    return _pcall(
        body, name=name, out_shape=pltpu.HBM(land_thru.shape, land_thru.dtype),
        in_specs=(HBM_BLK, SEM_BLK, SEM_BLK, HBM_SPEC), out_specs=HBM_BLK, input_output_aliases={0: 0},
        compiler_params=pltpu.CompilerParams(has_side_effects=DATAFLOW),
    )(land_thru, send_sems, recv_sems, after)


def _gather_pass_on(landed, shard_shape, col_sharded, name):
    R, C = shard_shape

    def body(in_ref, o_ref, send_sems, recv_sems):
        x, y, c = _pos()
        src, dst = _gather_win(in_ref, R, C, col_sharded), _gather_win(o_ref, R, C, col_sharded)
        chips = [(k, 2 * cx + cy) for k, (cx, cy) in enumerate(_other_chips(x, y))]
        passed = [_remote(src(j, c), dst(j, c), send_sems, recv_sems, k, (x, y, 1 - c)) for k, j in chips]
        for cp in passed:
            cp.start()
        for k, j in chips:
            w = dst(j, 1 - c)
            _remote(w, w, send_sems, recv_sems, k, (x, y, c)).wait_recv()
        for cp in passed:
            cp.wait_send()

    return _pcall(body, name=name, in_specs=[HBM_SPEC], out_specs=HBM_SPEC,
                  out_shape=jax.ShapeDtypeStruct(landed.shape, landed.dtype), input_output_aliases={0: 0},
                  scratch_shapes=[pltpu.SemaphoreType.DMA((3,)), pltpu.SemaphoreType.DMA((3,))])(landed)


def _pair_start(g, token, name):
    def body(g_ref, land_ref, tok_ref, send_sems, recv_sems, g_thru, land_thru, tok_out):
        x, y, c = _pos()
        _remote(g_ref, land_ref, send_sems, recv_sems, 0, (x, y, 1 - c)).start()
        tok_out[...] = tok_ref[...]

    return _pcall(
        body, name=name,
        out_shape=(pltpu.SemaphoreType.DMA((1,)), pltpu.SemaphoreType.DMA((1,)),
                   pltpu.HBM(g.shape, g.dtype), pltpu.HBM(g.shape, g.dtype), TOKEN),
        in_specs=(HBM_BLK, HBM_BLK, VMEM_BLK), out_specs=(SEM_BLK, SEM_BLK, HBM_BLK, HBM_BLK, VMEM_BLK),
        input_output_aliases={0: 2, 1: 3},
        compiler_params=pltpu.CompilerParams(has_side_effects=DATAFLOW),
    )(_in_hbm(g), _in_hbm(lax.empty(g.shape, g.dtype)), token)


def _pair_wait(started, after, name):
    send_sems, recv_sems, g_thru, land_thru, _ = started

    def body(g_ref, land_ref, send_sems, recv_sems, after_ref, g_dead, got_ref):
        x, y, c = _pos()
        cp = _remote(g_ref, land_ref, send_sems, recv_sems, 0, (x, y, 1 - c))
        cp.wait_send()
        cp.wait_recv()

    return _pcall(
        body, name=name,
        out_shape=(pltpu.HBM(g_thru.shape, g_thru.dtype), pltpu.HBM(land_thru.shape, land_thru.dtype)),
        in_specs=(HBM_BLK, HBM_BLK, SEM_BLK, SEM_BLK, HBM_SPEC), out_specs=(HBM_BLK, HBM_BLK),
        input_output_aliases={0: 0, 1: 1},
        compiler_params=pltpu.CompilerParams(has_side_effects=DATAFLOW),
    )(g_thru, land_thru, send_sems, recv_sems, after)[1]


def _own_piece_into_slots(cs, col_sharded, name):
    J, Rp, W = cs.shape
    Cp = W // 4 if col_sharded else W
    tr = _tile(Rp, 256, 16)
    tw = _tile(Cp, 8192, V7X_LANES)
    nw = Cp // tw
    x, y, c = _pos()
    chip = (2 * x + y).astype(jnp.int32).reshape(1)
    core = c.astype(jnp.int32).reshape(1)

    def kern(j_ref, c_ref, s_ref, o_ref):
        o_ref[...] = s_ref[...]

    in_map = ((lambda i, w, j, cc: (0, i, j[0] * nw + w)) if col_sharded
              else (lambda i, w, j, cc: (j[0], i, w)))
    grid_spec = pltpu.PrefetchScalarGridSpec(
        num_scalar_prefetch=2, grid=(Rp // tr, nw),
        in_specs=[pl.BlockSpec((None, tr, tw), in_map)],
        out_specs=pl.BlockSpec((None, None, tr, tw), lambda i, w, j, cc: (j[0], cc[0], i, w)))
    return _pcall(kern, name=name, grid_spec=grid_spec,
                  out_shape=jax.ShapeDtypeStruct((4, 2, Rp, Cp), cs.dtype),
                  compiler_params=_params(("parallel", "parallel"), 8 * tr * tw * 4))(chip, core, cs)


def _chip_sum_piece(cs_ref, C, col_sharded):
    return lambda j: cs_ref.at[0, :, pl.ds(j * C, C)] if col_sharded else cs_ref.at[j]


def _scatter_start(cs, slots, col_sharded, token, name):
    C = slots.shape[3]

    def body(cs_ref, land_ref, tok_ref, send_sems, recv_sems, cs_thru, land_thru, tok_out):
        x, y, c = _pos()
        piece = _chip_sum_piece(cs_ref, C, col_sharded)
        for k, (cx, cy) in enumerate(_other_chips(x, y)):
            _remote(piece(2 * cx + cy), land_ref.at[2 * x + y, c], send_sems, recv_sems, k, (cx, cy, c)).start()
        tok_out[...] = tok_ref[...]

    return _pcall(
        body, name=name,
        out_shape=(pltpu.SemaphoreType.DMA((3,)), pltpu.SemaphoreType.DMA((3,)),
                   pltpu.HBM(cs.shape, cs.dtype), pltpu.HBM(slots.shape, cs.dtype), TOKEN),
        in_specs=(HBM_BLK, HBM_BLK, VMEM_BLK), out_specs=(SEM_BLK, SEM_BLK, HBM_BLK, HBM_BLK, VMEM_BLK),
        input_output_aliases={0: 2, 1: 3},
        compiler_params=pltpu.CompilerParams(has_side_effects=DATAFLOW),
    )(_in_hbm(cs), _in_hbm(slots), token)


def _scatter_wait(started, after, col_sharded, name):
    send_sems, recv_sems, cs_thru, land_thru, _ = started
    C = land_thru.shape[3]

    def body(cs_ref, land_ref, send_sems, recv_sems, after_ref, cs_dead, got_ref):
        x, y, c = _pos()
        piece = _chip_sum_piece(cs_ref, C, col_sharded)
        for k, (cx, cy) in enumerate(_other_chips(x, y)):
            cp = _remote(piece(2 * cx + cy), land_ref.at[2 * cx + cy, c], send_sems, recv_sems, k, (cx, cy, c))
            cp.wait_send()
            cp.wait_recv()

    return _pcall(
        body, name=name,
        out_shape=(pltpu.HBM(cs_thru.shape, cs_thru.dtype), pltpu.HBM(land_thru.shape, land_thru.dtype)),
        in_specs=(HBM_BLK, HBM_BLK, SEM_BLK, SEM_BLK, HBM_SPEC), out_specs=(HBM_BLK, HBM_BLK),
        input_output_aliases={0: 0, 1: 1},
        compiler_params=pltpu.CompilerParams(has_side_effects=DATAFLOW),
    )(cs_thru, land_thru, send_sems, recv_sems, after)


def _scatter_pass_on(landed, name):
    def body(in_ref, o_ref, send_sems, recv_sems):
        x, y, c = _pos()
        sends = [_remote(in_ref.at[i, c], o_ref.at[i, c], send_sems, recv_sems, i, (x, y, 1 - c)) for i in range(4)]
        for cp in sends:
            cp.start()
        for i in range(4):
            w = o_ref.at[i, 1 - c]
            _remote(w, w, send_sems, recv_sems, i, (x, y, c)).wait_recv()
        for cp in sends:
            cp.wait_send()

    return _pcall(body, name=name, in_specs=[HBM_SPEC], out_specs=HBM_SPEC,
                  out_shape=jax.ShapeDtypeStruct(landed.shape, landed.dtype), input_output_aliases={0: 0},
                  scratch_shapes=[pltpu.SemaphoreType.DMA((4,)), pltpu.SemaphoreType.DMA((4,))])(landed)


def _gather_windows(shard_shape, col_sharded):
    R, C = shard_shape

    def windows(ref, x, y, cc):
        win = _gather_win(ref, R, C, col_sharded)
        return [win(2 * cx + cy, cc) for cx, cy in _other_chips(x, y)]

    return windows


def _slot_windows(ref, x, y, cc):
    return [ref.at[i, cc] for i in range(4)]


def _pass_start(landed, windows, n, token, name):
    def body(land_ref, tok_ref, send_sems, recv_sems, land_thru, tok_out):
        x, y, c = _pos()
        for k, w in enumerate(windows(land_ref, x, y, c)):
            _remote(w, w, send_sems, recv_sems, k, (x, y, 1 - c)).start()
        tok_out[...] = tok_ref[...]

    return _pcall(
        body, name=name,
        out_shape=(pltpu.SemaphoreType.DMA((n,)), pltpu.SemaphoreType.DMA((n,)),
                   pltpu.HBM(landed.shape, landed.dtype), TOKEN),
        in_specs=(HBM_BLK, VMEM_BLK), out_specs=(SEM_BLK, SEM_BLK, HBM_BLK, VMEM_BLK),
        input_output_aliases={0: 2},
        compiler_params=pltpu.CompilerParams(has_side_effects=DATAFLOW),
    )(_in_hbm(landed), token)


def _pass_wait(started, windows, after, name):
    send_sems, recv_sems, land_thru, _ = started

    def body(land_ref, send_sems, recv_sems, after_ref, got_ref):
        x, y, c = _pos()
        mine, theirs = windows(land_ref, x, y, c), windows(land_ref, x, y, 1 - c)
        for k, (src, dst) in enumerate(zip(mine, theirs)):
            cp = _remote(src, dst, send_sems, recv_sems, k, (x, y, 1 - c))
            cp.wait_send()
            cp.wait_recv()

    return _pcall(
        body, name=name, out_shape=pltpu.HBM(land_thru.shape, land_thru.dtype),
        in_specs=(HBM_BLK, SEM_BLK, SEM_BLK, HBM_SPEC), out_specs=HBM_BLK, input_output_aliases={0: 0},
        compiler_params=pltpu.CompilerParams(has_side_effects=DATAFLOW),
    )(land_thru, send_sems, recv_sems, after)


def _gather_small(buf, name):
    rows = buf.shape[0]

    def body(b_ref, o_ref, send_sems, recv_sems):
        x, y, c = _pos()
        jme = 2 * x + y
        chips = _other_chips(x, y)
        o_ref[jme] = b_ref[...]
        sends = [_remote(b_ref, o_ref.at[jme], send_sems, recv_sems, k, (cx, cy, c))
                 for k, (cx, cy) in enumerate(chips)]
        for cp in sends:
            cp.start()
        for k, (cx, cy) in enumerate(chips):
            w = o_ref.at[2 * cx + cy]
            _remote(w, w, send_sems, recv_sems, k, (x, y, c)).wait_recv()
        for cp in sends:
            cp.wait_send()

    vm = pl.BlockSpec(memory_space=pltpu.VMEM)
    return _pcall(body, name=name, in_specs=[vm], out_specs=vm,
                  out_shape=jax.ShapeDtypeStruct((4, rows, V7X_LANES), buf.dtype),
                  scratch_shapes=[pltpu.SemaphoreType.DMA((3,)), pltpu.SemaphoreType.DMA((3,))],
                  compiler_params=_params(None, 16 * rows * V7X_LANES * 4))(buf)


def _allreduce_small(buf, name):
    rows = buf.shape[0]
    rh = rows // 2

    def body(b_ref, o_ref, pair, mine, slots, send_sems, recv_sems):
        x, y, c = _pos()
        me, sib, jme = (x, y, c), (x, y, 1 - c), 2 * x + y
        half = pl.ds(pl.multiple_of(c * rh, 8), rh)
        other = pl.ds(pl.multiple_of((1 - c) * rh, 8), rh)
        to_sib = _remote(b_ref, pair.at[c], send_sems, recv_sems, 0, sib)
        to_sib.start()
        pair[c] = b_ref[...]
        _remote(pair.at[1 - c], pair.at[1 - c], send_sems, recv_sems, 0, me).wait_recv()
        mine[...] = pair[0, half, :] + pair[1, half, :]
        chips = _other_chips(x, y)
        sends = [_remote(mine, slots.at[jme], send_sems, recv_sems, 1 + k, (cx, cy, c))
                 for k, (cx, cy) in enumerate(chips)]
        for cp in sends:
            cp.start()
        slots[jme] = mine[...]
        for k, (cx, cy) in enumerate(chips):
            w = slots.at[2 * cx + cy]
            _remote(w, w, send_sems, recv_sems, 1 + k, me).wait_recv()
        o_ref[half, :] = (slots[0] + slots[1]) + (slots[2] + slots[3])
        back = _remote(o_ref.at[half, :], o_ref.at[half, :], send_sems, recv_sems, 4, sib)
        back.start()
        _remote(o_ref.at[other, :], o_ref.at[other, :], send_sems, recv_sems, 4, me).wait_recv()
        for cp in [to_sib, back] + sends:
            cp.wait_send()

    vm = pl.BlockSpec(memory_space=pltpu.VMEM)
    return _pcall(body, name=name, in_specs=[vm], out_specs=vm,
                  out_shape=jax.ShapeDtypeStruct(buf.shape, buf.dtype),
                  scratch_shapes=[pltpu.VMEM((2, rows, V7X_LANES), buf.dtype),
                                  pltpu.VMEM((rh, V7X_LANES), buf.dtype),
                                  pltpu.VMEM((4, rh, V7X_LANES), buf.dtype),
                                  pltpu.SemaphoreType.DMA((5,)), pltpu.SemaphoreType.DMA((5,))],
                  compiler_params=_params(None, 10 * rows * V7X_LANES * 4))(buf)


def _adamw_math(w, g, m, v):
    m = ADAM_B1 * m + (1.0 - ADAM_B1) * g
    v = ADAM_B2 * v + (1.0 - ADAM_B2) * (g * g)
    m_hat = m / (1.0 - ADAM_B1 ** ADAM_STEP)
    v_hat = v / (1.0 - ADAM_B2 ** ADAM_STEP)
    delta = -ADAM_LR * (m_hat / (jnp.sqrt(v_hat) + ADAM_EPS) + ADAM_WD * w)
    return delta, m, v


def _adamw_big(w, m, v, slots, name):
    R, C = w.shape
    by_rows = slots.shape[3] == C
    tr = _tile(R, 64, 16)

    def kern(w_ref, m_ref, v_ref, s_ref, g_ref, d_ref, mo_ref, vo_ref):
        def chip_sum(h):
            g = s_ref[0, h].astype(F32)
            for i in range(1, 4):
                g = g + s_ref[i, h].astype(F32)
            return g

        g = chip_sum(0) if by_rows else jnp.concatenate([chip_sum(0), chip_sum(1)], axis=1)
        d, mn, vn = _adamw_math(w_ref[...], g, m_ref[...], v_ref[...])
        g_ref[...], d_ref[...], mo_ref[...], vo_ref[...] = g, d, mn, vn

    row = pl.BlockSpec((tr, C), lambda i: (i, 0))
    shp = jax.ShapeDtypeStruct((R, C), F32)
    if by_rows:
        per_half = R // 2 // tr
        s_spec = pl.BlockSpec((4, 1, tr, C), lambda i: (0, i // per_half, i % per_half, 0))
    else:
        s_spec = pl.BlockSpec((4, 2, tr, C // 2), lambda i: (0, 0, i, 0))
    return _pcall(kern, name=name, grid=(R // tr,),
                  in_specs=[row, row, row, s_spec], out_specs=[row] * 4, out_shape=[shp] * 4,
                  compiler_params=_params(("parallel",), 30 * tr * C * 4))(w, m, v, slots)


def _adamw_small(w, g, m, v, name):
    def kern(w_ref, g_ref, m_ref, v_ref, d_ref, mo_ref, vo_ref):
        d_ref[...], mo_ref[...], vo_ref[...] = _adamw_math(w_ref[...], g_ref[...], m_ref[...], v_ref[...])

    vm = pl.BlockSpec(memory_space=pltpu.VMEM)
    shp = jax.ShapeDtypeStruct(w.shape, F32)
    return _pcall(kern, name=name, in_specs=[vm] * 4, out_specs=[vm] * 3, out_shape=[shp] * 3,
                  compiler_params=_params(None, 10 * w.size * 4))(w, g, m, v)


def _pack(arrs):
    flat = jnp.concatenate([a.reshape(-1).astype(F32) for a in arrs])
    n = flat.shape[0]
    rows = -(-n // (16 * V7X_LANES)) * 16
    return jnp.pad(flat, (0, rows * V7X_LANES - n)).reshape(rows, V7X_LANES)


def _unpack(buf, shapes):
    flat = buf.reshape(-1)
    out, off = [], 0
    for s in shapes:
        n = int(np.prod(s))
        out.append(flat[off:off + n].reshape(s))
        off += n
    return out


def _cut_gradient(a, d, col_sharded, tm, tn, token, tag, other_work):
    c = lax.axis_index("c").astype(jnp.int32)
    for_sibling = _matmul_tn_half(a, d, (1 - c).reshape(1), None, tm, tn, col_sharded, "mm_g_%s_sibling" % tag)
    sent = _pair_start(for_sibling, token, "pair_start_" + tag)
    other, last = other_work(sent[4])
    arrived = _pair_wait(sent, last, "pair_wait_" + tag)
    cs = _matmul_tn_half(a, d, c.reshape(1), arrived, tm, tn, col_sharded, "mm_g_%s_own" % tag)
    cs = cs.reshape((1,) + cs.shape if col_sharded else (4, cs.shape[0] // 4, cs.shape[1]))
    slots = _own_piece_into_slots(cs, col_sharded, "own_piece_" + tag)
    return _scatter_start(cs, slots, col_sharded, sent[4], "scatter_start_" + tag), other


def _reduce_finish(started, after, col_sharded, w, m, v, tag):
    _, landed = _scatter_wait(started, after, col_sharded, "scatter_wait_" + tag)
    slots = _scatter_pass_on(landed, "scatter_pass_on_" + tag)
    return _adamw_big(w, m, v, slots, "adamw_" + tag)


def kernel(x, ln1_w, w_in, lb_gamma, hg_norm_w, lru_conv_w, lru_conv_b, lru_wa, lru_ba, lru_wx, lru_bx, lru_lambda, lru_norm_w, w_out, ln2_w, ffn_w_up, ffn_conv_w, ffn_conv_b, ffn_w_down, final_norm_w, loss_target, m_ln1_w, m_w_in, m_lb_gamma, m_hg_norm_w, m_lru_conv_w, m_lru_conv_b, m_lru_wa, m_lru_ba, m_lru_wx, m_lru_bx, m_lru_lambda, m_lru_norm_w, m_w_out, m_ln2_w, m_ffn_w_up, m_ffn_conv_w, m_ffn_conv_b, m_ffn_w_down, m_final_norm_w, v_ln1_w, v_w_in, v_lb_gamma, v_hg_norm_w, v_lru_conv_w, v_lru_conv_b, v_lru_wa, v_lru_ba, v_lru_wx, v_lru_bx, v_lru_lambda, v_lru_norm_w, v_w_out, v_ln2_w, v_ffn_w_up, v_ffn_conv_w, v_ffn_conv_b, v_ffn_w_down, v_final_norm_w):
    B, S, D = x.shape
    T = B * S
    HW = lb_gamma.shape[1]
    LW = lru_conv_b.shape[1]
    assert S % CHUNK == 0 and HW % HEAD_DIM == 0 and lru_wa.shape[2] == HEAD_DIM
    x2 = x.reshape(T, D)
    tgt = loss_target.reshape(T, D)
    jchip = 2 * lax.axis_index("x") + lax.axis_index("y")

    conv_shapes = [lru_conv_w[0].shape, ffn_conv_w[0].shape]
    convs = _gather_small(_pack([lru_conv_w[0], ffn_conv_w[0]]), "gather_conv_w")
    per_chip = [_unpack(convs[j], conv_shapes) for j in range(4)]
    lcw = jnp.concatenate([pc[0] for pc in per_chip], axis=1)
    fcw = jnp.concatenate([pc[1] for pc in per_chip], axis=1)
    masters = dict(w_in=w_in[0], w_out=w_out[0], w_up=ffn_w_up[0], w_down=ffn_w_down[0])
    col_of = dict(w_in=True, w_out=False, w_up=True, w_down=False)
    started, token, after = {}, jnp.zeros(TOKEN.shape, F32), convs
    for n in ("w_in", "w_out", "w_up", "w_down"):
        land, *alone = _cast_into_window(masters[n], col_of[n], after, n == "w_in", "cast_" + n)
        if n == "w_in":
            own_w_in = alone[0]
        started[n] = _gather_start(land, masters[n].shape, col_of[n], token, "gather_start_" + n)
        token = after = started[n][3]

    def landed(n, after):
        return _gather_wait(started[n], masters[n].shape, after, col_of[n], "gather_wait_" + n)

    def pass_on_start(n, after, tok):
        wins = _gather_windows(masters[n].shape, col_of[n])
        return _pass_start(landed(n, after), wins, 3, tok, "gather_pass_start_" + n)

    def pass_on_wait(n, sent, after):
        return _pass_wait(sent, _gather_windows(masters[n].shape, col_of[n]), after, "gather_pass_wait_" + n)

    hn1 = _rms_fwd(x2, ln1_w, "rms1")
    mx, my = lax.axis_index("x"), lax.axis_index("y")
    n_in = 4 * masters["w_in"].shape[1]
    as_j = lambda v: v.astype(jnp.int32).reshape(1)
    proj = _matmul_col_slice(hn1, own_w_in, as_j(2 * mx + my), n_in, token, 1024, 512, "mm_proj_own")
    W_in = _gather_pass_on(landed("w_in", proj), masters["w_in"].shape, True, "gather_pass_on_w_in")
    for tag, j in (("x", 2 * (1 - mx) + my), ("y", 2 * mx + 1 - my), ("xy", 2 * (1 - mx) + 1 - my)):
        proj = _matmul_col_slice(hn1, W_in, as_j(j), n_in, proj, 1024, 512, "mm_proj_" + tag)
    sent_out = pass_on_start("w_out", proj, token)
    o_raw, o_hg, states, scores = _hgrn_fwd(proj, lb_gamma, hg_norm_w, B, HW, "hgrn_fwd")
    h_lru, z = _lru_fwd(proj, lcw, lru_conv_b, lru_wa[0], lru_ba, lru_wx[0], lru_bx, lru_lambda, B, HW, LW, "lru_fwd")
    o_lru = _rms_fwd(z, lru_norm_w, "rms_lru")
    mix = jnp.concatenate([o_hg, o_lru], axis=1)
    sent_up = pass_on_start("w_up", mix, sent_out[3])
    W_out = pass_on_wait("w_out", sent_out, sent_up[3])
    h1 = _matmul(mix, W_out, "NN", F32, 1024, 512, 4096, add=x2, name="mm_out")
    hn2 = _rms_fwd(h1, ln2_w, "rms2")
    W_up = pass_on_wait("w_up", sent_up, hn2)
    up = _matmul(hn2, W_up, "NN", F32, 1024, 512, 4096, name="mm_up")
    act, act_dg, act_dv = _ffn_act(up, fcw, ffn_conv_b, B, "ffn_act")
    W_down = _gather_pass_on(landed("w_down", act), masters["w_down"].shape, False, "gather_pass_on_w_down")
    h2 = _matmul(act, W_down, "NN", F32, 1024, 512, 5504, add=h1, name="mm_down")
    token = sent_up[3]

    dh2, dh2a, d_final_w, loss_part = _loss_bwd(h2, tgt, final_norm_w.reshape(1, D), "loss_bwd")
    def through_w_down(tok):
        d = _matmul(dh2a, W_down, "NT", ACT_DTYPE, 512, 5504, 512, after=tok, name="mm_d_act")
        return d, d

    red_down, d_act = _cut_gradient(act, dh2a, False, 256, 1024, token, "w_down", through_w_down)
    d_up, d_fcw, d_fcb = _ffn_act_bwd(up, fcw, act_dg, act_dv, d_act, B, "ffn_act_bwd")
    def through_w_up(tok):
        d = _matmul(d_up, W_up, "NT", F32, 2048, 1024, 512, after=tok, name="mm_d_hn2")
        return d, d

    red_up, d_hn2 = _cut_gradient(hn2, d_up, True, 1024, 512, red_down[4], "w_up", through_w_up)
    dh1, dh1a, d_ln2 = _rms_bwd(h1, ln2_w, d_hn2, 0, dh2, True, "rms2_bwd", after=red_up[4])

    def through_w_out(tok):
        d = _matmul(dh1a, W_out, "NT", F32, 1024, 512, 4096, after=tok, name="mm_d_mix")
        return d, d

    red_out, d_mix = _cut_gradient(mix, dh1a, False, 1024, 512, red_up[4], "w_out", through_w_out)
    dz, d_lru_norm = _rms_bwd(z, lru_norm_w, d_mix, HW // LW, None, False, "rms_lru_bwd")
    (d_xr, d_yr, d_wa, d_wx, d_ba, d_bx, d_lam, d_lcw, d_lcb) = _lru_bwd(
        proj, lcw, lru_conv_b, lru_wa[0], lru_ba, lru_wx[0], lru_bx, lru_lambda, h_lru, dz, B, HW, LW, "lru_bwd")
    d_q, d_f, d_i, d_g, d_lbg, d_hgw = _hgrn_bwd(
        proj, lb_gamma, hg_norm_w, o_raw, states, scores, d_mix, B, HW, "hgrn_bwd")
    d_proj = jnp.concatenate([d_q, d_f, d_i, d_g, d_xr, d_yr], axis=1)
    small_names = ["ln1_w", "lb_gamma", "hg_norm_w", "lru_conv_w", "lru_conv_b", "lru_wa", "lru_ba", "lru_wx",
                   "lru_bx", "lru_lambda", "lru_norm_w", "ln2_w", "ffn_conv_w", "ffn_conv_b", "final_norm_w"]
    small_rest = [d_lbg, d_hgw, d_lcw, d_lcb, d_wa, d_ba, d_wx, d_bx, d_lam, d_lru_norm, d_ln2, d_fcw, d_fcb, d_final_w]

    def through_w_in(tok):
        d_hn1 = _matmul(d_proj, W_in, "NT", F32, 2048, 1024, 1024, after=tok, name="mm_d_hn1")
        dx, d_ln1 = _rms_bwd(x2, ln1_w, d_hn1, 0, dh1, False, "rms1_bwd")
        red = _allreduce_small(_pack([loss_part[0:1, 0:1], d_ln1] + small_rest), "allreduce_small")
        return (dx, d_ln1, red), red

    red_in, (dx, d_ln1, red) = _cut_gradient(hn1, d_proj, True, 1024, 512, red_out[4], "w_in", through_w_in)
    small_grads = [d_ln1] + small_rest

    def slots_on_their_way(red, col_sharded, after, tok, tag):
        _, got = _scatter_wait(red, after, col_sharded, "scatter_wait_" + tag)
        return _pass_start(got, _slot_windows, 4, tok, "scatter_pass_start_" + tag)

    def update(sent, after, w, m, v, tag):
        slots = _pass_wait(sent, _slot_windows, after, "scatter_pass_wait_" + tag)
        return _adamw_big(w, m, v, slots, "adamw_" + tag)

    sent_down = slots_on_their_way(red_down, False, red_in[4], red_in[4], "w_down")
    sent_up = slots_on_their_way(red_up, True, sent_down[3], sent_down[3], "w_up")
    sent_out = slots_on_their_way(red_out, False, sent_up[3], sent_up[3], "w_out")
    big = {}
    big["ffn_w_down"] = update(sent_down, sent_out[3], ffn_w_down[0], m_ffn_w_down[0], v_ffn_w_down[0], "w_down")
    big["ffn_w_up"] = update(sent_up, big["ffn_w_down"][1], ffn_w_up[0], m_ffn_w_up[0], v_ffn_w_up[0], "w_up")
    big["w_out"] = update(sent_out, big["ffn_w_up"][1], w_out[0], m_w_out[0], v_w_out[0], "w_out")
    big["w_in"] = _reduce_finish(red_in, big["w_out"][1], True, w_in[0], m_w_in[0], v_w_in[0], "w_in")

    red = _unpack(red, [(1, 1)] + [g.shape for g in small_grads])
    loss = red[0].reshape(())
    gs = dict(zip(small_names, red[1:]))
    nlc, nfc = lru_conv_w.shape[2], ffn_conv_w.shape[2]
    gs["lru_conv_w"] = lax.dynamic_slice_in_dim(gs["lru_conv_w"], jchip * nlc, nlc, axis=1)
    gs["ffn_conv_w"] = lax.dynamic_slice_in_dim(gs["ffn_conv_w"], jchip * nfc, nfc, axis=1)
    args = dict(ln1_w=(ln1_w, m_ln1_w, v_ln1_w), lb_gamma=(lb_gamma, m_lb_gamma, v_lb_gamma),
                hg_norm_w=(hg_norm_w, m_hg_norm_w, v_hg_norm_w), lru_conv_w=(lru_conv_w, m_lru_conv_w, v_lru_conv_w),
                lru_conv_b=(lru_conv_b, m_lru_conv_b, v_lru_conv_b), lru_wa=(lru_wa, m_lru_wa, v_lru_wa),
                lru_ba=(lru_ba, m_lru_ba, v_lru_ba), lru_wx=(lru_wx, m_lru_wx, v_lru_wx),
                lru_bx=(lru_bx, m_lru_bx, v_lru_bx), lru_lambda=(lru_lambda, m_lru_lambda, v_lru_lambda),
                lru_norm_w=(lru_norm_w, m_lru_norm_w, v_lru_norm_w), ln2_w=(ln2_w, m_ln2_w, v_ln2_w),
                ffn_conv_w=(ffn_conv_w, m_ffn_conv_w, v_ffn_conv_w), ffn_conv_b=(ffn_conv_b, m_ffn_conv_b, v_ffn_conv_b),
                final_norm_w=(final_norm_w, m_final_norm_w, v_final_norm_w))
    shapes = [args[n][0].shape for n in small_names]
    upd = _adamw_small(_pack([args[n][0] for n in small_names]), _pack([gs[n] for n in small_names]),
                       _pack([args[n][1] for n in small_names]), _pack([args[n][2] for n in small_names]), "adamw_small")
    s_delta, s_m, s_v = (dict(zip(small_names, _unpack(u, shapes))) for u in upd)

    order = ["ln1_w", "w_in", "lb_gamma", "hg_norm_w", "lru_conv_w", "lru_conv_b", "lru_wa", "lru_ba", "lru_wx",
             "lru_bx", "lru_lambda", "lru_norm_w", "w_out", "ln2_w", "ffn_w_up", "ffn_conv_w", "ffn_conv_b",
             "ffn_w_down", "final_norm_w"]
    full_shape = dict(w_in=w_in.shape, w_out=w_out.shape, ffn_w_up=ffn_w_up.shape, ffn_w_down=ffn_w_down.shape)
    grads, deltas, new_m, new_v = [], [], [], []
    for n in order:
        if n in big:
            g, d, mn, vn = (t.reshape(full_shape[n]) for t in big[n])
        else:
            g, d, mn, vn = gs[n].reshape(args[n][0].shape), s_delta[n], s_m[n], s_v[n]
        grads.append(g), deltas.append(d), new_m.append(mn), new_v.append(vn)
    return (loss, dx.reshape(B, S, D), *grads, *deltas, *new_m, *new_v)
```
